```python
import math
import jax, jax.numpy as jnp
from jax import lax
import numpy as np

D_MODEL = 1024
BATCH = 2
SEQ = 8192
DEPTH = 1
DEC_BATCH = 8
DEC_SEQ = 32
PAST_LEN = 2048

CHUNK = 64
Q_BLOCK = 128
SB_HEADS = 8
SB_HEAD_DIM = 64
DIFF_HEADS = 4
DIFF_HEAD_DIM = 64
SB_WIDTH = SB_HEADS * SB_HEAD_DIM
DIFF_WIDTH = DIFF_HEADS * 2 * DIFF_HEAD_DIM
MIX_WIDTH = SB_WIDTH + DIFF_WIDTH
IN_WIDTH = 3 * SB_WIDTH + 3 * DIFF_WIDTH
D_FF = 2816
RMS_EPS = 1e-6
LAMBDA_STD = 0.1

kernel_name = 'hymba_stickbreak_diffattn_macaron_step'


def rms_norm(x, g):
    xf = x.astype(jnp.float32)
    y = xf * lax.rsqrt(jnp.mean(xf * xf, axis=-1, keepdims=True) + RMS_EPS)
    return (y * g.astype(jnp.float32)).astype(x.dtype)


def swiglu(x, w_gate, w_up, w_down):
    return (jax.nn.silu(x @ w_gate) * (x @ w_up)) @ w_down


def alibi_slopes(n):
    return 2.0 ** (-8.0 * jnp.arange(1, n + 1, dtype=jnp.float32) / n)


def stick_breaking(q, k, v, q_pos, k_pos):
    z = jnp.einsum('bqhd,bkhd->bhqk', q, k).astype(jnp.float32) * (SB_HEAD_DIM ** -0.5)
    mask = k_pos[None, :] < q_pos[:, None]
    log_stay = jnp.where(mask, jax.nn.log_sigmoid(-z), 0.0)
    between = lax.cumsum(log_stay, axis=3, reverse=True) - log_stay
    w = jnp.where(mask, jnp.exp(jax.nn.log_sigmoid(z) + between), 0.0)
    return jnp.einsum('bhqk,bkhd->bqhd', w, v.astype(jnp.float32))


def diff_attention(q, k, v, q_pos, k_pos, lam):
    scale = DIFF_HEAD_DIM ** -0.5
    q1, q2 = q[..., :DIFF_HEAD_DIM], q[..., DIFF_HEAD_DIM:]
    k1, k2 = k[..., :DIFF_HEAD_DIM], k[..., DIFF_HEAD_DIM:]
    dist = jnp.abs(q_pos[:, None] - k_pos[None, :]).astype(jnp.float32)
    bias = -alibi_slopes(DIFF_HEADS)[:, None, None] * dist
    mask = (k_pos // CHUNK)[None, :] <= (q_pos // CHUNK)[:, None]

    def attn_map(qc, kc):
        s = jnp.einsum('bqhd,bkhd->bhqk', qc, kc).astype(jnp.float32) * scale + bias
        return jax.nn.softmax(jnp.where(mask, s, -jnp.inf), axis=-1)

    w = attn_map(q1, k1) - lam * attn_map(q2, k2)
    return jnp.einsum('bhqk,bkhd->bqhd', w, v.astype(jnp.float32))


def sweep_query_blocks(fn, q, q_pos):
    b, s = q.shape[:2]
    nb = s // Q_BLOCK
    qb = jnp.swapaxes(q.reshape((b, nb, Q_BLOCK) + q.shape[2:]), 0, 1)
    pb = q_pos.reshape(nb, Q_BLOCK)
    ob = lax.map(lambda a: fn(a[0], a[1]), (qb, pb))
    return jnp.swapaxes(ob, 0, 1).reshape((b, s) + ob.shape[3:])


def run_attention(fn, q, q_pos):
    n = q.shape[1]
    if n % Q_BLOCK == 0 and n > Q_BLOCK:
        return sweep_query_blocks(fn, q, q_pos)
    return fn(q, q_pos)


def token_mixer(hn, past, w_in, lam_q1, lam_k1, lam_q2, lam_k2, subln_g, w_out, lam_init):
    b, n, _ = hn.shape
    p = hn @ w_in
    cuts = [SB_WIDTH, 2 * SB_WIDTH, 3 * SB_WIDTH, 3 * SB_WIDTH + DIFF_WIDTH, 3 * SB_WIDTH + 2 * DIFF_WIDTH]
    sb_q, sb_k, sb_v, d_q, d_k, d_v = jnp.split(p, cuts, axis=-1)
    sb_shape = (b, n, SB_HEADS, SB_HEAD_DIM)
    d_shape = (b, n, DIFF_HEADS, 2 * DIFF_HEAD_DIM)
    sb_q, sb_k, sb_v = sb_q.reshape(sb_shape), sb_k.reshape(sb_shape), sb_v.reshape(sb_shape)
    d_q, d_k, d_v = d_q.reshape(d_shape), d_k.reshape(d_shape), d_v.reshape(d_shape)
    if past is None:
        offset = 0
        sb_k_all, sb_v_all, d_k_all, d_v_all = sb_k, sb_v, d_k, d_v
    else:
        c_sb_k, c_sb_v, c_d_k, c_d_v = past
        offset = c_sb_k.shape[1]
        sb_k_all = jnp.concatenate([c_sb_k.astype(sb_k.dtype), sb_k], axis=1)
        sb_v_all = jnp.concatenate([c_sb_v.astype(sb_v.dtype), sb_v], axis=1)
        d_k_all = jnp.concatenate([c_d_k.astype(d_k.dtype), d_k], axis=1)
        d_v_all = jnp.concatenate([c_d_v.astype(d_v.dtype), d_v], axis=1)
    q_pos = offset + jnp.arange(n, dtype=jnp.int32)
    k_pos = jnp.arange(offset + n, dtype=jnp.int32)
    lam = (jnp.exp(jnp.sum(lam_q1.astype(jnp.float32) * lam_k1.astype(jnp.float32)))
           - jnp.exp(jnp.sum(lam_q2.astype(jnp.float32) * lam_k2.astype(jnp.float32))) + lam_init)
    sb_o = run_attention(lambda qb, pb: stick_breaking(qb, sb_k_all, sb_v_all, pb, k_pos), sb_q, q_pos)
    d_o = run_attention(lambda qb, pb: diff_attention(qb, d_k_all, d_v_all, pb, k_pos, lam), d_q, q_pos)
    d_o = rms_norm(d_o, subln_g) * (1.0 - lam_init)
    o = jnp.concatenate([sb_o.reshape(b, n, SB_WIDTH), d_o.reshape(b, n, DIFF_WIDTH)], axis=-1)
    return o.astype(hn.dtype) @ w_out, (sb_k, sb_v, d_k, d_v)


def encoder_layer(x, past, ff1_pre_g, ff1_w_gate, ff1_w_up, ff1_w_down, ff1_post_g,
                  mix_pre_g, w_in, lam_q1, lam_k1, lam_q2, lam_k2, subln_g, w_out, mix_post_g,
                  ff2_pre_g, ff2_w_gate, ff2_w_up, ff2_w_down, ff2_post_g, final_g, lam_init):
    h = x + 0.5 * rms_norm(swiglu(rms_norm(x, ff1_pre_g), ff1_w_gate, ff1_w_up, ff1_w_down), ff1_post_g)
    mix, rows = token_mixer(rms_norm(h, mix_pre_g), past, w_in, lam_q1, lam_k1, lam_q2, lam_k2,
                            subln_g, w_out, lam_init)
    h = h + rms_norm(mix, mix_post_g)
    h = h + 0.5 * rms_norm(swiglu(rms_norm(h, ff2_pre_g), ff2_w_gate, ff2_w_up, ff2_w_down), ff2_post_g)
    return rms_norm(h, final_g), rows


def setup_inputs(seed: int = 0) -> dict:
    key = jax.random.key(seed)
    ks = iter(jax.random.split(key, 40))
    f32 = jnp.float32

    def nrm(shape, scale=1.0):
        return jax.random.normal(next(ks), shape, f32) * scale

    def gain(shape):
        return 1.0 + 0.05 * nrm(shape)

    L = DEPTH
    sb_c = (L, DEC_BATCH, PAST_LEN, SB_HEADS, SB_HEAD_DIM)
    d_c = (L, DEC_BATCH, PAST_LEN, DIFF_HEADS, 2 * DIFF_HEAD_DIM)
    return {
        'x_prompt': nrm((BATCH, SEQ, D_MODEL)),
        'x_sample': nrm((DEC_BATCH, DEC_SEQ, D_MODEL)),
        'cache_sb_k': nrm(sb_c),
        'cache_sb_v': nrm(sb_c),
        'cache_diff_k': nrm(d_c),
        'cache_diff_v': nrm(d_c),
        'ff1_pre_g': gain((L, D_MODEL)),
        'ff1_w_gate': nrm((L, D_MODEL, D_FF), D_MODEL ** -0.5),
        'ff1_w_up': nrm((L, D_MODEL, D_FF), D_MODEL ** -0.5),
        'ff1_w_down': nrm((L, D_FF, D_MODEL), D_FF ** -0.5),
        'ff1_post_g': gain((L, D_MODEL)),
        'mix_pre_g': gain((L, D_MODEL)),
        'w_in': nrm((L, D_MODEL, IN_WIDTH), D_MODEL ** -0.5),
        'lam_q1': nrm((L, DIFF_HEAD_DIM), LAMBDA_STD),
        'lam_k1': nrm((L, DIFF_HEAD_DIM), LAMBDA_STD),
        'lam_q2': nrm((L, DIFF_HEAD_DIM), LAMBDA_STD),
        'lam_k2': nrm((L, DIFF_HEAD_DIM), LAMBDA_STD),
        'subln_g': gain((L, 2 * DIFF_HEAD_DIM)),
        'w_out': nrm((L, MIX_WIDTH, D_MODEL), MIX_WIDTH ** -0.5),
        'mix_post_g': gain((L, D_MODEL)),
        'ff2_pre_g': gain((L, D_MODEL)),
        'ff2_w_gate': nrm((L, D_MODEL, D_FF), D_MODEL ** -0.5),
        'ff2_w_up': nrm((L, D_MODEL, D_FF), D_MODEL ** -0.5),
        'ff2_w_down': nrm((L, D_FF, D_MODEL), D_FF ** -0.5),
        'ff2_post_g': gain((L, D_MODEL)),
        'final_g': gain((L, D_MODEL)),
    }


def reference(x_prompt, x_sample, cache_sb_k, cache_sb_v, cache_diff_k, cache_diff_v,
              ff1_pre_g, ff1_w_gate, ff1_w_up, ff1_w_down, ff1_post_g,
              mix_pre_g, w_in, lam_q1, lam_k1, lam_q2, lam_k2, subln_g, w_out, mix_post_g,
              ff2_pre_g, ff2_w_gate, ff2_w_up, ff2_w_down, ff2_post_g, final_g):
    yp, ys = x_prompt, x_sample
    rows_p = [[], [], [], []]
    rows_s = [[], [], [], []]
    for l in range(DEPTH):
        lam_init = 0.8 - 0.6 * math.exp(-0.3 * l)
        w = (ff1_pre_g[l], ff1_w_gate[l], ff1_w_up[l], ff1_w_down[l], ff1_post_g[l],
             mix_pre_g[l], w_in[l], lam_q1[l], lam_k1[l], lam_q2[l], lam_k2[l], subln_g[l], w_out[l],
             mix_post_g[l], ff2_pre_g[l], ff2_w_gate[l], ff2_w_up[l], ff2_w_down[l], ff2_post_g[l],
             final_g[l])
        yp, rp = encoder_layer(yp, None, *w, lam_init)
        past = (cache_sb_k[l], cache_sb_v[l], cache_diff_k[l], cache_diff_v[l])
        ys, rs = encoder_layer(ys, past, *w, lam_init)
        for i in range(4):
            rows_p[i].append(rp[i])
            rows_s[i].append(rs[i])
    sb_k_p, sb_v_p, d_k_p, d_v_p = [jnp.stack(r, axis=0) for r in rows_p]
    sb_k_s, sb_v_s, d_k_s, d_v_s = [jnp.stack(r, axis=0) for r in rows_s]
    return (yp, ys, sb_k_p, sb_v_p, d_k_p, d_v_p, sb_k_s, sb_v_s, d_k_s, d_v_s)
```

```python
import functools
import math

import jax
import jax.numpy as jnp
from jax import lax
from jax.experimental import pallas as pl
from jax.experimental.pallas import tpu as pltpu

D_MODEL = 1024
DEPTH = 1
CHUNK = 64
CHUNK_SHIFT = 6
SB_HEADS = 8
SB_HEAD_DIM = 64
DIFF_HEADS = 4
DIFF_HEAD_DIM = 64
SB_WIDTH = SB_HEADS * SB_HEAD_DIM
DIFF_WIDTH = DIFF_HEADS * 2 * DIFF_HEAD_DIM
MIX_WIDTH = SB_WIDTH + DIFF_WIDTH
IN_WIDTH = 3 * SB_WIDTH + 3 * DIFF_WIDTH
D_FF = 2816
RMS_EPS = 1e-6

LANES = 128
VMEM_LIMIT_BYTES = 56 * 1024 * 1024
F32_EXP_ZERO = 104.0
NEG_BIG = -1e30

_F32 = jnp.float32
_BF16 = jnp.bfloat16


def _rms(x, g):
    return x * lax.rsqrt(jnp.mean(x * x, axis=-1, keepdims=True) + RMS_EPS) * g


def _dot(a, b):
    return jnp.dot(a, b, preferred_element_type=_F32)


def _dot_nt(a, b):
    return lax.dot_general(a, b, (((1,), (1,)), ((), ())), preferred_element_type=_F32)


def _ffn_kernel(*refs, final_norm):
    if final_norm:
        x_ref, pre_ref, wg_ref, wu_ref, wd_ref, post_ref, fin_ref, o_ref, xn_ref, acc_ref = refs
    else:
        x_ref, pre_ref, wg_ref, wu_ref, wd_ref, post_ref, o_ref, xn_ref, acc_ref = refs
    j = pl.program_id(1)

    @pl.when(j == 0)
    def _():
        xn_ref[...] = _rms(x_ref[...], pre_ref[...]).astype(_BF16)
        acc_ref[...] = jnp.zeros_like(acc_ref)

    xn = xn_ref[...]
    g = _dot(xn, wg_ref[...])
    u = _dot(xn, wu_ref[...])
    a = (g * jax.nn.sigmoid(g) * u).astype(_BF16)
    acc_ref[...] += _dot(a, wd_ref[...])

    @pl.when(j == pl.num_programs(1) - 1)
    def _():
        h = x_ref[...] + 0.5 * _rms(acc_ref[...], post_ref[...])
        if final_norm:
            h = _rms(h, fin_ref[...])
        o_ref[...] = h


def _ffn(x, pre_g, wg, wu, wd, post_g, final_g=None):
    m = x.shape[0]
    tm = min(512, m)
    tf = D_FF // 2
    assert m % tm == 0 and D_FF % tf == 0 and tf % LANES == 0
    row = pl.BlockSpec((tm, D_MODEL), lambda i, j: (i, 0))
    vec = pl.BlockSpec((1, D_MODEL), lambda i, j: (0, 0))
    w_in = pl.BlockSpec((D_MODEL, tf), lambda i, j: (0, j))
    w_out = pl.BlockSpec((tf, D_MODEL), lambda i, j: (j, 0))
    in_specs = [row, vec, w_in, w_in, w_out, vec]
    args = [x, pre_g, wg, wu, wd, post_g]
    if final_g is not None:
        in_specs.append(vec)
        args.append(final_g)
    return pl.pallas_call(
        functools.partial(_ffn_kernel, final_norm=final_g is not None),
        grid=(m // tm, D_FF // tf),
        in_specs=in_specs,
        out_specs=row,
        out_shape=jax.ShapeDtypeStruct((m, D_MODEL), _F32),
        scratch_shapes=[pltpu.VMEM((tm, D_MODEL), _BF16), pltpu.VMEM((tm, D_MODEL), _F32)],
        compiler_params=pltpu.CompilerParams(
            dimension_semantics=("parallel", "arbitrary"), vmem_limit_bytes=VMEM_LIMIT_BYTES),
        name="ffn_final" if final_g is not None else "ffn",
    )(*args)


def _qkv_kernel(h_ref, g_ref, w_ref, sbq_ref, sbk_ref, sbv_ref, dq_ref, dk_ref, dv_ref,
                sbk16_ref, sbv16_ref, dk16_ref, dv16_ref):
    hn = _rms(h_ref[...], g_ref[...]).astype(_BF16)

    def proj(idx):
        return _dot(hn, w_ref[:, idx * SB_WIDTH:(idx + 1) * SB_WIDTH])

    sbq_ref[...] = (proj(0) * (SB_HEAD_DIM ** -0.5)).astype(_BF16)
    dq_ref[...] = (proj(3) * (DIFF_HEAD_DIM ** -0.5)).astype(_BF16)
    for idx, full_ref, half_ref in ((1, sbk_ref, sbk16_ref), (2, sbv_ref, sbv16_ref),
                                    (4, dk_ref, dk16_ref), (5, dv_ref, dv16_ref)):
        p = proj(idx)
        full_ref[...] = p
        half_ref[...] = p.astype(_BF16)


def _qkv(h, g, w):
    assert SB_WIDTH == DIFF_WIDTH
    assert math.log2(SB_HEAD_DIM) % 2 == 0 and math.log2(DIFF_HEAD_DIM) % 2 == 0
    m = h.shape[0]
    tm = min(512, m)
    assert m % tm == 0
    row = pl.BlockSpec((tm, D_MODEL), lambda i: (i, 0))
    out = pl.BlockSpec((tm, SB_WIDTH), lambda i: (i, 0))
    f32 = jax.ShapeDtypeStruct((m, SB_WIDTH), _F32)
    b16 = jax.ShapeDtypeStruct((m, SB_WIDTH), _BF16)
    return pl.pallas_call(
        _qkv_kernel,
        grid=(m // tm,),
        in_specs=[row, pl.BlockSpec((1, D_MODEL), lambda i: (0, 0)),
                  pl.BlockSpec((D_MODEL, IN_WIDTH), lambda i: (0, 0))],
        out_specs=[out] * 10,
        out_shape=[b16, f32, f32, b16, f32, f32, b16, b16, b16, b16],
        compiler_params=pltpu.CompilerParams(
            dimension_semantics=("parallel",), vmem_limit_bytes=VMEM_LIMIT_BYTES),
        name="qkv",
    )(h, g, w)


def _sb_kernel(q_ref, k_ref, v_ref, o_ref, acc_ref, c_ref, *, tq, tk, offset):
    qi = pl.program_id(2)
    q0 = offset + qi * tq
    lane = lax.broadcasted_iota(jnp.int32, (1, LANES), 1)
    row = lax.broadcasted_iota(jnp.int32, (tq, tk), 0)
    col = lax.broadcasted_iota(jnp.int32, (tq, tk), 1)
    later = (lax.broadcasted_iota(jnp.int32, (tk, tk), 0)
             > lax.broadcasted_iota(jnp.int32, (tk, tk), 1)).astype(_BF16)
    q = q_ref[0].astype(_F32)

    def walk_block(qh, j, mask):
        k0 = pl.multiple_of(j * tk, tk)
        z = _dot_nt(qh, k_ref[0, pl.ds(k0, tk), :])
        t = jnp.log1p(jnp.exp(-jnp.abs(z)))
        log_beta = jnp.minimum(z, 0.0) - t
        log_stay = -jnp.maximum(z, 0.0) - t
        if mask is not None:
            log_stay = jnp.where(mask, log_stay, 0.0)
        hi = log_stay.astype(_BF16)
        lo = (log_stay - hi.astype(_F32)).astype(_BF16)
        sums = _dot(jnp.concatenate([hi, lo], axis=0), later)
        between = sums[:tq] + sums[tq:] + c_ref[...]
        w = jnp.exp(log_beta + between)
        if mask is not None:
            w = jnp.where(mask, w, 0.0)
        acc_ref[...] += _dot(w.astype(_BF16), v_ref[0, pl.ds(k0, tk), :])
        c_new = c_ref[...] + jnp.sum(log_stay, axis=-1, keepdims=True)
        c_ref[...] = c_new
        return jnp.max(c_new)

    heads = []
    for head in range(LANES // SB_HEAD_DIM):
        in_head = (lane >= head * SB_HEAD_DIM) & (lane < (head + 1) * SB_HEAD_DIM)
        qh = jnp.where(in_head, q, 0.0).astype(_BF16)
        acc_ref[...] = jnp.zeros_like(acc_ref)
        c_ref[...] = jnp.zeros_like(c_ref)
        j_top = (q0 + tq - 1) // tk
        c_max = jnp.float32(0.0)
        for d in range(_straddle(tq, tk)):
            j = j_top - d
            c_max = walk_block(qh, j, (j * tk + col) < (q0 + row))

        def cond(state):
            j, c_max = state
            return (j >= 0) & (c_max > -F32_EXP_ZERO)

        def body(state):
            j, _ = state
            return j - 1, walk_block(qh, j, None)

        lax.while_loop(cond, body, (j_top - _straddle(tq, tk), c_max))
        heads.append(acc_ref[...])

    o = heads[0]
    for head in range(1, len(heads)):
        o = jnp.where(lane >= head * SB_HEAD_DIM, heads[head], o)
    o_ref[0] = o.astype(o_ref.dtype)


def _attention_specs(nq, nk, tq, tk, offset):
    assert nq % tq == 0 and nk % tk == 0 and offset % tk == 0 and offset + nq <= nk
    assert tq % tk == 0 or tk % tq == 0
    q_spec = pl.BlockSpec((1, tq, LANES), lambda bi, g, qi: (bi, qi, g))
    kv_spec = pl.BlockSpec((1, nk, LANES), lambda bi, g, qi: (bi, 0, g))
    return q_spec, kv_spec


def _straddle(tq, tk):
    return max(1, tq // tk)


def _sb_call(q, k, v, *, tq, tk, offset):
    b, nq, _ = q.shape
    nk = k.shape[1]
    q_spec, kv_spec = _attention_specs(nq, nk, tq, tk, offset)
    return pl.pallas_call(
        functools.partial(_sb_kernel, tq=tq, tk=tk, offset=offset),
        grid=(b, SB_WIDTH // LANES, nq // tq),
        in_specs=[q_spec, kv_spec, kv_spec],
        out_specs=q_spec,
        out_shape=jax.ShapeDtypeStruct((b, nq, SB_WIDTH), _BF16),
        scratch_shapes=[pltpu.VMEM((tq, LANES), _F32), pltpu.VMEM((tq, 1), _F32)],
        compiler_params=pltpu.CompilerParams(
            dimension_semantics=("parallel", "parallel", "arbitrary"),
            vmem_limit_bytes=VMEM_LIMIT_BYTES),
        name="sb_attention",
    )(q, k, v)


def _diff_kernel(q_ref, k_ref, v_ref, lq1_ref, lk1_ref, lq2_ref, lk2_ref, g_ref, o_ref,
                 m_ref, l_ref, acc_ref, *, tq, tk, offset, nk_valid, lam_init):
    head = pl.program_id(1)
    qi = pl.program_id(2)
    q0 = offset + qi * tq
    lane = lax.broadcasted_iota(jnp.int32, (1, LANES), 1)
    row = lax.broadcasted_iota(jnp.int32, (tq, tk), 0)
    col = lax.broadcasted_iota(jnp.int32, (tq, tk), 1)
    exponent = (8 // DIFF_HEADS) * (head + 1)
    slope = lax.bitcast_convert_type(jnp.full((1, 1), (127 - exponent) << 23, jnp.int32), _F32)
    rel = (row - col).astype(_F32)
    q = q_ref[0].astype(_F32)
    qs = [jnp.where((lane >= c * DIFF_HEAD_DIM) & (lane < (c + 1) * DIFF_HEAD_DIM), q, 0.0).astype(_BF16)
          for c in range(2)]
    m_ref[...] = jnp.full_like(m_ref, NEG_BIG)
    l_ref[...] = jnp.zeros_like(l_ref)
    acc_ref[...] = jnp.zeros_like(acc_ref)

    def update(c, s, v_blk):
        m_old = m_ref[c]
        m_new = jnp.maximum(m_old, jnp.max(s, axis=-1, keepdims=True))
        alpha = jnp.exp(m_old - m_new)
        p = jnp.exp(s - m_new)
        l_ref[c] = alpha * l_ref[c] + jnp.sum(p, axis=-1, keepdims=True)
        acc_ref[c] = alpha * acc_ref[c] + _dot(p.astype(_BF16), v_blk)
        m_ref[c] = m_new

    def past_block(j, carry):
        k0 = pl.multiple_of(j * tk, tk)
        k_blk = k_ref[0, pl.ds(k0, tk), :]
        v_blk = v_ref[0, pl.ds(k0, tk), :]
        bias = -slope * (rel + (q0 - k0).astype(_F32))
        for c in range(2):
            update(c, _dot_nt(qs[c], k_blk) + bias, v_blk)
        return carry

    j_first = q0 // tk
    lax.fori_loop(0, j_first, past_block, 0)

    for d in range(_straddle(tq, tk)):
        j = j_first + d
        k0 = pl.multiple_of(j * tk, tk)
        k_blk = k_ref[0, pl.ds(k0, tk), :]
        v_blk = v_ref[0, pl.ds(k0, tk), :]
        k_pos = k0 + col
        q_pos = q0 + row
        mask = ((lax.shift_right_logical(k_pos, CHUNK_SHIFT) <= lax.shift_right_logical(q_pos, CHUNK_SHIFT))
                & (k_pos < nk_valid))
        bias = -slope * jnp.abs(q_pos - k_pos).astype(_F32)
        for c in range(2):
            s = jnp.where(mask, _dot_nt(qs[c], k_blk) + bias, -jnp.inf)
            update(c, s, v_blk)

    lam = (jnp.exp(jnp.sum(lq1_ref[...] * lk1_ref[...], axis=-1, keepdims=True))
           - jnp.exp(jnp.sum(lq2_ref[...] * lk2_ref[...], axis=-1, keepdims=True)) + lam_init)
    o = acc_ref[0] / l_ref[0] - lam * (acc_ref[1] / l_ref[1])
    o = _rms(o, g_ref[...]) * (1.0 - lam_init)
    o_ref[0] = o.astype(o_ref.dtype)


def _diff_call(q, k, v, lq1, lk1, lq2, lk2, subln_g, *, tq, tk, offset, nk_valid, lam_init):
    b, nq, _ = q.shape
    nk = k.shape[1]
    assert 2 * DIFF_HEAD_DIM == LANES and 8 % DIFF_HEADS == 0
    assert tk % CHUNK == 0 and CHUNK == 1 << CHUNK_SHIFT and offset + nq <= nk_valid <= nk
    q_spec, kv_spec = _attention_specs(nq, nk, tq, tk, offset)
    lam_spec = pl.BlockSpec((1, DIFF_HEAD_DIM), lambda bi, g, qi: (0, 0))
    return pl.pallas_call(
        functools.partial(_diff_kernel, tq=tq, tk=tk, offset=offset, nk_valid=nk_valid,
                          lam_init=lam_init),
        grid=(b, DIFF_HEADS, nq // tq),
        in_specs=[q_spec, kv_spec, kv_spec, lam_spec, lam_spec, lam_spec, lam_spec,
                  pl.BlockSpec((1, LANES), lambda bi, g, qi: (0, 0))],
        out_specs=q_spec,
        out_shape=jax.ShapeDtypeStruct((b, nq, DIFF_WIDTH), _BF16),
        scratch_shapes=[pltpu.VMEM((2, tq, 1), _F32), pltpu.VMEM((2, tq, 1), _F32),
                        pltpu.VMEM((2, tq, LANES), _F32)],
        compiler_params=pltpu.CompilerParams(
            dimension_semantics=("parallel", "parallel", "arbitrary"),
            vmem_limit_bytes=VMEM_LIMIT_BYTES),
        name="diff_attention",
    )(q, k, v, lq1, lk1, lq2, lk2, subln_g)


def _mix_out_kernel(sbo_ref, do_ref, h_ref, w_ref, g_ref, o_ref):
    mix = _dot(sbo_ref[...], w_ref[:SB_WIDTH, :]) + _dot(do_ref[...], w_ref[SB_WIDTH:, :])
    o_ref[...] = h_ref[...] + _rms(mix, g_ref[...])


def _mix_out(sb_o, d_o, h, w, g):
    m = h.shape[0]
    tm = min(512, m)
    assert m % tm == 0
    half = pl.BlockSpec((tm, SB_WIDTH), lambda i: (i, 0))
    row = pl.BlockSpec((tm, D_MODEL), lambda i: (i, 0))
    return pl.pallas_call(
        _mix_out_kernel,
        grid=(m // tm,),
        in_specs=[half, half, row, pl.BlockSpec((MIX_WIDTH, D_MODEL), lambda i: (0, 0)),
                  pl.BlockSpec((1, D_MODEL), lambda i: (0, 0))],
        out_specs=row,
        out_shape=jax.ShapeDtypeStruct((m, D_MODEL), _F32),
        compiler_params=pltpu.CompilerParams(
            dimension_semantics=("parallel",), vmem_limit_bytes=VMEM_LIMIT_BYTES),
        name="mix_out",
    )(sb_o, d_o, h, w, g)


def _encoder_layer(x, past, w, lam_init, *, tq, tk):
    b, n, _ = x.shape
    m = b * n
    h = _ffn(x.reshape(m, D_MODEL), w["ff1_pre_g"], w["ff1_w_gate"], w["ff1_w_up"], w["ff1_w_down"],
             w["ff1_post_g"])
    sbq, sbk, sbv, dq, dk, dv, sbk16, sbv16, dk16, dv16 = _qkv(h, w["mix_pre_g"], w["w_in"])

    def seq(a):
        return a.reshape(b, n, a.shape[-1])

    keys = [seq(a) for a in (sbk16, sbv16, dk16, dv16)]
    offset = 0
    if past is not None:
        offset = past[0].shape[1]
        pad = -(offset + n) % tk
        keys = [jnp.concatenate([c.astype(_BF16), new, jnp.zeros((b, pad, new.shape[-1]), _BF16)], axis=1)
                for c, new in zip(past, keys)]
    sb_o = _sb_call(seq(sbq), keys[0], keys[1], tq=tq, tk=tk, offset=offset)
    d_o = _diff_call(seq(dq), keys[2], keys[3], w["lam_q1"], w["lam_k1"], w["lam_q2"], w["lam_k2"],
                     w["subln_g"], tq=tq, tk=tk, offset=offset, nk_valid=offset + n, lam_init=lam_init)
    h = _mix_out(sb_o.reshape(m, SB_WIDTH), d_o.reshape(m, DIFF_WIDTH), h, w["w_out"], w["mix_post_g"])
    y = _ffn(h, w["ff2_pre_g"], w["ff2_w_gate"], w["ff2_w_up"], w["ff2_w_down"], w["ff2_post_g"],
             w["final_g"])
    rows = (sbk.reshape(b, n, SB_HEADS, SB_HEAD_DIM), sbv.reshape(b, n, SB_HEADS, SB_HEAD_DIM),
            dk.reshape(b, n, DIFF_HEADS, 2 * DIFF_HEAD_DIM), dv.reshape(b, n, DIFF_HEADS, 2 * DIFF_HEAD_DIM))
    return y.reshape(b, n, D_MODEL), rows


_MATRICES = ("ff1_w_gate", "ff1_w_up", "ff1_w_down", "w_in", "w_out", "ff2_w_gate", "ff2_w_up", "ff2_w_down")


def kernel(x_prompt, x_sample, cache_sb_k, cache_sb_v, cache_diff_k, cache_diff_v, ff1_pre_g, ff1_w_gate, ff1_w_up, ff1_w_down, ff1_post_g, mix_pre_g, w_in, lam_q1, lam_k1, lam_q2, lam_k2, subln_g, w_out, mix_post_g, ff2_pre_g, ff2_w_gate, ff2_w_up, ff2_w_down, ff2_post_g, final_g):
    params = dict(ff1_pre_g=ff1_pre_g, ff1_w_gate=ff1_w_gate, ff1_w_up=ff1_w_up, ff1_w_down=ff1_w_down,
                  ff1_post_g=ff1_post_g, mix_pre_g=mix_pre_g, w_in=w_in, lam_q1=lam_q1, lam_k1=lam_k1,
                  lam_q2=lam_q2, lam_k2=lam_k2, subln_g=subln_g, w_out=w_out, mix_post_g=mix_post_g,
                  ff2_pre_g=ff2_pre_g, ff2_w_gate=ff2_w_gate, ff2_w_up=ff2_w_up, ff2_w_down=ff2_w_down,
                  ff2_post_g=ff2_post_g, final_g=final_g)
    yp, ys = x_prompt, x_sample
    rows_p, rows_s = [], []
    for l in range(DEPTH):
        lam_init = 0.8 - 0.6 * math.exp(-0.3 * l)
        w = {name: (p[l].astype(_BF16) if name in _MATRICES else p[l][None, :].astype(_F32))
             for name, p in params.items()}
        yp, rp = _encoder_layer(yp, None, w, lam_init, tq=256, tk=256)
        past = tuple(c[l].reshape(c.shape[1], c.shape[2], -1)
                     for c in (cache_sb_k, cache_sb_v, cache_diff_k, cache_diff_v))
        ys, rs = _encoder_layer(ys, past, w, lam_init, tq=x_sample.shape[1], tk=128)
        rows_p.append(rp)
        rows_s.append(rs)
    stacked_p = [jnp.stack(r, axis=0) for r in zip(*rows_p)]
    stacked_s = [jnp.stack(r, axis=0) for r in zip(*rows_s)]
    return (yp, ys, *stacked_p, *stacked_s)
```

```python
import functools
import math

import jax
import jax.numpy as jnp
from jax import lax
from jax.experimental import pallas as pl
from jax.experimental.pallas import tpu as pltpu

D_MODEL = 1024
DEPTH = 1
CHUNK = 64
CHUNK_SHIFT = 6
SB_HEADS = 8
SB_HEAD_DIM = 64
DIFF_HEADS = 4
DIFF_HEAD_DIM = 64
SB_WIDTH = SB_HEADS * SB_HEAD_DIM
DIFF_WIDTH = DIFF_HEADS * 2 * DIFF_HEAD_DIM
MIX_WIDTH = SB_WIDTH + DIFF_WIDTH
IN_WIDTH = 3 * SB_WIDTH + 3 * DIFF_WIDTH
D_FF = 2816
RMS_EPS = 1e-6

LANES = 128
VMEM_LIMIT_BYTES = 56 * 1024 * 1024
F32_EXP_ZERO = 104.0
NEG_BIG = -1e30

_F32 = jnp.float32
_BF16 = jnp.bfloat16


def _rms(x, g):
    return x * lax.rsqrt(jnp.mean(x * x, axis=-1, keepdims=True) + RMS_EPS) * g


def _dot(a, b):
    return jnp.dot(a, b, preferred_element_type=_F32)


def _dot_nt(a, b):
    return lax.dot_general(a, b, (((1,), (1,)), ((), ())), preferred_element_type=_F32)


def _ffn_kernel(*refs, final_norm):
    if final_norm:
        x_ref, pre_ref, wg_ref, wu_ref, wd_ref, post_ref, fin_ref, o_ref, xn_ref, acc_ref = refs
    else:
        x_ref, pre_ref, wg_ref, wu_ref, wd_ref, post_ref, o_ref, xn_ref, acc_ref = refs
    j = pl.program_id(1)

    @pl.when(j == 0)
    def _():
        xn_ref[...] = _rms(x_ref[...], pre_ref[...]).astype(_BF16)
        acc_ref[...] = jnp.zeros_like(acc_ref)

    xn = xn_ref[...]
    g = _dot(xn, wg_ref[...])
    u = _dot(xn, wu_ref[...])
    a = (g * jax.nn.sigmoid(g) * u).astype(_BF16)
    acc_ref[...] += _dot(a, wd_ref[...])

    @pl.when(j == pl.num_programs(1) - 1)
    def _():
        h = x_ref[...] + 0.5 * _rms(acc_ref[...], post_ref[...])
        if final_norm:
            h = _rms(h, fin_ref[...])
        o_ref[...] = h


def _ffn(x, pre_g, wg, wu, wd, post_g, final_g=None):
    m = x.shape[0]
    tm = min(512, m)
    tf = D_FF // 2
    assert m % tm == 0 and D_FF % tf == 0 and tf % LANES == 0
    row = pl.BlockSpec((tm, D_MODEL), lambda i, j: (i, 0))
    vec = pl.BlockSpec((1, D_MODEL), lambda i, j: (0, 0))
    w_in = pl.BlockSpec((D_MODEL, tf), lambda i, j: (0, j))
    w_out = pl.BlockSpec((tf, D_MODEL), lambda i, j: (j, 0))
    in_specs = [row, vec, w_in, w_in, w_out, vec]
    args = [x, pre_g, wg, wu, wd, post_g]
    if final_g is not None:
        in_specs.append(vec)
        args.append(final_g)
    return pl.pallas_call(
        functools.partial(_ffn_kernel, final_norm=final_g is not None),
        grid=(m // tm, D_FF // tf),
        in_specs=in_specs,
        out_specs=row,
        out_shape=jax.ShapeDtypeStruct((m, D_MODEL), _F32),
        scratch_shapes=[pltpu.VMEM((tm, D_MODEL), _BF16), pltpu.VMEM((tm, D_MODEL), _F32)],
        compiler_params=pltpu.CompilerParams(
            dimension_semantics=("parallel", "arbitrary"), vmem_limit_bytes=VMEM_LIMIT_BYTES),
        name="ffn_final" if final_g is not None else "ffn",
    )(*args)


def _qkv_kernel(h_ref, g_ref, w_ref, sbq_ref, sbk_ref, sbv_ref, dq_ref, dk_ref, dv_ref,
                sbk16_ref, sbv16_ref, dk16_ref, dv16_ref, *, v_transposed):
    hn = _rms(h_ref[...], g_ref[...]).astype(_BF16)

    def proj(idx):
        return _dot(hn, w_ref[:, idx * SB_WIDTH:(idx + 1) * SB_WIDTH])

    sbq_ref[...] = (proj(0) * (SB_HEAD_DIM ** -0.5)).astype(_BF16)
    dq_ref[...] = (proj(3) * (DIFF_HEAD_DIM ** -0.5)).astype(_BF16)
    for idx, full_ref, half_ref in ((1, sbk_ref, sbk16_ref), (4, dk_ref, dk16_ref)):
        p = proj(idx)
        full_ref[...] = p
        half_ref[...] = p.astype(_BF16)
    for idx, full_ref, half_ref in ((2, sbv_ref, sbv16_ref), (5, dv_ref, dv16_ref)):
        p = proj(idx)
        full_ref[...] = p
        if v_transposed:
            half_ref[0] = p.T.astype(_BF16)
        else:
            half_ref[...] = p.astype(_BF16)


def _qkv(h, g, w, *, seq_len=None):
    assert SB_WIDTH == DIFF_WIDTH
    assert math.log2(SB_HEAD_DIM) % 2 == 0 and math.log2(DIFF_HEAD_DIM) % 2 == 0
    m = h.shape[0]
    tm = min(512, m)
    assert m % tm == 0
    row = pl.BlockSpec((tm, D_MODEL), lambda i: (i, 0))
    out = pl.BlockSpec((tm, SB_WIDTH), lambda i: (i, 0))
    f32 = jax.ShapeDtypeStruct((m, SB_WIDTH), _F32)
    b16 = jax.ShapeDtypeStruct((m, SB_WIDTH), _BF16)
    v_out, v16 = out, b16
    if seq_len is not None:
        assert seq_len % tm == 0
        tiles = seq_len // tm
        v_out = pl.BlockSpec((1, SB_WIDTH, tm), lambda i: (i // tiles, 0, i % tiles))
        v16 = jax.ShapeDtypeStruct((m // seq_len, SB_WIDTH, seq_len), _BF16)
    return pl.pallas_call(
        functools.partial(_qkv_kernel, v_transposed=seq_len is not None),
        grid=(m // tm,),
        in_specs=[row, pl.BlockSpec((1, D_MODEL), lambda i: (0, 0)),
                  pl.BlockSpec((D_MODEL, IN_WIDTH), lambda i: (0, 0))],
        out_specs=[out] * 7 + [v_out, out, v_out],
        out_shape=[b16, f32, f32, b16, f32, f32, b16, v16, b16, v16],
        compiler_params=pltpu.CompilerParams(
            dimension_semantics=("parallel",), vmem_limit_bytes=VMEM_LIMIT_BYTES),
        name="qkv",
    )(h, g, w)


def _sb_kernel(q_ref, k_ref, v_ref, o_ref, acc_ref, c_ref, *, tq, tk, offset):
    qi = pl.program_id(2)
    q0 = offset + qi * tq
    lane = lax.broadcasted_iota(jnp.int32, (1, LANES), 1)
    row = lax.broadcasted_iota(jnp.int32, (tq, tk), 0)
    col = lax.broadcasted_iota(jnp.int32, (tq, tk), 1)
    later = (lax.broadcasted_iota(jnp.int32, (tk, tk), 0)
             > lax.broadcasted_iota(jnp.int32, (tk, tk), 1)).astype(_BF16)
    q = q_ref[0].astype(_F32)

    def walk_block(qh, j, mask):
        k0 = pl.multiple_of(j * tk, tk)
        z = _dot_nt(qh, k_ref[0, pl.ds(k0, tk), :])
        t = jnp.log1p(jnp.exp(-jnp.abs(z)))
        log_beta = jnp.minimum(z, 0.0) - t
        log_stay = -jnp.maximum(z, 0.0) - t
        if mask is not None:
            log_stay = jnp.where(mask, log_stay, 0.0)
        hi = log_stay.astype(_BF16)
        lo = (log_stay - hi.astype(_F32)).astype(_BF16)
        sums = _dot(jnp.concatenate([hi, lo], axis=0), later)
        between = sums[:tq] + sums[tq:] + c_ref[...]
        w = jnp.exp(log_beta + between)
        if mask is not None:
            w = jnp.where(mask, w, 0.0)
        acc_ref[...] += _dot(w.astype(_BF16), v_ref[0, pl.ds(k0, tk), :])
        c_new = c_ref[...] + jnp.sum(log_stay, axis=-1, keepdims=True)
        c_ref[...] = c_new
        return jnp.max(c_new)

    heads = []
    for head in range(LANES // SB_HEAD_DIM):
        in_head = (lane >= head * SB_HEAD_DIM) & (lane < (head + 1) * SB_HEAD_DIM)
        qh = jnp.where(in_head, q, 0.0).astype(_BF16)
        acc_ref[...] = jnp.zeros_like(acc_ref)
        c_ref[...] = jnp.zeros_like(c_ref)
        j_top = (q0 + tq - 1) // tk
        c_max = jnp.float32(0.0)
        for d in range(_straddle(tq, tk)):
            j = j_top - d
            c_max = walk_block(qh, j, (j * tk + col) < (q0 + row))

        def cond(state):
            j, c_max = state
            return (j >= 0) & (c_max > -F32_EXP_ZERO)

        def body(state):
            j, _ = state
            return j - 1, walk_block(qh, j, None)

        lax.while_loop(cond, body, (j_top - _straddle(tq, tk), c_max))
        heads.append(acc_ref[...])

    o = heads[0]
    for head in range(1, len(heads)):
        o = jnp.where(lane >= head * SB_HEAD_DIM, heads[head], o)
    o_ref[0] = o.astype(o_ref.dtype)


def _attention_specs(nq, nk, tq, tk, offset):
    assert nq % tq == 0 and nk % tk == 0 and offset % tk == 0 and offset + nq <= nk
    assert tq % tk == 0 or tk % tq == 0
    q_spec = pl.BlockSpec((1, tq, LANES), lambda bi, g, qi: (bi, qi, g))
    kv_spec = pl.BlockSpec((1, nk, LANES), lambda bi, g, qi: (bi, 0, g))
    return q_spec, kv_spec


def _straddle(tq, tk):
    return max(1, tq // tk)


def _sb_call(q, k, v, *, tq, tk, offset):
    b, nq, _ = q.shape
    nk = k.shape[1]
    q_spec, kv_spec = _attention_specs(nq, nk, tq, tk, offset)
    return pl.pallas_call(
        functools.partial(_sb_kernel, tq=tq, tk=tk, offset=offset),
        grid=(b, SB_WIDTH // LANES, nq // tq),
        in_specs=[q_spec, kv_spec, kv_spec],
        out_specs=q_spec,
        out_shape=jax.ShapeDtypeStruct((b, nq, SB_WIDTH), _BF16),
        scratch_shapes=[pltpu.VMEM((tq, LANES), _F32), pltpu.VMEM((tq, 1), _F32)],
        compiler_params=pltpu.CompilerParams(
            dimension_semantics=("parallel", "parallel", "arbitrary"),
            vmem_limit_bytes=VMEM_LIMIT_BYTES),
        name="sb_attention",
    )(q, k, v)


def _diff_kernel(q_ref, k_ref, v_ref, lq1_ref, lk1_ref, lq2_ref, lk2_ref, g_ref, o_ref,
                 m_ref, l_ref, acc_ref, *, tq, tk, offset, nk_valid, lam_init):
    head = pl.program_id(1)
    qi = pl.program_id(2)
    q0 = offset + qi * tq
    lane = lax.broadcasted_iota(jnp.int32, (1, LANES), 1)
    row = lax.broadcasted_iota(jnp.int32, (tq, tk), 0)
    col = lax.broadcasted_iota(jnp.int32, (tq, tk), 1)
    exponent = (8 // DIFF_HEADS) * (head + 1)
    slope = lax.bitcast_convert_type(jnp.full((1, 1), (127 - exponent) << 23, jnp.int32), _F32)
    rel = (row - col).astype(_F32)
    q = q_ref[0].astype(_F32)
    qs = [jnp.where((lane >= c * DIFF_HEAD_DIM) & (lane < (c + 1) * DIFF_HEAD_DIM), q, 0.0).astype(_BF16)
          for c in range(2)]
    m_ref[...] = jnp.full_like(m_ref, NEG_BIG)
    l_ref[...] = jnp.zeros_like(l_ref)
    acc_ref[...] = jnp.zeros_like(acc_ref)

    def update(c, s, v_blk):
        m_old = m_ref[c]
        m_new = jnp.maximum(m_old, jnp.max(s, axis=-1, keepdims=True))
        alpha = jnp.exp(m_old - m_new)
        p = jnp.exp(s - m_new)
        l_ref[c] = alpha * l_ref[c] + jnp.sum(p, axis=-1, keepdims=True)
        acc_ref[c] = alpha * acc_ref[c] + _dot(p.astype(_BF16), v_blk)
        m_ref[c] = m_new

    def past_block(j, carry):
        k0 = pl.multiple_of(j * tk, tk)
        k_blk = k_ref[0, pl.ds(k0, tk), :]
        v_blk = v_ref[0, pl.ds(k0, tk), :]
        bias = -slope * (rel + (q0 - k0).astype(_F32))
        for c in range(2):
            update(c, _dot_nt(qs[c], k_blk) + bias, v_blk)
        return carry

    j_first = q0 // tk
    lax.fori_loop(0, j_first, past_block, 0)

    for d in range(_straddle(tq, tk)):
        j = j_first + d
        k0 = pl.multiple_of(j * tk, tk)
        k_blk = k_ref[0, pl.ds(k0, tk), :]
        v_blk = v_ref[0, pl.ds(k0, tk), :]
        k_pos = k0 + col
        q_pos = q0 + row
        mask = ((lax.shift_right_logical(k_pos, CHUNK_SHIFT) <= lax.shift_right_logical(q_pos, CHUNK_SHIFT))
                & (k_pos < nk_valid))
        bias = -slope * jnp.abs(q_pos - k_pos).astype(_F32)
        for c in range(2):
            s = jnp.where(mask, _dot_nt(qs[c], k_blk) + bias, -jnp.inf)
            update(c, s, v_blk)

    lam = (jnp.exp(jnp.sum(lq1_ref[...] * lk1_ref[...], axis=-1, keepdims=True))
           - jnp.exp(jnp.sum(lq2_ref[...] * lk2_ref[...], axis=-1, keepdims=True)) + lam_init)
    o = acc_ref[0] / l_ref[0] - lam * (acc_ref[1] / l_ref[1])
    o = _rms(o, g_ref[...]) * (1.0 - lam_init)
    o_ref[0] = o.astype(o_ref.dtype)


def _diff_call(q, k, v, lq1, lk1, lq2, lk2, subln_g, *, tq, tk, offset, nk_valid, lam_init):
    b, nq, _ = q.shape
    nk = k.shape[1]
    assert 2 * DIFF_HEAD_DIM == LANES and 8 % DIFF_HEADS == 0
    assert tk % CHUNK == 0 and CHUNK == 1 << CHUNK_SHIFT and offset + nq <= nk_valid <= nk
    q_spec, kv_spec = _attention_specs(nq, nk, tq, tk, offset)
    lam_spec = pl.BlockSpec((1, DIFF_HEAD_DIM), lambda bi, g, qi: (0, 0))
    return pl.pallas_call(
        functools.partial(_diff_kernel, tq=tq, tk=tk, offset=offset, nk_valid=nk_valid,
                          lam_init=lam_init),
        grid=(b, DIFF_HEADS, nq // tq),
        in_specs=[q_spec, kv_spec, kv_spec, lam_spec, lam_spec, lam_spec, lam_spec,
                  pl.BlockSpec((1, LANES), lambda bi, g, qi: (0, 0))],
        out_specs=q_spec,
        out_shape=jax.ShapeDtypeStruct((b, nq, DIFF_WIDTH), _BF16),
        scratch_shapes=[pltpu.VMEM((2, tq, 1), _F32), pltpu.VMEM((2, tq, 1), _F32),
                        pltpu.VMEM((2, tq, LANES), _F32)],
        compiler_params=pltpu.CompilerParams(
            dimension_semantics=("parallel", "parallel", "arbitrary"),
            vmem_limit_bytes=VMEM_LIMIT_BYTES),
        name="diff_attention",
    )(q, k, v, lq1, lk1, lq2, lk2, subln_g)


def _prompt_specs(nq, t):
    assert nq % t == 0
    q_spec = pl.BlockSpec((1, t, LANES), lambda bi, g, qi: (bi, qi, g))
    k_spec = pl.BlockSpec((1, nq, LANES), lambda bi, g, qi: (bi, 0, g))
    vt_spec = pl.BlockSpec((1, LANES, nq), lambda bi, g, qi: (bi, g, 0))
    return q_spec, k_spec, vt_spec


def _sb_prompt_kernel(q_ref, k_ref, vt_ref, o_ref, acc_ref, c_ref, *, t):
    qi = pl.program_id(2)
    lane = lax.broadcasted_iota(jnp.int32, (1, LANES), 1)
    key = lax.broadcasted_iota(jnp.int32, (t, t), 0)
    qry = lax.broadcasted_iota(jnp.int32, (t, t), 1)
    causal = key < qry
    later = (qry > key).astype(_BF16)
    q = q_ref[0].astype(_F32)

    def walk_block(head, qh, j, mask):
        k0 = pl.multiple_of(j * t, t)
        z = _dot_nt(k_ref[0, pl.ds(k0, t), :], qh)
        sp = jnp.log1p(jnp.exp(-jnp.abs(z)))
        log_beta = jnp.minimum(z, 0.0) - sp
        log_stay = -jnp.maximum(z, 0.0) - sp
        if mask is not None:
            log_stay = jnp.where(mask, log_stay, 0.0)
        hi = log_stay.astype(_BF16)
        lo = (log_stay - hi.astype(_F32)).astype(_BF16)
        sums = _dot(later, jnp.concatenate([hi, lo], axis=1))
        between = sums[:, :t] + sums[:, t:] + c_ref[...]
        w = jnp.exp(log_beta + between)
        if mask is not None:
            w = jnp.where(mask, w, 0.0)
        rows = slice(head * SB_HEAD_DIM, (head + 1) * SB_HEAD_DIM)
        acc_ref[rows, :] += _dot(vt_ref[0, rows, pl.ds(k0, t)], w.astype(_BF16))
        c_new = c_ref[...] + jnp.sum(log_stay, axis=0, keepdims=True)
        c_ref[...] = c_new
        return jnp.max(c_new)

    acc_ref[...] = jnp.zeros_like(acc_ref)
    for head in range(LANES // SB_HEAD_DIM):
        in_head = (lane >= head * SB_HEAD_DIM) & (lane < (head + 1) * SB_HEAD_DIM)
        qh = jnp.where(in_head, q, 0.0).astype(_BF16)
        c_ref[...] = jnp.zeros_like(c_ref)
        c_max = walk_block(head, qh, qi, causal)

        def cond(state):
            j, c_max = state
            return (j >= 0) & (c_max > -F32_EXP_ZERO)

        def body(state, head=head, qh=qh):
            j, _ = state
            return j - 1, walk_block(head, qh, j, None)

        lax.while_loop(cond, body, (qi - 1, c_max))
    o_ref[0] = acc_ref[...].T.astype(o_ref.dtype)


def _sb_prompt_call(q, k, vt, *, t):
    b, nq, _ = q.shape
    q_spec, k_spec, vt_spec = _prompt_specs(nq, t)
    return pl.pallas_call(
        functools.partial(_sb_prompt_kernel, t=t),
        grid=(b, SB_WIDTH // LANES, nq // t),
        in_specs=[q_spec, k_spec, vt_spec],
        out_specs=q_spec,
        out_shape=jax.ShapeDtypeStruct((b, nq, SB_WIDTH), _BF16),
        scratch_shapes=[pltpu.VMEM((LANES, t), _F32), pltpu.VMEM((1, t), _F32)],
        compiler_params=pltpu.CompilerParams(
            dimension_semantics=("parallel", "parallel", "arbitrary"),
            vmem_limit_bytes=VMEM_LIMIT_BYTES),
        name="sb_prompt",
    )(q, k, vt)


def _diff_prompt_kernel(q_ref, k_ref, vt_ref, lq1_ref, lk1_ref, lq2_ref, lk2_ref, g_ref, o_ref,
                        m_ref, l_ref, acc_ref, bias_ref, s0_ref, s1_ref, p0_ref, p1_ref, a0_ref, a1_ref,
                        *, t, lam_init):
    head = pl.program_id(1)
    qi = pl.program_id(2)
    lane = lax.broadcasted_iota(jnp.int32, (1, LANES), 1)
    key = lax.broadcasted_iota(jnp.int32, (t, 2 * t), 0)
    qry = lax.broadcasted_iota(jnp.int32, (t, 2 * t), 1) & (t - 1)
    exponent = (8 // DIFF_HEADS) * (head + 1)
    slope = lax.bitcast_convert_type(jnp.full((1, 1), (127 - exponent) << 23, jnp.int32), _F32)
    rel = (qry - key).astype(_F32)
    bias_ref[...] = -slope * rel
    q = q_ref[0].astype(_F32)
    q_maps = jnp.concatenate([jnp.where(lane < DIFF_HEAD_DIM, q, 0.0),
                              jnp.where(lane >= DIFF_HEAD_DIM, q, 0.0)], axis=0).astype(_BF16)
    m_ref[...] = jnp.full_like(m_ref, NEG_BIG)
    l_ref[...] = jnp.zeros_like(l_ref)
    acc_ref[...] = jnp.zeros_like(acc_ref)

    def tile_start(j):
        return pl.multiple_of(jnp.clip(j, 0, qi) * t, t)

    def scores(j):
        return _dot_nt(k_ref[0, pl.ds(tile_start(j), t), :], q_maps)

    def softmax(s, shift):
        m_old = m_ref[...]
        m_new = jnp.maximum(m_old, jnp.max(s, axis=0, keepdims=True) + shift)
        alpha = jnp.exp(m_old - m_new)
        p = jnp.exp(s - (m_new - shift))
        l_ref[...] = alpha * l_ref[...] + jnp.sum(p, axis=0, keepdims=True)
        m_ref[...] = m_new
        return alpha, p.astype(_BF16)

    def weigh(j, alpha, p):
        acc_ref[...] = alpha * acc_ref[...] + _dot(vt_ref[0, :, pl.ds(tile_start(j), t)], p)

    mask = lax.shift_right_logical(key, CHUNK_SHIFT) <= lax.shift_right_logical(qry, CHUNK_SHIFT)
    alpha, p = softmax(jnp.where(mask, scores(qi) - slope * jnp.abs(rel), -jnp.inf), 0.0)
    weigh(qi, alpha, p)

    def past_shift(j):
        return jnp.where(j < qi, -slope * ((qi - j) * t).astype(_F32), NEG_BIG)

    @pl.when(qi > 0)
    def _():
        s0_ref[...] = scores(0) + bias_ref[...]
        p1_ref[...] = jnp.zeros_like(p1_ref)
        a1_ref[...] = jnp.ones_like(a1_ref)

        def tile_pair(i, carry):
            j = 2 * i
            s1_ref[...] = scores(j + 1) + bias_ref[...]
            a0_ref[...], p0_ref[...] = softmax(s0_ref[...], past_shift(j))
            weigh(j - 1, a1_ref[...], p1_ref[...])
            s0_ref[...] = scores(j + 2) + bias_ref[...]
            a1_ref[...], p1_ref[...] = softmax(s1_ref[...], past_shift(j + 1))
            weigh(j, a0_ref[...], p0_ref[...])
            return carry

        pairs = (qi + 1) // 2
        lax.fori_loop(0, pairs, tile_pair, 0)
        weigh(2 * pairs - 1, a1_ref[...], p1_ref[...])

    lam = (jnp.exp(jnp.sum(lq1_ref[...] * lk1_ref[...], axis=-1, keepdims=True))
           - jnp.exp(jnp.sum(lq2_ref[...] * lk2_ref[...], axis=-1, keepdims=True)) + lam_init)
    o = acc_ref[...] / l_ref[...]
    o = o[:, :t] - lam * o[:, t:]
    o = o * lax.rsqrt(jnp.mean(o * o, axis=0, keepdims=True) + RMS_EPS)
    o_ref[0] = (o.T * g_ref[...] * (1.0 - lam_init)).astype(o_ref.dtype)


def _diff_prompt_call(q, k, vt, lq1, lk1, lq2, lk2, subln_g, *, t, lam_init):
    b, nq, _ = q.shape
    assert 2 * DIFF_HEAD_DIM == LANES and 8 % DIFF_HEADS == 0
    assert t % CHUNK == 0 and CHUNK == 1 << CHUNK_SHIFT and t & (t - 1) == 0
    q_spec, k_spec, vt_spec = _prompt_specs(nq, t)
    lam_spec = pl.BlockSpec((1, DIFF_HEAD_DIM), lambda bi, g, qi: (0, 0))
    return pl.pallas_call(
        functools.partial(_diff_prompt_kernel, t=t, lam_init=lam_init),
        grid=(b, DIFF_HEADS, nq // t),
        in_specs=[q_spec, k_spec, vt_spec, lam_spec, lam_spec, lam_spec, lam_spec,
                  pl.BlockSpec((1, LANES), lambda bi, g, qi: (0, 0))],
        out_specs=q_spec,
        out_shape=jax.ShapeDtypeStruct((b, nq, DIFF_WIDTH), _BF16),
        scratch_shapes=[pltpu.VMEM((1, 2 * t), _F32), pltpu.VMEM((1, 2 * t), _F32),
                        pltpu.VMEM((LANES, 2 * t), _F32), pltpu.VMEM((t, 2 * t), _F32),
                        pltpu.VMEM((t, 2 * t), _F32), pltpu.VMEM((t, 2 * t), _F32),
                        pltpu.VMEM((t, 2 * t), _BF16), pltpu.VMEM((t, 2 * t), _BF16),
                        pltpu.VMEM((1, 2 * t), _F32), pltpu.VMEM((1, 2 * t), _F32)],
        compiler_params=pltpu.CompilerParams(
            dimension_semantics=("parallel", "parallel", "arbitrary"),
            vmem_limit_bytes=VMEM_LIMIT_BYTES),
        name="diff_prompt",
    )(q, k, vt, lq1, lk1, lq2, lk2, subln_g)


def _mix_out_kernel(sbo_ref, do_ref, h_ref, w_ref, g_ref, o_ref):
    mix = _dot(sbo_ref[...], w_ref[:SB_WIDTH, :]) + _dot(do_ref[...], w_ref[SB_WIDTH:, :])
    o_ref[...] = h_ref[...] + _rms(mix, g_ref[...])


def _mix_out(sb_o, d_o, h, w, g):
    m = h.shape[0]
    tm = min(512, m)
    assert m % tm == 0
    half = pl.BlockSpec((tm, SB_WIDTH), lambda i: (i, 0))
    row = pl.BlockSpec((tm, D_MODEL), lambda i: (i, 0))
    return pl.pallas_call(
        _mix_out_kernel,
        grid=(m // tm,),
        in_specs=[half, half, row, pl.BlockSpec((MIX_WIDTH, D_MODEL), lambda i: (0, 0)),
                  pl.BlockSpec((1, D_MODEL), lambda i: (0, 0))],
        out_specs=row,
        out_shape=jax.ShapeDtypeStruct((m, D_MODEL), _F32),
        compiler_params=pltpu.CompilerParams(
            dimension_semantics=("parallel",), vmem_limit_bytes=VMEM_LIMIT_BYTES),
        name="mix_out",
    )(sb_o, d_o, h, w, g)


def _encoder_layer(x, past, w, lam_init, *, tq, tk):
    b, n, _ = x.shape
    m = b * n
    h = _ffn(x.reshape(m, D_MODEL), w["ff1_pre_g"], w["ff1_w_gate"], w["ff1_w_up"], w["ff1_w_down"],
             w["ff1_post_g"])
    lam_args = (w["lam_q1"], w["lam_k1"], w["lam_q2"], w["lam_k2"], w["subln_g"])

    def seq(a):
        return a.reshape(b, n, a.shape[-1])

    if past is None:
        sbq, sbk, sbv, dq, dk, dv, sbk16, sbvt16, dk16, dvt16 = _qkv(h, w["mix_pre_g"], w["w_in"], seq_len=n)
        sb_o = _sb_prompt_call(seq(sbq), seq(sbk16), sbvt16, t=tq)
        d_o = _diff_prompt_call(seq(dq), seq(dk16), dvt16, *lam_args, t=tq, lam_init=lam_init)
    else:
        sbq, sbk, sbv, dq, dk, dv, sbk16, sbv16, dk16, dv16 = _qkv(h, w["mix_pre_g"], w["w_in"])
        offset = past[0].shape[1]
        pad = -(offset + n) % tk
        keys = [jnp.concatenate([c.astype(_BF16), seq(new), jnp.zeros((b, pad, new.shape[-1]), _BF16)], axis=1)
                for c, new in zip(past, (sbk16, sbv16, dk16, dv16))]
        sb_o = _sb_call(seq(sbq), keys[0], keys[1], tq=tq, tk=tk, offset=offset)
        d_o = _diff_call(seq(dq), keys[2], keys[3], *lam_args, tq=tq, tk=tk, offset=offset,
                         nk_valid=offset + n, lam_init=lam_init)
    h = _mix_out(sb_o.reshape(m, SB_WIDTH), d_o.reshape(m, DIFF_WIDTH), h, w["w_out"], w["mix_post_g"])
    y = _ffn(h, w["ff2_pre_g"], w["ff2_w_gate"], w["ff2_w_up"], w["ff2_w_down"], w["ff2_post_g"],
             w["final_g"])
    rows = (sbk.reshape(b, n, SB_HEADS, SB_HEAD_DIM), sbv.reshape(b, n, SB_HEADS, SB_HEAD_DIM),
            dk.reshape(b, n, DIFF_HEADS, 2 * DIFF_HEAD_DIM), dv.reshape(b, n, DIFF_HEADS, 2 * DIFF_HEAD_DIM))
    return y.reshape(b, n, D_MODEL), rows


_MATRICES = ("ff1_w_gate", "ff1_w_up", "ff1_w_down", "w_in", "w_out", "ff2_w_gate", "ff2_w_up", "ff2_w_down")


def kernel(x_prompt, x_sample, cache_sb_k, cache_sb_v, cache_diff_k, cache_diff_v, ff1_pre_g, ff1_w_gate, ff1_w_up, ff1_w_down, ff1_post_g, mix_pre_g, w_in, lam_q1, lam_k1, lam_q2, lam_k2, subln_g, w_out, mix_post_g, ff2_pre_g, ff2_w_gate, ff2_w_up, ff2_w_down, ff2_post_g, final_g):
    params = dict(ff1_pre_g=ff1_pre_g, ff1_w_gate=ff1_w_gate, ff1_w_up=ff1_w_up, ff1_w_down=ff1_w_down,
                  ff1_post_g=ff1_post_g, mix_pre_g=mix_pre_g, w_in=w_in, lam_q1=lam_q1, lam_k1=lam_k1,
                  lam_q2=lam_q2, lam_k2=lam_k2, subln_g=subln_g, w_out=w_out, mix_post_g=mix_post_g,
                  ff2_pre_g=ff2_pre_g, ff2_w_gate=ff2_w_gate, ff2_w_up=ff2_w_up, ff2_w_down=ff2_w_down,
                  ff2_post_g=ff2_post_g, final_g=final_g)
    yp, ys = x_prompt, x_sample
    rows_p, rows_s = [], []
    for l in range(DEPTH):
        lam_init = 0.8 - 0.6 * math.exp(-0.3 * l)
        w = {name: (p[l].astype(_BF16) if name in _MATRICES else p[l][None, :].astype(_F32))
             for name, p in params.items()}
        yp, rp = _encoder_layer(yp, None, w, lam_init, tq=256, tk=256)
        past = tuple(c[l].reshape(c.shape[1], c.shape[2], -1)
                     for c in (cache_sb_k, cache_sb_v, cache_diff_k, cache_diff_v))
        ys, rs = _encoder_layer(ys, past, w, lam_init, tq=x_sample.shape[1], tk=128)
        rows_p.append(rp)
        rows_s.append(rs)
    stacked_p = [jnp.stack(r, axis=0) for r in zip(*rows_p)]
    stacked_s = [jnp.stack(r, axis=0) for r in zip(*rows_s)]
    return (yp, ys, *stacked_p, *stacked_s)
```

```python
import functools
import math

import jax
import jax.numpy as jnp
from jax import lax
from jax.experimental import pallas as pl
from jax.experimental.pallas import tpu as pltpu

D_MODEL = 1024
DEPTH = 1
CHUNK = 64
CHUNK_SHIFT = 6
SB_HEADS = 8
SB_HEAD_DIM = 64
DIFF_HEADS = 4
DIFF_HEAD_DIM = 64
SB_WIDTH = SB_HEADS * SB_HEAD_DIM
DIFF_WIDTH = DIFF_HEADS * 2 * DIFF_HEAD_DIM
MIX_WIDTH = SB_WIDTH + DIFF_WIDTH
IN_WIDTH = 3 * SB_WIDTH + 3 * DIFF_WIDTH
D_FF = 2816
RMS_EPS = 1e-6

LANES = 128
VMEM_LIMIT_BYTES = 56 * 1024 * 1024
F32_EXP_ZERO = 104.0
NEG_BIG = -1e30
LOG2E = math.log2(math.e)

_F32 = jnp.float32
_BF16 = jnp.bfloat16


def _rms(x, g):
    return x * lax.rsqrt(jnp.mean(x * x, axis=-1, keepdims=True) + RMS_EPS) * g


def _softplus_tail(z):
    return jnp.log(1.0 + jnp.exp(-jnp.abs(z)))


def _dot(a, b):
    return jnp.dot(a, b, preferred_element_type=_F32)


def _dot_nt(a, b):
    return lax.dot_general(a, b, (((1,), (1,)), ((), ())), preferred_element_type=_F32)


def _ffn_kernel(*refs, final_norm):
    if final_norm:
        x_ref, pre_ref, wg_ref, wu_ref, wd_ref, post_ref, fin_ref, o_ref, xn_ref, acc_ref = refs
    else:
        x_ref, pre_ref, wg_ref, wu_ref, wd_ref, post_ref, o_ref, xn_ref, acc_ref = refs
    j = pl.program_id(1)

    @pl.when(j == 0)
    def _():
        xn_ref[...] = _rms(x_ref[...], pre_ref[...]).astype(_BF16)
        acc_ref[...] = jnp.zeros_like(acc_ref)

    xn = xn_ref[...]
    g = _dot(xn, wg_ref[...])
    u = _dot(xn, wu_ref[...])
    a = (g * jax.nn.sigmoid(g) * u).astype(_BF16)
    acc_ref[...] += _dot(a, wd_ref[...])

    @pl.when(j == pl.num_programs(1) - 1)
    def _():
        h = x_ref[...] + 0.5 * _rms(acc_ref[...], post_ref[...])
        if final_norm:
            h = _rms(h, fin_ref[...])
        o_ref[...] = h


def _ffn(x, pre_g, wg, wu, wd, post_g, final_g=None):
    m = x.shape[0]
    tm = min(512, m)
    tf = D_FF // 2
    assert m % tm == 0 and D_FF % tf == 0 and tf % LANES == 0
    row = pl.BlockSpec((tm, D_MODEL), lambda i, j: (i, 0))
    vec = pl.BlockSpec((1, D_MODEL), lambda i, j: (0, 0))
    w_in = pl.BlockSpec((D_MODEL, tf), lambda i, j: (0, j))
    w_out = pl.BlockSpec((tf, D_MODEL), lambda i, j: (j, 0))
    in_specs = [row, vec, w_in, w_in, w_out, vec]
    args = [x, pre_g, wg, wu, wd, post_g]
    if final_g is not None:
        in_specs.append(vec)
        args.append(final_g)
    return pl.pallas_call(
        functools.partial(_ffn_kernel, final_norm=final_g is not None),
        grid=(m // tm, D_FF // tf),
        in_specs=in_specs,
        out_specs=row,
        out_shape=jax.ShapeDtypeStruct((m, D_MODEL), _F32),
        scratch_shapes=[pltpu.VMEM((tm, D_MODEL), _BF16), pltpu.VMEM((tm, D_MODEL), _F32)],
        compiler_params=pltpu.CompilerParams(
            dimension_semantics=("parallel", "arbitrary"), vmem_limit_bytes=VMEM_LIMIT_BYTES),
        name="ffn_final" if final_g is not None else "ffn",
    )(*args)


def _qkv_kernel(h_ref, g_ref, w_ref, sbq_ref, sbk_ref, sbv_ref, dq_ref, dk_ref, dv_ref,
                sbk16_ref, sbv16_ref, dk16_ref, dv16_ref, *, v_transposed):
    hn = _rms(h_ref[...], g_ref[...]).astype(_BF16)

    def proj(idx):
        return _dot(hn, w_ref[:, idx * SB_WIDTH:(idx + 1) * SB_WIDTH])

    sbq_ref[...] = (proj(0) * (SB_HEAD_DIM ** -0.5)).astype(_BF16)
    dq_ref[...] = (proj(3) * (DIFF_HEAD_DIM ** -0.5 * LOG2E)).astype(_BF16)
    for idx, full_ref, half_ref in ((1, sbk_ref, sbk16_ref), (4, dk_ref, dk16_ref)):
        p = proj(idx)
        full_ref[...] = p
        half_ref[...] = p.astype(_BF16)
    for idx, full_ref, half_ref in ((2, sbv_ref, sbv16_ref), (5, dv_ref, dv16_ref)):
        p = proj(idx)
        full_ref[...] = p
        if v_transposed:
            half_ref[0] = p.T.astype(_BF16)
        else:
            half_ref[...] = p.astype(_BF16)


def _qkv(h, g, w, *, seq_len=None):
    assert SB_WIDTH == DIFF_WIDTH
    assert math.log2(SB_HEAD_DIM) % 2 == 0 and math.log2(DIFF_HEAD_DIM) % 2 == 0
    m = h.shape[0]
    tm = min(512, m)
    assert m % tm == 0
    row = pl.BlockSpec((tm, D_MODEL), lambda i: (i, 0))
    out = pl.BlockSpec((tm, SB_WIDTH), lambda i: (i, 0))
    f32 = jax.ShapeDtypeStruct((m, SB_WIDTH), _F32)
    b16 = jax.ShapeDtypeStruct((m, SB_WIDTH), _BF16)
    v_out, v16 = out, b16
    if seq_len is not None:
        assert seq_len % tm == 0
        tiles = seq_len // tm
        v_out = pl.BlockSpec((1, SB_WIDTH, tm), lambda i: (i // tiles, 0, i % tiles))
        v16 = jax.ShapeDtypeStruct((m // seq_len, SB_WIDTH, seq_len), _BF16)
    return pl.pallas_call(
        functools.partial(_qkv_kernel, v_transposed=seq_len is not None),
        grid=(m // tm,),
        in_specs=[row, pl.BlockSpec((1, D_MODEL), lambda i: (0, 0)),
                  pl.BlockSpec((D_MODEL, IN_WIDTH), lambda i: (0, 0))],
        out_specs=[out] * 7 + [v_out, out, v_out],
        out_shape=[b16, f32, f32, b16, f32, f32, b16, v16, b16, v16],
        compiler_params=pltpu.CompilerParams(
            dimension_semantics=("parallel",), vmem_limit_bytes=VMEM_LIMIT_BYTES),
        name="qkv",
    )(h, g, w)


def _sb_kernel(q_ref, k_ref, v_ref, o_ref, acc_ref, c_ref, *, tq, tk, offset):
    qi = pl.program_id(2)
    q0 = offset + qi * tq
    lane = lax.broadcasted_iota(jnp.int32, (1, LANES), 1)
    row = lax.broadcasted_iota(jnp.int32, (tq, tk), 0)
    col = lax.broadcasted_iota(jnp.int32, (tq, tk), 1)
    later = (lax.broadcasted_iota(jnp.int32, (tk, tk), 0)
             > lax.broadcasted_iota(jnp.int32, (tk, tk), 1)).astype(_BF16)
    q = q_ref[0].astype(_F32)

    def walk_block(qh, j, mask):
        k0 = pl.multiple_of(j * tk, tk)
        z = _dot_nt(qh, k_ref[0, pl.ds(k0, tk), :])
        t = _softplus_tail(z)
        log_beta = jnp.minimum(z, 0.0) - t
        log_stay = -jnp.maximum(z, 0.0) - t
        if mask is not None:
            log_stay = jnp.where(mask, log_stay, 0.0)
        hi = log_stay.astype(_BF16)
        lo = (log_stay - hi.astype(_F32)).astype(_BF16)
        sums = _dot(jnp.concatenate([hi, lo], axis=0), later)
        between = sums[:tq] + sums[tq:] + c_ref[...]
        w = jnp.exp(log_beta + between)
        if mask is not None:
            w = jnp.where(mask, w, 0.0)
        acc_ref[...] += _dot(w.astype(_BF16), v_ref[0, pl.ds(k0, tk), :])
        c_new = c_ref[...] + jnp.sum(log_stay, axis=-1, keepdims=True)
        c_ref[...] = c_new
        return jnp.max(c_new)

    heads = []
    for head in range(LANES // SB_HEAD_DIM):
        in_head = (lane >= head * SB_HEAD_DIM) & (lane < (head + 1) * SB_HEAD_DIM)
        qh = jnp.where(in_head, q, 0.0).astype(_BF16)
        acc_ref[...] = jnp.zeros_like(acc_ref)
        c_ref[...] = jnp.zeros_like(c_ref)
        j_top = (q0 + tq - 1) // tk
        c_max = jnp.float32(0.0)
        for d in range(_straddle(tq, tk)):
            j = j_top - d
            c_max = walk_block(qh, j, (j * tk + col) < (q0 + row))

        def cond(state):
            j, c_max = state
            return (j >= 0) & (c_max > -F32_EXP_ZERO)

        def body(state):
            j, _ = state
            return j - 1, walk_block(qh, j, None)

        lax.while_loop(cond, body, (j_top - _straddle(tq, tk), c_max))
        heads.append(acc_ref[...])

    o = heads[0]
    for head in range(1, len(heads)):
        o = jnp.where(lane >= head * SB_HEAD_DIM, heads[head], o)
    o_ref[0] = o.astype(o_ref.dtype)


def _attention_specs(nq, nk, tq, tk, offset):
    assert nq % tq == 0 and nk % tk == 0 and offset % tk == 0 and offset + nq <= nk
    assert tq % tk == 0 or tk % tq == 0
    q_spec = pl.BlockSpec((1, tq, LANES), lambda bi, g, qi: (bi, qi, g))
    kv_spec = pl.BlockSpec((1, nk, LANES), lambda bi, g, qi: (bi, 0, g))
    return q_spec, kv_spec


def _straddle(tq, tk):
    return max(1, tq // tk)


def _sb_call(q, k, v, *, tq, tk, offset):
    b, nq, _ = q.shape
    nk = k.shape[1]
    q_spec, kv_spec = _attention_specs(nq, nk, tq, tk, offset)
    return pl.pallas_call(
        functools.partial(_sb_kernel, tq=tq, tk=tk, offset=offset),
        grid=(b, SB_WIDTH // LANES, nq // tq),
        in_specs=[q_spec, kv_spec, kv_spec],
        out_specs=q_spec,
        out_shape=jax.ShapeDtypeStruct((b, nq, SB_WIDTH), _BF16),
        scratch_shapes=[pltpu.VMEM((tq, LANES), _F32), pltpu.VMEM((tq, 1), _F32)],
        compiler_params=pltpu.CompilerParams(
            dimension_semantics=("parallel", "parallel", "arbitrary"),
            vmem_limit_bytes=VMEM_LIMIT_BYTES),
        name="sb_attention",
    )(q, k, v)


def _alibi_slope_log2(head):
    assert 8 % DIFF_HEADS == 0
    exponent = (8 // DIFF_HEADS) * (head + 1)
    slope = lax.bitcast_convert_type(jnp.full((1, 1), (127 - exponent) << 23, jnp.int32), _F32)
    return slope * LOG2E


def _diff_kernel(q_ref, k_ref, v_ref, lq1_ref, lk1_ref, lq2_ref, lk2_ref, g_ref, o_ref,
                 m_ref, l_ref, acc_ref, *, tq, tk, offset, nk_valid, lam_init):
    head = pl.program_id(1)
    qi = pl.program_id(2)
    q0 = offset + qi * tq
    lane = lax.broadcasted_iota(jnp.int32, (1, LANES), 1)
    row = lax.broadcasted_iota(jnp.int32, (tq, tk), 0)
    col = lax.broadcasted_iota(jnp.int32, (tq, tk), 1)
    slope = _alibi_slope_log2(head)
    rel = (row - col).astype(_F32)
    q = q_ref[0].astype(_F32)
    qs = [jnp.where((lane >= c * DIFF_HEAD_DIM) & (lane < (c + 1) * DIFF_HEAD_DIM), q, 0.0).astype(_BF16)
          for c in range(2)]
    m_ref[...] = jnp.full_like(m_ref, NEG_BIG)
    l_ref[...] = jnp.zeros_like(l_ref)
    acc_ref[...] = jnp.zeros_like(acc_ref)

    def update(c, s, v_blk):
        m_old = m_ref[c]
        m_new = jnp.maximum(m_old, jnp.max(s, axis=-1, keepdims=True))
        alpha = jnp.exp2(m_old - m_new)
        p = jnp.exp2(s - m_new)
        l_ref[c] = alpha * l_ref[c] + jnp.sum(p, axis=-1, keepdims=True)
        acc_ref[c] = alpha * acc_ref[c] + _dot(p.astype(_BF16), v_blk)
        m_ref[c] = m_new

    def past_block(j, carry):
        k0 = pl.multiple_of(j * tk, tk)
        k_blk = k_ref[0, pl.ds(k0, tk), :]
        v_blk = v_ref[0, pl.ds(k0, tk), :]
        bias = -slope * (rel + (q0 - k0).astype(_F32))
        for c in range(2):
            update(c, _dot_nt(qs[c], k_blk) + bias, v_blk)
        return carry

    j_first = q0 // tk
    lax.fori_loop(0, j_first, past_block, 0)

    for d in range(_straddle(tq, tk)):
        j = j_first + d
        k0 = pl.multiple_of(j * tk, tk)
        k_blk = k_ref[0, pl.ds(k0, tk), :]
        v_blk = v_ref[0, pl.ds(k0, tk), :]
        k_pos = k0 + col
        q_pos = q0 + row
        mask = ((lax.shift_right_logical(k_pos, CHUNK_SHIFT) <= lax.shift_right_logical(q_pos, CHUNK_SHIFT))
                & (k_pos < nk_valid))
        bias = -slope * jnp.abs(q_pos - k_pos).astype(_F32)
        for c in range(2):
            s = jnp.where(mask, _dot_nt(qs[c], k_blk) + bias, -jnp.inf)
            update(c, s, v_blk)

    lam = (jnp.exp(jnp.sum(lq1_ref[...] * lk1_ref[...], axis=-1, keepdims=True))
           - jnp.exp(jnp.sum(lq2_ref[...] * lk2_ref[...], axis=-1, keepdims=True)) + lam_init)
    o = acc_ref[0] / l_ref[0] - lam * (acc_ref[1] / l_ref[1])
    o = _rms(o, g_ref[...]) * (1.0 - lam_init)
    o_ref[0] = o.astype(o_ref.dtype)


def _diff_call(q, k, v, lq1, lk1, lq2, lk2, subln_g, *, tq, tk, offset, nk_valid, lam_init):
    b, nq, _ = q.shape
    nk = k.shape[1]
    assert 2 * DIFF_HEAD_DIM == LANES and 8 % DIFF_HEADS == 0
    assert tk % CHUNK == 0 and CHUNK == 1 << CHUNK_SHIFT and offset + nq <= nk_valid <= nk
    q_spec, kv_spec = _attention_specs(nq, nk, tq, tk, offset)
    lam_spec = pl.BlockSpec((1, DIFF_HEAD_DIM), lambda bi, g, qi: (0, 0))
    return pl.pallas_call(
        functools.partial(_diff_kernel, tq=tq, tk=tk, offset=offset, nk_valid=nk_valid,
                          lam_init=lam_init),
        grid=(b, DIFF_HEADS, nq // tq),
        in_specs=[q_spec, kv_spec, kv_spec, lam_spec, lam_spec, lam_spec, lam_spec,
                  pl.BlockSpec((1, LANES), lambda bi, g, qi: (0, 0))],
        out_specs=q_spec,
        out_shape=jax.ShapeDtypeStruct((b, nq, DIFF_WIDTH), _BF16),
        scratch_shapes=[pltpu.VMEM((2, tq, 1), _F32), pltpu.VMEM((2, tq, 1), _F32),
                        pltpu.VMEM((2, tq, LANES), _F32)],
        compiler_params=pltpu.CompilerParams(
            dimension_semantics=("parallel", "parallel", "arbitrary"),
            vmem_limit_bytes=VMEM_LIMIT_BYTES),
        name="diff_attention",
    )(q, k, v, lq1, lk1, lq2, lk2, subln_g)


def _prompt_specs(nq, t):
    assert nq % t == 0 and t & (t - 1) == 0
    q_spec = pl.BlockSpec((1, t, LANES), lambda bi, g, qi: (bi, qi, g))
    k_spec = pl.BlockSpec((1, nq, LANES), lambda bi, g, qi: (bi, 0, g))
    vt_spec = pl.BlockSpec((1, LANES, nq), lambda bi, g, qi: (bi, g, 0))
    return q_spec, k_spec, vt_spec


def _sb_prompt_kernel(q_ref, k_ref, vt_ref, o_ref, acc_ref, c_ref, *, t):
    assert LANES == 2 * SB_HEAD_DIM
    qi = pl.program_id(2)
    lane = lax.broadcasted_iota(jnp.int32, (1, LANES), 1)
    key = lax.broadcasted_iota(jnp.int32, (t, 2 * t), 0)
    qry = lax.broadcasted_iota(jnp.int32, (t, 2 * t), 1) & (t - 1)
    causal = key < qry
    later = (lax.broadcasted_iota(jnp.int32, (t, t), 1)
             > lax.broadcasted_iota(jnp.int32, (t, t), 0)).astype(_BF16)
    q = q_ref[0].astype(_F32)
    q_heads = jnp.concatenate([jnp.where(lane < SB_HEAD_DIM, q, 0.0),
                               jnp.where(lane >= SB_HEAD_DIM, q, 0.0)], axis=0).astype(_BF16)

    def walk_tile(j, c, mask):
        k0 = pl.multiple_of(j * t, t)
        z = _dot_nt(k_ref[0, pl.ds(k0, t), :], q_heads)
        sp = _softplus_tail(z)
        log_beta = jnp.minimum(z, 0.0) - sp
        log_stay = -jnp.maximum(z, 0.0) - sp
        if mask is not None:
            log_stay = jnp.where(mask, log_stay, 0.0)
        hi = log_stay.astype(_BF16)
        lo = (log_stay - hi.astype(_F32)).astype(_BF16)
        sums = _dot(later, jnp.concatenate([hi, lo], axis=1))
        w = jnp.exp(log_beta + (sums[:, :2 * t] + sums[:, 2 * t:] + c))
        if mask is not None:
            w = jnp.where(mask, w, 0.0)
        w = w.astype(_BF16)
        vt = vt_ref[0, :, pl.ds(k0, t)]
        o = jnp.concatenate([_dot(vt[:SB_HEAD_DIM], w[:, :t]), _dot(vt[SB_HEAD_DIM:], w[:, t:])], axis=0)
        return c + jnp.sum(log_stay, axis=0, keepdims=True), o

    c0 = jnp.zeros((1, 2 * t), _F32)

    @pl.when(qi == 0)
    def _():
        c_ref[...], acc_ref[...] = walk_tile(0, c0, causal)

    @pl.when(qi > 0)
    def _():
        c1, o_own = walk_tile(qi, c0, causal)
        c2, o_prev = walk_tile(qi - 1, c1, None)
        c_ref[...] = c2
        acc_ref[...] = o_own + o_prev

        def cond(state):
            j, c_max = state
            return (j >= 0) & (c_max > -F32_EXP_ZERO)

        def body(state):
            j, _ = state
            c, o = walk_tile(j, c_ref[...], None)
            c_ref[...] = c
            acc_ref[...] += o
            return j - 1, jnp.max(c)

        lax.while_loop(cond, body, (qi - 2, jnp.max(c2)))

    o_ref[0] = acc_ref[...].T.astype(o_ref.dtype)


def _sb_prompt_call(q, k, vt, *, t):
    b, nq, _ = q.shape
    q_spec, k_spec, vt_spec = _prompt_specs(nq, t)
    return pl.pallas_call(
        functools.partial(_sb_prompt_kernel, t=t),
        grid=(b, SB_WIDTH // LANES, nq // t),
        in_specs=[q_spec, k_spec, vt_spec],
        out_specs=q_spec,
        out_shape=jax.ShapeDtypeStruct((b, nq, SB_WIDTH), _BF16),
        scratch_shapes=[pltpu.VMEM((LANES, t), _F32), pltpu.VMEM((1, 2 * t), _F32)],
        compiler_params=pltpu.CompilerParams(
            dimension_semantics=("parallel", "parallel", "arbitrary"),
            vmem_limit_bytes=VMEM_LIMIT_BYTES),
        name="sb_prompt",
    )(q, k, vt)


def _diff_prompt_kernel(q_ref, k_ref, vt_ref, lq1_ref, lk1_ref, lq2_ref, lk2_ref, g_ref, o_ref,
                        m_ref, l_ref, acc_ref, bias_ref, s0_ref, s1_ref, p0_ref, p1_ref, a0_ref, a1_ref,
                        *, t, lam_init):
    head = pl.program_id(1)
    qi = pl.program_id(2)
    lane = lax.broadcasted_iota(jnp.int32, (1, LANES), 1)
    key = lax.broadcasted_iota(jnp.int32, (t, 2 * t), 0)
    qry = lax.broadcasted_iota(jnp.int32, (t, 2 * t), 1) & (t - 1)
    slope = _alibi_slope_log2(head)
    rel = (qry - key).astype(_F32)
    bias_ref[...] = -slope * rel
    q = q_ref[0].astype(_F32)
    q_maps = jnp.concatenate([jnp.where(lane < DIFF_HEAD_DIM, q, 0.0),
                              jnp.where(lane >= DIFF_HEAD_DIM, q, 0.0)], axis=0).astype(_BF16)
    m_ref[...] = jnp.full_like(m_ref, NEG_BIG)
    l_ref[...] = jnp.zeros_like(l_ref)
    acc_ref[...] = jnp.zeros_like(acc_ref)

    def tile_start(j):
        return pl.multiple_of(jnp.clip(j, 0, qi) * t, t)

    def scores(j):
        return _dot_nt(k_ref[0, pl.ds(tile_start(j), t), :], q_maps)

    def softmax(s, shift):
        m_old = m_ref[...]
        m_new = jnp.maximum(m_old, jnp.max(s, axis=0, keepdims=True) + shift)
        alpha = jnp.exp2(m_old - m_new)
        p = jnp.exp2(s - (m_new - shift))
        l_ref[...] = alpha * l_ref[...] + jnp.sum(p, axis=0, keepdims=True)
        m_ref[...] = m_new
        return alpha, p.astype(_BF16)

    def weigh(j, alpha, p):
        acc_ref[...] = alpha * acc_ref[...] + _dot(vt_ref[0, :, pl.ds(tile_start(j), t)], p)

    mask = lax.shift_right_logical(key, CHUNK_SHIFT) <= lax.shift_right_logical(qry, CHUNK_SHIFT)
    alpha, p = softmax(jnp.where(mask, scores(qi) - slope * jnp.abs(rel), -jnp.inf), 0.0)
    weigh(qi, alpha, p)

    def past_shift(j):
        return jnp.where(j < qi, -slope * ((qi - j) * t).astype(_F32), NEG_BIG)

    @pl.when(qi > 0)
    def _():
        s0_ref[...] = scores(0) + bias_ref[...]
        p1_ref[...] = jnp.zeros_like(p1_ref)
        a1_ref[...] = jnp.ones_like(a1_ref)

        def tile_pair(i, carry):
            j = 2 * i
            s1_ref[...] = scores(j + 1) + bias_ref[...]
            a0_ref[...], p0_ref[...] = softmax(s0_ref[...], past_shift(j))
            weigh(j - 1, a1_ref[...], p1_ref[...])
            s0_ref[...] = scores(j + 2) + bias_ref[...]
            a1_ref[...], p1_ref[...] = softmax(s1_ref[...], past_shift(j + 1))
            weigh(j, a0_ref[...], p0_ref[...])
            return carry

        pairs = (qi + 1) // 2
        lax.fori_loop(0, pairs, tile_pair, 0)
        weigh(2 * pairs - 1, a1_ref[...], p1_ref[...])

    lam = (jnp.exp(jnp.sum(lq1_ref[...] * lk1_ref[...], axis=-1, keepdims=True))
           - jnp.exp(jnp.sum(lq2_ref[...] * lk2_ref[...], axis=-1, keepdims=True)) + lam_init)
    o = acc_ref[...] / l_ref[...]
    o = o[:, :t] - lam * o[:, t:]
    o = o * lax.rsqrt(jnp.mean(o * o, axis=0, keepdims=True) + RMS_EPS)
    o_ref[0] = (o.T * g_ref[...] * (1.0 - lam_init)).astype(o_ref.dtype)


def _diff_prompt_call(q, k, vt, lq1, lk1, lq2, lk2, subln_g, *, t, lam_init):
    b, nq, _ = q.shape
    assert 2 * DIFF_HEAD_DIM == LANES and 8 % DIFF_HEADS == 0
    assert t % CHUNK == 0 and CHUNK == 1 << CHUNK_SHIFT and t & (t - 1) == 0
    q_spec, k_spec, vt_spec = _prompt_specs(nq, t)
    lam_spec = pl.BlockSpec((1, DIFF_HEAD_DIM), lambda bi, g, qi: (0, 0))
    return pl.pallas_call(
        functools.partial(_diff_prompt_kernel, t=t, lam_init=lam_init),
        grid=(b, DIFF_HEADS, nq // t),
        in_specs=[q_spec, k_spec, vt_spec, lam_spec, lam_spec, lam_spec, lam_spec,
                  pl.BlockSpec((1, LANES), lambda bi, g, qi: (0, 0))],
        out_specs=q_spec,
        out_shape=jax.ShapeDtypeStruct((b, nq, DIFF_WIDTH), _BF16),
        scratch_shapes=[pltpu.VMEM((1, 2 * t), _F32), pltpu.VMEM((1, 2 * t), _F32),
                        pltpu.VMEM((LANES, 2 * t), _F32), pltpu.VMEM((t, 2 * t), _F32),
                        pltpu.VMEM((t, 2 * t), _F32), pltpu.VMEM((t, 2 * t), _F32),
                        pltpu.VMEM((t, 2 * t), _BF16), pltpu.VMEM((t, 2 * t), _BF16),
                        pltpu.VMEM((1, 2 * t), _F32), pltpu.VMEM((1, 2 * t), _F32)],
        compiler_params=pltpu.CompilerParams(
            dimension_semantics=("parallel", "parallel", "arbitrary"),
            vmem_limit_bytes=VMEM_LIMIT_BYTES),
        name="diff_prompt",
    )(q, k, vt, lq1, lk1, lq2, lk2, subln_g)


def _mix_out_kernel(sbo_ref, do_ref, h_ref, w_ref, g_ref, o_ref):
    mix = _dot(sbo_ref[...], w_ref[:SB_WIDTH, :]) + _dot(do_ref[...], w_ref[SB_WIDTH:, :])
    o_ref[...] = h_ref[...] + _rms(mix, g_ref[...])


def _mix_out(sb_o, d_o, h, w, g):
    m = h.shape[0]
    tm = min(512, m)
    assert m % tm == 0
    half = pl.BlockSpec((tm, SB_WIDTH), lambda i: (i, 0))
    row = pl.BlockSpec((tm, D_MODEL), lambda i: (i, 0))
    return pl.pallas_call(
        _mix_out_kernel,
        grid=(m // tm,),
        in_specs=[half, half, row, pl.BlockSpec((MIX_WIDTH, D_MODEL), lambda i: (0, 0)),
                  pl.BlockSpec((1, D_MODEL), lambda i: (0, 0))],
        out_specs=row,
        out_shape=jax.ShapeDtypeStruct((m, D_MODEL), _F32),
        compiler_params=pltpu.CompilerParams(
            dimension_semantics=("parallel",), vmem_limit_bytes=VMEM_LIMIT_BYTES),
        name="mix_out",
    )(sb_o, d_o, h, w, g)


def _encoder_layer(x, past, w, lam_init, *, tq, tk):
    b, n, _ = x.shape
    m = b * n
    h = _ffn(x.reshape(m, D_MODEL), w["ff1_pre_g"], w["ff1_w_gate"], w["ff1_w_up"], w["ff1_w_down"],
             w["ff1_post_g"])
    lam_args = (w["lam_q1"], w["lam_k1"], w["lam_q2"], w["lam_k2"], w["subln_g"])

    def seq(a):
        return a.reshape(b, n, a.shape[-1])

    if past is None:
        sbq, sbk, sbv, dq, dk, dv, sbk16, sbvt16, dk16, dvt16 = _qkv(h, w["mix_pre_g"], w["w_in"], seq_len=n)
        sb_o = _sb_prompt_call(seq(sbq), seq(sbk16), sbvt16, t=tq)
        d_o = _diff_prompt_call(seq(dq), seq(dk16), dvt16, *lam_args, t=tq, lam_init=lam_init)
    else:
        sbq, sbk, sbv, dq, dk, dv, sbk16, sbv16, dk16, dv16 = _qkv(h, w["mix_pre_g"], w["w_in"])
        offset = past[0].shape[1]
        pad = -(offset + n) % tk
        keys = [jnp.concatenate([c.astype(_BF16), seq(new), jnp.zeros((b, pad, new.shape[-1]), _BF16)], axis=1)
                for c, new in zip(past, (sbk16, sbv16, dk16, dv16))]
        sb_o = _sb_call(seq(sbq), keys[0], keys[1], tq=tq, tk=tk, offset=offset)
        d_o = _diff_call(seq(dq), keys[2], keys[3], *lam_args, tq=tq, tk=tk, offset=offset,
                         nk_valid=offset + n, lam_init=lam_init)
    h = _mix_out(sb_o.reshape(m, SB_WIDTH), d_o.reshape(m, DIFF_WIDTH), h, w["w_out"], w["mix_post_g"])
    y = _ffn(h, w["ff2_pre_g"], w["ff2_w_gate"], w["ff2_w_up"], w["ff2_w_down"], w["ff2_post_g"],
             w["final_g"])
    rows = (sbk.reshape(b, n, SB_HEADS, SB_HEAD_DIM), sbv.reshape(b, n, SB_HEADS, SB_HEAD_DIM),
            dk.reshape(b, n, DIFF_HEADS, 2 * DIFF_HEAD_DIM), dv.reshape(b, n, DIFF_HEADS, 2 * DIFF_HEAD_DIM))
    return y.reshape(b, n, D_MODEL), rows


_MATRICES = ("ff1_w_gate", "ff1_w_up", "ff1_w_down", "w_in", "w_out", "ff2_w_gate", "ff2_w_up", "ff2_w_down")


def kernel(x_prompt, x_sample, cache_sb_k, cache_sb_v, cache_diff_k, cache_diff_v, ff1_pre_g, ff1_w_gate, ff1_w_up, ff1_w_down, ff1_post_g, mix_pre_g, w_in, lam_q1, lam_k1, lam_q2, lam_k2, subln_g, w_out, mix_post_g, ff2_pre_g, ff2_w_gate, ff2_w_up, ff2_w_down, ff2_post_g, final_g):
    params = dict(ff1_pre_g=ff1_pre_g, ff1_w_gate=ff1_w_gate, ff1_w_up=ff1_w_up, ff1_w_down=ff1_w_down,
                  ff1_post_g=ff1_post_g, mix_pre_g=mix_pre_g, w_in=w_in, lam_q1=lam_q1, lam_k1=lam_k1,
                  lam_q2=lam_q2, lam_k2=lam_k2, subln_g=subln_g, w_out=w_out, mix_post_g=mix_post_g,
                  ff2_pre_g=ff2_pre_g, ff2_w_gate=ff2_w_gate, ff2_w_up=ff2_w_up, ff2_w_down=ff2_w_down,
                  ff2_post_g=ff2_post_g, final_g=final_g)
    yp, ys = x_prompt, x_sample
    rows_p, rows_s = [], []
    for l in range(DEPTH):
        lam_init = 0.8 - 0.6 * math.exp(-0.3 * l)
        w = {name: (p[l].astype(_BF16) if name in _MATRICES else p[l][None, :].astype(_F32))
             for name, p in params.items()}
        yp, rp = _encoder_layer(yp, None, w, lam_init, tq=256, tk=256)
        past = tuple(c[l].reshape(c.shape[1], c.shape[2], -1)
                     for c in (cache_sb_k, cache_sb_v, cache_diff_k, cache_diff_v))
        ys, rs = _encoder_layer(ys, past, w, lam_init, tq=x_sample.shape[1], tk=128)
        rows_p.append(rp)
        rows_s.append(rs)
    stacked_p = [jnp.stack(r, axis=0) for r in zip(*rows_p)]
    stacked_s = [jnp.stack(r, axis=0) for r in zip(*rows_s)]
    return (yp, ys, *stacked_p, *stacked_s)
```

```python
import functools
import math

import jax
import jax.numpy as jnp
from jax import lax
from jax.experimental import pallas as pl
from jax.experimental.pallas import tpu as pltpu

D_MODEL = 1024
DEPTH = 1
CHUNK = 64
CHUNK_SHIFT = 6
SB_HEADS = 8
SB_HEAD_DIM = 64
DIFF_HEADS = 4
DIFF_HEAD_DIM = 64
SB_WIDTH = SB_HEADS * SB_HEAD_DIM
DIFF_WIDTH = DIFF_HEADS * 2 * DIFF_HEAD_DIM
MIX_WIDTH = SB_WIDTH + DIFF_WIDTH
IN_WIDTH = 3 * SB_WIDTH + 3 * DIFF_WIDTH
D_FF = 2816
RMS_EPS = 1e-6

LANES = 128
MXU_WIDTH = 256
VMEM_LIMIT_BYTES = 56 * 1024 * 1024
F32_EXP_ZERO = 104.0
NEG_BIG = -1e30
LOG2E = math.log2(math.e)
DEPTH_SLOTS = 4

_F32 = jnp.float32
_BF16 = jnp.bfloat16


def _rms(x, g):
    return x * lax.rsqrt(jnp.mean(x * x, axis=-1, keepdims=True) + RMS_EPS) * g


def _softplus_tail(z):
    return jnp.log(1.0 + jnp.exp(-jnp.abs(z)))


def _dot(a, b):
    return jnp.dot(a, b, preferred_element_type=_F32)


def _dot_nt(a, b):
    return lax.dot_general(a, b, (((1,), (1,)), ((), ())), preferred_element_type=_F32)


def _ffn_kernel(*refs, final_norm):
    if final_norm:
        x_ref, pre_ref, wg_ref, wu_ref, wd_ref, post_ref, fin_ref, o_ref, xn_ref, acc_ref = refs
    else:
        x_ref, pre_ref, wg_ref, wu_ref, wd_ref, post_ref, o_ref, xn_ref, acc_ref = refs
    j = pl.program_id(1)

    @pl.when(j == 0)
    def _():
        xn_ref[...] = _rms(x_ref[...], pre_ref[...]).astype(_BF16)
        acc_ref[...] = jnp.zeros_like(acc_ref)

    xn = xn_ref[...]
    g = _dot(xn, wg_ref[...])
    u = _dot(xn, wu_ref[...])
    a = (g * jax.nn.sigmoid(g) * u).astype(_BF16)
    acc_ref[...] += _dot(a, wd_ref[...])

    @pl.when(j == pl.num_programs(1) - 1)
    def _():
        h = x_ref[...] + 0.5 * _rms(acc_ref[...], post_ref[...])
        if final_norm:
            h = _rms(h, fin_ref[...])
        o_ref[...] = h


def _ffn(x, pre_g, wg, wu, wd, post_g, final_g=None):
    m = x.shape[0]
    tm = min(512, m)
    tf = D_FF // 2
    assert m % tm == 0 and D_FF % tf == 0 and tf % LANES == 0
    row = pl.BlockSpec((tm, D_MODEL), lambda i, j: (i, 0))
    vec = pl.BlockSpec((1, D_MODEL), lambda i, j: (0, 0))
    w_in = pl.BlockSpec((D_MODEL, tf), lambda i, j: (0, j))
    w_out = pl.BlockSpec((tf, D_MODEL), lambda i, j: (j, 0))
    in_specs = [row, vec, w_in, w_in, w_out, vec]
    args = [x, pre_g, wg, wu, wd, post_g]
    if final_g is not None:
        in_specs.append(vec)
        args.append(final_g)
    return pl.pallas_call(
        functools.partial(_ffn_kernel, final_norm=final_g is not None),
        grid=(m // tm, D_FF // tf),
        in_specs=in_specs,
        out_specs=row,
        out_shape=jax.ShapeDtypeStruct((m, D_MODEL), _F32),
        scratch_shapes=[pltpu.VMEM((tm, D_MODEL), _BF16), pltpu.VMEM((tm, D_MODEL), _F32)],
        compiler_params=pltpu.CompilerParams(
            dimension_semantics=("parallel", "arbitrary"), vmem_limit_bytes=VMEM_LIMIT_BYTES),
        name="ffn_final" if final_g is not None else "ffn",
    )(*args)


def _qkv_kernel(h_ref, g_ref, w_ref, sbq_ref, sbk_ref, sbv_ref, dq_ref, dk_ref, dv_ref,
                sbk16_ref, sbv16_ref, dk16_ref, dv16_ref, *, v_transposed):
    hn = _rms(h_ref[...], g_ref[...]).astype(_BF16)

    def proj(idx):
        return _dot(hn, w_ref[:, idx * SB_WIDTH:(idx + 1) * SB_WIDTH])

    sbq_ref[...] = (proj(0) * (SB_HEAD_DIM ** -0.5)).astype(_BF16)
    dq_ref[...] = (proj(3) * (DIFF_HEAD_DIM ** -0.5 * LOG2E)).astype(_BF16)
    for idx, full_ref, half_ref in ((1, sbk_ref, sbk16_ref), (4, dk_ref, dk16_ref)):
        p = proj(idx)
        full_ref[...] = p
        half_ref[...] = p.astype(_BF16)
    for idx, full_ref, half_ref in ((2, sbv_ref, sbv16_ref), (5, dv_ref, dv16_ref)):
        p = proj(idx)
        full_ref[...] = p
        if v_transposed:
            half_ref[0] = p.T.astype(_BF16)
        else:
            half_ref[...] = p.astype(_BF16)


def _qkv(h, g, w, *, seq_len=None):
    assert SB_WIDTH == DIFF_WIDTH
    assert math.log2(SB_HEAD_DIM) % 2 == 0 and math.log2(DIFF_HEAD_DIM) % 2 == 0
    m = h.shape[0]
    tm = min(512, m)
    assert m % tm == 0
    row = pl.BlockSpec((tm, D_MODEL), lambda i: (i, 0))
    out = pl.BlockSpec((tm, SB_WIDTH), lambda i: (i, 0))
    f32 = jax.ShapeDtypeStruct((m, SB_WIDTH), _F32)
    b16 = jax.ShapeDtypeStruct((m, SB_WIDTH), _BF16)
    v_out, v16 = out, b16
    if seq_len is not None:
        assert seq_len % tm == 0
        tiles = seq_len // tm
        v_out = pl.BlockSpec((1, SB_WIDTH, tm), lambda i: (i // tiles, 0, i % tiles))
        v16 = jax.ShapeDtypeStruct((m // seq_len, SB_WIDTH, seq_len), _BF16)
    return pl.pallas_call(
        functools.partial(_qkv_kernel, v_transposed=seq_len is not None),
        grid=(m // tm,),
        in_specs=[row, pl.BlockSpec((1, D_MODEL), lambda i: (0, 0)),
                  pl.BlockSpec((D_MODEL, IN_WIDTH), lambda i: (0, 0))],
        out_specs=[out] * 7 + [v_out, out, v_out],
        out_shape=[b16, f32, f32, b16, f32, f32, b16, v16, b16, v16],
        compiler_params=pltpu.CompilerParams(
            dimension_semantics=("parallel",), vmem_limit_bytes=VMEM_LIMIT_BYTES),
        name="qkv",
    )(h, g, w)


def _sb_kernel(q_ref, k_ref, v_ref, o_ref, acc_ref, c_ref, *, tq, tk, offset):
    qi = pl.program_id(2)
    q0 = offset + qi * tq
    lane = lax.broadcasted_iota(jnp.int32, (1, LANES), 1)
    row = lax.broadcasted_iota(jnp.int32, (tq, tk), 0)
    col = lax.broadcasted_iota(jnp.int32, (tq, tk), 1)
    later = (lax.broadcasted_iota(jnp.int32, (tk, tk), 0)
             > lax.broadcasted_iota(jnp.int32, (tk, tk), 1)).astype(_BF16)
    q = q_ref[0].astype(_F32)

    def walk_block(qh, j, mask):
        k0 = pl.multiple_of(j * tk, tk)
        z = _dot_nt(qh, k_ref[0, pl.ds(k0, tk), :])
        t = _softplus_tail(z)
        log_beta = jnp.minimum(z, 0.0) - t
        log_stay = -jnp.maximum(z, 0.0) - t
        if mask is not None:
            log_stay = jnp.where(mask, log_stay, 0.0)
        hi = log_stay.astype(_BF16)
        lo = (log_stay - hi.astype(_F32)).astype(_BF16)
        sums = _dot(jnp.concatenate([hi, lo], axis=0), later)
        between = sums[:tq] + sums[tq:] + c_ref[...]
        w = jnp.exp(log_beta + between)
        if mask is not None:
            w = jnp.where(mask, w, 0.0)
        acc_ref[...] += _dot(w.astype(_BF16), v_ref[0, pl.ds(k0, tk), :])
        c_new = c_ref[...] + jnp.sum(log_stay, axis=-1, keepdims=True)
        c_ref[...] = c_new
        return jnp.max(c_new)

    heads = []
    for head in range(LANES // SB_HEAD_DIM):
        in_head = (lane >= head * SB_HEAD_DIM) & (lane < (head + 1) * SB_HEAD_DIM)
        qh = jnp.where(in_head, q, 0.0).astype(_BF16)
        acc_ref[...] = jnp.zeros_like(acc_ref)
        c_ref[...] = jnp.zeros_like(c_ref)
        j_top = (q0 + tq - 1) // tk
        c_max = jnp.float32(0.0)
        for d in range(_straddle(tq, tk)):
            j = j_top - d
            c_max = walk_block(qh, j, (j * tk + col) < (q0 + row))

        def cond(state):
            j, c_max = state
            return (j >= 0) & (c_max > -F32_EXP_ZERO)

        def body(state):
            j, _ = state
            return j - 1, walk_block(qh, j, None)

        lax.while_loop(cond, body, (j_top - _straddle(tq, tk), c_max))
        heads.append(acc_ref[...])

    o = heads[0]
    for head in range(1, len(heads)):
        o = jnp.where(lane >= head * SB_HEAD_DIM, heads[head], o)
    o_ref[0] = o.astype(o_ref.dtype)


def _attention_specs(nq, nk, tq, tk, offset):
    assert nq % tq == 0 and nk % tk == 0 and offset + nq <= nk
    for q0 in range(offset, offset + nq, tq):
        assert (q0 + tq - 1) // tk - q0 // tk + 1 == _straddle(tq, tk)
    q_spec = pl.BlockSpec((1, tq, LANES), lambda bi, g, qi: (bi, qi, g))
    kv_spec = pl.BlockSpec((1, nk, LANES), lambda bi, g, qi: (bi, 0, g))
    return q_spec, kv_spec


def _straddle(tq, tk):
    return max(1, tq // tk)


def _sb_call(q, k, v, *, tq, tk, offset):
    b, nq, _ = q.shape
    nk = k.shape[1]
    q_spec, kv_spec = _attention_specs(nq, nk, tq, tk, offset)
    return pl.pallas_call(
        functools.partial(_sb_kernel, tq=tq, tk=tk, offset=offset),
        grid=(b, SB_WIDTH // LANES, nq // tq),
        in_specs=[q_spec, kv_spec, kv_spec],
        out_specs=q_spec,
        out_shape=jax.ShapeDtypeStruct((b, nq, SB_WIDTH), _BF16),
        scratch_shapes=[pltpu.VMEM((tq, LANES), _F32), pltpu.VMEM((tq, 1), _F32)],
        compiler_params=pltpu.CompilerParams(
            dimension_semantics=("parallel", "parallel", "arbitrary"),
            vmem_limit_bytes=VMEM_LIMIT_BYTES),
        name="sb_attention",
    )(q, k, v)


def _alibi_slope_log2(head):
    assert 8 % DIFF_HEADS == 0
    exponent = (8 // DIFF_HEADS) * (head + 1)
    slope = lax.bitcast_convert_type(jnp.full((1, 1), (127 - exponent) << 23, jnp.int32), _F32)
    return slope * LOG2E


def _diff_kernel(q_ref, k_ref, v_ref, lq1_ref, lk1_ref, lq2_ref, lk2_ref, g_ref, o_ref,
                 m_ref, l_ref, acc_ref, *, tq, tk, offset, nk_valid, lam_init):
    head = pl.program_id(1)
    qi = pl.program_id(2)
    q0 = offset + qi * tq
    lane = lax.broadcasted_iota(jnp.int32, (1, LANES), 1)
    row = lax.broadcasted_iota(jnp.int32, (tq, tk), 0)
    col = lax.broadcasted_iota(jnp.int32, (tq, tk), 1)
    slope = _alibi_slope_log2(head)
    rel = (row - col).astype(_F32)
    q = q_ref[0].astype(_F32)
    qs = [jnp.where((lane >= c * DIFF_HEAD_DIM) & (lane < (c + 1) * DIFF_HEAD_DIM), q, 0.0).astype(_BF16)
          for c in range(2)]
    m_ref[...] = jnp.full_like(m_ref, NEG_BIG)
    l_ref[...] = jnp.zeros_like(l_ref)
    acc_ref[...] = jnp.zeros_like(acc_ref)

    def update(c, s, v_blk):
        m_old = m_ref[c]
        m_new = jnp.maximum(m_old, jnp.max(s, axis=-1, keepdims=True))
        alpha = jnp.exp2(m_old - m_new)
        p = jnp.exp2(s - m_new)
        l_ref[c] = alpha * l_ref[c] + jnp.sum(p, axis=-1, keepdims=True)
        acc_ref[c] = alpha * acc_ref[c] + _dot(p.astype(_BF16), v_blk)
        m_ref[c] = m_new

    def past_block(j, carry):
        k0 = pl.multiple_of(j * tk, tk)
        k_blk = k_ref[0, pl.ds(k0, tk), :]
        v_blk = v_ref[0, pl.ds(k0, tk), :]
        bias = -slope * (rel + (q0 - k0).astype(_F32))
        for c in range(2):
            update(c, _dot_nt(qs[c], k_blk) + bias, v_blk)
        return carry

    j_first = q0 // tk
    lax.fori_loop(0, j_first, past_block, 0)

    for d in range(_straddle(tq, tk)):
        j = j_first + d
        k0 = pl.multiple_of(j * tk, tk)
        k_blk = k_ref[0, pl.ds(k0, tk), :]
        v_blk = v_ref[0, pl.ds(k0, tk), :]
        k_pos = k0 + col
        q_pos = q0 + row
        mask = ((lax.shift_right_logical(k_pos, CHUNK_SHIFT) <= lax.shift_right_logical(q_pos, CHUNK_SHIFT))
                & (k_pos < nk_valid))
        bias = -slope * jnp.abs(q_pos - k_pos).astype(_F32)
        for c in range(2):
            s = jnp.where(mask, _dot_nt(qs[c], k_blk) + bias, -jnp.inf)
            update(c, s, v_blk)

    lam = (jnp.exp(jnp.sum(lq1_ref[...] * lk1_ref[...], axis=-1, keepdims=True))
           - jnp.exp(jnp.sum(lq2_ref[...] * lk2_ref[...], axis=-1, keepdims=True)) + lam_init)
    o = acc_ref[0] / l_ref[0] - lam * (acc_ref[1] / l_ref[1])
    o = _rms(o, g_ref[...]) * (1.0 - lam_init)
    o_ref[0] = o.astype(o_ref.dtype)


def _diff_call(q, k, v, lq1, lk1, lq2, lk2, subln_g, *, tq, tk, offset, nk_valid, lam_init):
    b, nq, _ = q.shape
    nk = k.shape[1]
    assert 2 * DIFF_HEAD_DIM == LANES and 8 % DIFF_HEADS == 0
    assert tk % CHUNK == 0 and CHUNK == 1 << CHUNK_SHIFT and offset + nq <= nk_valid <= nk
    q_spec, kv_spec = _attention_specs(nq, nk, tq, tk, offset)
    lam_spec = pl.BlockSpec((1, DIFF_HEAD_DIM), lambda bi, g, qi: (0, 0))
    return pl.pallas_call(
        functools.partial(_diff_kernel, tq=tq, tk=tk, offset=offset, nk_valid=nk_valid,
                          lam_init=lam_init),
        grid=(b, DIFF_HEADS, nq // tq),
        in_specs=[q_spec, kv_spec, kv_spec, lam_spec, lam_spec, lam_spec, lam_spec,
                  pl.BlockSpec((1, LANES), lambda bi, g, qi: (0, 0))],
        out_specs=q_spec,
        out_shape=jax.ShapeDtypeStruct((b, nq, DIFF_WIDTH), _BF16),
        scratch_shapes=[pltpu.VMEM((2, tq, 1), _F32), pltpu.VMEM((2, tq, 1), _F32),
                        pltpu.VMEM((2, tq, LANES), _F32)],
        compiler_params=pltpu.CompilerParams(
            dimension_semantics=("parallel", "parallel", "arbitrary"),
            vmem_limit_bytes=VMEM_LIMIT_BYTES),
        name="diff_attention",
    )(q, k, v, lq1, lk1, lq2, lk2, subln_g)


def _prompt_specs(nq, t):
    assert nq % t == 0 and t & (t - 1) == 0
    q_spec = pl.BlockSpec((1, t, LANES), lambda bi, g, qi: (bi, qi, g))
    k_spec = pl.BlockSpec((1, nq, LANES), lambda bi, g, qi: (bi, 0, g))
    vt_spec = pl.BlockSpec((1, LANES, nq), lambda bi, g, qi: (bi, g, 0))
    return q_spec, k_spec, vt_spec


def _sb_prompt_kernel(q_ref, k_ref, vt_ref, o_ref, acc_ref, c_ref, *, t):
    assert LANES == 2 * SB_HEAD_DIM
    qi = pl.program_id(2)
    lane = lax.broadcasted_iota(jnp.int32, (1, LANES), 1)
    key = lax.broadcasted_iota(jnp.int32, (t, 2 * t), 0)
    qry = lax.broadcasted_iota(jnp.int32, (t, 2 * t), 1) & (t - 1)
    causal = key < qry
    later = (lax.broadcasted_iota(jnp.int32, (t, t), 1)
             > lax.broadcasted_iota(jnp.int32, (t, t), 0)).astype(_BF16)
    q = q_ref[0].astype(_F32)
    q_heads = jnp.concatenate([jnp.where(lane < SB_HEAD_DIM, q, 0.0),
                               jnp.where(lane >= SB_HEAD_DIM, q, 0.0)], axis=0).astype(_BF16)

    def walk_tile(j, c, mask):
        k0 = pl.multiple_of(j * t, t)
        z = _dot_nt(k_ref[0, pl.ds(k0, t), :], q_heads)
        sp = _softplus_tail(z)
        log_beta = jnp.minimum(z, 0.0) - sp
        log_stay = -jnp.maximum(z, 0.0) - sp
        if mask is not None:
            log_stay = jnp.where(mask, log_stay, 0.0)
        hi = log_stay.astype(_BF16)
        lo = (log_stay - hi.astype(_F32)).astype(_BF16)
        sums = _dot(later, jnp.concatenate([hi, lo], axis=1))
        w = jnp.exp(log_beta + (sums[:, :2 * t] + sums[:, 2 * t:] + c))
        if mask is not None:
            w = jnp.where(mask, w, 0.0)
        w = w.astype(_BF16)
        vt = vt_ref[0, :, pl.ds(k0, t)]
        o = jnp.concatenate([_dot(vt[:SB_HEAD_DIM], w[:, :t]), _dot(vt[SB_HEAD_DIM:], w[:, t:])], axis=0)
        return c + jnp.sum(log_stay, axis=0, keepdims=True), o

    c0 = jnp.zeros((1, 2 * t), _F32)

    @pl.when(qi == 0)
    def _():
        c_ref[...], acc_ref[...] = walk_tile(0, c0, causal)

    @pl.when(qi > 0)
    def _():
        c1, o_own = walk_tile(qi, c0, causal)
        c2, o_prev = walk_tile(qi - 1, c1, None)
        c_ref[...] = c2
        acc_ref[...] = o_own + o_prev

        def cond(state):
            j, c_max = state
            return (j >= 0) & (c_max > -F32_EXP_ZERO)

        def body(state):
            j, _ = state
            c, o = walk_tile(j, c_ref[...], None)
            c_ref[...] = c
            acc_ref[...] += o
            return j - 1, jnp.max(c)

        lax.while_loop(cond, body, (qi - 2, jnp.max(c2)))

    o_ref[0] = acc_ref[...].T.astype(o_ref.dtype)


def _sb_prompt_call(q, k, vt, *, t):
    b, nq, _ = q.shape
    q_spec, k_spec, vt_spec = _prompt_specs(nq, t)
    return pl.pallas_call(
        functools.partial(_sb_prompt_kernel, t=t),
        grid=(b, SB_WIDTH // LANES, nq // t),
        in_specs=[q_spec, k_spec, vt_spec],
        out_specs=q_spec,
        out_shape=jax.ShapeDtypeStruct((b, nq, SB_WIDTH), _BF16),
        scratch_shapes=[pltpu.VMEM((LANES, t), _F32), pltpu.VMEM((1, 2 * t), _F32)],
        compiler_params=pltpu.CompilerParams(
            dimension_semantics=("parallel", "parallel", "arbitrary"),
            vmem_limit_bytes=VMEM_LIMIT_BYTES),
        name="sb_prompt",
    )(q, k, vt)


def _diff_prompt_kernel(q_ref, k_ref, vt_ref, lq1_ref, lk1_ref, lq2_ref, lk2_ref, g_ref, o_ref,
                        m_ref, l_ref, acc_ref, pen_ref, kpos_ref, s_ref, p_ref, a_ref,
                        *, tq, tk, lam_init):
    own = 2
    assert tq == own * tk
    head = pl.program_id(1)
    qi = pl.program_id(2)
    n_past = qi * own
    lane = lax.broadcasted_iota(jnp.int32, (1, LANES), 1)
    slope = _alibi_slope_log2(head)

    @pl.when(qi == 0)
    def _():
        key = lax.broadcasted_iota(jnp.int32, (tk, 2 * tq), 0)
        qry = lax.broadcasted_iota(jnp.int32, (tk, 2 * tq), 1) & (tq - 1)
        for d in range(own):
            k_pos = d * tk + key
            visible = lax.shift_right_logical(k_pos, CHUNK_SHIFT) <= lax.shift_right_logical(qry, CHUNK_SHIFT)
            bias = slope * (qry - key - jnp.abs(qry - k_pos)).astype(_F32)
            pen_ref[d] = jnp.where(visible, bias, -jnp.inf)
        row = lax.broadcasted_iota(jnp.int32, (tk, LANES), 0)
        col = lax.broadcasted_iota(jnp.int32, (tk, LANES), 1)
        kpos_ref[...] = jnp.where(col < 3, row, 0).astype(_F32).astype(_BF16)

    slope_hi = slope.astype(_BF16).astype(_F32)
    slope_mid = (slope - slope_hi).astype(_BF16).astype(_F32)
    slope_lo = slope - slope_hi - slope_mid
    slope_cols = jnp.where(lane == 0, slope_hi, jnp.where(lane == 1, slope_mid,
                                                          jnp.where(lane == 2, slope_lo, 0.0)))
    slope_cols = jnp.broadcast_to(slope_cols, (tq, LANES))
    q = q_ref[0].astype(_F32)
    q_maps = jnp.concatenate(
        [jnp.concatenate([jnp.where(lane < DIFF_HEAD_DIM, q, 0.0), slope_cols], axis=1),
         jnp.concatenate([jnp.where(lane >= DIFF_HEAD_DIM, q, 0.0), slope_cols], axis=1)],
        axis=0).astype(_BF16)
    m_ref[...] = jnp.full_like(m_ref, NEG_BIG)
    l_ref[...] = jnp.zeros_like(l_ref)
    acc_ref[...] = jnp.zeros_like(acc_ref)

    def slot_tile(n):
        return jnp.where(n < own, n_past + n, n_past - 1 - (n - own))

    def slot_shift(n):
        j = slot_tile(n)
        distance = (qi * tq - j * tk).astype(_F32)
        return jnp.where(n < own, 0.0, jnp.where(j >= 0, -slope * distance, NEG_BIG))

    def tile_start(n):
        return pl.multiple_of(jnp.clip(slot_tile(n), 0, n_past + own - 1) * tk, tk)

    def scores(n):
        keys = jnp.concatenate([k_ref[0, pl.ds(tile_start(n), tk), :], kpos_ref[...]], axis=1)
        return _dot_nt(keys, q_maps)

    def softmax(buf, shift):
        for c in range(0, 2 * tq, LANES):
            strip = slice(c, c + LANES)
            s = s_ref[buf, :, strip]
            m_old = m_ref[:, strip]
            m_new = jnp.maximum(m_old, jnp.max(s, axis=0, keepdims=True) + shift)
            alpha = jnp.exp2(m_old - m_new)
            p = jnp.exp2(s - (m_new - shift))
            l_ref[:, strip] = alpha * l_ref[:, strip] + jnp.sum(p, axis=0, keepdims=True)
            m_ref[:, strip] = m_new
            a_ref[buf, :, strip] = alpha
            p_ref[buf, :, strip] = p.astype(_BF16)

    def weigh(n, buf):
        values = vt_ref[0, :, pl.ds(tile_start(n), tk)]
        for c in range(0, 2 * tq, MXU_WIDTH):
            strip = slice(c, c + MXU_WIDTH)
            acc_ref[:, strip] = (a_ref[buf, :, strip] * acc_ref[:, strip]
                                 + _dot(values, p_ref[buf, :, strip]))

    for n in range(own):
        s_ref[n] = scores(n) + pen_ref[n]
    for n in range(own, DEPTH_SLOTS):
        p_ref[n] = jnp.zeros((tk, 2 * tq), _BF16)
        a_ref[n] = jnp.ones((1, 2 * tq), _F32)

    def stage_pair(n, buf):
        for d in range(2):
            s_ref[(buf + 2 + d) % DEPTH_SLOTS] = scores(n + 2 + d)
        for d in range(2):
            softmax(buf + d, slot_shift(n + d))
        for d in range(2):
            weigh(n - 2 + d, (buf + 2 + d) % DEPTH_SLOTS)

    def four_slots(i, carry):
        stage_pair(4 * i, 0)
        stage_pair(4 * i + 2, 2)
        return carry

    trips = (qi + 2) // 2
    lax.fori_loop(0, trips, four_slots, 0)
    for d in range(2):
        weigh(4 * trips - 2 + d, 2 + d)

    lam = (jnp.exp(jnp.sum(lq1_ref[...] * lk1_ref[...], axis=-1, keepdims=True))
           - jnp.exp(jnp.sum(lq2_ref[...] * lk2_ref[...], axis=-1, keepdims=True)) + lam_init)
    o = acc_ref[...] / l_ref[...]
    o = o[:, :tq] - lam * o[:, tq:]
    o = o * lax.rsqrt(jnp.mean(o * o, axis=0, keepdims=True) + RMS_EPS)
    o_ref[0] = (o.T * g_ref[...] * (1.0 - lam_init)).astype(o_ref.dtype)


def _diff_prompt_call(q, k, vt, lq1, lk1, lq2, lk2, subln_g, *, tq, lam_init):
    b, nq, _ = q.shape
    tk = tq // 2
    assert 2 * DIFF_HEAD_DIM == LANES
    assert tk % CHUNK == 0 and CHUNK == 1 << CHUNK_SHIFT and tk <= 256
    q_spec, k_spec, vt_spec = _prompt_specs(nq, tq)
    lam_spec = pl.BlockSpec((1, DIFF_HEAD_DIM), lambda bi, g, qi: (0, 0))
    stat = pltpu.VMEM((1, 2 * tq), _F32)
    return pl.pallas_call(
        functools.partial(_diff_prompt_kernel, tq=tq, tk=tk, lam_init=lam_init),
        grid=(b, DIFF_HEADS, nq // tq),
        in_specs=[q_spec, k_spec, vt_spec, lam_spec, lam_spec, lam_spec, lam_spec,
                  pl.BlockSpec((1, LANES), lambda bi, g, qi: (0, 0))],
        out_specs=q_spec,
        out_shape=jax.ShapeDtypeStruct((b, nq, DIFF_WIDTH), _BF16),
        scratch_shapes=[stat, stat, pltpu.VMEM((LANES, 2 * tq), _F32),
                        pltpu.VMEM((2, tk, 2 * tq), _F32), pltpu.VMEM((tk, LANES), _BF16),
                        pltpu.VMEM((DEPTH_SLOTS, tk, 2 * tq), _F32),
                        pltpu.VMEM((DEPTH_SLOTS, tk, 2 * tq), _BF16),
                        pltpu.VMEM((DEPTH_SLOTS, 1, 2 * tq), _F32)],
        compiler_params=pltpu.CompilerParams(
            dimension_semantics=("arbitrary", "arbitrary", "arbitrary"),
            vmem_limit_bytes=VMEM_LIMIT_BYTES),
        name="diff_prompt",
    )(q, k, vt, lq1, lk1, lq2, lk2, subln_g)


def _mix_out_kernel(sbo_ref, do_ref, h_ref, w_ref, g_ref, o_ref):
    mix = _dot(sbo_ref[...], w_ref[:SB_WIDTH, :]) + _dot(do_ref[...], w_ref[SB_WIDTH:, :])
    o_ref[...] = h_ref[...] + _rms(mix, g_ref[...])


def _mix_out(sb_o, d_o, h, w, g):
    m = h.shape[0]
    tm = min(512, m)
    assert m % tm == 0
    half = pl.BlockSpec((tm, SB_WIDTH), lambda i: (i, 0))
    row = pl.BlockSpec((tm, D_MODEL), lambda i: (i, 0))
    return pl.pallas_call(
        _mix_out_kernel,
        grid=(m // tm,),
        in_specs=[half, half, row, pl.BlockSpec((MIX_WIDTH, D_MODEL), lambda i: (0, 0)),
                  pl.BlockSpec((1, D_MODEL), lambda i: (0, 0))],
        out_specs=row,
        out_shape=jax.ShapeDtypeStruct((m, D_MODEL), _F32),
        compiler_params=pltpu.CompilerParams(
            dimension_semantics=("parallel",), vmem_limit_bytes=VMEM_LIMIT_BYTES),
        name="mix_out",
    )(sb_o, d_o, h, w, g)


def _encoder_layer(x, past, w, lam_init, *, tq, tk):
    b, n, _ = x.shape
    m = b * n
    h = _ffn(x.reshape(m, D_MODEL), w["ff1_pre_g"], w["ff1_w_gate"], w["ff1_w_up"], w["ff1_w_down"],
             w["ff1_post_g"])
    lam_args = (w["lam_q1"], w["lam_k1"], w["lam_q2"], w["lam_k2"], w["subln_g"])

    def seq(a):
        return a.reshape(b, n, a.shape[-1])

    if past is None:
        sbq, sbk, sbv, dq, dk, dv, sbk16, sbvt16, dk16, dvt16 = _qkv(h, w["mix_pre_g"], w["w_in"], seq_len=n)
        sb_o = _sb_prompt_call(seq(sbq), seq(sbk16), sbvt16, t=tq)
        d_o = _diff_prompt_call(seq(dq), seq(dk16), dvt16, *lam_args, tq=2 * tk, lam_init=lam_init)
    else:
        sbq, sbk, sbv, dq, dk, dv, sbk16, sbv16, dk16, dv16 = _qkv(h, w["mix_pre_g"], w["w_in"])
        offset = past[0].shape[1]
        pad = -(offset + n) % tk
        keys = [jnp.concatenate([c.astype(_BF16), seq(new), jnp.zeros((b, pad, new.shape[-1]), _BF16)], axis=1)
                for c, new in zip(past, (sbk16, sbv16, dk16, dv16))]
        sb_o = _sb_call(seq(sbq), keys[0], keys[1], tq=tq, tk=tk, offset=offset)
        d_o = _diff_call(seq(dq), keys[2], keys[3], *lam_args, tq=tq, tk=keys[2].shape[1], offset=offset,
                         nk_valid=offset + n, lam_init=lam_init)
    h = _mix_out(sb_o.reshape(m, SB_WIDTH), d_o.reshape(m, DIFF_WIDTH), h, w["w_out"], w["mix_post_g"])
    y = _ffn(h, w["ff2_pre_g"], w["ff2_w_gate"], w["ff2_w_up"], w["ff2_w_down"], w["ff2_post_g"],
             w["final_g"])
    rows = (sbk.reshape(b, n, SB_HEADS, SB_HEAD_DIM), sbv.reshape(b, n, SB_HEADS, SB_HEAD_DIM),
            dk.reshape(b, n, DIFF_HEADS, 2 * DIFF_HEAD_DIM), dv.reshape(b, n, DIFF_HEADS, 2 * DIFF_HEAD_DIM))
    return y.reshape(b, n, D_MODEL), rows


_MATRICES = ("ff1_w_gate", "ff1_w_up", "ff1_w_down", "w_in", "w_out", "ff2_w_gate", "ff2_w_up", "ff2_w_down")


def kernel(x_prompt, x_sample, cache_sb_k, cache_sb_v, cache_diff_k, cache_diff_v, ff1_pre_g, ff1_w_gate, ff1_w_up, ff1_w_down, ff1_post_g, mix_pre_g, w_in, lam_q1, lam_k1, lam_q2, lam_k2, subln_g, w_out, mix_post_g, ff2_pre_g, ff2_w_gate, ff2_w_up, ff2_w_down, ff2_post_g, final_g):
    params = dict(ff1_pre_g=ff1_pre_g, ff1_w_gate=ff1_w_gate, ff1_w_up=ff1_w_up, ff1_w_down=ff1_w_down,
                  ff1_post_g=ff1_post_g, mix_pre_g=mix_pre_g, w_in=w_in, lam_q1=lam_q1, lam_k1=lam_k1,
                  lam_q2=lam_q2, lam_k2=lam_k2, subln_g=subln_g, w_out=w_out, mix_post_g=mix_post_g,
                  ff2_pre_g=ff2_pre_g, ff2_w_gate=ff2_w_gate, ff2_w_up=ff2_w_up, ff2_w_down=ff2_w_down,
                  ff2_post_g=ff2_post_g, final_g=final_g)
    yp, ys = x_prompt, x_sample
    rows_p, rows_s = [], []
    for l in range(DEPTH):
        lam_init = 0.8 - 0.6 * math.exp(-0.3 * l)
        w = {name: (p[l].astype(_BF16) if name in _MATRICES else p[l][None, :].astype(_F32))
             for name, p in params.items()}
        yp, rp = _encoder_layer(yp, None, w, lam_init, tq=256, tk=256)
        past = tuple(c[l].reshape(c.shape[1], c.shape[2], -1)
                     for c in (cache_sb_k, cache_sb_v, cache_diff_k, cache_diff_v))
        ys, rs = _encoder_layer(ys, past, w, lam_init, tq=x_sample.shape[1], tk=128)
        rows_p.append(rp)
        rows_s.append(rs)
    stacked_p = [jnp.stack(r, axis=0) for r in zip(*rows_p)]
    stacked_s = [jnp.stack(r, axis=0) for r in zip(*rows_s)]
    return (yp, ys, *stacked_p, *stacked_s)
```

```python
import functools
import math

import jax
import jax.numpy as jnp
from jax import lax
from jax.experimental import pallas as pl
from jax.experimental.pallas import tpu as pltpu

D_MODEL = 1024
DEPTH = 1
CHUNK = 64
CHUNK_SHIFT = 6
SB_HEADS = 8
SB_HEAD_DIM = 64
DIFF_HEADS = 4
DIFF_HEAD_DIM = 64
SB_WIDTH = SB_HEADS * SB_HEAD_DIM
DIFF_WIDTH = DIFF_HEADS * 2 * DIFF_HEAD_DIM
MIX_WIDTH = SB_WIDTH + DIFF_WIDTH
IN_WIDTH = 3 * SB_WIDTH + 3 * DIFF_WIDTH
D_FF = 2816
RMS_EPS = 1e-6

LANES = 128
MXU_WIDTH = 256
VMEM_LIMIT_BYTES = 56 * 1024 * 1024
F32_EXP_ZERO = 104.0
NEG_BIG = -1e30
LOG2E = math.log2(math.e)
DEPTH_SLOTS = 4
SB_PROMPT_TILE = 256
DIFF_PROMPT_TILE = 512

_F32 = jnp.float32
_BF16 = jnp.bfloat16


def _rms(x, g):
    return x * lax.rsqrt(jnp.mean(x * x, axis=-1, keepdims=True) + RMS_EPS) * g


def _softplus_tail(z):
    return jnp.log(1.0 + jnp.exp(-jnp.abs(z)))


def _dot(a, b):
    return jnp.dot(a, b, preferred_element_type=_F32)


def _dot_nt(a, b):
    return lax.dot_general(a, b, (((1,), (1,)), ((), ())), preferred_element_type=_F32)


def _ffn_kernel(*refs, final_norm):
    if final_norm:
        x_ref, pre_ref, wg_ref, wu_ref, wd_ref, post_ref, fin_ref, o_ref, xn_ref, acc_ref = refs
    else:
        x_ref, pre_ref, wg_ref, wu_ref, wd_ref, post_ref, o_ref, xn_ref, acc_ref = refs
    j = pl.program_id(1)

    @pl.when(j == 0)
    def _():
        xn_ref[...] = _rms(x_ref[...], pre_ref[...]).astype(_BF16)
        acc_ref[...] = jnp.zeros_like(acc_ref)

    xn = xn_ref[...]
    g = _dot(xn, wg_ref[...])
    u = _dot(xn, wu_ref[...])
    a = (g * jax.nn.sigmoid(g) * u).astype(_BF16)
    acc_ref[...] += _dot(a, wd_ref[...])

    @pl.when(j == pl.num_programs(1) - 1)
    def _():
        h = x_ref[...] + 0.5 * _rms(acc_ref[...], post_ref[...])
        if final_norm:
            h = _rms(h, fin_ref[...])
        o_ref[...] = h


def _ffn(x, pre_g, wg, wu, wd, post_g, final_g=None):
    m = x.shape[0]
    tm = min(512, m)
    tf = D_FF // 2
    assert m % tm == 0 and D_FF % tf == 0 and tf % LANES == 0
    row = pl.BlockSpec((tm, D_MODEL), lambda i, j: (i, 0))
    vec = pl.BlockSpec((1, D_MODEL), lambda i, j: (0, 0))
    w_in = pl.BlockSpec((D_MODEL, tf), lambda i, j: (0, j))
    w_out = pl.BlockSpec((tf, D_MODEL), lambda i, j: (j, 0))
    in_specs = [row, vec, w_in, w_in, w_out, vec]
    args = [x, pre_g, wg, wu, wd, post_g]
    if final_g is not None:
        in_specs.append(vec)
        args.append(final_g)
    return pl.pallas_call(
        functools.partial(_ffn_kernel, final_norm=final_g is not None),
        grid=(m // tm, D_FF // tf),
        in_specs=in_specs,
        out_specs=row,
        out_shape=jax.ShapeDtypeStruct((m, D_MODEL), _F32),
        scratch_shapes=[pltpu.VMEM((tm, D_MODEL), _BF16), pltpu.VMEM((tm, D_MODEL), _F32)],
        compiler_params=pltpu.CompilerParams(
            dimension_semantics=("parallel", "arbitrary"), vmem_limit_bytes=VMEM_LIMIT_BYTES),
        name="ffn_final" if final_g is not None else "ffn",
    )(*args)


def _qkv_kernel(h_ref, g_ref, w_ref, sbq_ref, sbk_ref, sbv_ref, dq_ref, dk_ref, dv_ref,
                sbk16_ref, sbv16_ref, dk16_ref, dv16_ref, *, v_transposed):
    hn = _rms(h_ref[...], g_ref[...]).astype(_BF16)

    def proj(idx):
        return _dot(hn, w_ref[:, idx * SB_WIDTH:(idx + 1) * SB_WIDTH])

    sbq_ref[...] = (proj(0) * (SB_HEAD_DIM ** -0.5)).astype(_BF16)
    dq_ref[...] = (proj(3) * (DIFF_HEAD_DIM ** -0.5 * LOG2E)).astype(_BF16)
    for idx, full_ref, half_ref in ((1, sbk_ref, sbk16_ref), (4, dk_ref, dk16_ref)):
        p = proj(idx)
        full_ref[...] = p
        half_ref[...] = p.astype(_BF16)
    for idx, full_ref, half_ref in ((2, sbv_ref, sbv16_ref), (5, dv_ref, dv16_ref)):
        p = proj(idx)
        full_ref[...] = p
        if v_transposed:
            half_ref[0] = p.T.astype(_BF16)
        else:
            half_ref[...] = p.astype(_BF16)


def _qkv(h, g, w, *, seq_len=None):
    assert SB_WIDTH == DIFF_WIDTH
    assert math.log2(SB_HEAD_DIM) % 2 == 0 and math.log2(DIFF_HEAD_DIM) % 2 == 0
    m = h.shape[0]
    tm = min(512, m)
    assert m % tm == 0
    row = pl.BlockSpec((tm, D_MODEL), lambda i: (i, 0))
    out = pl.BlockSpec((tm, SB_WIDTH), lambda i: (i, 0))
    f32 = jax.ShapeDtypeStruct((m, SB_WIDTH), _F32)
    b16 = jax.ShapeDtypeStruct((m, SB_WIDTH), _BF16)
    v_out, v16 = out, b16
    if seq_len is not None:
        assert seq_len % tm == 0
        tiles = seq_len // tm
        v_out = pl.BlockSpec((1, SB_WIDTH, tm), lambda i: (i // tiles, 0, i % tiles))
        v16 = jax.ShapeDtypeStruct((m // seq_len, SB_WIDTH, seq_len), _BF16)
    return pl.pallas_call(
        functools.partial(_qkv_kernel, v_transposed=seq_len is not None),
        grid=(m // tm,),
        in_specs=[row, pl.BlockSpec((1, D_MODEL), lambda i: (0, 0)),
                  pl.BlockSpec((D_MODEL, IN_WIDTH), lambda i: (0, 0))],
        out_specs=[out] * 7 + [v_out, out, v_out],
        out_shape=[b16, f32, f32, b16, f32, f32, b16, v16, b16, v16],
        compiler_params=pltpu.CompilerParams(
            dimension_semantics=("parallel",), vmem_limit_bytes=VMEM_LIMIT_BYTES),
        name="qkv",
    )(h, g, w)


def _stream_specs(n, past, width):
    new = pl.BlockSpec((1, n, width), lambda i: (i, 0, 0))
    cache = pl.BlockSpec((1, past, width), lambda i: (i, 0, 0))
    return new, cache


def _sb_step_kernel(q_ref, kn_ref, vn_ref, kc_ref, vc_ref, o_ref, acc_ref, c_ref, *, tk):
    n = q_ref.shape[1]
    past = kc_ref.shape[1]

    def later(t):
        return (lax.broadcasted_iota(jnp.int32, (t, t), 0)
                > lax.broadcasted_iota(jnp.int32, (t, t), 1)).astype(_BF16)

    later_new, later_tile = later(n), later(tk)
    newer = (lax.broadcasted_iota(jnp.int32, (n, n), 1) < lax.broadcasted_iota(jnp.int32, (n, n), 0))

    def walk(qh, k, v, c, later_t, mask):
        z = _dot_nt(qh, k)
        sp = _softplus_tail(z)
        log_beta = jnp.minimum(z, 0.0) - sp
        log_stay = -jnp.maximum(z, 0.0) - sp
        if mask is not None:
            log_stay = jnp.where(mask, log_stay, 0.0)
        hi = log_stay.astype(_BF16)
        lo = (log_stay - hi.astype(_F32)).astype(_BF16)
        sums = _dot(jnp.concatenate([hi, lo], axis=0), later_t)
        w = jnp.exp(log_beta + (sums[:n] + sums[n:] + c))
        if mask is not None:
            w = jnp.where(mask, w, 0.0)
        return c + jnp.sum(log_stay, axis=-1, keepdims=True), _dot(w.astype(_BF16), v)

    lane = lax.broadcasted_iota(jnp.int32, (1, LANES), 1)
    pairs = [slice(g * LANES, (g + 1) * LANES) for g in range(SB_WIDTH // LANES)]

    def head_q(head):
        q_pair = q_ref[0, :, pairs[head // 2]].astype(_F32)
        in_head = (lane >= SB_HEAD_DIM) if head % 2 else (lane < SB_HEAD_DIM)
        return jnp.where(in_head, q_pair, 0.0).astype(_BF16)

    def walk_cache(start, c_of, first):
        for g, cols in enumerate(pairs):
            k = kc_ref[0, pl.ds(start, tk), cols].astype(_BF16)
            v = vc_ref[0, pl.ds(start, tk), cols].astype(_BF16)
            for head in (2 * g, 2 * g + 1):
                c, o = walk(head_q(head), k, v, c_of(head), later_tile, None)
                c_ref[head] = c
                acc_ref[head] = o + first[head] if first else acc_ref[head] + o

    new = [walk(head_q(head), kn_ref[0, :, pairs[head // 2]], vn_ref[0, :, pairs[head // 2]],
                jnp.zeros((n, 1), _F32), later_new, newer) for head in range(SB_HEADS)]
    walk_cache(past - tk, lambda head: new[head][0], [o for _, o in new])

    def cond(state):
        j, c_max = state
        return (j >= 0) & (c_max > -F32_EXP_ZERO)

    def body(state):
        j, _ = state
        walk_cache(pl.multiple_of(j * tk, tk), lambda head: c_ref[head], None)
        return j - 1, jnp.max(c_ref[...])

    lax.while_loop(cond, body, (past // tk - 2, jnp.max(c_ref[...])))
    o_ref[0] = jnp.concatenate(
        [jnp.where(lane < SB_HEAD_DIM, acc_ref[2 * g], acc_ref[2 * g + 1]) for g in range(len(pairs))],
        axis=-1).astype(o_ref.dtype)


def _sb_step_call(q, k_new, v_new, k_cache, v_cache):
    b, n, _ = q.shape
    past = k_cache.shape[1]
    tk = min(256, past)
    assert past % tk == 0 and LANES == 2 * SB_HEAD_DIM
    new, cache = _stream_specs(n, past, SB_WIDTH)
    return pl.pallas_call(
        functools.partial(_sb_step_kernel, tk=tk),
        grid=(b,),
        in_specs=[new, new, new, cache, cache],
        out_specs=new,
        out_shape=jax.ShapeDtypeStruct((b, n, SB_WIDTH), _BF16),
        scratch_shapes=[pltpu.VMEM((SB_HEADS, n, LANES), _F32), pltpu.VMEM((SB_HEADS, n, 1), _F32)],
        compiler_params=pltpu.CompilerParams(
            dimension_semantics=("arbitrary",), vmem_limit_bytes=VMEM_LIMIT_BYTES),
        name="sb_step",
    )(q, k_new, v_new, k_cache, v_cache)


def _alibi_slope_log2(head):
    assert 8 % DIFF_HEADS == 0
    exponent = (8 // DIFF_HEADS) * (head + 1)
    slope = lax.bitcast_convert_type(jnp.full((1, 1), (127 - exponent) << 23, jnp.int32), _F32)
    return slope * LOG2E


def _diff_step_kernel(q_ref, kn_ref, vn_ref, kc_ref, vc_ref, lq1_ref, lk1_ref, lq2_ref, lk2_ref, g_ref,
                      o_ref, *, lam_init):
    n = q_ref.shape[1]
    past = kc_ref.shape[1]
    lane = lax.broadcasted_iota(jnp.int32, (1, LANES), 1)
    q_idx_c = lax.broadcasted_iota(jnp.int32, (2 * n, past), 0) & (n - 1)
    k_idx_c = lax.broadcasted_iota(jnp.int32, (2 * n, past), 1)
    q_idx_n = lax.broadcasted_iota(jnp.int32, (2 * n, n), 0) & (n - 1)
    k_idx_n = lax.broadcasted_iota(jnp.int32, (2 * n, n), 1)
    distance_c = (past + q_idx_c - k_idx_c).astype(_F32)
    distance_n = jnp.abs(q_idx_n - k_idx_n).astype(_F32)
    visible_n =(lax.shift_right_logical(past + k_idx_n, CHUNK_SHIFT)
                 <= lax.shift_right_logical(past + q_idx_n, CHUNK_SHIFT))
    lam = (jnp.exp(jnp.sum(lq1_ref[...] * lk1_ref[...], axis=-1, keepdims=True))
           - jnp.exp(jnp.sum(lq2_ref[...] * lk2_ref[...], axis=-1, keepdims=True)) + lam_init)
    for head in range(DIFF_HEADS):
        cols = slice(head * LANES, (head + 1) * LANES)
        slope = _alibi_slope_log2(head)
        q = q_ref[0, :, cols].astype(_F32)
        q_maps = jnp.concatenate([jnp.where(lane < DIFF_HEAD_DIM, q, 0.0),
                                  jnp.where(lane >= DIFF_HEAD_DIM, q, 0.0)], axis=0).astype(_BF16)
        s_c = _dot_nt(q_maps, kc_ref[0, :, cols].astype(_BF16)) - slope * distance_c
        s_n = jnp.where(visible_n, _dot_nt(q_maps, kn_ref[0, :, cols]) - slope * distance_n, -jnp.inf)
        m = jnp.maximum(jnp.max(s_c, axis=-1, keepdims=True), jnp.max(s_n, axis=-1, keepdims=True))
        p_c = jnp.exp2(s_c - m)
        p_n = jnp.exp2(s_n - m)
        l = jnp.sum(p_c, axis=-1, keepdims=True) + jnp.sum(p_n, axis=-1, keepdims=True)
        o = (_dot(p_c.astype(_BF16), vc_ref[0, :, cols].astype(_BF16))
             + _dot(p_n.astype(_BF16), vn_ref[0, :, cols])) / l
        o = o[:n] - lam * o[n:]
        o_ref[0, :, cols] = (_rms(o, g_ref[...]) * (1.0 - lam_init)).astype(o_ref.dtype)


def _diff_step_call(q, k_new, v_new, k_cache, v_cache, lq1, lk1, lq2, lk2, subln_g, *, lam_init):
    b, n, _ = q.shape
    past = k_cache.shape[1]
    assert 2 * DIFF_HEAD_DIM == LANES and CHUNK == 1 << CHUNK_SHIFT and n & (n - 1) == 0
    new, cache = _stream_specs(n, past, DIFF_WIDTH)
    lam_spec = pl.BlockSpec((1, DIFF_HEAD_DIM), lambda i: (0, 0))
    return pl.pallas_call(
        functools.partial(_diff_step_kernel, lam_init=lam_init),
        grid=(b,),
        in_specs=[new, new, new, cache, cache, lam_spec, lam_spec, lam_spec, lam_spec,
                  pl.BlockSpec((1, LANES), lambda i: (0, 0))],
        out_specs=new,
        out_shape=jax.ShapeDtypeStruct((b, n, DIFF_WIDTH), _BF16),
        compiler_params=pltpu.CompilerParams(
            dimension_semantics=("arbitrary",), vmem_limit_bytes=VMEM_LIMIT_BYTES),
        name="diff_step",
    )(q, k_new, v_new, k_cache, v_cache, lq1, lk1, lq2, lk2, subln_g)


def _prompt_specs(nq, t):
    assert nq % t == 0 and t & (t - 1) == 0
    q_spec = pl.BlockSpec((1, t, LANES), lambda bi, g, qi: (bi, qi, g))
    k_spec = pl.BlockSpec((1, nq, LANES), lambda bi, g, qi: (bi, 0, g))
    vt_spec = pl.BlockSpec((1, LANES, nq), lambda bi, g, qi: (bi, g, 0))
    return q_spec, k_spec, vt_spec


def _sb_prompt_kernel(q_ref, k_ref, vt_ref, o_ref, acc_ref, c_ref, *, t):
    assert LANES == 2 * SB_HEAD_DIM
    qi = pl.program_id(2)
    lane = lax.broadcasted_iota(jnp.int32, (1, LANES), 1)
    key = lax.broadcasted_iota(jnp.int32, (t, 2 * t), 0)
    qry = lax.broadcasted_iota(jnp.int32, (t, 2 * t), 1) & (t - 1)
    causal = key < qry
    later = (lax.broadcasted_iota(jnp.int32, (t, t), 1)
             > lax.broadcasted_iota(jnp.int32, (t, t), 0)).astype(_BF16)
    q = q_ref[0].astype(_F32)
    q_heads = jnp.concatenate([jnp.where(lane < SB_HEAD_DIM, q, 0.0),
                               jnp.where(lane >= SB_HEAD_DIM, q, 0.0)], axis=0).astype(_BF16)

    def walk_tile(j, c, mask):
        k0 = pl.multiple_of(j * t, t)
        z = _dot_nt(k_ref[0, pl.ds(k0, t), :], q_heads)
        sp = _softplus_tail(z)
        log_beta = jnp.minimum(z, 0.0) - sp
        log_stay = -jnp.maximum(z, 0.0) - sp
        if mask is not None:
            log_stay = jnp.where(mask, log_stay, 0.0)
        hi = log_stay.astype(_BF16)
        lo = (log_stay - hi.astype(_F32)).astype(_BF16)
        sums = _dot(later, jnp.concatenate([hi, lo], axis=1))
        w = jnp.exp(log_beta + (sums[:, :2 * t] + sums[:, 2 * t:] + c))
        if mask is not None:
            w = jnp.where(mask, w, 0.0)
        w = w.astype(_BF16)
        vt = vt_ref[0, :, pl.ds(k0, t)]
        o = jnp.concatenate([_dot(vt[:SB_HEAD_DIM], w[:, :t]), _dot(vt[SB_HEAD_DIM:], w[:, t:])], axis=0)
        return c + jnp.sum(log_stay, axis=0, keepdims=True), o

    c0 = jnp.zeros((1, 2 * t), _F32)

    @pl.when(qi == 0)
    def _():
        c_ref[...], acc_ref[...] = walk_tile(0, c0, causal)

    @pl.when(qi > 0)
    def _():
        c1, o_own = walk_tile(qi, c0, causal)
        c2, o_prev = walk_tile(qi - 1, c1, None)
        c_ref[...] = c2
        acc_ref[...] = o_own + o_prev

        def cond(state):
            j, c_max = state
            return (j >= 0) & (c_max > -F32_EXP_ZERO)

        def body(state):
            j, _ = state
            c, o = walk_tile(j, c_ref[...], None)
            c_ref[...] = c
            acc_ref[...] += o
            return j - 1, jnp.max(c)

        lax.while_loop(cond, body, (qi - 2, jnp.max(c2)))

    o_ref[0] = acc_ref[...].T.astype(o_ref.dtype)


def _sb_prompt_call(q, k, vt, *, t):
    b, nq, _ = q.shape
    q_spec, k_spec, vt_spec = _prompt_specs(nq, t)
    return pl.pallas_call(
        functools.partial(_sb_prompt_kernel, t=t),
        grid=(b, SB_WIDTH // LANES, nq // t),
        in_specs=[q_spec, k_spec, vt_spec],
        out_specs=q_spec,
        out_shape=jax.ShapeDtypeStruct((b, nq, SB_WIDTH), _BF16),
        scratch_shapes=[pltpu.VMEM((LANES, t), _F32), pltpu.VMEM((1, 2 * t), _F32)],
        compiler_params=pltpu.CompilerParams(
            dimension_semantics=("parallel", "parallel", "arbitrary"),
            vmem_limit_bytes=VMEM_LIMIT_BYTES),
        name="sb_prompt",
    )(q, k, vt)


def _diff_prompt_kernel(q_ref, k_ref, vt_ref, lq1_ref, lk1_ref, lq2_ref, lk2_ref, g_ref, o_ref,
                        m_ref, l_ref, acc_ref, pen_ref, kpos_ref, s_ref, p_ref, a_ref,
                        *, tq, tk, lam_init):
    own = 2
    assert tq == own * tk
    head = pl.program_id(1)
    qi = pl.program_id(2)
    n_past = qi * own
    lane = lax.broadcasted_iota(jnp.int32, (1, LANES), 1)
    slope = _alibi_slope_log2(head)

    @pl.when(qi == 0)
    def _():
        key = lax.broadcasted_iota(jnp.int32, (tk, 2 * tq), 0)
        qry = lax.broadcasted_iota(jnp.int32, (tk, 2 * tq), 1) & (tq - 1)
        for d in range(own):
            k_pos = d * tk + key
            visible = lax.shift_right_logical(k_pos, CHUNK_SHIFT) <= lax.shift_right_logical(qry, CHUNK_SHIFT)
            bias = slope * (qry - key - jnp.abs(qry - k_pos)).astype(_F32)
            pen_ref[d] = jnp.where(visible, bias, -jnp.inf)
        row = lax.broadcasted_iota(jnp.int32, (tk, LANES), 0)
        col = lax.broadcasted_iota(jnp.int32, (tk, LANES), 1)
        kpos_ref[...] = jnp.where(col < 3, row, 0).astype(_F32).astype(_BF16)

    slope_hi = slope.astype(_BF16).astype(_F32)
    slope_mid = (slope - slope_hi).astype(_BF16).astype(_F32)
    slope_lo = slope - slope_hi - slope_mid
    slope_cols = jnp.where(lane == 0, slope_hi, jnp.where(lane == 1, slope_mid,
                                                          jnp.where(lane == 2, slope_lo, 0.0)))
    slope_cols = jnp.broadcast_to(slope_cols, (tq, LANES))
    q = q_ref[0].astype(_F32)
    q_maps = jnp.concatenate(
        [jnp.concatenate([jnp.where(lane < DIFF_HEAD_DIM, q, 0.0), slope_cols], axis=1),
         jnp.concatenate([jnp.where(lane >= DIFF_HEAD_DIM, q, 0.0), slope_cols], axis=1)],
        axis=0).astype(_BF16)
    m_ref[...] = jnp.full_like(m_ref, NEG_BIG)
    l_ref[...] = jnp.zeros_like(l_ref)
    acc_ref[...] = jnp.zeros_like(acc_ref)

    def slot_tile(n):
        return jnp.where(n < own, n_past + n, n_past - 1 - (n - own))

    def slot_shift(n):
        j = slot_tile(n)
        distance = (qi * tq - j * tk).astype(_F32)
        return jnp.where(n < own, 0.0, jnp.where(j >= 0, -slope * distance, NEG_BIG))

    def tile_start(n):
        return pl.multiple_of(jnp.clip(slot_tile(n), 0, n_past + own - 1) * tk, tk)

    def scores(n):
        keys = jnp.concatenate([k_ref[0, pl.ds(tile_start(n), tk), :], kpos_ref[...]], axis=1)
        return _dot_nt(keys, q_maps)

    def softmax(buf, shift):
        for c in range(0, 2 * tq, LANES):
            strip = slice(c, c + LANES)
            s = s_ref[buf, :, strip]
            m_old = m_ref[:, strip]
            m_new = jnp.maximum(m_old, jnp.max(s, axis=0, keepdims=True) + shift)
            alpha = jnp.exp2(m_old - m_new)
            p = jnp.exp2(s - (m_new - shift))
            l_ref[:, strip] = alpha * l_ref[:, strip] + jnp.sum(p, axis=0, keepdims=True)
            m_ref[:, strip] = m_new
            a_ref[buf, :, strip] = alpha
            p_ref[buf, :, strip] = p.astype(_BF16)

    def weigh(n, buf):
        values = vt_ref[0, :, pl.ds(tile_start(n), tk)]
        for c in range(0, 2 * tq, MXU_WIDTH):
            strip = slice(c, c + MXU_WIDTH)
            acc_ref[:, strip] = (a_ref[buf, :, strip] * acc_ref[:, strip]
                                 + _dot(values, p_ref[buf, :, strip]))

    for n in range(own):
        s_ref[n] = scores(n) + pen_ref[n]
    for n in range(own, DEPTH_SLOTS):
        p_ref[n] = jnp.zeros((tk, 2 * tq), _BF16)
        a_ref[n] = jnp.ones((1, 2 * tq), _F32)

    def stage_pair(n, buf):
        for d in range(2):
            s_ref[(buf + 2 + d) % DEPTH_SLOTS] = scores(n + 2 + d)
        for d in range(2):
            softmax(buf + d, slot_shift(n + d))
        for d in range(2):
            weigh(n - 2 + d, (buf + 2 + d) % DEPTH_SLOTS)

    def four_slots(i, carry):
        stage_pair(4 * i, 0)
        stage_pair(4 * i + 2, 2)
        return carry

    trips = (qi + 2) // 2
    lax.fori_loop(0, trips, four_slots, 0)
    for d in range(2):
        weigh(4 * trips - 2 + d, 2 + d)

    lam = (jnp.exp(jnp.sum(lq1_ref[...] * lk1_ref[...], axis=-1, keepdims=True))
           - jnp.exp(jnp.sum(lq2_ref[...] * lk2_ref[...], axis=-1, keepdims=True)) + lam_init)
    o = acc_ref[...] / l_ref[...]
    o = o[:, :tq] - lam * o[:, tq:]
    o = o * lax.rsqrt(jnp.mean(o * o, axis=0, keepdims=True) + RMS_EPS)
    o_ref[0] = (o.T * g_ref[...] * (1.0 - lam_init)).astype(o_ref.dtype)


def _diff_prompt_call(q, k, vt, lq1, lk1, lq2, lk2, subln_g, *, tq, lam_init):
    b, nq, _ = q.shape
    tk = tq // 2
    assert 2 * DIFF_HEAD_DIM == LANES
    assert tk % CHUNK == 0 and CHUNK == 1 << CHUNK_SHIFT and tk <= 256
    q_spec, k_spec, vt_spec = _prompt_specs(nq, tq)
    lam_spec = pl.BlockSpec((1, DIFF_HEAD_DIM), lambda bi, g, qi: (0, 0))
    stat = pltpu.VMEM((1, 2 * tq), _F32)
    return pl.pallas_call(
        functools.partial(_diff_prompt_kernel, tq=tq, tk=tk, lam_init=lam_init),
        grid=(b, DIFF_HEADS, nq // tq),
        in_specs=[q_spec, k_spec, vt_spec, lam_spec, lam_spec, lam_spec, lam_spec,
                  pl.BlockSpec((1, LANES), lambda bi, g, qi: (0, 0))],
        out_specs=q_spec,
        out_shape=jax.ShapeDtypeStruct((b, nq, DIFF_WIDTH), _BF16),
        scratch_shapes=[stat, stat, pltpu.VMEM((LANES, 2 * tq), _F32),
                        pltpu.VMEM((2, tk, 2 * tq), _F32), pltpu.VMEM((tk, LANES), _BF16),
                        pltpu.VMEM((DEPTH_SLOTS, tk, 2 * tq), _F32),
                        pltpu.VMEM((DEPTH_SLOTS, tk, 2 * tq), _BF16),
                        pltpu.VMEM((DEPTH_SLOTS, 1, 2 * tq), _F32)],
        compiler_params=pltpu.CompilerParams(
            dimension_semantics=("arbitrary", "arbitrary", "arbitrary"),
            vmem_limit_bytes=VMEM_LIMIT_BYTES),
        name="diff_prompt",
    )(q, k, vt, lq1, lk1, lq2, lk2, subln_g)


def _mix_out_kernel(sbo_ref, do_ref, h_ref, w_ref, g_ref, o_ref):
    mix = _dot(sbo_ref[...], w_ref[:SB_WIDTH, :]) + _dot(do_ref[...], w_ref[SB_WIDTH:, :])
    o_ref[...] = h_ref[...] + _rms(mix, g_ref[...])


def _mix_out(sb_o, d_o, h, w, g):
    m = h.shape[0]
    tm = min(512, m)
    assert m % tm == 0
    half = pl.BlockSpec((tm, SB_WIDTH), lambda i: (i, 0))
    row = pl.BlockSpec((tm, D_MODEL), lambda i: (i, 0))
    return pl.pallas_call(
        _mix_out_kernel,
        grid=(m // tm,),
        in_specs=[half, half, row, pl.BlockSpec((MIX_WIDTH, D_MODEL), lambda i: (0, 0)),
                  pl.BlockSpec((1, D_MODEL), lambda i: (0, 0))],
        out_specs=row,
        out_shape=jax.ShapeDtypeStruct((m, D_MODEL), _F32),
        compiler_params=pltpu.CompilerParams(
            dimension_semantics=("parallel",), vmem_limit_bytes=VMEM_LIMIT_BYTES),
        name="mix_out",
    )(sb_o, d_o, h, w, g)


def _encoder_layer(x, past, w, lam_init):
    b, n, _ = x.shape
    m = b * n
    h = _ffn(x.reshape(m, D_MODEL), w["ff1_pre_g"], w["ff1_w_gate"], w["ff1_w_up"], w["ff1_w_down"],
             w["ff1_post_g"])
    lam_args = (w["lam_q1"], w["lam_k1"], w["lam_q2"], w["lam_k2"], w["subln_g"])

    def seq(a):
        return a.reshape(b, n, a.shape[-1])

    if past is None:
        sbq, sbk, sbv, dq, dk, dv, sbk16, sbvt16, dk16, dvt16 = _qkv(h, w["mix_pre_g"], w["w_in"], seq_len=n)
        sb_o = _sb_prompt_call(seq(sbq), seq(sbk16), sbvt16, t=SB_PROMPT_TILE)
        d_o = _diff_prompt_call(seq(dq), seq(dk16), dvt16, *lam_args, tq=DIFF_PROMPT_TILE, lam_init=lam_init)
    else:
        sbq, sbk, sbv, dq, dk, dv, sbk16, sbv16, dk16, dv16 = _qkv(h, w["mix_pre_g"], w["w_in"])
        sb_o = _sb_step_call(seq(sbq), seq(sbk16), seq(sbv16), past[0], past[1])
        d_o = _diff_step_call(seq(dq), seq(dk16), seq(dv16), past[2], past[3], *lam_args, lam_init=lam_init)
    h = _mix_out(sb_o.reshape(m, SB_WIDTH), d_o.reshape(m, DIFF_WIDTH), h, w["w_out"], w["mix_post_g"])
    y = _ffn(h, w["ff2_pre_g"], w["ff2_w_gate"], w["ff2_w_up"], w["ff2_w_down"], w["ff2_post_g"],
             w["final_g"])
    rows = (sbk.reshape(b, n, SB_HEADS, SB_HEAD_DIM), sbv.reshape(b, n, SB_HEADS, SB_HEAD_DIM),
            dk.reshape(b, n, DIFF_HEADS, 2 * DIFF_HEAD_DIM), dv.reshape(b, n, DIFF_HEADS, 2 * DIFF_HEAD_DIM))
    return y.reshape(b, n, D_MODEL), rows


_MATRICES = ("ff1_w_gate", "ff1_w_up", "ff1_w_down", "w_in", "w_out", "ff2_w_gate", "ff2_w_up", "ff2_w_down")


def kernel(x_prompt, x_sample, cache_sb_k, cache_sb_v, cache_diff_k, cache_diff_v, ff1_pre_g, ff1_w_gate, ff1_w_up, ff1_w_down, ff1_post_g, mix_pre_g, w_in, lam_q1, lam_k1, lam_q2, lam_k2, subln_g, w_out, mix_post_g, ff2_pre_g, ff2_w_gate, ff2_w_up, ff2_w_down, ff2_post_g, final_g):
    params = dict(ff1_pre_g=ff1_pre_g, ff1_w_gate=ff1_w_gate, ff1_w_up=ff1_w_up, ff1_w_down=ff1_w_down,
                  ff1_post_g=ff1_post_g, mix_pre_g=mix_pre_g, w_in=w_in, lam_q1=lam_q1, lam_k1=lam_k1,
                  lam_q2=lam_q2, lam_k2=lam_k2, subln_g=subln_g, w_out=w_out, mix_post_g=mix_post_g,
                  ff2_pre_g=ff2_pre_g, ff2_w_gate=ff2_w_gate, ff2_w_up=ff2_w_up, ff2_w_down=ff2_w_down,
                  ff2_post_g=ff2_post_g, final_g=final_g)
    yp, ys = x_prompt, x_sample
    rows_p, rows_s = [], []
    for l in range(DEPTH):
        lam_init = 0.8 - 0.6 * math.exp(-0.3 * l)
        w = {name: (p[l].astype(_BF16) if name in _MATRICES else p[l][None, :].astype(_F32))
             for name, p in params.items()}
        yp, rp = _encoder_layer(yp, None, w, lam_init)
        past = tuple(c[l].reshape(c.shape[1], c.shape[2], -1)
                     for c in (cache_sb_k, cache_sb_v, cache_diff_k, cache_diff_v))
        ys, rs = _encoder_layer(ys, past, w, lam_init)
        rows_p.append(rp)
        rows_s.append(rs)
    stacked_p = [jnp.stack(r, axis=0) for r in zip(*rows_p)]
    stacked_s = [jnp.stack(r, axis=0) for r in zip(*rows_s)]
    return (yp, ys, *stacked_p, *stacked_s)
```

```python
import functools
import math

import jax
import jax.numpy as jnp
from jax import lax
from jax.experimental import pallas as pl
from jax.experimental.pallas import tpu as pltpu

D_MODEL = 1024
DEPTH = 1
CHUNK = 64
CHUNK_SHIFT = 6
SB_HEADS = 8
SB_HEAD_DIM = 64
DIFF_HEADS = 4
DIFF_HEAD_DIM = 64
SB_WIDTH = SB_HEADS * SB_HEAD_DIM
DIFF_WIDTH = DIFF_HEADS * 2 * DIFF_HEAD_DIM
MIX_WIDTH = SB_WIDTH + DIFF_WIDTH
IN_WIDTH = 3 * SB_WIDTH + 3 * DIFF_WIDTH
D_FF = 2816
RMS_EPS = 1e-6

LANES = 128
MXU_WIDTH = 256
VMEM_LIMIT_BYTES = 56 * 1024 * 1024
F32_EXP_ZERO = 104.0
NEG_BIG = -1e30
LOG2E = math.log2(math.e)
DEPTH_SLOTS = 4
SB_PROMPT_TILE = 256
DIFF_PROMPT_TILE = 512

_F32 = jnp.float32
_BF16 = jnp.bfloat16


def _rms(x, g):
    return x * lax.rsqrt(jnp.mean(x * x, axis=-1, keepdims=True) + RMS_EPS) * g


def _softplus_tail(z):
    return jnp.log(1.0 + jnp.exp(-jnp.abs(z)))


def _dot(a, b):
    return jnp.dot(a, b, preferred_element_type=_F32)


def _dot_nt(a, b):
    return lax.dot_general(a, b, (((1,), (1,)), ((), ())), preferred_element_type=_F32)


def _ffn_kernel(*refs, final_norm):
    if final_norm:
        x_ref, pre_ref, wg_ref, wu_ref, wd_ref, post_ref, fin_ref, o_ref, xn_ref, acc_ref = refs
    else:
        x_ref, pre_ref, wg_ref, wu_ref, wd_ref, post_ref, o_ref, xn_ref, acc_ref = refs
    j = pl.program_id(1)

    @pl.when(j == 0)
    def _():
        xn_ref[...] = _rms(x_ref[...], pre_ref[...]).astype(_BF16)
        acc_ref[...] = jnp.zeros_like(acc_ref)

    xn = xn_ref[...]
    g = _dot(xn, wg_ref[...])
    u = _dot(xn, wu_ref[...])
    a = (g * jax.nn.sigmoid(g) * u).astype(_BF16)
    acc_ref[...] += _dot(a, wd_ref[...])

    @pl.when(j == pl.num_programs(1) - 1)
    def _():
        h = x_ref[...] + 0.5 * _rms(acc_ref[...], post_ref[...])
        if final_norm:
            h = _rms(h, fin_ref[...])
        o_ref[...] = h


def _ffn(x, pre_g, wg, wu, wd, post_g, final_g=None):
    m = x.shape[0]
    tm = min(512, m)
    tf = D_FF // 2
    assert m % tm == 0 and D_FF % tf == 0 and tf % LANES == 0
    row = pl.BlockSpec((tm, D_MODEL), lambda i, j: (i, 0))
    vec = pl.BlockSpec((1, D_MODEL), lambda i, j: (0, 0))
    w_in = pl.BlockSpec((D_MODEL, tf), lambda i, j: (0, j))
    w_out = pl.BlockSpec((tf, D_MODEL), lambda i, j: (j, 0))
    in_specs = [row, vec, w_in, w_in, w_out, vec]
    args = [x, pre_g, wg, wu, wd, post_g]
    if final_g is not None:
        in_specs.append(vec)
        args.append(final_g)
    return pl.pallas_call(
        functools.partial(_ffn_kernel, final_norm=final_g is not None),
        grid=(m // tm, D_FF // tf),
        in_specs=in_specs,
        out_specs=row,
        out_shape=jax.ShapeDtypeStruct((m, D_MODEL), _F32),
        scratch_shapes=[pltpu.VMEM((tm, D_MODEL), _BF16), pltpu.VMEM((tm, D_MODEL), _F32)],
        compiler_params=pltpu.CompilerParams(
            dimension_semantics=("parallel", "arbitrary"), vmem_limit_bytes=VMEM_LIMIT_BYTES),
        name="ffn_final" if final_g is not None else "ffn",
    )(*args)


def _qkv_kernel(h_ref, g_ref, w_ref, sbq_ref, sbk_ref, sbv_ref, dq_ref, dk_ref, dv_ref,
                sbk16_ref, sbv16_ref, dk16_ref, dv16_ref, *, v_transposed):
    hn = _rms(h_ref[...], g_ref[...]).astype(_BF16)

    def proj(idx):
        return _dot(hn, w_ref[:, idx * SB_WIDTH:(idx + 1) * SB_WIDTH])

    sbq_ref[...] = (proj(0) * (SB_HEAD_DIM ** -0.5)).astype(_BF16)
    dq_ref[...] = (proj(3) * (DIFF_HEAD_DIM ** -0.5 * LOG2E)).astype(_BF16)
    for idx, full_ref, half_ref in ((1, sbk_ref, sbk16_ref), (4, dk_ref, dk16_ref)):
        p = proj(idx)
        full_ref[...] = p
        half_ref[...] = p.astype(_BF16)
    for idx, full_ref, half_ref in ((2, sbv_ref, sbv16_ref), (5, dv_ref, dv16_ref)):
        p = proj(idx)
        full_ref[...] = p
        if v_transposed:
            half_ref[0] = p.T.astype(_BF16)
        else:
            half_ref[...] = p.astype(_BF16)


def _qkv(h, g, w, *, seq_len=None):
    assert SB_WIDTH == DIFF_WIDTH
    assert math.log2(SB_HEAD_DIM) % 2 == 0 and math.log2(DIFF_HEAD_DIM) % 2 == 0
    m = h.shape[0]
    tm = min(512, m)
    assert m % tm == 0
    row = pl.BlockSpec((tm, D_MODEL), lambda i: (i, 0))
    out = pl.BlockSpec((tm, SB_WIDTH), lambda i: (i, 0))
    f32 = jax.ShapeDtypeStruct((m, SB_WIDTH), _F32)
    b16 = jax.ShapeDtypeStruct((m, SB_WIDTH), _BF16)
    v_out, v16 = out, b16
    if seq_len is not None:
        assert seq_len % tm == 0
        tiles = seq_len // tm
        v_out = pl.BlockSpec((1, SB_WIDTH, tm), lambda i: (i // tiles, 0, i % tiles))
        v16 = jax.ShapeDtypeStruct((m // seq_len, SB_WIDTH, seq_len), _BF16)
    return pl.pallas_call(
        functools.partial(_qkv_kernel, v_transposed=seq_len is not None),
        grid=(m // tm,),
        in_specs=[row, pl.BlockSpec((1, D_MODEL), lambda i: (0, 0)),
                  pl.BlockSpec((D_MODEL, IN_WIDTH), lambda i: (0, 0))],
        out_specs=[out] * 7 + [v_out, out, v_out],
        out_shape=[b16, f32, f32, b16, f32, f32, b16, v16, b16, v16],
        compiler_params=pltpu.CompilerParams(
            dimension_semantics=("parallel",), vmem_limit_bytes=VMEM_LIMIT_BYTES),
        name="qkv",
    )(h, g, w)


def _stream_specs(n, past, width):
    new = pl.BlockSpec((1, n, width), lambda i: (i, 0, 0))
    cache = pl.BlockSpec((1, past, width), lambda i: (i, 0, 0))
    return new, cache


def _sb_step_kernel(q_ref, kn_ref, vn_ref, kc_ref, vc_ref, o_ref, acc_ref, c_ref, *, tk):
    n = q_ref.shape[1]
    past = kc_ref.shape[1]

    def later(t):
        return (lax.broadcasted_iota(jnp.int32, (t, t), 0)
                > lax.broadcasted_iota(jnp.int32, (t, t), 1)).astype(_BF16)

    later_new, later_tile = later(n), later(tk)
    newer = (lax.broadcasted_iota(jnp.int32, (n, n), 1) < lax.broadcasted_iota(jnp.int32, (n, n), 0))

    def walk(qh, k, v, c, later_t, mask):
        z = _dot_nt(qh, k)
        sp = _softplus_tail(z)
        log_beta = jnp.minimum(z, 0.0) - sp
        log_stay = -jnp.maximum(z, 0.0) - sp
        if mask is not None:
            log_stay = jnp.where(mask, log_stay, 0.0)
        hi = log_stay.astype(_BF16)
        lo = (log_stay - hi.astype(_F32)).astype(_BF16)
        sums = _dot(jnp.concatenate([hi, lo], axis=0), later_t)
        w = jnp.exp(log_beta + (sums[:n] + sums[n:] + c))
        if mask is not None:
            w = jnp.where(mask, w, 0.0)
        return c + jnp.sum(log_stay, axis=-1, keepdims=True), _dot(w.astype(_BF16), v)

    lane = lax.broadcasted_iota(jnp.int32, (1, LANES), 1)
    pairs = [slice(g * LANES, (g + 1) * LANES) for g in range(SB_WIDTH // LANES)]

    def head_q(head):
        q_pair = q_ref[0, :, pairs[head // 2]].astype(_F32)
        in_head = (lane >= SB_HEAD_DIM) if head % 2 else (lane < SB_HEAD_DIM)
        return jnp.where(in_head, q_pair, 0.0).astype(_BF16)

    def walk_cache(start, c_of, first):
        for g, cols in enumerate(pairs):
            k = kc_ref[0, pl.ds(start, tk), cols].astype(_BF16)
            v = vc_ref[0, pl.ds(start, tk), cols].astype(_BF16)
            for head in (2 * g, 2 * g + 1):
                c, o = walk(head_q(head), k, v, c_of(head), later_tile, None)
                c_ref[head] = c
                acc_ref[head] = o + first[head] if first else acc_ref[head] + o

    new = [walk(head_q(head), kn_ref[0, :, pairs[head // 2]], vn_ref[0, :, pairs[head // 2]],
                jnp.zeros((n, 1), _F32), later_new, newer) for head in range(SB_HEADS)]
    walk_cache(past - tk, lambda head: new[head][0], [o for _, o in new])

    def cond(state):
        j, c_max = state
        return (j >= 0) & (c_max > -F32_EXP_ZERO)

    def body(state):
        j, _ = state
        walk_cache(pl.multiple_of(j * tk, tk), lambda head: c_ref[head], None)
        return j - 1, jnp.max(c_ref[...])

    lax.while_loop(cond, body, (past // tk - 2, jnp.max(c_ref[...])))
    o_ref[0] = jnp.concatenate(
        [jnp.where(lane < SB_HEAD_DIM, acc_ref[2 * g], acc_ref[2 * g + 1]) for g in range(len(pairs))],
        axis=-1).astype(o_ref.dtype)


def _sb_step_call(q, k_new, v_new, k_cache, v_cache):
    b, n, _ = q.shape
    past = k_cache.shape[1]
    tk = min(256, past)
    assert past % tk == 0 and LANES == 2 * SB_HEAD_DIM
    new, cache = _stream_specs(n, past, SB_WIDTH)
    return pl.pallas_call(
        functools.partial(_sb_step_kernel, tk=tk),
        grid=(b,),
        in_specs=[new, new, new, cache, cache],
        out_specs=new,
        out_shape=jax.ShapeDtypeStruct((b, n, SB_WIDTH), _BF16),
        scratch_shapes=[pltpu.VMEM((SB_HEADS, n, LANES), _F32), pltpu.VMEM((SB_HEADS, n, 1), _F32)],
        compiler_params=pltpu.CompilerParams(
            dimension_semantics=("arbitrary",), vmem_limit_bytes=VMEM_LIMIT_BYTES),
        name="sb_step",
    )(q, k_new, v_new, k_cache, v_cache)


def _alibi_slope_log2(head):
    assert 8 % DIFF_HEADS == 0
    exponent = (8 // DIFF_HEADS) * (head + 1)
    slope = lax.bitcast_convert_type(jnp.full((1, 1), (127 - exponent) << 23, jnp.int32), _F32)
    return slope * LOG2E


def _diff_step_kernel(q_ref, kn_ref, vn_ref, kc_ref, vc_ref, lq1_ref, lk1_ref, lq2_ref, lk2_ref, g_ref,
                      o_ref, *, lam_init):
    n = q_ref.shape[1]
    past = kc_ref.shape[1]
    lane = lax.broadcasted_iota(jnp.int32, (1, LANES), 1)
    q_idx_c = lax.broadcasted_iota(jnp.int32, (2 * n, past), 0) & (n - 1)
    k_idx_c = lax.broadcasted_iota(jnp.int32, (2 * n, past), 1)
    q_idx_n = lax.broadcasted_iota(jnp.int32, (2 * n, n), 0) & (n - 1)
    k_idx_n = lax.broadcasted_iota(jnp.int32, (2 * n, n), 1)
    distance_c = (past + q_idx_c - k_idx_c).astype(_F32)
    distance_n = jnp.abs(q_idx_n - k_idx_n).astype(_F32)
    visible_n =(lax.shift_right_logical(past + k_idx_n, CHUNK_SHIFT)
                 <= lax.shift_right_logical(past + q_idx_n, CHUNK_SHIFT))
    lam = (jnp.exp(jnp.sum(lq1_ref[...] * lk1_ref[...], axis=-1, keepdims=True))
           - jnp.exp(jnp.sum(lq2_ref[...] * lk2_ref[...], axis=-1, keepdims=True)) + lam_init)
    for head in range(DIFF_HEADS):
        cols = slice(head * LANES, (head + 1) * LANES)
        slope = _alibi_slope_log2(head)
        q = q_ref[0, :, cols].astype(_F32)
        q_maps = jnp.concatenate([jnp.where(lane < DIFF_HEAD_DIM, q, 0.0),
                                  jnp.where(lane >= DIFF_HEAD_DIM, q, 0.0)], axis=0).astype(_BF16)
        s_c = _dot_nt(q_maps, kc_ref[0, :, cols].astype(_BF16)) - slope * distance_c
        s_n = jnp.where(visible_n, _dot_nt(q_maps, kn_ref[0, :, cols]) - slope * distance_n, -jnp.inf)
        m = jnp.maximum(jnp.max(s_c, axis=-1, keepdims=True), jnp.max(s_n, axis=-1, keepdims=True))
        p_c = jnp.exp2(s_c - m)
        p_n = jnp.exp2(s_n - m)
        l = jnp.sum(p_c, axis=-1, keepdims=True) + jnp.sum(p_n, axis=-1, keepdims=True)
        o = (_dot(p_c.astype(_BF16), vc_ref[0, :, cols].astype(_BF16))
             + _dot(p_n.astype(_BF16), vn_ref[0, :, cols])) / l
        o = o[:n] - lam * o[n:]
        o_ref[0, :, cols] = (_rms(o, g_ref[...]) * (1.0 - lam_init)).astype(o_ref.dtype)


def _diff_step_call(q, k_new, v_new, k_cache, v_cache, lq1, lk1, lq2, lk2, subln_g, *, lam_init):
    b, n, _ = q.shape
    past = k_cache.shape[1]
    assert 2 * DIFF_HEAD_DIM == LANES and CHUNK == 1 << CHUNK_SHIFT and n & (n - 1) == 0
    new, cache = _stream_specs(n, past, DIFF_WIDTH)
    lam_spec = pl.BlockSpec((1, DIFF_HEAD_DIM), lambda i: (0, 0))
    return pl.pallas_call(
        functools.partial(_diff_step_kernel, lam_init=lam_init),
        grid=(b,),
        in_specs=[new, new, new, cache, cache, lam_spec, lam_spec, lam_spec, lam_spec,
                  pl.BlockSpec((1, LANES), lambda i: (0, 0))],
        out_specs=new,
        out_shape=jax.ShapeDtypeStruct((b, n, DIFF_WIDTH), _BF16),
        compiler_params=pltpu.CompilerParams(
            dimension_semantics=("arbitrary",), vmem_limit_bytes=VMEM_LIMIT_BYTES),
        name="diff_step",
    )(q, k_new, v_new, k_cache, v_cache, lq1, lk1, lq2, lk2, subln_g)


def _prompt_specs(nq, t):
    assert nq % t == 0 and t & (t - 1) == 0
    q_spec = pl.BlockSpec((1, t, LANES), lambda bi, g, qi: (bi, qi, g))
    k_spec = pl.BlockSpec((1, nq, LANES), lambda bi, g, qi: (bi, 0, g))
    vt_spec = pl.BlockSpec((1, LANES, nq), lambda bi, g, qi: (bi, g, 0))
    return q_spec, k_spec, vt_spec


def _sb_prompt_kernel(q_ref, k_ref, vt_ref, o_ref, acc_ref, c_ref, z_ref, lb_ref, hl_ref, w_ref, *, t):
    assert LANES == 2 * SB_HEAD_DIM
    strips = 2 * t // LANES
    qi = pl.program_id(2)
    lane = lax.broadcasted_iota(jnp.int32, (1, LANES), 1)
    key = lax.broadcasted_iota(jnp.int32, (t, LANES), 0)
    qry = lax.broadcasted_iota(jnp.int32, (t, LANES), 1)
    later = (lax.broadcasted_iota(jnp.int32, (t, t), 1)
             > lax.broadcasted_iota(jnp.int32, (t, t), 0)).astype(_BF16)
    q = q_ref[0].astype(_F32)
    q_heads = jnp.concatenate([jnp.where(lane < SB_HEAD_DIM, q, 0.0),
                               jnp.where(lane >= SB_HEAD_DIM, q, 0.0)], axis=0).astype(_BF16)

    def walk_tile(j, c, own, buf):
        k0 = pl.multiple_of(j * t, t)
        z = _dot_nt(k_ref[0, pl.ds(k0, t), :], q_heads)
        for s in range(strips):
            z_ref[buf, s] = z[:, s * LANES:(s + 1) * LANES]
        masks, c_new = [], []
        for s in range(strips):
            cols = slice(s * LANES, (s + 1) * LANES)
            zs = z_ref[buf, s]
            sp = _softplus_tail(zs)
            log_stay = -jnp.maximum(zs, 0.0) - sp
            masks.append(key < ((s * LANES + qry) & (t - 1)) if own else None)
            if own:
                log_stay = jnp.where(masks[s], log_stay, 0.0)
            lb_ref[buf, s] = jnp.minimum(zs, 0.0) - sp
            hi = log_stay.astype(_BF16)
            lo = (log_stay - hi.astype(_F32)).astype(_BF16)
            hl_ref[buf, s] = jnp.concatenate([hi, lo], axis=1)
            c_new.append(c[:, cols] + jnp.sum(log_stay, axis=0, keepdims=True))
        for s in range(strips):
            cols = slice(s * LANES, (s + 1) * LANES)
            sums = _dot(later, hl_ref[buf, s])
            w = jnp.exp(lb_ref[buf, s] + (sums[:, :LANES] + sums[:, LANES:] + c[:, cols]))
            if own:
                w = jnp.where(masks[s], w, 0.0)
            head, part = divmod(s, t // LANES)
            w_ref[buf, head, :, part * LANES:(part + 1) * LANES] = w.astype(_BF16)
        vt = vt_ref[0, :, pl.ds(k0, t)]
        o = jnp.concatenate([_dot(vt[:SB_HEAD_DIM], w_ref[buf, 0]), _dot(vt[SB_HEAD_DIM:], w_ref[buf, 1])],
                            axis=0)
        return jnp.concatenate(c_new, axis=1), o

    c0 = jnp.zeros((1, 2 * t), _F32)

    @pl.when(qi == 0)
    def _():
        c_ref[...], acc_ref[...] = walk_tile(0, c0, True, 0)

    @pl.when(qi > 0)
    def _():
        c1, o_own = walk_tile(qi, c0, True, 0)
        c2, o_prev = walk_tile(qi - 1, c1, False, 1)
        c_ref[...] = c2
        acc_ref[...] = o_own + o_prev

        def cond(state):
            j, c_max = state
            return (j >= 0) & (c_max > -F32_EXP_ZERO)

        def body(state):
            j, _ = state
            c, o = walk_tile(j, c_ref[...], False, 0)
            c_ref[...] = c
            acc_ref[...] += o
            return j - 1, jnp.max(c)

        lax.while_loop(cond, body, (qi - 2, jnp.max(c2)))

    o_ref[0] = acc_ref[...].T.astype(o_ref.dtype)


def _sb_prompt_call(q, k, vt, *, t):
    b, nq, _ = q.shape
    q_spec, k_spec, vt_spec = _prompt_specs(nq, t)
    return pl.pallas_call(
        functools.partial(_sb_prompt_kernel, t=t),
        grid=(b, SB_WIDTH // LANES, nq // t),
        in_specs=[q_spec, k_spec, vt_spec],
        out_specs=q_spec,
        out_shape=jax.ShapeDtypeStruct((b, nq, SB_WIDTH), _BF16),
        scratch_shapes=[pltpu.VMEM((LANES, t), _F32), pltpu.VMEM((1, 2 * t), _F32),
                        pltpu.VMEM((2, 2 * t // LANES, t, LANES), _F32),
                        pltpu.VMEM((2, 2 * t // LANES, t, LANES), _F32),
                        pltpu.VMEM((2, 2 * t // LANES, t, 2 * LANES), _BF16),
                        pltpu.VMEM((2, 2, t, t), _BF16)],
        compiler_params=pltpu.CompilerParams(
            dimension_semantics=("parallel", "parallel", "arbitrary"),
            vmem_limit_bytes=VMEM_LIMIT_BYTES),
        name="sb_prompt",
    )(q, k, vt)


def _diff_prompt_kernel(q_ref, k_ref, vt_ref, lq1_ref, lk1_ref, lq2_ref, lk2_ref, g_ref, o_ref,
                        m_ref, l_ref, acc_ref, pen_ref, kpos_ref, s_ref, p_ref, a_ref,
                        *, tq, tk, lam_init):
    own = 2
    assert tq == own * tk
    head = pl.program_id(1)
    qi = pl.program_id(2)
    n_past = qi * own
    lane = lax.broadcasted_iota(jnp.int32, (1, LANES), 1)
    slope = _alibi_slope_log2(head)

    @pl.when(qi == 0)
    def _():
        key = lax.broadcasted_iota(jnp.int32, (tk, 2 * tq), 0)
        qry = lax.broadcasted_iota(jnp.int32, (tk, 2 * tq), 1) & (tq - 1)
        for d in range(own):
            k_pos = d * tk + key
            visible = lax.shift_right_logical(k_pos, CHUNK_SHIFT) <= lax.shift_right_logical(qry, CHUNK_SHIFT)
            bias = slope * (qry - key - jnp.abs(qry - k_pos)).astype(_F32)
            pen_ref[d] = jnp.where(visible, bias, -jnp.inf)
        row = lax.broadcasted_iota(jnp.int32, (tk, LANES), 0)
        col = lax.broadcasted_iota(jnp.int32, (tk, LANES), 1)
        kpos_ref[...] = jnp.where(col < 3, row, 0).astype(_F32).astype(_BF16)

    slope_hi = slope.astype(_BF16).astype(_F32)
    slope_mid = (slope - slope_hi).astype(_BF16).astype(_F32)
    slope_lo = slope - slope_hi - slope_mid
    slope_cols = jnp.where(lane == 0, slope_hi, jnp.where(lane == 1, slope_mid,
                                                          jnp.where(lane == 2, slope_lo, 0.0)))
    slope_cols = jnp.broadcast_to(slope_cols, (tq, LANES))
    q = q_ref[0].astype(_F32)
    q_maps = jnp.concatenate(
        [jnp.concatenate([jnp.where(lane < DIFF_HEAD_DIM, q, 0.0), slope_cols], axis=1),
         jnp.concatenate([jnp.where(lane >= DIFF_HEAD_DIM, q, 0.0), slope_cols], axis=1)],
        axis=0).astype(_BF16)
    m_ref[...] = jnp.full_like(m_ref, NEG_BIG)
    l_ref[...] = jnp.zeros_like(l_ref)
    acc_ref[...] = jnp.zeros_like(acc_ref)

    def slot_tile(n):
        return jnp.where(n < own, n_past + n, n_past - 1 - (n - own))

    def slot_shift(n):
        j = slot_tile(n)
        distance = (qi * tq - j * tk).astype(_F32)
        return jnp.where(n < own, 0.0, jnp.where(j >= 0, -slope * distance, NEG_BIG))

    def tile_start(n):
        return pl.multiple_of(jnp.clip(slot_tile(n), 0, n_past + own - 1) * tk, tk)

    def scores(n):
        keys = jnp.concatenate([k_ref[0, pl.ds(tile_start(n), tk), :], kpos_ref[...]], axis=1)
        return _dot_nt(keys, q_maps)

    def put_scores(buf, s):
        for c in range(2 * tq // LANES):
            s_ref[buf, c] = s[:, c * LANES:(c + 1) * LANES]

    def softmax(buf, shift):
        for c in range(2 * tq // LANES):
            strip = slice(c * LANES, (c + 1) * LANES)
            s = s_ref[buf, c]
            m_old = m_ref[:, strip]
            m_new = jnp.maximum(m_old, jnp.max(s, axis=0, keepdims=True) + shift)
            alpha = jnp.exp2(m_old - m_new)
            p = jnp.exp2(s - (m_new - shift))
            l_ref[:, strip] = alpha * l_ref[:, strip] + jnp.sum(p, axis=0, keepdims=True)
            m_ref[:, strip] = m_new
            a_ref[buf, :, strip] = alpha
            half = c % (MXU_WIDTH // LANES)
            p_ref[buf, c // (MXU_WIDTH // LANES), :, half * LANES:(half + 1) * LANES] = p.astype(_BF16)

    def weigh(n, buf):
        values = vt_ref[0, :, pl.ds(tile_start(n), tk)]
        for c in range(2 * tq // MXU_WIDTH):
            strip = slice(c * MXU_WIDTH, (c + 1) * MXU_WIDTH)
            acc_ref[c] = a_ref[buf, :, strip] * acc_ref[c] + _dot(values, p_ref[buf, c])

    for n in range(own):
        put_scores(n, scores(n) + pen_ref[n])
    for n in range(own, DEPTH_SLOTS):
        p_ref[n] = jnp.zeros(p_ref.shape[1:], _BF16)
        a_ref[n] = jnp.ones((1, 2 * tq), _F32)

    def stage_pair(n, buf):
        for d in range(2):
            put_scores((buf + 2 + d) % DEPTH_SLOTS, scores(n + 2 + d))
        for d in range(2):
            softmax(buf + d, slot_shift(n + d))
        for d in range(2):
            weigh(n - 2 + d, (buf + 2 + d) % DEPTH_SLOTS)

    def four_slots(i, carry):
        stage_pair(4 * i, 0)
        stage_pair(4 * i + 2, 2)
        return carry

    trips = (qi + 2) // 2
    lax.fori_loop(0, trips, four_slots, 0)
    for d in range(2):
        weigh(4 * trips - 2 + d, 2 + d)

    lam = (jnp.exp(jnp.sum(lq1_ref[...] * lk1_ref[...], axis=-1, keepdims=True))
           - jnp.exp(jnp.sum(lq2_ref[...] * lk2_ref[...], axis=-1, keepdims=True)) + lam_init)
    o = jnp.concatenate([acc_ref[c] for c in range(2 * tq // MXU_WIDTH)], axis=1) / l_ref[...]
    o = o[:, :tq] - lam * o[:, tq:]
    o = o * lax.rsqrt(jnp.mean(o * o, axis=0, keepdims=True) + RMS_EPS)
    o_ref[0] = (o.T * g_ref[...] * (1.0 - lam_init)).astype(o_ref.dtype)


def _diff_prompt_call(q, k, vt, lq1, lk1, lq2, lk2, subln_g, *, tq, lam_init):
    b, nq, _ = q.shape
    tk = tq // 2
    assert 2 * DIFF_HEAD_DIM == LANES
    assert tk % CHUNK == 0 and CHUNK == 1 << CHUNK_SHIFT and tk <= 256
    q_spec, k_spec, vt_spec = _prompt_specs(nq, tq)
    lam_spec = pl.BlockSpec((1, DIFF_HEAD_DIM), lambda bi, g, qi: (0, 0))
    stat = pltpu.VMEM((1, 2 * tq), _F32)
    return pl.pallas_call(
        functools.partial(_diff_prompt_kernel, tq=tq, tk=tk, lam_init=lam_init),
        grid=(b, DIFF_HEADS, nq // tq),
        in_specs=[q_spec, k_spec, vt_spec, lam_spec, lam_spec, lam_spec, lam_spec,
                  pl.BlockSpec((1, LANES), lambda bi, g, qi: (0, 0))],
        out_specs=q_spec,
        out_shape=jax.ShapeDtypeStruct((b, nq, DIFF_WIDTH), _BF16),
        scratch_shapes=[stat, stat, pltpu.VMEM((2 * tq // MXU_WIDTH, LANES, MXU_WIDTH), _F32),
                        pltpu.VMEM((2, tk, 2 * tq), _F32), pltpu.VMEM((tk, LANES), _BF16),
                        pltpu.VMEM((DEPTH_SLOTS, 2 * tq // LANES, tk, LANES), _F32),
                        pltpu.VMEM((DEPTH_SLOTS, 2 * tq // MXU_WIDTH, tk, MXU_WIDTH), _BF16),
                        pltpu.VMEM((DEPTH_SLOTS, 1, 2 * tq), _F32)],
        compiler_params=pltpu.CompilerParams(
            dimension_semantics=("arbitrary", "arbitrary", "arbitrary"),
            vmem_limit_bytes=VMEM_LIMIT_BYTES),
        name="diff_prompt",
    )(q, k, vt, lq1, lk1, lq2, lk2, subln_g)


def _mix_out_kernel(sbo_ref, do_ref, h_ref, w_ref, g_ref, o_ref):
    mix = _dot(sbo_ref[...], w_ref[:SB_WIDTH, :]) + _dot(do_ref[...], w_ref[SB_WIDTH:, :])
    o_ref[...] = h_ref[...] + _rms(mix, g_ref[...])


def _mix_out(sb_o, d_o, h, w, g):
    m = h.shape[0]
    tm = min(512, m)
    assert m % tm == 0
    half = pl.BlockSpec((tm, SB_WIDTH), lambda i: (i, 0))
    row = pl.BlockSpec((tm, D_MODEL), lambda i: (i, 0))
    return pl.pallas_call(
        _mix_out_kernel,
        grid=(m // tm,),
        in_specs=[half, half, row, pl.BlockSpec((MIX_WIDTH, D_MODEL), lambda i: (0, 0)),
                  pl.BlockSpec((1, D_MODEL), lambda i: (0, 0))],
        out_specs=row,
        out_shape=jax.ShapeDtypeStruct((m, D_MODEL), _F32),
        compiler_params=pltpu.CompilerParams(
            dimension_semantics=("parallel",), vmem_limit_bytes=VMEM_LIMIT_BYTES),
        name="mix_out",
    )(sb_o, d_o, h, w, g)


def _encoder_layer(x, past, w, lam_init):
    b, n, _ = x.shape
    m = b * n
    h = _ffn(x.reshape(m, D_MODEL), w["ff1_pre_g"], w["ff1_w_gate"], w["ff1_w_up"], w["ff1_w_down"],
             w["ff1_post_g"])
    lam_args = (w["lam_q1"], w["lam_k1"], w["lam_q2"], w["lam_k2"], w["subln_g"])

    def seq(a):
        return a.reshape(b, n, a.shape[-1])

    if past is None:
        sbq, sbk, sbv, dq, dk, dv, sbk16, sbvt16, dk16, dvt16 = _qkv(h, w["mix_pre_g"], w["w_in"], seq_len=n)
        sb_o = _sb_prompt_call(seq(sbq), seq(sbk16), sbvt16, t=SB_PROMPT_TILE)
        d_o = _diff_prompt_call(seq(dq), seq(dk16), dvt16, *lam_args, tq=DIFF_PROMPT_TILE, lam_init=lam_init)
    else:
        sbq, sbk, sbv, dq, dk, dv, sbk16, sbv16, dk16, dv16 = _qkv(h, w["mix_pre_g"], w["w_in"])
        sb_o = _sb_step_call(seq(sbq), seq(sbk16), seq(sbv16), past[0], past[1])
        d_o = _diff_step_call(seq(dq), seq(dk16), seq(dv16), past[2], past[3], *lam_args, lam_init=lam_init)
    h = _mix_out(sb_o.reshape(m, SB_WIDTH), d_o.reshape(m, DIFF_WIDTH), h, w["w_out"], w["mix_post_g"])
    y = _ffn(h, w["ff2_pre_g"], w["ff2_w_gate"], w["ff2_w_up"], w["ff2_w_down"], w["ff2_post_g"],
             w["final_g"])
    rows = (sbk.reshape(b, n, SB_HEADS, SB_HEAD_DIM), sbv.reshape(b, n, SB_HEADS, SB_HEAD_DIM),
            dk.reshape(b, n, DIFF_HEADS, 2 * DIFF_HEAD_DIM), dv.reshape(b, n, DIFF_HEADS, 2 * DIFF_HEAD_DIM))
    return y.reshape(b, n, D_MODEL), rows


_MATRICES = ("ff1_w_gate", "ff1_w_up", "ff1_w_down", "w_in", "w_out", "ff2_w_gate", "ff2_w_up", "ff2_w_down")


def kernel(x_prompt, x_sample, cache_sb_k, cache_sb_v, cache_diff_k, cache_diff_v, ff1_pre_g, ff1_w_gate, ff1_w_up, ff1_w_down, ff1_post_g, mix_pre_g, w_in, lam_q1, lam_k1, lam_q2, lam_k2, subln_g, w_out, mix_post_g, ff2_pre_g, ff2_w_gate, ff2_w_up, ff2_w_down, ff2_post_g, final_g):
    params = dict(ff1_pre_g=ff1_pre_g, ff1_w_gate=ff1_w_gate, ff1_w_up=ff1_w_up, ff1_w_down=ff1_w_down,
                  ff1_post_g=ff1_post_g, mix_pre_g=mix_pre_g, w_in=w_in, lam_q1=lam_q1, lam_k1=lam_k1,
                  lam_q2=lam_q2, lam_k2=lam_k2, subln_g=subln_g, w_out=w_out, mix_post_g=mix_post_g,
                  ff2_pre_g=ff2_pre_g, ff2_w_gate=ff2_w_gate, ff2_w_up=ff2_w_up, ff2_w_down=ff2_w_down,
                  ff2_post_g=ff2_post_g, final_g=final_g)
    yp, ys = x_prompt, x_sample
    rows_p, rows_s = [], []
    for l in range(DEPTH):
        lam_init = 0.8 - 0.6 * math.exp(-0.3 * l)
        w = {name: (p[l].astype(_BF16) if name in _MATRICES else p[l][None, :].astype(_F32))
             for name, p in params.items()}
        yp, rp = _encoder_layer(yp, None, w, lam_init)
        past = tuple(c[l].reshape(c.shape[1], c.shape[2], -1)
                     for c in (cache_sb_k, cache_sb_v, cache_diff_k, cache_diff_v))
        ys, rs = _encoder_layer(ys, past, w, lam_init)
        rows_p.append(rp)
        rows_s.append(rs)
    stacked_p = [jnp.stack(r, axis=0) for r in zip(*rows_p)]
    stacked_s = [jnp.stack(r, axis=0) for r in zip(*rows_s)]
    return (yp, ys, *stacked_p, *stacked_s)
```

```python
import functools
import math

import jax
import jax.numpy as jnp
from jax import lax
from jax.experimental import pallas as pl
from jax.experimental.pallas import tpu as pltpu

D_MODEL = 1024
DEPTH = 1
CHUNK = 64
CHUNK_SHIFT = 6
SB_HEADS = 8
SB_HEAD_DIM = 64
DIFF_HEADS = 4
DIFF_HEAD_DIM = 64
SB_WIDTH = SB_HEADS * SB_HEAD_DIM
DIFF_WIDTH = DIFF_HEADS * 2 * DIFF_HEAD_DIM
MIX_WIDTH = SB_WIDTH + DIFF_WIDTH
IN_WIDTH = 3 * SB_WIDTH + 3 * DIFF_WIDTH
D_FF = 2816
RMS_EPS = 1e-6

LANES = 128
MXU_WIDTH = 256
VMEM_LIMIT_BYTES = 56 * 1024 * 1024
F32_EXP2_ZERO = 151.0
NEG_BIG = -1e30
LOG2E = math.log2(math.e)
DEPTH_SLOTS = 4
SB_PROMPT_TILE = 256
SB_PROMPT_GROUP = 2
DIFF_PROMPT_TILE = 512

_F32 = jnp.float32
_BF16 = jnp.bfloat16


def _rms(x, g):
    return x * lax.rsqrt(jnp.mean(x * x, axis=-1, keepdims=True) + RMS_EPS) * g


def _stick_logs(z):
    log_beta = jnp.minimum(z, 0.0) - jnp.log2(1.0 + jnp.exp2(-jnp.abs(z)))
    return log_beta, log_beta - z


def _split_hi_lo(x):
    hi = lax.bitcast_convert_type(lax.bitcast_convert_type(x, jnp.uint32) & jnp.uint32(0xFFFF0000), _F32)
    return hi.astype(_BF16), (x - hi).astype(_BF16)


def _dot(a, b):
    return jnp.dot(a, b, preferred_element_type=_F32)


def _dot_nt(a, b):
    return lax.dot_general(a, b, (((1,), (1,)), ((), ())), preferred_element_type=_F32)


def _ffn_kernel(*refs, mixer, final_norm):
    refs = list(refs)
    o_ref = refs.pop()
    x = refs.pop(0)[...]
    if mixer:
        sbo_ref, do_ref, wo_ref, mix_g_ref = refs[:4]
        refs = refs[4:]
        mix = _dot(sbo_ref[...], wo_ref[:SB_WIDTH, :]) + _dot(do_ref[...], wo_ref[SB_WIDTH:, :])
        x = x + _rms(mix, mix_g_ref[...])
    pre_ref, wg_ref, wu_ref, wd_ref, post_ref = refs[:5]
    xn = _rms(x, pre_ref[...]).astype(_BF16)
    g = _dot(xn, wg_ref[...])
    u = _dot(xn, wu_ref[...])
    a = (g * jax.nn.sigmoid(g) * u).astype(_BF16)
    h = x + 0.5 * _rms(_dot(a, wd_ref[...]), post_ref[...])
    if final_norm:
        h = _rms(h, refs[5][...])
    o_ref[...] = h


def _ffn(x, pre_g, wg, wu, wd, post_g, *, mixer=None, final_g=None):
    m = x.shape[0]
    tm = min(512, m)
    assert m % tm == 0

    def resident(shape):
        return pl.BlockSpec(shape, lambda i: (0, 0), pipeline_mode=pl.Buffered(1))

    row = pl.BlockSpec((tm, D_MODEL), lambda i: (i, 0))
    vec = resident((1, D_MODEL))
    in_specs, args = [row], [x]
    if mixer is not None:
        half = pl.BlockSpec((tm, SB_WIDTH), lambda i: (i, 0))
        in_specs += [half, half, resident((MIX_WIDTH, D_MODEL)), vec]
        args += list(mixer)
    in_specs += [vec, resident((D_MODEL, D_FF)), resident((D_MODEL, D_FF)), resident((D_FF, D_MODEL)), vec]
    args += [pre_g, wg, wu, wd, post_g]
    if final_g is not None:
        in_specs.append(vec)
        args.append(final_g)
    return pl.pallas_call(
        functools.partial(_ffn_kernel, mixer=mixer is not None, final_norm=final_g is not None),
        grid=(m // tm,),
        in_specs=in_specs,
        out_specs=row,
        out_shape=jax.ShapeDtypeStruct((m, D_MODEL), _F32),
        compiler_params=pltpu.CompilerParams(
            dimension_semantics=("parallel",), vmem_limit_bytes=VMEM_LIMIT_BYTES),
        name="ffn_mix_final" if mixer is not None else "ffn",
    )(*args)


def _qkv_kernel(h_ref, g_ref, w_ref, sbq_ref, sbk_ref, sbv_ref, dq_ref, dk_ref, dv_ref,
                sbk16_ref, sbv16_ref, dk16_ref, dv16_ref, *, v_transposed):
    hn = _rms(h_ref[...], g_ref[...]).astype(_BF16)

    def proj(idx):
        return _dot(hn, w_ref[:, idx * SB_WIDTH:(idx + 1) * SB_WIDTH])

    sbq_ref[...] = (proj(0) * (SB_HEAD_DIM ** -0.5 * LOG2E)).astype(_BF16)
    dq_ref[...] = (proj(3) * (DIFF_HEAD_DIM ** -0.5 * LOG2E)).astype(_BF16)
    for idx, full_ref, half_ref in ((1, sbk_ref, sbk16_ref), (4, dk_ref, dk16_ref)):
        p = proj(idx)
        full_ref[...] = p
        half_ref[...] = p.astype(_BF16)
    for idx, full_ref, half_ref in ((2, sbv_ref, sbv16_ref), (5, dv_ref, dv16_ref)):
        p = proj(idx)
        full_ref[...] = p
        if v_transposed:
            half_ref[0] = p.T.astype(_BF16)
        else:
            half_ref[...] = p.astype(_BF16)


def _qkv(h, g, w, *, seq_len=None):
    assert SB_WIDTH == DIFF_WIDTH
    assert math.log2(SB_HEAD_DIM) % 2 == 0 and math.log2(DIFF_HEAD_DIM) % 2 == 0
    m = h.shape[0]
    tm = min(512, m)
    assert m % tm == 0
    row = pl.BlockSpec((tm, D_MODEL), lambda i: (i, 0))
    out = pl.BlockSpec((tm, SB_WIDTH), lambda i: (i, 0))
    f32 = jax.ShapeDtypeStruct((m, SB_WIDTH), _F32)
    b16 = jax.ShapeDtypeStruct((m, SB_WIDTH), _BF16)
    v_out, v16 = out, b16
    if seq_len is not None:
        assert seq_len % tm == 0
        tiles = seq_len // tm
        v_out = pl.BlockSpec((1, SB_WIDTH, tm), lambda i: (i // tiles, 0, i % tiles))
        v16 = jax.ShapeDtypeStruct((m // seq_len, SB_WIDTH, seq_len), _BF16)
    return pl.pallas_call(
        functools.partial(_qkv_kernel, v_transposed=seq_len is not None),
        grid=(m // tm,),
        in_specs=[row, pl.BlockSpec((1, D_MODEL), lambda i: (0, 0)),
                  pl.BlockSpec((D_MODEL, IN_WIDTH), lambda i: (0, 0))],
        out_specs=[out] * 7 + [v_out, out, v_out],
        out_shape=[b16, f32, f32, b16, f32, f32, b16, v16, b16, v16],
        compiler_params=pltpu.CompilerParams(
            dimension_semantics=("parallel",), vmem_limit_bytes=VMEM_LIMIT_BYTES),
        name="qkv",
    )(h, g, w)


def _stream_specs(n, past, width):
    new = pl.BlockSpec((1, n, width), lambda i: (i, 0, 0))
    cache = pl.BlockSpec((1, past, width), lambda i: (i, 0, 0))
    return new, cache


def _sb_step_kernel(q_ref, kn_ref, vn_ref, kc_ref, vc_ref, o_ref, acc_ref, c_ref, *, tk):
    n = q_ref.shape[1]
    past = kc_ref.shape[1]

    def later(t):
        return (lax.broadcasted_iota(jnp.int32, (t, t), 0)
                > lax.broadcasted_iota(jnp.int32, (t, t), 1)).astype(_BF16)

    later_new, later_tile = later(n), later(tk)
    newer = (lax.broadcasted_iota(jnp.int32, (n, n), 1) < lax.broadcasted_iota(jnp.int32, (n, n), 0))

    def walk(qh, k, v, c, later_t, mask):
        z = _dot_nt(qh, k)
        log_beta, log_stay = _stick_logs(z)
        if mask is not None:
            log_stay = jnp.where(mask, log_stay, 0.0)
        sums = _dot(jnp.concatenate(_split_hi_lo(log_stay), axis=0), later_t)
        w = jnp.exp2(log_beta + (sums[:n] + sums[n:] + c))
        if mask is not None:
            w = jnp.where(mask, w, 0.0)
        return c + jnp.sum(log_stay, axis=-1, keepdims=True), _dot(w.astype(_BF16), v)

    lane = lax.broadcasted_iota(jnp.int32, (1, LANES), 1)
    pairs = [slice(g * LANES, (g + 1) * LANES) for g in range(SB_WIDTH // LANES)]

    def head_q(head):
        q_pair = q_ref[0, :, pairs[head // 2]].astype(_F32)
        in_head = (lane >= SB_HEAD_DIM) if head % 2 else (lane < SB_HEAD_DIM)
        return jnp.where(in_head, q_pair, 0.0).astype(_BF16)

    def walk_cache(start, c_of, first):
        for g, cols in enumerate(pairs):
            k = kc_ref[0, pl.ds(start, tk), cols].astype(_BF16)
            v = vc_ref[0, pl.ds(start, tk), cols].astype(_BF16)
            for head in (2 * g, 2 * g + 1):
                c, o = walk(head_q(head), k, v, c_of(head), later_tile, None)
                c_ref[head] = c
                acc_ref[head] = o + first[head] if first else acc_ref[head] + o

    new = [walk(head_q(head), kn_ref[0, :, pairs[head // 2]], vn_ref[0, :, pairs[head // 2]],
                jnp.zeros((n, 1), _F32), later_new, newer) for head in range(SB_HEADS)]
    walk_cache(past - tk, lambda head: new[head][0], [o for _, o in new])

    def cond(state):
        j, c_max = state
        return (j >= 0) & (c_max > -F32_EXP2_ZERO)

    def body(state):
        j, _ = state
        walk_cache(pl.multiple_of(j * tk, tk), lambda head: c_ref[head], None)
        return j - 1, jnp.max(c_ref[...])

    lax.while_loop(cond, body, (past // tk - 2, jnp.max(c_ref[...])))
    o_ref[0] = jnp.concatenate(
        [jnp.where(lane < SB_HEAD_DIM, acc_ref[2 * g], acc_ref[2 * g + 1]) for g in range(len(pairs))],
        axis=-1).astype(o_ref.dtype)


def _sb_step_call(q, k_new, v_new, k_cache, v_cache):
    b, n, _ = q.shape
    past = k_cache.shape[1]
    tk = min(256, past)
    assert past % tk == 0 and LANES == 2 * SB_HEAD_DIM
    new, cache = _stream_specs(n, past, SB_WIDTH)
    return pl.pallas_call(
        functools.partial(_sb_step_kernel, tk=tk),
        grid=(b,),
        in_specs=[new, new, new, cache, cache],
        out_specs=new,
        out_shape=jax.ShapeDtypeStruct((b, n, SB_WIDTH), _BF16),
        scratch_shapes=[pltpu.VMEM((SB_HEADS, n, LANES), _F32), pltpu.VMEM((SB_HEADS, n, 1), _F32)],
        compiler_params=pltpu.CompilerParams(
            dimension_semantics=("arbitrary",), vmem_limit_bytes=VMEM_LIMIT_BYTES),
        name="sb_step",
    )(q, k_new, v_new, k_cache, v_cache)


def _alibi_slope_log2(head):
    assert 8 % DIFF_HEADS == 0
    exponent = (8 // DIFF_HEADS) * (head + 1)
    slope = lax.bitcast_convert_type(jnp.full((1, 1), (127 - exponent) << 23, jnp.int32), _F32)
    return slope * LOG2E


def _diff_step_kernel(q_ref, kn_ref, vn_ref, kc_ref, vc_ref, lq1_ref, lk1_ref, lq2_ref, lk2_ref, g_ref,
                      o_ref, *, lam_init):
    n = q_ref.shape[1]
    past = kc_ref.shape[1]
    lane = lax.broadcasted_iota(jnp.int32, (1, LANES), 1)
    q_idx_c = lax.broadcasted_iota(jnp.int32, (2 * n, past), 0) & (n - 1)
    k_idx_c = lax.broadcasted_iota(jnp.int32, (2 * n, past), 1)
    q_idx_n = lax.broadcasted_iota(jnp.int32, (2 * n, n), 0) & (n - 1)
    k_idx_n = lax.broadcasted_iota(jnp.int32, (2 * n, n), 1)
    distance_c = (past + q_idx_c - k_idx_c).astype(_F32)
    distance_n = jnp.abs(q_idx_n - k_idx_n).astype(_F32)
    visible_n =(lax.shift_right_logical(past + k_idx_n, CHUNK_SHIFT)
                 <= lax.shift_right_logical(past + q_idx_n, CHUNK_SHIFT))
    lam = (jnp.exp(jnp.sum(lq1_ref[...] * lk1_ref[...], axis=-1, keepdims=True))
           - jnp.exp(jnp.sum(lq2_ref[...] * lk2_ref[...], axis=-1, keepdims=True)) + lam_init)
    for head in range(DIFF_HEADS):
        cols = slice(head * LANES, (head + 1) * LANES)
        slope = _alibi_slope_log2(head)
        q = q_ref[0, :, cols].astype(_F32)
        q_maps = jnp.concatenate([jnp.where(lane < DIFF_HEAD_DIM, q, 0.0),
                                  jnp.where(lane >= DIFF_HEAD_DIM, q, 0.0)], axis=0).astype(_BF16)
        s_c = _dot_nt(q_maps, kc_ref[0, :, cols].astype(_BF16)) - slope * distance_c
        s_n = jnp.where(visible_n, _dot_nt(q_maps, kn_ref[0, :, cols]) - slope * distance_n, -jnp.inf)
        m = jnp.maximum(jnp.max(s_c, axis=-1, keepdims=True), jnp.max(s_n, axis=-1, keepdims=True))
        p_c = jnp.exp2(s_c - m)
        p_n = jnp.exp2(s_n - m)
        l = jnp.sum(p_c, axis=-1, keepdims=True) + jnp.sum(p_n, axis=-1, keepdims=True)
        o = (_dot(p_c.astype(_BF16), vc_ref[0, :, cols].astype(_BF16))
             + _dot(p_n.astype(_BF16), vn_ref[0, :, cols])) / l
        o = o[:n] - lam * o[n:]
        o_ref[0, :, cols] = (_rms(o, g_ref[...]) * (1.0 - lam_init)).astype(o_ref.dtype)


def _diff_step_call(q, k_new, v_new, k_cache, v_cache, lq1, lk1, lq2, lk2, subln_g, *, lam_init):
    b, n, _ = q.shape
    past = k_cache.shape[1]
    assert 2 * DIFF_HEAD_DIM == LANES and CHUNK == 1 << CHUNK_SHIFT and n & (n - 1) == 0
    new, cache = _stream_specs(n, past, DIFF_WIDTH)
    lam_spec = pl.BlockSpec((1, DIFF_HEAD_DIM), lambda i: (0, 0))
    return pl.pallas_call(
        functools.partial(_diff_step_kernel, lam_init=lam_init),
        grid=(b,),
        in_specs=[new, new, new, cache, cache, lam_spec, lam_spec, lam_spec, lam_spec,
                  pl.BlockSpec((1, LANES), lambda i: (0, 0))],
        out_specs=new,
        out_shape=jax.ShapeDtypeStruct((b, n, DIFF_WIDTH), _BF16),
        compiler_params=pltpu.CompilerParams(
            dimension_semantics=("arbitrary",), vmem_limit_bytes=VMEM_LIMIT_BYTES),
        name="diff_step",
    )(q, k_new, v_new, k_cache, v_cache, lq1, lk1, lq2, lk2, subln_g)


def _prompt_specs(nq, t):
    assert nq % t == 0 and t & (t - 1) == 0
    q_spec = pl.BlockSpec((1, t, LANES), lambda bi, g, qi: (bi, qi, g))
    k_spec = pl.BlockSpec((1, nq, LANES), lambda bi, g, qi: (bi, 0, g))
    vt_spec = pl.BlockSpec((1, LANES, nq), lambda bi, g, qi: (bi, g, 0))
    return q_spec, k_spec, vt_spec


def _sb_prompt_kernel(q_ref, k_ref, vt_ref, o_ref, acc_ref, c_ref, z_ref, lb_ref, hl_ref, w_ref, *,
                      t, group):
    assert LANES == 2 * SB_HEAD_DIM
    strips = 2 * t // LANES
    qi = pl.program_id(2)
    lane = lax.broadcasted_iota(jnp.int32, (1, LANES), 1)
    key = lax.broadcasted_iota(jnp.int32, (t, LANES), 0)
    qry = lax.broadcasted_iota(jnp.int32, (t, LANES), 1)
    later = (lax.broadcasted_iota(jnp.int32, (t, t), 1)
             > lax.broadcasted_iota(jnp.int32, (t, t), 0)).astype(_BF16)
    def tile_queries(g):
        q = q_ref[0, g * t:(g + 1) * t, :].astype(_F32)
        return jnp.concatenate([jnp.where(lane < SB_HEAD_DIM, q, 0.0),
                                jnp.where(lane >= SB_HEAD_DIM, q, 0.0)], axis=0).astype(_BF16)

    q_heads = [tile_queries(g) for g in range(group)]

    def walk_tile(g, j, c, own, buf):
        k0 = pl.multiple_of(j * t, t)
        z = _dot_nt(k_ref[0, pl.ds(k0, t), :], q_heads[g])
        for s in range(strips):
            z_ref[buf, s] = z[:, s * LANES:(s + 1) * LANES]
        masks, c_new = [], []
        for s in range(strips):
            cols = slice(s * LANES, (s + 1) * LANES)
            zs = z_ref[buf, s]
            lb_ref[buf, s], log_stay = _stick_logs(zs)
            masks.append(key < ((s * LANES + qry) & (t - 1)) if own else None)
            if own:
                log_stay = jnp.where(masks[s], log_stay, 0.0)
            hl_ref[buf, s] = jnp.concatenate(_split_hi_lo(log_stay), axis=1)
            c_new.append(c[:, cols] + jnp.sum(log_stay, axis=0, keepdims=True))
        for s in range(strips):
            cols = slice(s * LANES, (s + 1) * LANES)
            sums = _dot(later, hl_ref[buf, s])
            w = jnp.exp2(lb_ref[buf, s] + (sums[:, :LANES] + sums[:, LANES:] + c[:, cols]))
            if own:
                w = jnp.where(masks[s], w, 0.0)
            head, part = divmod(s, t // LANES)
            w_ref[buf, head, :, part * LANES:(part + 1) * LANES] = w.astype(_BF16)
        vt = vt_ref[0, :, pl.ds(k0, t)]
        o = jnp.concatenate([_dot(vt[:SB_HEAD_DIM], w_ref[buf, 0]), _dot(vt[SB_HEAD_DIM:], w_ref[buf, 1])],
                            axis=0)
        return jnp.concatenate(c_new, axis=1), o

    def first_tiles(has_prev):
        for g in range(group):
            c, o = walk_tile(g, qi * group + g, jnp.zeros((1, 2 * t), _F32), True, 2 * g)
            if has_prev(g):
                c, o_prev = walk_tile(g, qi * group + g - 1, c, False, 2 * g + 1)
                o = o + o_prev
            c_ref[g] = c
            acc_ref[g] = o

    @pl.when(qi == 0)
    def _():
        first_tiles(lambda g: g > 0)

    @pl.when(qi > 0)
    def _():
        first_tiles(lambda g: True)

    def older_tile(g, step):
        return qi * group + g - 2 - step

    def any_active(step):
        flags = [(older_tile(g, step) >= 0) & (jnp.max(c_ref[g]) > -F32_EXP2_ZERO) for g in range(group)]
        return functools.reduce(jnp.logical_or, flags)

    def walk_older(state):
        step, _ = state
        for g in range(group):
            @pl.when((older_tile(g, step) >= 0) & (jnp.max(c_ref[g]) > -F32_EXP2_ZERO))
            def _(g=g):
                c, o = walk_tile(g, older_tile(g, step), c_ref[g], False, 2 * g)
                c_ref[g] = c
                acc_ref[g] += o
        return step + 1, any_active(step + 1)

    lax.while_loop(lambda state: state[1], walk_older, (0, any_active(0)))
    for g in range(group):
        o_ref[0, g * t:(g + 1) * t, :] = acc_ref[g].T.astype(o_ref.dtype)


def _sb_prompt_call(q, k, vt, *, t, group):
    b, nq, _ = q.shape
    q_spec, k_spec, vt_spec = _prompt_specs(nq, group * t)
    sets = 2 * group
    strips = 2 * t // LANES
    return pl.pallas_call(
        functools.partial(_sb_prompt_kernel, t=t, group=group),
        grid=(b, SB_WIDTH // LANES, nq // (group * t)),
        in_specs=[q_spec, k_spec, vt_spec],
        out_specs=q_spec,
        out_shape=jax.ShapeDtypeStruct((b, nq, SB_WIDTH), _BF16),
        scratch_shapes=[pltpu.VMEM((group, LANES, t), _F32), pltpu.VMEM((group, 1, 2 * t), _F32),
                        pltpu.VMEM((sets, strips, t, LANES), _F32),
                        pltpu.VMEM((sets, strips, t, LANES), _F32),
                        pltpu.VMEM((sets, strips, t, 2 * LANES), _BF16),
                        pltpu.VMEM((sets, 2, t, t), _BF16)],
        compiler_params=pltpu.CompilerParams(
            dimension_semantics=("parallel", "parallel", "arbitrary"),
            vmem_limit_bytes=VMEM_LIMIT_BYTES),
        name="sb_prompt",
    )(q, k, vt)


def _diff_prompt_kernel(q_ref, k_ref, vt_ref, lq1_ref, lk1_ref, lq2_ref, lk2_ref, g_ref, o_ref,
                        m_ref, l_ref, acc_ref, pen_ref, kpos_ref, s_ref, p_ref, a_ref,
                        *, tq, tk, lam_init):
    own = 2
    assert tq == own * tk
    head = pl.program_id(1)
    qi = pl.program_id(2)
    n_past = qi * own
    lane = lax.broadcasted_iota(jnp.int32, (1, LANES), 1)
    slope = _alibi_slope_log2(head)

    @pl.when(qi == 0)
    def _():
        key = lax.broadcasted_iota(jnp.int32, (tk, 2 * tq), 0)
        qry = lax.broadcasted_iota(jnp.int32, (tk, 2 * tq), 1) & (tq - 1)
        for d in range(own):
            k_pos = d * tk + key
            visible = lax.shift_right_logical(k_pos, CHUNK_SHIFT) <= lax.shift_right_logical(qry, CHUNK_SHIFT)
            bias = slope * (qry - key - jnp.abs(qry - k_pos)).astype(_F32)
            pen_ref[d] = jnp.where(visible, bias, -jnp.inf)
        row = lax.broadcasted_iota(jnp.int32, (tk, LANES), 0)
        col = lax.broadcasted_iota(jnp.int32, (tk, LANES), 1)
        kpos_ref[...] = jnp.where(col < 3, row, 0).astype(_F32).astype(_BF16)

    slope_hi = slope.astype(_BF16).astype(_F32)
    slope_mid = (slope - slope_hi).astype(_BF16).astype(_F32)
    slope_lo = slope - slope_hi - slope_mid
    slope_cols = jnp.where(lane == 0, slope_hi, jnp.where(lane == 1, slope_mid,
                                                          jnp.where(lane == 2, slope_lo, 0.0)))
    slope_cols = jnp.broadcast_to(slope_cols, (tq, LANES))
    q = q_ref[0].astype(_F32)
    q_maps = jnp.concatenate(
        [jnp.concatenate([jnp.where(lane < DIFF_HEAD_DIM, q, 0.0), slope_cols], axis=1),
         jnp.concatenate([jnp.where(lane >= DIFF_HEAD_DIM, q, 0.0), slope_cols], axis=1)],
        axis=0).astype(_BF16)
    m_ref[...] = jnp.full_like(m_ref, NEG_BIG)
    l_ref[...] = jnp.zeros_like(l_ref)
    acc_ref[...] = jnp.zeros_like(acc_ref)

    def slot_tile(n):
        return jnp.where(n < own, n_past + n, n_past - 1 - (n - own))

    def slot_shift(n):
        j = slot_tile(n)
        distance = (qi * tq - j * tk).astype(_F32)
        return jnp.where(n < own, 0.0, jnp.where(j >= 0, -slope * distance, NEG_BIG))

    def tile_start(n):
        return pl.multiple_of(jnp.clip(slot_tile(n), 0, n_past + own - 1) * tk, tk)

    def scores(n):
        keys = jnp.concatenate([k_ref[0, pl.ds(tile_start(n), tk), :], kpos_ref[...]], axis=1)
        return _dot_nt(keys, q_maps)

    def put_scores(buf, s):
        for c in range(2 * tq // LANES):
            s_ref[buf, c] = s[:, c * LANES:(c + 1) * LANES]

    def softmax(buf, shift):
        for c in range(2 * tq // LANES):
            strip = slice(c * LANES, (c + 1) * LANES)
            s = s_ref[buf, c]
            m_old = m_ref[:, strip]
            m_new = jnp.maximum(m_old, jnp.max(s, axis=0, keepdims=True) + shift)
            alpha = jnp.exp2(m_old - m_new)
            p = jnp.exp2(s - (m_new - shift))
            l_ref[:, strip] = alpha * l_ref[:, strip] + jnp.sum(p, axis=0, keepdims=True)
            m_ref[:, strip] = m_new
            a_ref[buf, :, strip] = alpha
            half = c % (MXU_WIDTH // LANES)
            p_ref[buf, c // (MXU_WIDTH // LANES), :, half * LANES:(half + 1) * LANES] = p.astype(_BF16)

    def weigh(n, buf):
        values = vt_ref[0, :, pl.ds(tile_start(n), tk)]
        for c in range(2 * tq // MXU_WIDTH):
            strip = slice(c * MXU_WIDTH, (c + 1) * MXU_WIDTH)
            acc_ref[c] = a_ref[buf, :, strip] * acc_ref[c] + _dot(values, p_ref[buf, c])

    for n in range(own):
        put_scores(n, scores(n) + pen_ref[n])
    for n in range(own, DEPTH_SLOTS):
        p_ref[n] = jnp.zeros(p_ref.shape[1:], _BF16)
        a_ref[n] = jnp.ones((1, 2 * tq), _F32)

    def stage_pair(n, buf):
        for d in range(2):
            put_scores((buf + 2 + d) % DEPTH_SLOTS, scores(n + 2 + d))
        for d in range(2):
            softmax(buf + d, slot_shift(n + d))
        for d in range(2):
            weigh(n - 2 + d, (buf + 2 + d) % DEPTH_SLOTS)

    def four_slots(i, carry):
        stage_pair(4 * i, 0)
        stage_pair(4 * i + 2, 2)
        return carry

    trips = (qi + 2) // 2
    lax.fori_loop(0, trips, four_slots, 0)
    for d in range(2):
        weigh(4 * trips - 2 + d, 2 + d)

    lam = (jnp.exp(jnp.sum(lq1_ref[...] * lk1_ref[...], axis=-1, keepdims=True))
           - jnp.exp(jnp.sum(lq2_ref[...] * lk2_ref[...], axis=-1, keepdims=True)) + lam_init)
    o = jnp.concatenate([acc_ref[c] for c in range(2 * tq // MXU_WIDTH)], axis=1) / l_ref[...]
    o = o[:, :tq] - lam * o[:, tq:]
    o = o * lax.rsqrt(jnp.mean(o * o, axis=0, keepdims=True) + RMS_EPS)
    o_ref[0] = (o.T * g_ref[...] * (1.0 - lam_init)).astype(o_ref.dtype)


def _diff_prompt_call(q, k, vt, lq1, lk1, lq2, lk2, subln_g, *, tq, lam_init):
    b, nq, _ = q.shape
    tk = tq // 2
    assert 2 * DIFF_HEAD_DIM == LANES
    assert tk % CHUNK == 0 and CHUNK == 1 << CHUNK_SHIFT and tk <= 256
    q_spec, k_spec, vt_spec = _prompt_specs(nq, tq)
    lam_spec = pl.BlockSpec((1, DIFF_HEAD_DIM), lambda bi, g, qi: (0, 0))
    stat = pltpu.VMEM((1, 2 * tq), _F32)
    return pl.pallas_call(
        functools.partial(_diff_prompt_kernel, tq=tq, tk=tk, lam_init=lam_init),
        grid=(b, DIFF_HEADS, nq // tq),
        in_specs=[q_spec, k_spec, vt_spec, lam_spec, lam_spec, lam_spec, lam_spec,
                  pl.BlockSpec((1, LANES), lambda bi, g, qi: (0, 0))],
        out_specs=q_spec,
        out_shape=jax.ShapeDtypeStruct((b, nq, DIFF_WIDTH), _BF16),
        scratch_shapes=[stat, stat, pltpu.VMEM((2 * tq // MXU_WIDTH, LANES, MXU_WIDTH), _F32),
                        pltpu.VMEM((2, tk, 2 * tq), _F32), pltpu.VMEM((tk, LANES), _BF16),
                        pltpu.VMEM((DEPTH_SLOTS, 2 * tq // LANES, tk, LANES), _F32),
                        pltpu.VMEM((DEPTH_SLOTS, 2 * tq // MXU_WIDTH, tk, MXU_WIDTH), _BF16),
                        pltpu.VMEM((DEPTH_SLOTS, 1, 2 * tq), _F32)],
        compiler_params=pltpu.CompilerParams(
            dimension_semantics=("arbitrary", "arbitrary", "arbitrary"),
            vmem_limit_bytes=VMEM_LIMIT_BYTES),
        name="diff_prompt",
    )(q, k, vt, lq1, lk1, lq2, lk2, subln_g)


def _encoder_layer(x, past, w, lam_init):
    b, n, _ = x.shape
    m = b * n
    h = _ffn(x.reshape(m, D_MODEL), w["ff1_pre_g"], w["ff1_w_gate"], w["ff1_w_up"], w["ff1_w_down"],
             w["ff1_post_g"])
    lam_args = (w["lam_q1"], w["lam_k1"], w["lam_q2"], w["lam_k2"], w["subln_g"])

    def seq(a):
        return a.reshape(b, n, a.shape[-1])

    if past is None:
        sbq, sbk, sbv, dq, dk, dv, sbk16, sbvt16, dk16, dvt16 = _qkv(h, w["mix_pre_g"], w["w_in"], seq_len=n)
        sb_o = _sb_prompt_call(seq(sbq), seq(sbk16), sbvt16, t=SB_PROMPT_TILE, group=SB_PROMPT_GROUP)
        d_o = _diff_prompt_call(seq(dq), seq(dk16), dvt16, *lam_args, tq=DIFF_PROMPT_TILE, lam_init=lam_init)
    else:
        sbq, sbk, sbv, dq, dk, dv, sbk16, sbv16, dk16, dv16 = _qkv(h, w["mix_pre_g"], w["w_in"])
        sb_o = _sb_step_call(seq(sbq), seq(sbk16), seq(sbv16), past[0], past[1])
        d_o = _diff_step_call(seq(dq), seq(dk16), seq(dv16), past[2], past[3], *lam_args, lam_init=lam_init)
    y = _ffn(h, w["ff2_pre_g"], w["ff2_w_gate"], w["ff2_w_up"], w["ff2_w_down"], w["ff2_post_g"],
             mixer=(sb_o.reshape(m, SB_WIDTH), d_o.reshape(m, DIFF_WIDTH), w["w_out"], w["mix_post_g"]),
             final_g=w["final_g"])
    rows = (sbk.reshape(b, n, SB_HEADS, SB_HEAD_DIM), sbv.reshape(b, n, SB_HEADS, SB_HEAD_DIM),
            dk.reshape(b, n, DIFF_HEADS, 2 * DIFF_HEAD_DIM), dv.reshape(b, n, DIFF_HEADS, 2 * DIFF_HEAD_DIM))
    return y.reshape(b, n, D_MODEL), rows


_MATRICES = ("ff1_w_gate", "ff1_w_up", "ff1_w_down", "w_in", "w_out", "ff2_w_gate", "ff2_w_up", "ff2_w_down")


def kernel(x_prompt, x_sample, cache_sb_k, cache_sb_v, cache_diff_k, cache_diff_v, ff1_pre_g, ff1_w_gate, ff1_w_up, ff1_w_down, ff1_post_g, mix_pre_g, w_in, lam_q1, lam_k1, lam_q2, lam_k2, subln_g, w_out, mix_post_g, ff2_pre_g, ff2_w_gate, ff2_w_up, ff2_w_down, ff2_post_g, final_g):
    params = dict(ff1_pre_g=ff1_pre_g, ff1_w_gate=ff1_w_gate, ff1_w_up=ff1_w_up, ff1_w_down=ff1_w_down,
                  ff1_post_g=ff1_post_g, mix_pre_g=mix_pre_g, w_in=w_in, lam_q1=lam_q1, lam_k1=lam_k1,
                  lam_q2=lam_q2, lam_k2=lam_k2, subln_g=subln_g, w_out=w_out, mix_post_g=mix_post_g,
                  ff2_pre_g=ff2_pre_g, ff2_w_gate=ff2_w_gate, ff2_w_up=ff2_w_up, ff2_w_down=ff2_w_down,
                  ff2_post_g=ff2_post_g, final_g=final_g)
    yp, ys = x_prompt, x_sample
    rows_p, rows_s = [], []
    for l in range(DEPTH):
        lam_init = 0.8 - 0.6 * math.exp(-0.3 * l)
        w = {name: (p[l].astype(_BF16) if name in _MATRICES else p[l][None, :].astype(_F32))
             for name, p in params.items()}
        yp, rp = _encoder_layer(yp, None, w, lam_init)
        past = tuple(c[l].reshape(c.shape[1], c.shape[2], -1)
                     for c in (cache_sb_k, cache_sb_v, cache_diff_k, cache_diff_v))
        ys, rs = _encoder_layer(ys, past, w, lam_init)
        rows_p.append(rp)
        rows_s.append(rs)
    stacked_p = [jnp.stack(r, axis=0) for r in zip(*rows_p)]
    stacked_s = [jnp.stack(r, axis=0) for r in zip(*rows_s)]
    return (yp, ys, *stacked_p, *stacked_s)
```

```python
import functools
import math

import jax
import jax.numpy as jnp
from jax import lax
from jax.experimental import pallas as pl
from jax.experimental.pallas import tpu as pltpu

D_MODEL = 1024
DEPTH = 1
CHUNK = 64
CHUNK_SHIFT = 6
SB_HEADS = 8
SB_HEAD_DIM = 64
DIFF_HEADS = 4
DIFF_HEAD_DIM = 64
SB_WIDTH = SB_HEADS * SB_HEAD_DIM
DIFF_WIDTH = DIFF_HEADS * 2 * DIFF_HEAD_DIM
MIX_WIDTH = SB_WIDTH + DIFF_WIDTH
IN_WIDTH = 3 * SB_WIDTH + 3 * DIFF_WIDTH
D_FF = 2816
RMS_EPS = 1e-6

LANES = 128
MXU_WIDTH = 256
VMEM_LIMIT_BYTES = 56 * 1024 * 1024
F32_EXP2_ZERO = 151.0
NEG_BIG = -1e30
LOG2E = math.log2(math.e)
DEPTH_SLOTS = 4
SB_PROMPT_TILE = 256
SB_PROMPT_GROUP = 2
DIFF_PROMPT_TILE = 512

_F32 = jnp.float32
_BF16 = jnp.bfloat16


def _rms(x, g):
    return x * lax.rsqrt(jnp.mean(x * x, axis=-1, keepdims=True) + RMS_EPS) * g


def _stick_logs(z):
    log_beta = jnp.minimum(z, 0.0) - jnp.log2(1.0 + jnp.exp2(-jnp.abs(z)))
    return log_beta, log_beta - z


def _split_hi_lo(x):
    hi = lax.bitcast_convert_type(lax.bitcast_convert_type(x, jnp.uint32) & jnp.uint32(0xFFFF0000), _F32)
    return hi.astype(_BF16), (x - hi).astype(_BF16)


def _dot(a, b):
    return jnp.dot(a, b, preferred_element_type=_F32)


def _dot_nt(a, b):
    return lax.dot_general(a, b, (((1,), (1,)), ((), ())), preferred_element_type=_F32)


def _ffn_kernel(*refs, mixer, final_norm):
    refs = list(refs)
    o_ref = refs.pop()
    x = refs.pop(0)[...]
    if mixer:
        sbo_ref, do_ref, wo_ref, mix_g_ref = refs[:4]
        refs = refs[4:]
        mix = _dot(sbo_ref[...], wo_ref[:SB_WIDTH, :]) + _dot(do_ref[...], wo_ref[SB_WIDTH:, :])
        x = x + _rms(mix, mix_g_ref[...])
    pre_ref, wg_ref, wu_ref, wd_ref, post_ref = refs[:5]
    xn = _rms(x, pre_ref[...]).astype(_BF16)
    g = _dot(xn, wg_ref[...])
    u = _dot(xn, wu_ref[...])
    a = (g * jax.nn.sigmoid(g) * u).astype(_BF16)
    h = x + 0.5 * _rms(_dot(a, wd_ref[...]), post_ref[...])
    if final_norm:
        h = _rms(h, refs[5][...])
    o_ref[...] = h


def _ffn(x, pre_g, wg, wu, wd, post_g, *, mixer=None, final_g=None):
    m = x.shape[0]
    tm = min(512, m)
    assert m % tm == 0

    def resident(shape):
        return pl.BlockSpec(shape, lambda i: (0, 0), pipeline_mode=pl.Buffered(1))

    row = pl.BlockSpec((tm, D_MODEL), lambda i: (i, 0))
    vec = resident((1, D_MODEL))
    in_specs, args = [row], [x]
    if mixer is not None:
        half = pl.BlockSpec((tm, SB_WIDTH), lambda i: (i, 0))
        in_specs += [half, half, resident((MIX_WIDTH, D_MODEL)), vec]
        args += list(mixer)
    in_specs += [vec, resident((D_MODEL, D_FF)), resident((D_MODEL, D_FF)), resident((D_FF, D_MODEL)), vec]
    args += [pre_g, wg, wu, wd, post_g]
    if final_g is not None:
        in_specs.append(vec)
        args.append(final_g)
    return pl.pallas_call(
        functools.partial(_ffn_kernel, mixer=mixer is not None, final_norm=final_g is not None),
        grid=(m // tm,),
        in_specs=in_specs,
        out_specs=row,
        out_shape=jax.ShapeDtypeStruct((m, D_MODEL), _F32),
        compiler_params=pltpu.CompilerParams(
            dimension_semantics=("parallel",), vmem_limit_bytes=VMEM_LIMIT_BYTES),
        name="ffn_mix_final" if mixer is not None else "ffn",
    )(*args)


def _qkv_kernel(h_ref, g_ref, w_ref, sbq_ref, sbk_ref, sbv_ref, dq_ref, dk_ref, dv_ref,
                sbk16_ref, sbv16_ref, dk16_ref, dv16_ref, *, transposed):
    hn = _rms(h_ref[...], g_ref[...]).astype(_BF16)

    def proj(idx):
        return _dot(hn, w_ref[:, idx * SB_WIDTH:(idx + 1) * SB_WIDTH])

    sbq_ref[...] = (proj(0) * (SB_HEAD_DIM ** -0.5 * LOG2E)).astype(_BF16)
    dq_ref[...] = (proj(3) * (DIFF_HEAD_DIM ** -0.5 * LOG2E)).astype(_BF16)

    def store_rows(ref, p):
        head_dim = ref.shape[1]
        heads = p.shape[1] // head_dim
        for head in range(heads):
            ref[pl.ds(head, p.shape[0], stride=heads), :] = p[:, head * head_dim:(head + 1) * head_dim]

    for idx, full_ref, half_ref in ((4, dk_ref, dk16_ref), (5, dv_ref, dv16_ref)):
        p = proj(idx)
        store_rows(full_ref, p)
        if transposed and half_ref is dv16_ref:
            half_ref[0] = p.T.astype(_BF16)
        else:
            half_ref[...] = p.astype(_BF16)
    for idx, full_ref, half_ref in ((1, sbk_ref, sbk16_ref), (2, sbv_ref, sbv16_ref)):
        p = proj(idx)
        if transposed:
            p_t = p.T
            full_ref[0] = p_t
            if half_ref is sbv16_ref:
                half_ref[0] = p_t.astype(_BF16)
            else:
                half_ref[...] = p.astype(_BF16)
        else:
            store_rows(full_ref, p)
            half_ref[...] = p.astype(_BF16)


def _qkv(h, g, w, *, seq_len=None):
    assert SB_WIDTH == DIFF_WIDTH
    assert math.log2(SB_HEAD_DIM) % 2 == 0 and math.log2(DIFF_HEAD_DIM) % 2 == 0
    m = h.shape[0]
    tm = min(512, m)
    assert m % tm == 0
    row = pl.BlockSpec((tm, D_MODEL), lambda i: (i, 0))
    out = pl.BlockSpec((tm, SB_WIDTH), lambda i: (i, 0))
    b16 = jax.ShapeDtypeStruct((m, SB_WIDTH), _BF16)

    def cache_rows(heads, head_dim):
        return (jax.ShapeDtypeStruct((m * heads, head_dim), _F32),
                pl.BlockSpec((tm * heads, head_dim), lambda i: (i, 0)))

    sb_rows, sb_out = cache_rows(SB_HEADS, SB_HEAD_DIM)
    d_rows, d_out = cache_rows(DIFF_HEADS, 2 * DIFF_HEAD_DIM)
    v_out, v16 = out, b16
    if seq_len is not None:
        assert seq_len % tm == 0
        tiles = seq_len // tm
        v_out = pl.BlockSpec((1, SB_WIDTH, tm), lambda i: (i // tiles, 0, i % tiles))
        v16 = jax.ShapeDtypeStruct((m // seq_len, SB_WIDTH, seq_len), _BF16)
        sb_rows, sb_out = jax.ShapeDtypeStruct(v16.shape, _F32), v_out
    return pl.pallas_call(
        functools.partial(_qkv_kernel, transposed=seq_len is not None),
        grid=(m // tm,),
        in_specs=[row, pl.BlockSpec((1, D_MODEL), lambda i: (0, 0)),
                  pl.BlockSpec((D_MODEL, IN_WIDTH), lambda i: (0, 0))],
        out_specs=[out, sb_out, sb_out, out, d_out, d_out, out, v_out, out, v_out],
        out_shape=[b16, sb_rows, sb_rows, b16, d_rows, d_rows, b16, v16, b16, v16],
        compiler_params=pltpu.CompilerParams(
            dimension_semantics=("parallel",), vmem_limit_bytes=VMEM_LIMIT_BYTES),
        name="qkv",
    )(h, g, w)


def _sb_step_kernel(q_ref, kn_ref, vn_ref, kc_ref, vc_ref, o_ref, acc_ref, c_ref, *, tk):
    n = q_ref.shape[1]
    past = kc_ref.shape[2]

    def later(t):
        return (lax.broadcasted_iota(jnp.int32, (t, t), 0)
                > lax.broadcasted_iota(jnp.int32, (t, t), 1)).astype(_BF16)

    later_new, later_tile = later(n), later(tk)
    newer = (lax.broadcasted_iota(jnp.int32, (n, n), 1) < lax.broadcasted_iota(jnp.int32, (n, n), 0))

    def walk(z, c, later_t, mask):
        log_beta, log_stay = _stick_logs(z)
        if mask is not None:
            log_stay = jnp.where(mask, log_stay, 0.0)
        sums = _dot(jnp.concatenate(_split_hi_lo(log_stay), axis=0), later_t)
        w = jnp.exp2(log_beta + (sums[:n] + sums[n:] + c))
        if mask is not None:
            w = jnp.where(mask, w, 0.0)
        return c + jnp.sum(log_stay, axis=-1, keepdims=True), w.astype(_BF16)

    def head_cols(head):
        return slice(head * SB_HEAD_DIM, (head + 1) * SB_HEAD_DIM)

    def walk_cache(start, c_of, first):
        for head in range(SB_HEADS):
            rows = head_cols(head)
            k_t = kc_ref[0, rows, pl.ds(start, tk)].astype(_BF16)
            v_t = vc_ref[0, rows, pl.ds(start, tk)].astype(_BF16)
            c, w = walk(_dot(q_ref[0, :, rows], k_t), c_of(head), later_tile, None)
            o = _dot_nt(w, v_t)
            c_ref[head] = c
            acc_ref[head] = o + first[head] if first else acc_ref[head] + o

    new = []
    for head in range(SB_HEADS):
        cols = head_cols(head)
        c, w = walk(_dot_nt(q_ref[0, :, cols], kn_ref[0, :, cols]), jnp.zeros((n, 1), _F32), later_new, newer)
        new.append((c, _dot(w, vn_ref[0, :, cols])))
    walk_cache(past - tk, lambda head: new[head][0], [o for _, o in new])

    def cond(state):
        j, c_max = state
        return (j >= 0) & (c_max > -F32_EXP2_ZERO)

    def body(state):
        j, _ = state
        walk_cache(pl.multiple_of(j * tk, tk), lambda head: c_ref[head], None)
        return j - 1, jnp.max(c_ref[...])

    lax.while_loop(cond, body, (past // tk - 2, jnp.max(c_ref[...])))
    o_ref[0] = jnp.concatenate([acc_ref[head] for head in range(SB_HEADS)], axis=-1).astype(o_ref.dtype)


def _sb_step_call(q, k_new, v_new, k_cache, v_cache):
    b, n, _ = q.shape
    past = k_cache.shape[2]
    tk = min(256, past)
    assert past % tk == 0
    new = pl.BlockSpec((1, n, SB_WIDTH), lambda i: (i, 0, 0))
    cache = pl.BlockSpec((1, SB_WIDTH, past), lambda i: (i, 0, 0))
    return pl.pallas_call(
        functools.partial(_sb_step_kernel, tk=tk),
        grid=(b,),
        in_specs=[new, new, new, cache, cache],
        out_specs=new,
        out_shape=jax.ShapeDtypeStruct((b, n, SB_WIDTH), _BF16),
        scratch_shapes=[pltpu.VMEM((SB_HEADS, n, SB_HEAD_DIM), _F32), pltpu.VMEM((SB_HEADS, n, 1), _F32)],
        compiler_params=pltpu.CompilerParams(
            dimension_semantics=("arbitrary",), vmem_limit_bytes=VMEM_LIMIT_BYTES),
        name="sb_step",
    )(q, k_new, v_new, k_cache, v_cache)


def _alibi_slope_log2(head):
    assert 8 % DIFF_HEADS == 0
    exponent = (8 // DIFF_HEADS) * (head + 1)
    slope = lax.bitcast_convert_type(jnp.full((1, 1), (127 - exponent) << 23, jnp.int32), _F32)
    return slope * LOG2E


def _diff_step_kernel(q_ref, kn_ref, vn_ref, kc_ref, vc_ref, lq1_ref, lk1_ref, lq2_ref, lk2_ref, g_ref,
                      o_ref, *, lam_init):
    n = q_ref.shape[1]
    past = kc_ref.shape[1] // DIFF_HEADS

    def cache_head(ref, head):
        return ref[0, pl.ds(head, past, stride=DIFF_HEADS), :].astype(_BF16)
    lane = lax.broadcasted_iota(jnp.int32, (1, LANES), 1)
    q_idx_c = lax.broadcasted_iota(jnp.int32, (2 * n, past), 0) & (n - 1)
    k_idx_c = lax.broadcasted_iota(jnp.int32, (2 * n, past), 1)
    q_idx_n = lax.broadcasted_iota(jnp.int32, (2 * n, n), 0) & (n - 1)
    k_idx_n = lax.broadcasted_iota(jnp.int32, (2 * n, n), 1)
    distance_c = (past + q_idx_c - k_idx_c).astype(_F32)
    distance_n = jnp.abs(q_idx_n - k_idx_n).astype(_F32)
    visible_n =(lax.shift_right_logical(past + k_idx_n, CHUNK_SHIFT)
                 <= lax.shift_right_logical(past + q_idx_n, CHUNK_SHIFT))
    lam = (jnp.exp(jnp.sum(lq1_ref[...] * lk1_ref[...], axis=-1, keepdims=True))
           - jnp.exp(jnp.sum(lq2_ref[...] * lk2_ref[...], axis=-1, keepdims=True)) + lam_init)
    for head in range(DIFF_HEADS):
        cols = slice(head * LANES, (head + 1) * LANES)
        slope = _alibi_slope_log2(head)
        q = q_ref[0, :, cols].astype(_F32)
        q_maps = jnp.concatenate([jnp.where(lane < DIFF_HEAD_DIM, q, 0.0),
                                  jnp.where(lane >= DIFF_HEAD_DIM, q, 0.0)], axis=0).astype(_BF16)
        s_c = _dot_nt(q_maps, cache_head(kc_ref, head)) - slope * distance_c
        s_n = jnp.where(visible_n, _dot_nt(q_maps, kn_ref[0, :, cols]) - slope * distance_n, -jnp.inf)
        m = jnp.maximum(jnp.max(s_c, axis=-1, keepdims=True), jnp.max(s_n, axis=-1, keepdims=True))
        p_c = jnp.exp2(s_c - m)
        p_n = jnp.exp2(s_n - m)
        l = jnp.sum(p_c, axis=-1, keepdims=True) + jnp.sum(p_n, axis=-1, keepdims=True)
        o = (_dot(p_c.astype(_BF16), cache_head(vc_ref, head))
             + _dot(p_n.astype(_BF16), vn_ref[0, :, cols])) / l
        o = o[:n] - lam * o[n:]
        o_ref[0, :, cols] = (_rms(o, g_ref[...]) * (1.0 - lam_init)).astype(o_ref.dtype)


def _diff_step_call(q, k_new, v_new, k_cache, v_cache, lq1, lk1, lq2, lk2, subln_g, *, lam_init):
    b, n, _ = q.shape
    rows = k_cache.shape[1]
    assert 2 * DIFF_HEAD_DIM == LANES and CHUNK == 1 << CHUNK_SHIFT and n & (n - 1) == 0
    new = pl.BlockSpec((1, n, DIFF_WIDTH), lambda i: (i, 0, 0))
    cache = pl.BlockSpec((1, rows, LANES), lambda i: (i, 0, 0))
    lam_spec = pl.BlockSpec((1, DIFF_HEAD_DIM), lambda i: (0, 0))
    return pl.pallas_call(
        functools.partial(_diff_step_kernel, lam_init=lam_init),
        grid=(b,),
        in_specs=[new, new, new, cache, cache, lam_spec, lam_spec, lam_spec, lam_spec,
                  pl.BlockSpec((1, LANES), lambda i: (0, 0))],
        out_specs=new,
        out_shape=jax.ShapeDtypeStruct((b, n, DIFF_WIDTH), _BF16),
        compiler_params=pltpu.CompilerParams(
            dimension_semantics=("arbitrary",), vmem_limit_bytes=VMEM_LIMIT_BYTES),
        name="diff_step",
    )(q, k_new, v_new, k_cache, v_cache, lq1, lk1, lq2, lk2, subln_g)


def _prompt_specs(nq, t):
    assert nq % t == 0 and t & (t - 1) == 0
    q_spec = pl.BlockSpec((1, t, LANES), lambda bi, g, qi: (bi, qi, g))
    k_spec = pl.BlockSpec((1, nq, LANES), lambda bi, g, qi: (bi, 0, g))
    vt_spec = pl.BlockSpec((1, LANES, nq), lambda bi, g, qi: (bi, g, 0))
    return q_spec, k_spec, vt_spec


def _sb_prompt_kernel(q_ref, k_ref, vt_ref, o_ref, acc_ref, c_ref, z_ref, lb_ref, hl_ref, w_ref, *,
                      t, group):
    assert LANES == 2 * SB_HEAD_DIM
    strips = 2 * t // LANES
    qi = pl.program_id(2)
    lane = lax.broadcasted_iota(jnp.int32, (1, LANES), 1)
    key = lax.broadcasted_iota(jnp.int32, (t, LANES), 0)
    qry = lax.broadcasted_iota(jnp.int32, (t, LANES), 1)
    later = (lax.broadcasted_iota(jnp.int32, (t, t), 1)
             > lax.broadcasted_iota(jnp.int32, (t, t), 0)).astype(_BF16)
    def tile_queries(g):
        q = q_ref[0, g * t:(g + 1) * t, :].astype(_F32)
        return jnp.concatenate([jnp.where(lane < SB_HEAD_DIM, q, 0.0),
                                jnp.where(lane >= SB_HEAD_DIM, q, 0.0)], axis=0).astype(_BF16)

    q_heads = [tile_queries(g) for g in range(group)]

    def walk_tile(g, j, c, own, buf):
        k0 = pl.multiple_of(j * t, t)
        z = _dot_nt(k_ref[0, pl.ds(k0, t), :], q_heads[g])
        for s in range(strips):
            z_ref[buf, s] = z[:, s * LANES:(s + 1) * LANES]
        masks, c_new = [], []
        for s in range(strips):
            cols = slice(s * LANES, (s + 1) * LANES)
            zs = z_ref[buf, s]
            lb_ref[buf, s], log_stay = _stick_logs(zs)
            masks.append(key < ((s * LANES + qry) & (t - 1)) if own else None)
            if own:
                log_stay = jnp.where(masks[s], log_stay, 0.0)
            hl_ref[buf, s] = jnp.concatenate(_split_hi_lo(log_stay), axis=1)
            c_new.append(c[:, cols] + jnp.sum(log_stay, axis=0, keepdims=True))
        for s in range(strips):
            cols = slice(s * LANES, (s + 1) * LANES)
            sums = _dot(later, hl_ref[buf, s])
            w = jnp.exp2(lb_ref[buf, s] + (sums[:, :LANES] + sums[:, LANES:] + c[:, cols]))
            if own:
                w = jnp.where(masks[s], w, 0.0)
            head, part = divmod(s, t // LANES)
            w_ref[buf, head, :, part * LANES:(part + 1) * LANES] = w.astype(_BF16)
        vt = vt_ref[0, :, pl.ds(k0, t)]
        o = jnp.concatenate([_dot(vt[:SB_HEAD_DIM], w_ref[buf, 0]), _dot(vt[SB_HEAD_DIM:], w_ref[buf, 1])],
                            axis=0)
        return jnp.concatenate(c_new, axis=1), o

    def first_tiles(has_prev):
        for g in range(group):
            c, o = walk_tile(g, qi * group + g, jnp.zeros((1, 2 * t), _F32), True, 2 * g)
            if has_prev(g):
                c, o_prev = walk_tile(g, qi * group + g - 1, c, False, 2 * g + 1)
                o = o + o_prev
            c_ref[g] = c
            acc_ref[g] = o

    @pl.when(qi == 0)
    def _():
        first_tiles(lambda g: g > 0)

    @pl.when(qi > 0)
    def _():
        first_tiles(lambda g: True)

    def older_tile(g, step):
        return qi * group + g - 2 - step

    def any_active(step):
        flags = [(older_tile(g, step) >= 0) & (jnp.max(c_ref[g]) > -F32_EXP2_ZERO) for g in range(group)]
        return functools.reduce(jnp.logical_or, flags)

    def walk_older(state):
        step, _ = state
        for g in range(group):
            @pl.when((older_tile(g, step) >= 0) & (jnp.max(c_ref[g]) > -F32_EXP2_ZERO))
            def _(g=g):
                c, o = walk_tile(g, older_tile(g, step), c_ref[g], False, 2 * g)
                c_ref[g] = c
                acc_ref[g] += o
        return step + 1, any_active(step + 1)

    lax.while_loop(lambda state: state[1], walk_older, (0, any_active(0)))
    for g in range(group):
        o_ref[0, g * t:(g + 1) * t, :] = acc_ref[g].T.astype(o_ref.dtype)


def _sb_prompt_call(q, k, vt, *, t, group):
    b, nq, _ = q.shape
    q_spec, k_spec, vt_spec = _prompt_specs(nq, group * t)
    sets = 2 * group
    strips = 2 * t // LANES
    return pl.pallas_call(
        functools.partial(_sb_prompt_kernel, t=t, group=group),
        grid=(b, SB_WIDTH // LANES, nq // (group * t)),
        in_specs=[q_spec, k_spec, vt_spec],
        out_specs=q_spec,
        out_shape=jax.ShapeDtypeStruct((b, nq, SB_WIDTH), _BF16),
        scratch_shapes=[pltpu.VMEM((group, LANES, t), _F32), pltpu.VMEM((group, 1, 2 * t), _F32),
                        pltpu.VMEM((sets, strips, t, LANES), _F32),
                        pltpu.VMEM((sets, strips, t, LANES), _F32),
                        pltpu.VMEM((sets, strips, t, 2 * LANES), _BF16),
                        pltpu.VMEM((sets, 2, t, t), _BF16)],
        compiler_params=pltpu.CompilerParams(
            dimension_semantics=("parallel", "parallel", "arbitrary"),
            vmem_limit_bytes=VMEM_LIMIT_BYTES),
        name="sb_prompt",
    )(q, k, vt)


def _diff_prompt_kernel(q_ref, k_ref, vt_ref, lq1_ref, lk1_ref, lq2_ref, lk2_ref, g_ref, o_ref,
                        m_ref, l_ref, acc_ref, pen_ref, kpos_ref, s_ref, p_ref, a_ref,
                        *, tq, tk, lam_init):
    own = 2
    assert tq == own * tk
    head = pl.program_id(1)
    qi = pl.program_id(2)
    n_past = qi * own
    lane = lax.broadcasted_iota(jnp.int32, (1, LANES), 1)
    slope = _alibi_slope_log2(head)

    @pl.when(qi == 0)
    def _():
        key = lax.broadcasted_iota(jnp.int32, (tk, 2 * tq), 0)
        qry = lax.broadcasted_iota(jnp.int32, (tk, 2 * tq), 1) & (tq - 1)
        for d in range(own):
            k_pos = d * tk + key
            visible = lax.shift_right_logical(k_pos, CHUNK_SHIFT) <= lax.shift_right_logical(qry, CHUNK_SHIFT)
            bias = slope * (qry - key - jnp.abs(qry - k_pos)).astype(_F32)
            pen_ref[d] = jnp.where(visible, bias, -jnp.inf)
        row = lax.broadcasted_iota(jnp.int32, (tk, LANES), 0)
        col = lax.broadcasted_iota(jnp.int32, (tk, LANES), 1)
        kpos_ref[...] = jnp.where(col < 3, row, 0).astype(_F32).astype(_BF16)

    slope_hi = slope.astype(_BF16).astype(_F32)
    slope_mid = (slope - slope_hi).astype(_BF16).astype(_F32)
    slope_lo = slope - slope_hi - slope_mid
    slope_cols = jnp.where(lane == 0, slope_hi, jnp.where(lane == 1, slope_mid,
                                                          jnp.where(lane == 2, slope_lo, 0.0)))
    slope_cols = jnp.broadcast_to(slope_cols, (tq, LANES))
    q = q_ref[0].astype(_F32)
    q_maps = jnp.concatenate(
        [jnp.concatenate([jnp.where(lane < DIFF_HEAD_DIM, q, 0.0), slope_cols], axis=1),
         jnp.concatenate([jnp.where(lane >= DIFF_HEAD_DIM, q, 0.0), slope_cols], axis=1)],
        axis=0).astype(_BF16)
    m_ref[...] = jnp.full_like(m_ref, NEG_BIG)
    l_ref[...] = jnp.zeros_like(l_ref)
    acc_ref[...] = jnp.zeros_like(acc_ref)

    def slot_tile(n):
        return jnp.where(n < own, n_past + n, n_past - 1 - (n - own))

    def slot_shift(n):
        j = slot_tile(n)
        distance = (qi * tq - j * tk).astype(_F32)
        return jnp.where(n < own, 0.0, jnp.where(j >= 0, -slope * distance, NEG_BIG))

    def tile_start(n):
        return pl.multiple_of(jnp.clip(slot_tile(n), 0, n_past + own - 1) * tk, tk)

    def scores(n):
        keys = jnp.concatenate([k_ref[0, pl.ds(tile_start(n), tk), :], kpos_ref[...]], axis=1)
        return _dot_nt(keys, q_maps)

    def put_scores(buf, s):
        for c in range(2 * tq // LANES):
            s_ref[buf, c] = s[:, c * LANES:(c + 1) * LANES]

    def softmax(buf, shift):
        for c in range(2 * tq // LANES):
            strip = slice(c * LANES, (c + 1) * LANES)
            s = s_ref[buf, c]
            m_old = m_ref[:, strip]
            m_new = jnp.maximum(m_old, jnp.max(s, axis=0, keepdims=True) + shift)
            alpha = jnp.exp2(m_old - m_new)
            p = jnp.exp2(s - (m_new - shift))
            l_ref[:, strip] = alpha * l_ref[:, strip] + jnp.sum(p, axis=0, keepdims=True)
            m_ref[:, strip] = m_new
            a_ref[buf, :, strip] = alpha
            half = c % (MXU_WIDTH // LANES)
            p_ref[buf, c // (MXU_WIDTH // LANES), :, half * LANES:(half + 1) * LANES] = p.astype(_BF16)

    def weigh(n, buf):
        values = vt_ref[0, :, pl.ds(tile_start(n), tk)]
        for c in range(2 * tq // MXU_WIDTH):
            strip = slice(c * MXU_WIDTH, (c + 1) * MXU_WIDTH)
            acc_ref[c] = a_ref[buf, :, strip] * acc_ref[c] + _dot(values, p_ref[buf, c])

    for n in range(own):
        put_scores(n, scores(n) + pen_ref[n])
    for n in range(own, DEPTH_SLOTS):
        p_ref[n] = jnp.zeros(p_ref.shape[1:], _BF16)
        a_ref[n] = jnp.ones((1, 2 * tq), _F32)

    def stage_pair(n, buf):
        for d in range(2):
            put_scores((buf + 2 + d) % DEPTH_SLOTS, scores(n + 2 + d))
        for d in range(2):
            softmax(buf + d, slot_shift(n + d))
        for d in range(2):
            weigh(n - 2 + d, (buf + 2 + d) % DEPTH_SLOTS)

    def four_slots(i, carry):
        stage_pair(4 * i, 0)
        stage_pair(4 * i + 2, 2)
        return carry

    trips = (qi + 2) // 2
    lax.fori_loop(0, trips, four_slots, 0)
    for d in range(2):
        weigh(4 * trips - 2 + d, 2 + d)

    lam = (jnp.exp(jnp.sum(lq1_ref[...] * lk1_ref[...], axis=-1, keepdims=True))
           - jnp.exp(jnp.sum(lq2_ref[...] * lk2_ref[...], axis=-1, keepdims=True)) + lam_init)
    o = jnp.concatenate([acc_ref[c] for c in range(2 * tq // MXU_WIDTH)], axis=1) / l_ref[...]
    o = o[:, :tq] - lam * o[:, tq:]
    o = o * lax.rsqrt(jnp.mean(o * o, axis=0, keepdims=True) + RMS_EPS)
    o_ref[0] = (o.T * g_ref[...] * (1.0 - lam_init)).astype(o_ref.dtype)


def _diff_prompt_call(q, k, vt, lq1, lk1, lq2, lk2, subln_g, *, tq, lam_init):
    b, nq, _ = q.shape
    tk = tq // 2
    assert 2 * DIFF_HEAD_DIM == LANES
    assert tk % CHUNK == 0 and CHUNK == 1 << CHUNK_SHIFT and tk <= 256
    q_spec, k_spec, vt_spec = _prompt_specs(nq, tq)
    lam_spec = pl.BlockSpec((1, DIFF_HEAD_DIM), lambda bi, g, qi: (0, 0))
    stat = pltpu.VMEM((1, 2 * tq), _F32)
    return pl.pallas_call(
        functools.partial(_diff_prompt_kernel, tq=tq, tk=tk, lam_init=lam_init),
        grid=(b, DIFF_HEADS, nq // tq),
        in_specs=[q_spec, k_spec, vt_spec, lam_spec, lam_spec, lam_spec, lam_spec,
                  pl.BlockSpec((1, LANES), lambda bi, g, qi: (0, 0))],
        out_specs=q_spec,
        out_shape=jax.ShapeDtypeStruct((b, nq, DIFF_WIDTH), _BF16),
        scratch_shapes=[stat, stat, pltpu.VMEM((2 * tq // MXU_WIDTH, LANES, MXU_WIDTH), _F32),
                        pltpu.VMEM((2, tk, 2 * tq), _F32), pltpu.VMEM((tk, LANES), _BF16),
                        pltpu.VMEM((DEPTH_SLOTS, 2 * tq // LANES, tk, LANES), _F32),
                        pltpu.VMEM((DEPTH_SLOTS, 2 * tq // MXU_WIDTH, tk, MXU_WIDTH), _BF16),
                        pltpu.VMEM((DEPTH_SLOTS, 1, 2 * tq), _F32)],
        compiler_params=pltpu.CompilerParams(
            dimension_semantics=("arbitrary", "arbitrary", "arbitrary"),
            vmem_limit_bytes=VMEM_LIMIT_BYTES),
        name="diff_prompt",
    )(q, k, vt, lq1, lk1, lq2, lk2, subln_g)


def _encoder_layer(x, past, w, lam_init):
    b, n, _ = x.shape
    m = b * n
    h = _ffn(x.reshape(m, D_MODEL), w["ff1_pre_g"], w["ff1_w_gate"], w["ff1_w_up"], w["ff1_w_down"],
             w["ff1_post_g"])
    lam_args = (w["lam_q1"], w["lam_k1"], w["lam_q2"], w["lam_k2"], w["subln_g"])

    def seq(a):
        return a.reshape(b, n, a.shape[-1])

    if past is None:
        sbq, sbk, sbv, dq, dk, dv, sbk16, sbvt16, dk16, dvt16 = _qkv(h, w["mix_pre_g"], w["w_in"], seq_len=n)
        sb_o = _sb_prompt_call(seq(sbq), seq(sbk16), sbvt16, t=SB_PROMPT_TILE, group=SB_PROMPT_GROUP)
        d_o = _diff_prompt_call(seq(dq), seq(dk16), dvt16, *lam_args, tq=DIFF_PROMPT_TILE, lam_init=lam_init)
        sbk, sbv = (a.reshape(b, SB_HEADS, SB_HEAD_DIM, n).transpose(0, 3, 1, 2) for a in (sbk, sbv))
    else:
        sbq, sbk, sbv, dq, dk, dv, sbk16, sbv16, dk16, dv16 = _qkv(h, w["mix_pre_g"], w["w_in"])
        sb_o = _sb_step_call(seq(sbq), seq(sbk16), seq(sbv16), past[0], past[1])
        d_o = _diff_step_call(seq(dq), seq(dk16), seq(dv16), past[2], past[3], *lam_args, lam_init=lam_init)
    y = _ffn(h, w["ff2_pre_g"], w["ff2_w_gate"], w["ff2_w_up"], w["ff2_w_down"], w["ff2_post_g"],
             mixer=(sb_o.reshape(m, SB_WIDTH), d_o.reshape(m, DIFF_WIDTH), w["w_out"], w["mix_post_g"]),
             final_g=w["final_g"])
    rows = (sbk.reshape(b, n, SB_HEADS, SB_HEAD_DIM), sbv.reshape(b, n, SB_HEADS, SB_HEAD_DIM),
            dk.reshape(b, n, DIFF_HEADS, 2 * DIFF_HEAD_DIM), dv.reshape(b, n, DIFF_HEADS, 2 * DIFF_HEAD_DIM))
    return y.reshape(b, n, D_MODEL), rows


_MATRICES = ("ff1_w_gate", "ff1_w_up", "ff1_w_down", "w_in", "w_out", "ff2_w_gate", "ff2_w_up", "ff2_w_down")


def kernel(x_prompt, x_sample, cache_sb_k, cache_sb_v, cache_diff_k, cache_diff_v, ff1_pre_g, ff1_w_gate, ff1_w_up, ff1_w_down, ff1_post_g, mix_pre_g, w_in, lam_q1, lam_k1, lam_q2, lam_k2, subln_g, w_out, mix_post_g, ff2_pre_g, ff2_w_gate, ff2_w_up, ff2_w_down, ff2_post_g, final_g):
    params = dict(ff1_pre_g=ff1_pre_g, ff1_w_gate=ff1_w_gate, ff1_w_up=ff1_w_up, ff1_w_down=ff1_w_down,
                  ff1_post_g=ff1_post_g, mix_pre_g=mix_pre_g, w_in=w_in, lam_q1=lam_q1, lam_k1=lam_k1,
                  lam_q2=lam_q2, lam_k2=lam_k2, subln_g=subln_g, w_out=w_out, mix_post_g=mix_post_g,
                  ff2_pre_g=ff2_pre_g, ff2_w_gate=ff2_w_gate, ff2_w_up=ff2_w_up, ff2_w_down=ff2_w_down,
                  ff2_post_g=ff2_post_g, final_g=final_g)
    yp, ys = x_prompt, x_sample
    rows_p, rows_s = [], []
    for l in range(DEPTH):
        lam_init = 0.8 - 0.6 * math.exp(-0.3 * l)
        w = {name: (p[l].astype(_BF16) if name in _MATRICES else p[l][None, :].astype(_F32))
             for name, p in params.items()}
        yp, rp = _encoder_layer(yp, None, w, lam_init)
        past = tuple([c[l].transpose(0, 2, 3, 1).reshape(c.shape[1], SB_WIDTH, c.shape[2])
                      for c in (cache_sb_k, cache_sb_v)]
                     + [c[l].reshape(c.shape[1], -1, c.shape[-1]) for c in (cache_diff_k, cache_diff_v)])
        ys, rs = _encoder_layer(ys, past, w, lam_init)
        rows_p.append(rp)
        rows_s.append(rs)
    stacked_p = [jnp.stack(r, axis=0) for r in zip(*rows_p)]
    stacked_s = [jnp.stack(r, axis=0) for r in zip(*rows_s)]
    return (yp, ys, *stacked_p, *stacked_s)
```

```python
import functools
import math

import jax
import jax.numpy as jnp
from jax import lax
from jax.experimental import pallas as pl
from jax.experimental.pallas import tpu as pltpu

D_MODEL = 1024
DEPTH = 1
CHUNK = 64
CHUNK_SHIFT = 6
SB_HEADS = 8
SB_HEAD_DIM = 64
DIFF_HEADS = 4
DIFF_HEAD_DIM = 64
SB_WIDTH = SB_HEADS * SB_HEAD_DIM
DIFF_WIDTH = DIFF_HEADS * 2 * DIFF_HEAD_DIM
MIX_WIDTH = SB_WIDTH + DIFF_WIDTH
IN_WIDTH = 3 * SB_WIDTH + 3 * DIFF_WIDTH
D_FF = 2816
RMS_EPS = 1e-6

LANES = 128
MXU_WIDTH = 256
VMEM_LIMIT_BYTES = 56 * 1024 * 1024
F32_EXP2_ZERO = 151.0
NEG_BIG = -1e30
LOG2E = math.log2(math.e)
DEPTH_SLOTS = 4
SB_PROMPT_TILE = 256
SB_PROMPT_GROUP = 2
DIFF_PROMPT_TILE = 512

_F32 = jnp.float32
_BF16 = jnp.bfloat16


def _rms(x, g):
    return x * lax.rsqrt(jnp.mean(x * x, axis=-1, keepdims=True) + RMS_EPS) * g


def _stick_logs(z):
    log_beta = jnp.minimum(z, 0.0) - jnp.log2(1.0 + jnp.exp2(-jnp.abs(z)))
    return log_beta, log_beta - z


def _split_hi_lo(x):
    hi = lax.bitcast_convert_type(lax.bitcast_convert_type(x, jnp.uint32) & jnp.uint32(0xFFFF0000), _F32)
    return hi.astype(_BF16), (x - hi).astype(_BF16)


def _dot(a, b):
    return jnp.dot(a, b, preferred_element_type=_F32)


def _dot_nt(a, b):
    return lax.dot_general(a, b, (((1,), (1,)), ((), ())), preferred_element_type=_F32)


def _ffn_kernel(*refs, mixer, final_norm):
    refs = list(refs)
    o_ref = refs.pop()
    x = refs.pop(0)[...]
    if mixer:
        sbo_ref, do_ref, wo_ref, mix_g_ref = refs[:4]
        refs = refs[4:]
        mix = _dot(sbo_ref[...], wo_ref[:SB_WIDTH, :]) + _dot(do_ref[...], wo_ref[SB_WIDTH:, :])
        x = x + _rms(mix, mix_g_ref[...])
    pre_ref, wg_ref, wu_ref, wd_ref, post_ref = refs[:5]
    xn = _rms(x, pre_ref[...]).astype(_BF16)
    g = _dot(xn, wg_ref[...])
    u = _dot(xn, wu_ref[...])
    a = (g * jax.nn.sigmoid(g) * u).astype(_BF16)
    h = x + 0.5 * _rms(_dot(a, wd_ref[...]), post_ref[...])
    if final_norm:
        h = _rms(h, refs[5][...])
    o_ref[...] = h


def _ffn(x, pre_g, wg, wu, wd, post_g, *, mixer=None, final_g=None):
    m = x.shape[0]
    tm = min(512, m)
    assert m % tm == 0

    def resident(shape):
        return pl.BlockSpec(shape, lambda i: (0, 0), pipeline_mode=pl.Buffered(1))

    row = pl.BlockSpec((tm, D_MODEL), lambda i: (i, 0))
    vec = resident((1, D_MODEL))
    in_specs, args = [row], [x]
    if mixer is not None:
        half = pl.BlockSpec((tm, SB_WIDTH), lambda i: (i, 0))
        in_specs += [half, half, resident((MIX_WIDTH, D_MODEL)), vec]
        args += list(mixer)
    in_specs += [vec, resident((D_MODEL, D_FF)), resident((D_MODEL, D_FF)), resident((D_FF, D_MODEL)), vec]
    args += [pre_g, wg, wu, wd, post_g]
    if final_g is not None:
        in_specs.append(vec)
        args.append(final_g)
    return pl.pallas_call(
        functools.partial(_ffn_kernel, mixer=mixer is not None, final_norm=final_g is not None),
        grid=(m // tm,),
        in_specs=in_specs,
        out_specs=row,
        out_shape=jax.ShapeDtypeStruct((m, D_MODEL), _F32),
        compiler_params=pltpu.CompilerParams(
            dimension_semantics=("parallel",), vmem_limit_bytes=VMEM_LIMIT_BYTES),
        name="ffn_mix_final" if mixer is not None else "ffn",
    )(*args)


def _qkv_kernel(h_ref, g_ref, w_ref, sbq_ref, sbk_ref, sbv_ref, dq_ref, dk_ref, dv_ref,
                sbk16_ref, sbv16_ref, dk16_ref, dv16_ref, *, transposed):
    hn = _rms(h_ref[...], g_ref[...]).astype(_BF16)

    def proj(idx):
        return _dot(hn, w_ref[:, idx * SB_WIDTH:(idx + 1) * SB_WIDTH])

    sbq_ref[...] = (proj(0) * (SB_HEAD_DIM ** -0.5 * LOG2E)).astype(_BF16)
    dq_ref[...] = (proj(3) * (DIFF_HEAD_DIM ** -0.5 * LOG2E)).astype(_BF16)

    def store_rows(ref, p):
        head_dim = ref.shape[1]
        heads = p.shape[1] // head_dim
        for head in range(heads):
            ref[pl.ds(head, p.shape[0], stride=heads), :] = p[:, head * head_dim:(head + 1) * head_dim]

    for idx, full_ref, half_ref in ((4, dk_ref, dk16_ref), (5, dv_ref, dv16_ref)):
        p = proj(idx)
        store_rows(full_ref, p)
        if transposed and half_ref is dv16_ref:
            half_ref[0] = p.T.astype(_BF16)
        else:
            half_ref[...] = p.astype(_BF16)
    for idx, full_ref, half_ref in ((1, sbk_ref, sbk16_ref), (2, sbv_ref, sbv16_ref)):
        p = proj(idx)
        if transposed:
            p_t = p.T
            full_ref[0] = p_t
            if half_ref is sbv16_ref:
                half_ref[0] = p_t.astype(_BF16)
            else:
                half_ref[...] = p.astype(_BF16)
        else:
            store_rows(full_ref, p)
            half_ref[...] = p.astype(_BF16)


def _qkv(h, g, w, *, seq_len=None):
    assert SB_WIDTH == DIFF_WIDTH
    assert math.log2(SB_HEAD_DIM) % 2 == 0 and math.log2(DIFF_HEAD_DIM) % 2 == 0
    m = h.shape[0]
    tm = min(512, m)
    assert m % tm == 0
    row = pl.BlockSpec((tm, D_MODEL), lambda i: (i, 0))
    out = pl.BlockSpec((tm, SB_WIDTH), lambda i: (i, 0))
    b16 = jax.ShapeDtypeStruct((m, SB_WIDTH), _BF16)

    def cache_rows(heads, head_dim):
        return (jax.ShapeDtypeStruct((m * heads, head_dim), _F32),
                pl.BlockSpec((tm * heads, head_dim), lambda i: (i, 0)))

    sb_rows, sb_out = cache_rows(SB_HEADS, SB_HEAD_DIM)
    d_rows, d_out = cache_rows(DIFF_HEADS, 2 * DIFF_HEAD_DIM)
    v_out, v16 = out, b16
    if seq_len is not None:
        assert seq_len % tm == 0
        tiles = seq_len // tm
        v_out = pl.BlockSpec((1, SB_WIDTH, tm), lambda i: (i // tiles, 0, i % tiles))
        v16 = jax.ShapeDtypeStruct((m // seq_len, SB_WIDTH, seq_len), _BF16)
        sb_rows, sb_out = jax.ShapeDtypeStruct(v16.shape, _F32), v_out
    return pl.pallas_call(
        functools.partial(_qkv_kernel, transposed=seq_len is not None),
        grid=(m // tm,),
        in_specs=[row, pl.BlockSpec((1, D_MODEL), lambda i: (0, 0)),
                  pl.BlockSpec((D_MODEL, IN_WIDTH), lambda i: (0, 0))],
        out_specs=[out, sb_out, sb_out, out, d_out, d_out, out, v_out, out, v_out],
        out_shape=[b16, sb_rows, sb_rows, b16, d_rows, d_rows, b16, v16, b16, v16],
        compiler_params=pltpu.CompilerParams(
            dimension_semantics=("parallel",), vmem_limit_bytes=VMEM_LIMIT_BYTES),
        name="qkv",
    )(h, g, w)


def _sb_step_kernel(q_ref, kn_ref, vn_ref, kc_ref, vc_ref, o_ref, acc_ref, c_ref, *, tk):
    n = q_ref.shape[1]
    past = kc_ref.shape[2]

    def later(t):
        return (lax.broadcasted_iota(jnp.int32, (t, t), 0)
                > lax.broadcasted_iota(jnp.int32, (t, t), 1)).astype(_BF16)

    later_new, later_tile = later(n), later(tk)
    newer = (lax.broadcasted_iota(jnp.int32, (n, n), 1) < lax.broadcasted_iota(jnp.int32, (n, n), 0))

    def walk(z, c, later_t, mask):
        log_beta, log_stay = _stick_logs(z)
        if mask is not None:
            log_stay = jnp.where(mask, log_stay, 0.0)
        sums = _dot(jnp.concatenate(_split_hi_lo(log_stay), axis=0), later_t)
        w = jnp.exp2(log_beta + (sums[:n] + sums[n:] + c))
        if mask is not None:
            w = jnp.where(mask, w, 0.0)
        return c + jnp.sum(log_stay, axis=-1, keepdims=True), w.astype(_BF16)

    def head_cols(head):
        return slice(head * SB_HEAD_DIM, (head + 1) * SB_HEAD_DIM)

    def walk_cache(start, c_of, first):
        for head in range(SB_HEADS):
            rows = head_cols(head)
            k_t = kc_ref[0, rows, pl.ds(start, tk)].astype(_BF16)
            v_t = vc_ref[0, rows, pl.ds(start, tk)].astype(_BF16)
            c, w = walk(_dot(q_ref[0, :, rows], k_t), c_of(head), later_tile, None)
            o = _dot_nt(w, v_t)
            c_ref[head] = c
            acc_ref[head] = o + first[head] if first else acc_ref[head] + o

    new = []
    for head in range(SB_HEADS):
        cols = head_cols(head)
        c, w = walk(_dot_nt(q_ref[0, :, cols], kn_ref[0, :, cols]), jnp.zeros((n, 1), _F32), later_new, newer)
        new.append((c, _dot(w, vn_ref[0, :, cols])))
    walk_cache(past - tk, lambda head: new[head][0], [o for _, o in new])

    def cond(state):
        j, c_max = state
        return (j >= 0) & (c_max > -F32_EXP2_ZERO)

    def body(state):
        j, _ = state
        walk_cache(pl.multiple_of(j * tk, tk), lambda head: c_ref[head], None)
        return j - 1, jnp.max(c_ref[...])

    lax.while_loop(cond, body, (past // tk - 2, jnp.max(c_ref[...])))
    o_ref[0] = jnp.concatenate([acc_ref[head] for head in range(SB_HEADS)], axis=-1).astype(o_ref.dtype)


def _sb_step_call(q, k_new, v_new, k_cache, v_cache):
    b, n, _ = q.shape
    past = k_cache.shape[2]
    tk = min(256, past)
    assert past % tk == 0
    new = pl.BlockSpec((1, n, SB_WIDTH), lambda i: (i, 0, 0))
    cache = pl.BlockSpec((1, SB_WIDTH, past), lambda i: (i, 0, 0))
    return pl.pallas_call(
        functools.partial(_sb_step_kernel, tk=tk),
        grid=(b,),
        in_specs=[new, new, new, cache, cache],
        out_specs=new,
        out_shape=jax.ShapeDtypeStruct((b, n, SB_WIDTH), _BF16),
        scratch_shapes=[pltpu.VMEM((SB_HEADS, n, SB_HEAD_DIM), _F32), pltpu.VMEM((SB_HEADS, n, 1), _F32)],
        compiler_params=pltpu.CompilerParams(
            dimension_semantics=("arbitrary",), vmem_limit_bytes=VMEM_LIMIT_BYTES),
        name="sb_step",
    )(q, k_new, v_new, k_cache, v_cache)


def _alibi_slope_log2(head):
    assert 8 % DIFF_HEADS == 0
    exponent = (8 // DIFF_HEADS) * (head + 1)
    slope = lax.bitcast_convert_type(jnp.full((1, 1), (127 - exponent) << 23, jnp.int32), _F32)
    return slope * LOG2E


def _diff_step_kernel(q_ref, kn_ref, vn_ref, kc_ref, vc_ref, lq1_ref, lk1_ref, lq2_ref, lk2_ref, g_ref,
                      o_ref, *, lam_init):
    n = q_ref.shape[1]
    past = kc_ref.shape[1] // DIFF_HEADS

    def cache_head(ref, head):
        return ref[0, pl.ds(head, past, stride=DIFF_HEADS), :].astype(_BF16)
    lane = lax.broadcasted_iota(jnp.int32, (1, LANES), 1)
    q_idx_c = lax.broadcasted_iota(jnp.int32, (2 * n, past), 0) & (n - 1)
    k_idx_c = lax.broadcasted_iota(jnp.int32, (2 * n, past), 1)
    q_idx_n = lax.broadcasted_iota(jnp.int32, (2 * n, n), 0) & (n - 1)
    k_idx_n = lax.broadcasted_iota(jnp.int32, (2 * n, n), 1)
    distance_c = (past + q_idx_c - k_idx_c).astype(_F32)
    distance_n = jnp.abs(q_idx_n - k_idx_n).astype(_F32)
    visible_n =(lax.shift_right_logical(past + k_idx_n, CHUNK_SHIFT)
                 <= lax.shift_right_logical(past + q_idx_n, CHUNK_SHIFT))
    lam = (jnp.exp(jnp.sum(lq1_ref[...] * lk1_ref[...], axis=-1, keepdims=True))
           - jnp.exp(jnp.sum(lq2_ref[...] * lk2_ref[...], axis=-1, keepdims=True)) + lam_init)
    for head in range(DIFF_HEADS):
        cols = slice(head * LANES, (head + 1) * LANES)
        slope = _alibi_slope_log2(head)
        q = q_ref[0, :, cols].astype(_F32)
        q_maps = jnp.concatenate([jnp.where(lane < DIFF_HEAD_DIM, q, 0.0),
                                  jnp.where(lane >= DIFF_HEAD_DIM, q, 0.0)], axis=0).astype(_BF16)
        s_c = _dot_nt(q_maps, cache_head(kc_ref, head)) - slope * distance_c
        s_n = jnp.where(visible_n, _dot_nt(q_maps, kn_ref[0, :, cols]) - slope * distance_n, -jnp.inf)
        m = jnp.maximum(jnp.max(s_c, axis=-1, keepdims=True), jnp.max(s_n, axis=-1, keepdims=True))
        p_c = jnp.exp2(s_c - m)
        p_n = jnp.exp2(s_n - m)
        l = jnp.sum(p_c, axis=-1, keepdims=True) + jnp.sum(p_n, axis=-1, keepdims=True)
        o = (_dot(p_c.astype(_BF16), cache_head(vc_ref, head))
             + _dot(p_n.astype(_BF16), vn_ref[0, :, cols])) / l
        o = o[:n] - lam * o[n:]
        o_ref[0, :, cols] = (_rms(o, g_ref[...]) * (1.0 - lam_init)).astype(o_ref.dtype)


def _diff_step_call(q, k_new, v_new, k_cache, v_cache, lq1, lk1, lq2, lk2, subln_g, *, lam_init):
    b, n, _ = q.shape
    rows = k_cache.shape[1]
    assert 2 * DIFF_HEAD_DIM == LANES and CHUNK == 1 << CHUNK_SHIFT and n & (n - 1) == 0
    new = pl.BlockSpec((1, n, DIFF_WIDTH), lambda i: (i, 0, 0))
    cache = pl.BlockSpec((1, rows, LANES), lambda i: (i, 0, 0))
    lam_spec = pl.BlockSpec((1, DIFF_HEAD_DIM), lambda i: (0, 0))
    return pl.pallas_call(
        functools.partial(_diff_step_kernel, lam_init=lam_init),
        grid=(b,),
        in_specs=[new, new, new, cache, cache, lam_spec, lam_spec, lam_spec, lam_spec,
                  pl.BlockSpec((1, LANES), lambda i: (0, 0))],
        out_specs=new,
        out_shape=jax.ShapeDtypeStruct((b, n, DIFF_WIDTH), _BF16),
        compiler_params=pltpu.CompilerParams(
            dimension_semantics=("arbitrary",), vmem_limit_bytes=VMEM_LIMIT_BYTES),
        name="diff_step",
    )(q, k_new, v_new, k_cache, v_cache, lq1, lk1, lq2, lk2, subln_g)


def _prompt_specs(nq, t):
    assert nq % t == 0 and t & (t - 1) == 0
    q_spec = pl.BlockSpec((1, t, LANES), lambda bi, g, qi: (bi, qi, g))
    k_spec = pl.BlockSpec((1, nq, LANES), lambda bi, g, qi: (bi, 0, g))
    vt_spec = pl.BlockSpec((1, LANES, nq), lambda bi, g, qi: (bi, g, 0))
    return q_spec, k_spec, vt_spec


def _sb_prompt_kernel(q_ref, k_ref, vt_ref, o_ref, acc_ref, c_ref, z_ref, lb_ref, hl_ref, w_ref, *,
                      t, group):
    assert LANES == 2 * SB_HEAD_DIM
    strips = 2 * t // LANES
    qi = pl.program_id(2)
    lane = lax.broadcasted_iota(jnp.int32, (1, LANES), 1)
    key = lax.broadcasted_iota(jnp.int32, (t, LANES), 0)
    qry = lax.broadcasted_iota(jnp.int32, (t, LANES), 1)
    later = (lax.broadcasted_iota(jnp.int32, (t, t), 1)
             > lax.broadcasted_iota(jnp.int32, (t, t), 0)).astype(_BF16)
    def tile_queries(g):
        q = q_ref[0, g * t:(g + 1) * t, :].astype(_F32)
        return jnp.concatenate([jnp.where(lane < SB_HEAD_DIM, q, 0.0),
                                jnp.where(lane >= SB_HEAD_DIM, q, 0.0)], axis=0).astype(_BF16)

    q_heads = [tile_queries(g) for g in range(group)]

    def walk_tile(g, j, c, own, buf):
        k0 = pl.multiple_of(j * t, t)
        z = _dot_nt(k_ref[0, pl.ds(k0, t), :], q_heads[g])
        for s in range(strips):
            z_ref[buf, s] = z[:, s * LANES:(s + 1) * LANES]
        masks, c_new = [], []
        for s in range(strips):
            cols = slice(s * LANES, (s + 1) * LANES)
            zs = z_ref[buf, s]
            lb_ref[buf, s], log_stay = _stick_logs(zs)
            masks.append(key < ((s * LANES + qry) & (t - 1)) if own else None)
            if own:
                log_stay = jnp.where(masks[s], log_stay, 0.0)
            hl_ref[buf, s] = jnp.concatenate(_split_hi_lo(log_stay), axis=1)
            c_new.append(c[:, cols] + jnp.sum(log_stay, axis=0, keepdims=True))
        for s in range(strips):
            cols = slice(s * LANES, (s + 1) * LANES)
            sums = _dot(later, hl_ref[buf, s])
            w = jnp.exp2(lb_ref[buf, s] + (sums[:, :LANES] + sums[:, LANES:] + c[:, cols]))
            if own:
                w = jnp.where(masks[s], w, 0.0)
            head, part = divmod(s, t // LANES)
            w_ref[buf, head, :, part * LANES:(part + 1) * LANES] = w.astype(_BF16)
        vt = vt_ref[0, :, pl.ds(k0, t)]
        o = jnp.concatenate([_dot(vt[:SB_HEAD_DIM], w_ref[buf, 0]), _dot(vt[SB_HEAD_DIM:], w_ref[buf, 1])],
                            axis=0)
        return jnp.concatenate(c_new, axis=1), o

    def first_tiles(has_prev):
        for g in range(group):
            c, o = walk_tile(g, qi * group + g, jnp.zeros((1, 2 * t), _F32), True, 2 * g)
            if has_prev(g):
                c, o_prev = walk_tile(g, qi * group + g - 1, c, False, 2 * g + 1)
                o = o + o_prev
            c_ref[g] = c
            acc_ref[g] = o

    @pl.when(qi == 0)
    def _():
        first_tiles(lambda g: g > 0)

    @pl.when(qi > 0)
    def _():
        first_tiles(lambda g: True)

    def older_tile(g, step):
        return qi * group + g - 2 - step

    def any_active(step):
        flags = [(older_tile(g, step) >= 0) & (jnp.max(c_ref[g]) > -F32_EXP2_ZERO) for g in range(group)]
        return functools.reduce(jnp.logical_or, flags)

    def walk_older(state):
        step, _ = state
        for g in range(group):
            @pl.when((older_tile(g, step) >= 0) & (jnp.max(c_ref[g]) > -F32_EXP2_ZERO))
            def _(g=g):
                c, o = walk_tile(g, older_tile(g, step), c_ref[g], False, 2 * g)
                c_ref[g] = c
                acc_ref[g] += o
        return step + 1, any_active(step + 1)

    lax.while_loop(lambda state: state[1], walk_older, (0, any_active(0)))
    for g in range(group):
        o_ref[0, g * t:(g + 1) * t, :] = acc_ref[g].T.astype(o_ref.dtype)


def _sb_prompt_call(q, k, vt, *, t, group):
    b, nq, _ = q.shape
    q_spec, k_spec, vt_spec = _prompt_specs(nq, group * t)
    sets = 2 * group
    strips = 2 * t // LANES
    return pl.pallas_call(
        functools.partial(_sb_prompt_kernel, t=t, group=group),
        grid=(b, SB_WIDTH // LANES, nq // (group * t)),
        in_specs=[q_spec, k_spec, vt_spec],
        out_specs=q_spec,
        out_shape=jax.ShapeDtypeStruct((b, nq, SB_WIDTH), _BF16),
        scratch_shapes=[pltpu.VMEM((group, LANES, t), _F32), pltpu.VMEM((group, 1, 2 * t), _F32),
                        pltpu.VMEM((sets, strips, t, LANES), _F32),
                        pltpu.VMEM((sets, strips, t, LANES), _F32),
                        pltpu.VMEM((sets, strips, t, 2 * LANES), _BF16),
                        pltpu.VMEM((sets, 2, t, t), _BF16)],
        compiler_params=pltpu.CompilerParams(
            dimension_semantics=("parallel", "parallel", "arbitrary"),
            vmem_limit_bytes=VMEM_LIMIT_BYTES),
        name="sb_prompt",
    )(q, k, vt)


def _diff_prompt_kernel(q_ref, k_ref, vt_ref, lq1_ref, lk1_ref, lq2_ref, lk2_ref, g_ref, o_ref,
                        m_ref, l_ref, acc_ref, pen_ref, kpos_ref, knorm_ref, s_ref, p_ref, a_ref,
                        *, tq, tk, lam_init):
    own = 2
    assert tq == own * tk
    head = pl.program_id(1)
    qi = pl.program_id(2)
    n_past = qi * own
    lane = lax.broadcasted_iota(jnp.int32, (1, LANES), 1)
    slope = _alibi_slope_log2(head)

    @pl.when(qi == 0)
    def _():
        key = lax.broadcasted_iota(jnp.int32, (tk, 2 * tq), 0)
        qry = lax.broadcasted_iota(jnp.int32, (tk, 2 * tq), 1) & (tq - 1)
        for d in range(own):
            k_pos = d * tk + key
            visible = lax.shift_right_logical(k_pos, CHUNK_SHIFT) <= lax.shift_right_logical(qry, CHUNK_SHIFT)
            bias = slope * (qry - key - jnp.abs(qry - k_pos)).astype(_F32)
            pen_ref[d] = jnp.where(visible, bias, -jnp.inf)
        row = lax.broadcasted_iota(jnp.int32, (tk, LANES), 0)
        col = lax.broadcasted_iota(jnp.int32, (tk, LANES), 1)
        kpos_ref[...] = jnp.where(col < 3, row, 0).astype(_F32).astype(_BF16)
        k_sq = jnp.square(k_ref[0].astype(_F32))
        for c in range(2):
            in_map = (lane >= c * DIFF_HEAD_DIM) & (lane < (c + 1) * DIFF_HEAD_DIM)
            norm_sq = jnp.sum(jnp.where(in_map, k_sq, 0.0), axis=1, keepdims=True)
            knorm_ref[c] = jnp.broadcast_to(jnp.sqrt(jnp.max(norm_sq, axis=0, keepdims=True)), (1, LANES))

    slope_hi = slope.astype(_BF16).astype(_F32)
    slope_mid = (slope - slope_hi).astype(_BF16).astype(_F32)
    slope_lo = slope - slope_hi - slope_mid
    slope_cols = jnp.where(lane == 0, slope_hi, jnp.where(lane == 1, slope_mid,
                                                          jnp.where(lane == 2, slope_lo, 0.0)))
    slope_cols = jnp.broadcast_to(slope_cols, (tq, LANES))
    q = q_ref[0].astype(_F32)
    q_maps = jnp.concatenate(
        [jnp.concatenate([jnp.where(lane < DIFF_HEAD_DIM, q, 0.0), slope_cols], axis=1),
         jnp.concatenate([jnp.where(lane >= DIFF_HEAD_DIM, q, 0.0), slope_cols], axis=1)],
        axis=0).astype(_BF16)
    m_ref[...] = jnp.full_like(m_ref, NEG_BIG)
    l_ref[...] = jnp.zeros_like(l_ref)
    acc_ref[...] = jnp.zeros_like(acc_ref)

    def slot_tile(n):
        return jnp.where(n < own, n_past + n, n_past - 1 - (n - own))

    def slot_shift(n):
        j = slot_tile(n)
        distance = (qi * tq - j * tk).astype(_F32)
        return jnp.where(n < own, 0.0, jnp.where(j >= 0, -slope * distance, NEG_BIG))

    def tile_start(n):
        return pl.multiple_of(jnp.clip(slot_tile(n), 0, n_past + own - 1) * tk, tk)

    def scores(n):
        keys = jnp.concatenate([k_ref[0, pl.ds(tile_start(n), tk), :], kpos_ref[...]], axis=1)
        return _dot_nt(keys, q_maps)

    def put_scores(buf, s):
        for c in range(2 * tq // LANES):
            s_ref[buf, c] = s[:, c * LANES:(c + 1) * LANES]

    def softmax(buf, shift):
        for c in range(2 * tq // LANES):
            strip = slice(c * LANES, (c + 1) * LANES)
            s = s_ref[buf, c]
            m_old = m_ref[:, strip]
            m_new = jnp.maximum(m_old, jnp.max(s, axis=0, keepdims=True) + shift)
            alpha = jnp.exp2(m_old - m_new)
            p = jnp.exp2(s - (m_new - shift))
            l_ref[:, strip] = alpha * l_ref[:, strip] + jnp.sum(p, axis=0, keepdims=True)
            m_ref[:, strip] = m_new
            a_ref[buf, :, strip] = alpha
            half = c % (MXU_WIDTH // LANES)
            p_ref[buf, c // (MXU_WIDTH // LANES), :, half * LANES:(half + 1) * LANES] = p.astype(_BF16)

    def weigh(n, buf):
        values = vt_ref[0, :, pl.ds(tile_start(n), tk)]
        for c in range(2 * tq // MXU_WIDTH):
            strip = slice(c * MXU_WIDTH, (c + 1) * MXU_WIDTH)
            acc_ref[c] = a_ref[buf, :, strip] * acc_ref[c] + _dot(values, p_ref[buf, c])

    for n in range(own):
        put_scores(n, scores(n) + pen_ref[n])
    for n in range(own, DEPTH_SLOTS):
        p_ref[n] = jnp.zeros(p_ref.shape[1:], _BF16)
        a_ref[n] = jnp.ones((1, 2 * tq), _F32)

    def stage_pair(n, buf):
        for d in range(2):
            put_scores((buf + 2 + d) % DEPTH_SLOTS, scores(n + 2 + d))
        for d in range(2):
            softmax(buf + d, slot_shift(n + d))
        for d in range(2):
            weigh(n - 2 + d, (buf + 2 + d) % DEPTH_SLOTS)

    def four_slots(i, carry):
        stage_pair(4 * i, 0)
        stage_pair(4 * i + 2, 2)
        return carry

    q_sq = jnp.square(q)
    reach = jnp.zeros((1, 1), _F32)
    for c in range(2):
        in_map = (lane >= c * DIFF_HEAD_DIM) & (lane < (c + 1) * DIFF_HEAD_DIM)
        q_norm = jnp.sqrt(jnp.max(jnp.sum(jnp.where(in_map, q_sq, 0.0), axis=1, keepdims=True),
                                  axis=0, keepdims=True))
        reach = jnp.maximum(reach, (2.0 * q_norm * knorm_ref[c][:, :1] + F32_EXP2_ZERO) / slope)
    tiles_in_reach = jnp.minimum(jnp.floor((reach - 1.0) / tk) + 1.0, float(1 << 20))
    n_walk = jnp.minimum(jnp.max(tiles_in_reach).astype(jnp.int32), n_past)
    trips = (own + n_walk + DEPTH_SLOTS - 1) // DEPTH_SLOTS
    lax.fori_loop(0, trips, four_slots, 0)
    for d in range(2):
        weigh(4 * trips - 2 + d, 2 + d)

    lam = (jnp.exp(jnp.sum(lq1_ref[...] * lk1_ref[...], axis=-1, keepdims=True))
           - jnp.exp(jnp.sum(lq2_ref[...] * lk2_ref[...], axis=-1, keepdims=True)) + lam_init)
    o = jnp.concatenate([acc_ref[c] for c in range(2 * tq // MXU_WIDTH)], axis=1) / l_ref[...]
    o = o[:, :tq] - lam * o[:, tq:]
    o = o * lax.rsqrt(jnp.mean(o * o, axis=0, keepdims=True) + RMS_EPS)
    o_ref[0] = (o.T * g_ref[...] * (1.0 - lam_init)).astype(o_ref.dtype)


def _diff_prompt_call(q, k, vt, lq1, lk1, lq2, lk2, subln_g, *, tq, lam_init):
    b, nq, _ = q.shape
    tk = tq // 2
    assert 2 * DIFF_HEAD_DIM == LANES
    assert tk % CHUNK == 0 and CHUNK == 1 << CHUNK_SHIFT and tk <= 256
    q_spec, k_spec, vt_spec = _prompt_specs(nq, tq)
    lam_spec = pl.BlockSpec((1, DIFF_HEAD_DIM), lambda bi, g, qi: (0, 0))
    stat = pltpu.VMEM((1, 2 * tq), _F32)
    return pl.pallas_call(
        functools.partial(_diff_prompt_kernel, tq=tq, tk=tk, lam_init=lam_init),
        grid=(b, DIFF_HEADS, nq // tq),
        in_specs=[q_spec, k_spec, vt_spec, lam_spec, lam_spec, lam_spec, lam_spec,
                  pl.BlockSpec((1, LANES), lambda bi, g, qi: (0, 0))],
        out_specs=q_spec,
        out_shape=jax.ShapeDtypeStruct((b, nq, DIFF_WIDTH), _BF16),
        scratch_shapes=[stat, stat, pltpu.VMEM((2 * tq // MXU_WIDTH, LANES, MXU_WIDTH), _F32),
                        pltpu.VMEM((2, tk, 2 * tq), _F32), pltpu.VMEM((tk, LANES), _BF16),
                        pltpu.VMEM((2, 1, LANES), _F32),
                        pltpu.VMEM((DEPTH_SLOTS, 2 * tq // LANES, tk, LANES), _F32),
                        pltpu.VMEM((DEPTH_SLOTS, 2 * tq // MXU_WIDTH, tk, MXU_WIDTH), _BF16),
                        pltpu.VMEM((DEPTH_SLOTS, 1, 2 * tq), _F32)],
        compiler_params=pltpu.CompilerParams(
            dimension_semantics=("arbitrary", "arbitrary", "arbitrary"),
            vmem_limit_bytes=VMEM_LIMIT_BYTES),
        name="diff_prompt",
    )(q, k, vt, lq1, lk1, lq2, lk2, subln_g)


def _encoder_layer(x, past, w, lam_init):
    b, n, _ = x.shape
    m = b * n
    h = _ffn(x.reshape(m, D_MODEL), w["ff1_pre_g"], w["ff1_w_gate"], w["ff1_w_up"], w["ff1_w_down"],
             w["ff1_post_g"])
    lam_args = (w["lam_q1"], w["lam_k1"], w["lam_q2"], w["lam_k2"], w["subln_g"])

    def seq(a):
        return a.reshape(b, n, a.shape[-1])

    if past is None:
        sbq, sbk, sbv, dq, dk, dv, sbk16, sbvt16, dk16, dvt16 = _qkv(h, w["mix_pre_g"], w["w_in"], seq_len=n)
        sb_o = _sb_prompt_call(seq(sbq), seq(sbk16), sbvt16, t=SB_PROMPT_TILE, group=SB_PROMPT_GROUP)
        d_o = _diff_prompt_call(seq(dq), seq(dk16), dvt16, *lam_args, tq=DIFF_PROMPT_TILE, lam_init=lam_init)
        sbk, sbv = (a.reshape(b, SB_HEADS, SB_HEAD_DIM, n).transpose(0, 3, 1, 2) for a in (sbk, sbv))
    else:
        sbq, sbk, sbv, dq, dk, dv, sbk16, sbv16, dk16, dv16 = _qkv(h, w["mix_pre_g"], w["w_in"])
        sb_o = _sb_step_call(seq(sbq), seq(sbk16), seq(sbv16), past[0], past[1])
        d_o = _diff_step_call(seq(dq), seq(dk16), seq(dv16), past[2], past[3], *lam_args, lam_init=lam_init)
    y = _ffn(h, w["ff2_pre_g"], w["ff2_w_gate"], w["ff2_w_up"], w["ff2_w_down"], w["ff2_post_g"],
             mixer=(sb_o.reshape(m, SB_WIDTH), d_o.reshape(m, DIFF_WIDTH), w["w_out"], w["mix_post_g"]),
             final_g=w["final_g"])
    rows = (sbk.reshape(b, n, SB_HEADS, SB_HEAD_DIM), sbv.reshape(b, n, SB_HEADS, SB_HEAD_DIM),
            dk.reshape(b, n, DIFF_HEADS, 2 * DIFF_HEAD_DIM), dv.reshape(b, n, DIFF_HEADS, 2 * DIFF_HEAD_DIM))
    return y.reshape(b, n, D_MODEL), rows


_MATRICES = ("ff1_w_gate", "ff1_w_up", "ff1_w_down", "w_in", "w_out", "ff2_w_gate", "ff2_w_up", "ff2_w_down")


def kernel(x_prompt, x_sample, cache_sb_k, cache_sb_v, cache_diff_k, cache_diff_v, ff1_pre_g, ff1_w_gate, ff1_w_up, ff1_w_down, ff1_post_g, mix_pre_g, w_in, lam_q1, lam_k1, lam_q2, lam_k2, subln_g, w_out, mix_post_g, ff2_pre_g, ff2_w_gate, ff2_w_up, ff2_w_down, ff2_post_g, final_g):
    params = dict(ff1_pre_g=ff1_pre_g, ff1_w_gate=ff1_w_gate, ff1_w_up=ff1_w_up, ff1_w_down=ff1_w_down,
                  ff1_post_g=ff1_post_g, mix_pre_g=mix_pre_g, w_in=w_in, lam_q1=lam_q1, lam_k1=lam_k1,
                  lam_q2=lam_q2, lam_k2=lam_k2, subln_g=subln_g, w_out=w_out, mix_post_g=mix_post_g,
                  ff2_pre_g=ff2_pre_g, ff2_w_gate=ff2_w_gate, ff2_w_up=ff2_w_up, ff2_w_down=ff2_w_down,
                  ff2_post_g=ff2_post_g, final_g=final_g)
    yp, ys = x_prompt, x_sample
    rows_p, rows_s = [], []
    for l in range(DEPTH):
        lam_init = 0.8 - 0.6 * math.exp(-0.3 * l)
        w = {name: (p[l].astype(_BF16) if name in _MATRICES else p[l][None, :].astype(_F32))
             for name, p in params.items()}
        yp, rp = _encoder_layer(yp, None, w, lam_init)
        past = tuple([c[l].transpose(0, 2, 3, 1).reshape(c.shape[1], SB_WIDTH, c.shape[2])
                      for c in (cache_sb_k, cache_sb_v)]
                     + [c[l].reshape(c.shape[1], -1, c.shape[-1]) for c in (cache_diff_k, cache_diff_v)])
        ys, rs = _encoder_layer(ys, past, w, lam_init)
        rows_p.append(rp)
        rows_s.append(rs)
    stacked_p = [jnp.stack(r, axis=0) for r in zip(*rows_p)]
    stacked_s = [jnp.stack(r, axis=0) for r in zip(*rows_s)]
    return (yp, ys, *stacked_p, *stacked_s)
```

```python
import functools
import math

import jax
import jax.numpy as jnp
from jax import lax
from jax.experimental import pallas as pl
from jax.experimental.pallas import tpu as pltpu

D_MODEL = 1024
DEPTH = 1
CHUNK = 64
CHUNK_SHIFT = 6
SB_HEADS = 8
SB_HEAD_DIM = 64
DIFF_HEADS = 4
DIFF_HEAD_DIM = 64
SB_WIDTH = SB_HEADS * SB_HEAD_DIM
DIFF_WIDTH = DIFF_HEADS * 2 * DIFF_HEAD_DIM
MIX_WIDTH = SB_WIDTH + DIFF_WIDTH
IN_WIDTH = 3 * SB_WIDTH + 3 * DIFF_WIDTH
D_FF = 2816
RMS_EPS = 1e-6

LANES = 128
MXU_WIDTH = 256
VMEM_LIMIT_BYTES = 56 * 1024 * 1024
F32_EXP2_ZERO = 151.0
NEG_BIG = -1e30
LOG2E = math.log2(math.e)
DEPTH_SLOTS = 4
SB_PROMPT_TILE = 256
SB_PROMPT_GROUP = 4
DIFF_PROMPT_TILE = 512

_F32 = jnp.float32
_BF16 = jnp.bfloat16


def _rms(x, g):
    return x * lax.rsqrt(jnp.mean(x * x, axis=-1, keepdims=True) + RMS_EPS) * g


def _stick_logs(z):
    log_beta = jnp.minimum(z, 0.0) - jnp.log2(1.0 + jnp.exp2(-jnp.abs(z)))
    return log_beta, log_beta - z


def _split_hi_lo(x):
    hi = lax.bitcast_convert_type(lax.bitcast_convert_type(x, jnp.uint32) & jnp.uint32(0xFFFF0000), _F32)
    return hi.astype(_BF16), (x - hi).astype(_BF16)


def _dot(a, b):
    return jnp.dot(a, b, preferred_element_type=_F32)


def _dot_nt(a, b):
    return lax.dot_general(a, b, (((1,), (1,)), ((), ())), preferred_element_type=_F32)


def _ffn_kernel(*refs, mixer, final_norm):
    refs = list(refs)
    o_ref = refs.pop()
    x = refs.pop(0)[...]
    if mixer:
        sbo_ref, do_ref, wo_ref, mix_g_ref = refs[:4]
        refs = refs[4:]
        mix = _dot(sbo_ref[...], wo_ref[:SB_WIDTH, :]) + _dot(do_ref[...], wo_ref[SB_WIDTH:, :])
        x = x + _rms(mix, mix_g_ref[...])
    pre_ref, wg_ref, wu_ref, wd_ref, post_ref = refs[:5]
    xn = _rms(x, pre_ref[...]).astype(_BF16)
    g = _dot(xn, wg_ref[...])
    u = _dot(xn, wu_ref[...])
    a = (g * jax.nn.sigmoid(g) * u).astype(_BF16)
    h = x + 0.5 * _rms(_dot(a, wd_ref[...]), post_ref[...])
    if final_norm:
        h = _rms(h, refs[5][...])
    o_ref[...] = h


def _ffn(x, pre_g, wg, wu, wd, post_g, *, mixer=None, final_g=None):
    m = x.shape[0]
    tm = min(512, m)
    assert m % tm == 0

    def resident(shape):
        return pl.BlockSpec(shape, lambda i: (0, 0), pipeline_mode=pl.Buffered(1))

    row = pl.BlockSpec((tm, D_MODEL), lambda i: (i, 0))
    vec = resident((1, D_MODEL))
    in_specs, args = [row], [x]
    if mixer is not None:
        half = pl.BlockSpec((tm, SB_WIDTH), lambda i: (i, 0))
        in_specs += [half, half, resident((MIX_WIDTH, D_MODEL)), vec]
        args += list(mixer)
    in_specs += [vec, resident((D_MODEL, D_FF)), resident((D_MODEL, D_FF)), resident((D_FF, D_MODEL)), vec]
    args += [pre_g, wg, wu, wd, post_g]
    if final_g is not None:
        in_specs.append(vec)
        args.append(final_g)
    return pl.pallas_call(
        functools.partial(_ffn_kernel, mixer=mixer is not None, final_norm=final_g is not None),
        grid=(m // tm,),
        in_specs=in_specs,
        out_specs=row,
        out_shape=jax.ShapeDtypeStruct((m, D_MODEL), _F32),
        compiler_params=pltpu.CompilerParams(
            dimension_semantics=("parallel",), vmem_limit_bytes=VMEM_LIMIT_BYTES),
        name="ffn_mix_final" if mixer is not None else "ffn",
    )(*args)


def _qkv_kernel(h_ref, g_ref, w_ref, sbq_ref, sbk_ref, sbv_ref, dq_ref, dk_ref, dv_ref,
                sbk16_ref, sbv16_ref, dk16_ref, dv16_ref, *, transposed):
    hn = _rms(h_ref[...], g_ref[...]).astype(_BF16)

    def proj(idx):
        return _dot(hn, w_ref[:, idx * SB_WIDTH:(idx + 1) * SB_WIDTH])

    sbq_ref[...] = (proj(0) * (SB_HEAD_DIM ** -0.5 * LOG2E)).astype(_BF16)
    dq_ref[...] = (proj(3) * (DIFF_HEAD_DIM ** -0.5 * LOG2E)).astype(_BF16)

    def store_rows(ref, p):
        head_dim = ref.shape[1]
        heads = p.shape[1] // head_dim
        for head in range(heads):
            ref[pl.ds(head, p.shape[0], stride=heads), :] = p[:, head * head_dim:(head + 1) * head_dim]

    for idx, full_ref, half_ref in ((4, dk_ref, dk16_ref), (5, dv_ref, dv16_ref)):
        p = proj(idx)
        store_rows(full_ref, p)
        if transposed and half_ref is dv16_ref:
            half_ref[0] = p.T.astype(_BF16)
        else:
            half_ref[...] = p.astype(_BF16)
    for idx, full_ref, half_ref in ((1, sbk_ref, sbk16_ref), (2, sbv_ref, sbv16_ref)):
        p = proj(idx)
        if transposed:
            p_t = p.T
            full_ref[0] = p_t
            if half_ref is sbv16_ref:
                half_ref[0] = p_t.astype(_BF16)
            else:
                half_ref[...] = p.astype(_BF16)
        else:
            store_rows(full_ref, p)
            half_ref[...] = p.astype(_BF16)


def _qkv(h, g, w, *, seq_len=None):
    assert SB_WIDTH == DIFF_WIDTH
    assert math.log2(SB_HEAD_DIM) % 2 == 0 and math.log2(DIFF_HEAD_DIM) % 2 == 0
    m = h.shape[0]
    tm = min(512, m)
    assert m % tm == 0
    row = pl.BlockSpec((tm, D_MODEL), lambda i: (i, 0))
    out = pl.BlockSpec((tm, SB_WIDTH), lambda i: (i, 0))
    b16 = jax.ShapeDtypeStruct((m, SB_WIDTH), _BF16)

    def cache_rows(heads, head_dim):
        return (jax.ShapeDtypeStruct((m * heads, head_dim), _F32),
                pl.BlockSpec((tm * heads, head_dim), lambda i: (i, 0)))

    sb_rows, sb_out = cache_rows(SB_HEADS, SB_HEAD_DIM)
    d_rows, d_out = cache_rows(DIFF_HEADS, 2 * DIFF_HEAD_DIM)
    v_out, v16 = out, b16
    if seq_len is not None:
        assert seq_len % tm == 0
        tiles = seq_len // tm
        v_out = pl.BlockSpec((1, SB_WIDTH, tm), lambda i: (i // tiles, 0, i % tiles))
        v16 = jax.ShapeDtypeStruct((m // seq_len, SB_WIDTH, seq_len), _BF16)
        sb_rows, sb_out = jax.ShapeDtypeStruct(v16.shape, _F32), v_out
    return pl.pallas_call(
        functools.partial(_qkv_kernel, transposed=seq_len is not None),
        grid=(m // tm,),
        in_specs=[row, pl.BlockSpec((1, D_MODEL), lambda i: (0, 0)),
                  pl.BlockSpec((D_MODEL, IN_WIDTH), lambda i: (0, 0))],
        out_specs=[out, sb_out, sb_out, out, d_out, d_out, out, v_out, out, v_out],
        out_shape=[b16, sb_rows, sb_rows, b16, d_rows, d_rows, b16, v16, b16, v16],
        compiler_params=pltpu.CompilerParams(
            dimension_semantics=("parallel",), vmem_limit_bytes=VMEM_LIMIT_BYTES),
        name="qkv",
    )(h, g, w)


def _sb_step_kernel(q_ref, kn_ref, vn_ref, kc_ref, vc_ref, o_ref, acc_ref, c_ref, *, tk):
    n = q_ref.shape[1]
    past = kc_ref.shape[2]

    def later(t):
        return (lax.broadcasted_iota(jnp.int32, (t, t), 0)
                > lax.broadcasted_iota(jnp.int32, (t, t), 1)).astype(_BF16)

    later_new, later_tile = later(n), later(tk)
    newer = (lax.broadcasted_iota(jnp.int32, (n, n), 1) < lax.broadcasted_iota(jnp.int32, (n, n), 0))

    def walk(z, c, later_t, mask):
        log_beta, log_stay = _stick_logs(z)
        if mask is not None:
            log_stay = jnp.where(mask, log_stay, 0.0)
        sums = _dot(jnp.concatenate(_split_hi_lo(log_stay), axis=0), later_t)
        w = jnp.exp2(log_beta + (sums[:n] + sums[n:] + c))
        if mask is not None:
            w = jnp.where(mask, w, 0.0)
        return c + jnp.sum(log_stay, axis=-1, keepdims=True), w.astype(_BF16)

    def head_cols(head):
        return slice(head * SB_HEAD_DIM, (head + 1) * SB_HEAD_DIM)

    def walk_cache(start, c_of, first):
        for head in range(SB_HEADS):
            rows = head_cols(head)
            k_t = kc_ref[0, rows, pl.ds(start, tk)].astype(_BF16)
            v_t = vc_ref[0, rows, pl.ds(start, tk)].astype(_BF16)
            c, w = walk(_dot(q_ref[0, :, rows], k_t), c_of(head), later_tile, None)
            o = _dot_nt(w, v_t)
            c_ref[head] = c
            acc_ref[head] = o + first[head] if first else acc_ref[head] + o

    new = []
    for head in range(SB_HEADS):
        cols = head_cols(head)
        c, w = walk(_dot_nt(q_ref[0, :, cols], kn_ref[0, :, cols]), jnp.zeros((n, 1), _F32), later_new, newer)
        new.append((c, _dot(w, vn_ref[0, :, cols])))
    walk_cache(past - tk, lambda head: new[head][0], [o for _, o in new])

    def cond(state):
        j, c_max = state
        return (j >= 0) & (c_max > -F32_EXP2_ZERO)

    def body(state):
        j, _ = state
        walk_cache(pl.multiple_of(j * tk, tk), lambda head: c_ref[head], None)
        return j - 1, jnp.max(c_ref[...])

    lax.while_loop(cond, body, (past // tk - 2, jnp.max(c_ref[...])))
    o_ref[0] = jnp.concatenate([acc_ref[head] for head in range(SB_HEADS)], axis=-1).astype(o_ref.dtype)


def _sb_step_call(q, k_new, v_new, k_cache, v_cache):
    b, n, _ = q.shape
    past = k_cache.shape[2]
    tk = min(256, past)
    assert past % tk == 0
    new = pl.BlockSpec((1, n, SB_WIDTH), lambda i: (i, 0, 0))
    cache = pl.BlockSpec((1, SB_WIDTH, past), lambda i: (i, 0, 0))
    return pl.pallas_call(
        functools.partial(_sb_step_kernel, tk=tk),
        grid=(b,),
        in_specs=[new, new, new, cache, cache],
        out_specs=new,
        out_shape=jax.ShapeDtypeStruct((b, n, SB_WIDTH), _BF16),
        scratch_shapes=[pltpu.VMEM((SB_HEADS, n, SB_HEAD_DIM), _F32), pltpu.VMEM((SB_HEADS, n, 1), _F32)],
        compiler_params=pltpu.CompilerParams(
            dimension_semantics=("arbitrary",), vmem_limit_bytes=VMEM_LIMIT_BYTES),
        name="sb_step",
    )(q, k_new, v_new, k_cache, v_cache)


def _alibi_slope_log2(head):
    assert 8 % DIFF_HEADS == 0
    exponent = (8 // DIFF_HEADS) * (head + 1)
    slope = lax.bitcast_convert_type(jnp.full((1, 1), (127 - exponent) << 23, jnp.int32), _F32)
    return slope * LOG2E


def _diff_step_kernel(q_ref, kn_ref, vn_ref, kc_ref, vc_ref, lq1_ref, lk1_ref, lq2_ref, lk2_ref, g_ref,
                      o_ref, *, lam_init):
    n = q_ref.shape[1]
    past = kc_ref.shape[1] // DIFF_HEADS

    def cache_head(ref, head):
        return ref[0, pl.ds(head, past, stride=DIFF_HEADS), :].astype(_BF16)
    lane = lax.broadcasted_iota(jnp.int32, (1, LANES), 1)
    q_idx_c = lax.broadcasted_iota(jnp.int32, (2 * n, past), 0) & (n - 1)
    k_idx_c = lax.broadcasted_iota(jnp.int32, (2 * n, past), 1)
    q_idx_n = lax.broadcasted_iota(jnp.int32, (2 * n, n), 0) & (n - 1)
    k_idx_n = lax.broadcasted_iota(jnp.int32, (2 * n, n), 1)
    distance_c = (past + q_idx_c - k_idx_c).astype(_F32)
    distance_n = jnp.abs(q_idx_n - k_idx_n).astype(_F32)
    visible_n =(lax.shift_right_logical(past + k_idx_n, CHUNK_SHIFT)
                 <= lax.shift_right_logical(past + q_idx_n, CHUNK_SHIFT))
    lam = (jnp.exp(jnp.sum(lq1_ref[...] * lk1_ref[...], axis=-1, keepdims=True))
           - jnp.exp(jnp.sum(lq2_ref[...] * lk2_ref[...], axis=-1, keepdims=True)) + lam_init)
    for head in range(DIFF_HEADS):
        cols = slice(head * LANES, (head + 1) * LANES)
        slope = _alibi_slope_log2(head)
        q = q_ref[0, :, cols].astype(_F32)
        q_maps = jnp.concatenate([jnp.where(lane < DIFF_HEAD_DIM, q, 0.0),
                                  jnp.where(lane >= DIFF_HEAD_DIM, q, 0.0)], axis=0).astype(_BF16)
        s_c = _dot_nt(q_maps, cache_head(kc_ref, head)) - slope * distance_c
        s_n = jnp.where(visible_n, _dot_nt(q_maps, kn_ref[0, :, cols]) - slope * distance_n, -jnp.inf)
        m = jnp.maximum(jnp.max(s_c, axis=-1, keepdims=True), jnp.max(s_n, axis=-1, keepdims=True))
        p_c = jnp.exp2(s_c - m)
        p_n = jnp.exp2(s_n - m)
        l = jnp.sum(p_c, axis=-1, keepdims=True) + jnp.sum(p_n, axis=-1, keepdims=True)
        o = (_dot(p_c.astype(_BF16), cache_head(vc_ref, head))
             + _dot(p_n.astype(_BF16), vn_ref[0, :, cols])) / l
        o = o[:n] - lam * o[n:]
        o_ref[0, :, cols] = (_rms(o, g_ref[...]) * (1.0 - lam_init)).astype(o_ref.dtype)


def _diff_step_call(q, k_new, v_new, k_cache, v_cache, lq1, lk1, lq2, lk2, subln_g, *, lam_init):
    b, n, _ = q.shape
    rows = k_cache.shape[1]
    assert 2 * DIFF_HEAD_DIM == LANES and CHUNK == 1 << CHUNK_SHIFT and n & (n - 1) == 0
    new = pl.BlockSpec((1, n, DIFF_WIDTH), lambda i: (i, 0, 0))
    cache = pl.BlockSpec((1, rows, LANES), lambda i: (i, 0, 0))
    lam_spec = pl.BlockSpec((1, DIFF_HEAD_DIM), lambda i: (0, 0))
    return pl.pallas_call(
        functools.partial(_diff_step_kernel, lam_init=lam_init),
        grid=(b,),
        in_specs=[new, new, new, cache, cache, lam_spec, lam_spec, lam_spec, lam_spec,
                  pl.BlockSpec((1, LANES), lambda i: (0, 0))],
        out_specs=new,
        out_shape=jax.ShapeDtypeStruct((b, n, DIFF_WIDTH), _BF16),
        compiler_params=pltpu.CompilerParams(
            dimension_semantics=("arbitrary",), vmem_limit_bytes=VMEM_LIMIT_BYTES),
        name="diff_step",
    )(q, k_new, v_new, k_cache, v_cache, lq1, lk1, lq2, lk2, subln_g)


def _prompt_specs(nq, t):
    assert nq % t == 0 and t & (t - 1) == 0
    q_spec = pl.BlockSpec((1, t, LANES), lambda bi, g, qi: (bi, qi, g))
    k_spec = pl.BlockSpec((1, nq, LANES), lambda bi, g, qi: (bi, 0, g))
    vt_spec = pl.BlockSpec((1, LANES, nq), lambda bi, g, qi: (bi, g, 0))
    return q_spec, k_spec, vt_spec


def _sb_prompt_kernel(q_ref, k_ref, vt_ref, o_ref, acc_ref, c_ref, z_ref, lb_ref, hl_ref, w_ref, *,
                      t, group):
    assert LANES == 2 * SB_HEAD_DIM
    strips = 2 * t // LANES
    qi = pl.program_id(2)
    lane = lax.broadcasted_iota(jnp.int32, (1, LANES), 1)
    key = lax.broadcasted_iota(jnp.int32, (t, LANES), 0)
    qry = lax.broadcasted_iota(jnp.int32, (t, LANES), 1)
    later = (lax.broadcasted_iota(jnp.int32, (t, t), 1)
             > lax.broadcasted_iota(jnp.int32, (t, t), 0)).astype(_BF16)
    def tile_queries(g):
        q = q_ref[0, g * t:(g + 1) * t, :].astype(_F32)
        return jnp.concatenate([jnp.where(lane < SB_HEAD_DIM, q, 0.0),
                                jnp.where(lane >= SB_HEAD_DIM, q, 0.0)], axis=0).astype(_BF16)

    q_heads = [tile_queries(g) for g in range(group)]

    def stage_scores(g, j, buf):
        z = _dot_nt(k_ref[0, pl.ds(pl.multiple_of(j * t, t), t), :], q_heads[g])
        for s in range(strips):
            z_ref[buf, s] = z[:, s * LANES:(s + 1) * LANES]

    def own_mask(s):
        return key < ((s * LANES + qry) & (t - 1))

    def stage_logs(c, own, buf):
        c_new = []
        for s in range(strips):
            lb_ref[buf, s], log_stay = _stick_logs(z_ref[buf, s])
            if own:
                log_stay = jnp.where(own_mask(s), log_stay, 0.0)
            hl_ref[buf, s] = jnp.concatenate(_split_hi_lo(log_stay), axis=1)
            c_new.append(c[:, s * LANES:(s + 1) * LANES] + jnp.sum(log_stay, axis=0, keepdims=True))
        return jnp.concatenate(c_new, axis=1)

    def stage_weights(c, own, buf):
        for s in range(strips):
            sums = _dot(later, hl_ref[buf, s])
            w = jnp.exp2(lb_ref[buf, s] + (sums[:, :LANES] + sums[:, LANES:] + c[:, s * LANES:(s + 1) * LANES]))
            if own:
                w = jnp.where(own_mask(s), w, 0.0)
            head, part = divmod(s, t // LANES)
            w_ref[buf, head, :, part * LANES:(part + 1) * LANES] = w.astype(_BF16)

    def stage_values(j, buf):
        vt = vt_ref[0, :, pl.ds(pl.multiple_of(j * t, t), t)]
        return jnp.concatenate([_dot(vt[:SB_HEAD_DIM], w_ref[buf, 0]), _dot(vt[SB_HEAD_DIM:], w_ref[buf, 1])],
                               axis=0)

    def walk_tile(g, j, c, own, buf):
        stage_scores(g, j, buf)
        c_new = stage_logs(c, own, buf)
        stage_weights(c, own, buf)
        return c_new, stage_values(j, buf)

    def first_tiles(has_prev):
        walks = [(g, qi * group + g - d, d == 0, 2 * g + d)
                 for g in range(group) for d in range(2) if d == 0 or has_prev(g)]
        for g, j, own, buf in walks:
            stage_scores(g, j, buf)
        c_in = {}
        for g, j, own, buf in walks:
            c_in[buf] = jnp.zeros((1, 2 * t), _F32) if own else c_ref[g]
            c_ref[g] = stage_logs(c_in[buf], own, buf)
        for g, j, own, buf in walks:
            stage_weights(c_in[buf], own, buf)
        for g, j, own, buf in walks:
            o = stage_values(j, buf)
            acc_ref[g] = o if own else acc_ref[g] + o

    @pl.when(qi == 0)
    def _():
        first_tiles(lambda g: g > 0)

    @pl.when(qi > 0)
    def _():
        first_tiles(lambda g: True)

    def older_tile(g, step):
        return qi * group + g - 2 - step

    def any_active(step):
        flags = [(older_tile(g, step) >= 0) & (jnp.max(c_ref[g]) > -F32_EXP2_ZERO) for g in range(group)]
        return functools.reduce(jnp.logical_or, flags)

    def walk_older(state):
        step, _ = state
        for g in range(group):
            @pl.when((older_tile(g, step) >= 0) & (jnp.max(c_ref[g]) > -F32_EXP2_ZERO))
            def _(g=g):
                c, o = walk_tile(g, older_tile(g, step), c_ref[g], False, 2 * g)
                c_ref[g] = c
                acc_ref[g] += o
        return step + 1, any_active(step + 1)

    lax.while_loop(lambda state: state[1], walk_older, (0, any_active(0)))
    for g in range(group):
        o_ref[0, g * t:(g + 1) * t, :] = acc_ref[g].T.astype(o_ref.dtype)


def _sb_prompt_call(q, k, vt, *, t, group):
    b, nq, _ = q.shape
    q_spec, k_spec, vt_spec = _prompt_specs(nq, group * t)
    sets = 2 * group
    strips = 2 * t // LANES
    return pl.pallas_call(
        functools.partial(_sb_prompt_kernel, t=t, group=group),
        grid=(b, SB_WIDTH // LANES, nq // (group * t)),
        in_specs=[q_spec, k_spec, vt_spec],
        out_specs=q_spec,
        out_shape=jax.ShapeDtypeStruct((b, nq, SB_WIDTH), _BF16),
        scratch_shapes=[pltpu.VMEM((group, LANES, t), _F32), pltpu.VMEM((group, 1, 2 * t), _F32),
                        pltpu.VMEM((sets, strips, t, LANES), _F32),
                        pltpu.VMEM((sets, strips, t, LANES), _F32),
                        pltpu.VMEM((sets, strips, t, 2 * LANES), _BF16),
                        pltpu.VMEM((sets, 2, t, t), _BF16)],
        compiler_params=pltpu.CompilerParams(
            dimension_semantics=("parallel", "parallel", "arbitrary"),
            vmem_limit_bytes=VMEM_LIMIT_BYTES),
        name="sb_prompt",
    )(q, k, vt)


def _diff_prompt_kernel(q_ref, k_ref, vt_ref, lq1_ref, lk1_ref, lq2_ref, lk2_ref, g_ref, o_ref,
                        m_ref, l_ref, acc_ref, pen_ref, kpos_ref, knorm_ref, s_ref, p_ref, a_ref,
                        *, tq, tk, lam_init):
    own = 2
    assert tq == own * tk
    head = pl.program_id(1)
    qi = pl.program_id(2)
    n_past = qi * own
    lane = lax.broadcasted_iota(jnp.int32, (1, LANES), 1)
    slope = _alibi_slope_log2(head)

    @pl.when(qi == 0)
    def _():
        key = lax.broadcasted_iota(jnp.int32, (tk, 2 * tq), 0)
        qry = lax.broadcasted_iota(jnp.int32, (tk, 2 * tq), 1) & (tq - 1)
        for d in range(own):
            k_pos = d * tk + key
            visible = lax.shift_right_logical(k_pos, CHUNK_SHIFT) <= lax.shift_right_logical(qry, CHUNK_SHIFT)
            bias = slope * (qry - key - jnp.abs(qry - k_pos)).astype(_F32)
            pen_ref[d] = jnp.where(visible, bias, -jnp.inf)
        row = lax.broadcasted_iota(jnp.int32, (tk, LANES), 0)
        col = lax.broadcasted_iota(jnp.int32, (tk, LANES), 1)
        kpos_ref[...] = jnp.where(col < 3, row, 0).astype(_F32).astype(_BF16)
        k_sq = jnp.square(k_ref[0].astype(_F32))
        for c in range(2):
            in_map = (lane >= c * DIFF_HEAD_DIM) & (lane < (c + 1) * DIFF_HEAD_DIM)
            norm_sq = jnp.sum(jnp.where(in_map, k_sq, 0.0), axis=1, keepdims=True)
            knorm_ref[c] = jnp.broadcast_to(jnp.sqrt(jnp.max(norm_sq, axis=0, keepdims=True)), (1, LANES))

    slope_hi = slope.astype(_BF16).astype(_F32)
    slope_mid = (slope - slope_hi).astype(_BF16).astype(_F32)
    slope_lo = slope - slope_hi - slope_mid
    slope_cols = jnp.where(lane == 0, slope_hi, jnp.where(lane == 1, slope_mid,
                                                          jnp.where(lane == 2, slope_lo, 0.0)))
    slope_cols = jnp.broadcast_to(slope_cols, (tq, LANES))
    q = q_ref[0].astype(_F32)
    q_maps = jnp.concatenate(
        [jnp.concatenate([jnp.where(lane < DIFF_HEAD_DIM, q, 0.0), slope_cols], axis=1),
         jnp.concatenate([jnp.where(lane >= DIFF_HEAD_DIM, q, 0.0), slope_cols], axis=1)],
        axis=0).astype(_BF16)
    m_ref[...] = jnp.full_like(m_ref, NEG_BIG)
    l_ref[...] = jnp.zeros_like(l_ref)
    acc_ref[...] = jnp.zeros_like(acc_ref)

    def slot_tile(n):
        return jnp.where(n < own, n_past + n, n_past - 1 - (n - own))

    def slot_shift(n):
        j = slot_tile(n)
        distance = (qi * tq - j * tk).astype(_F32)
        return jnp.where(n < own, 0.0, jnp.where(j >= 0, -slope * distance, NEG_BIG))

    def tile_start(n):
        return pl.multiple_of(jnp.clip(slot_tile(n), 0, n_past + own - 1) * tk, tk)

    def scores(n):
        keys = jnp.concatenate([k_ref[0, pl.ds(tile_start(n), tk), :], kpos_ref[...]], axis=1)
        return _dot_nt(keys, q_maps)

    def put_scores(buf, s):
        for c in range(2 * tq // LANES):
            s_ref[buf, c] = s[:, c * LANES:(c + 1) * LANES]

    def softmax(buf, shift):
        for c in range(2 * tq // LANES):
            strip = slice(c * LANES, (c + 1) * LANES)
            s = s_ref[buf, c]
            m_old = m_ref[:, strip]
            m_new = jnp.maximum(m_old, jnp.max(s, axis=0, keepdims=True) + shift)
            alpha = jnp.exp2(m_old - m_new)
            p = jnp.exp2(s - (m_new - shift))
            l_ref[:, strip] = alpha * l_ref[:, strip] + jnp.sum(p, axis=0, keepdims=True)
            m_ref[:, strip] = m_new
            a_ref[buf, :, strip] = alpha
            half = c % (MXU_WIDTH // LANES)
            p_ref[buf, c // (MXU_WIDTH // LANES), :, half * LANES:(half + 1) * LANES] = p.astype(_BF16)

    def weigh(n, buf):
        values = vt_ref[0, :, pl.ds(tile_start(n), tk)]
        for c in range(2 * tq // MXU_WIDTH):
            strip = slice(c * MXU_WIDTH, (c + 1) * MXU_WIDTH)
            acc_ref[c] = a_ref[buf, :, strip] * acc_ref[c] + _dot(values, p_ref[buf, c])

    for n in range(own):
        put_scores(n, scores(n) + pen_ref[n])
    for n in range(own, DEPTH_SLOTS):
        p_ref[n] = jnp.zeros(p_ref.shape[1:], _BF16)
        a_ref[n] = jnp.ones((1, 2 * tq), _F32)

    def stage_pair(n, buf):
        for d in range(2):
            put_scores((buf + 2 + d) % DEPTH_SLOTS, scores(n + 2 + d))
        for d in range(2):
            softmax(buf + d, slot_shift(n + d))
        for d in range(2):
            weigh(n - 2 + d, (buf + 2 + d) % DEPTH_SLOTS)

    def four_slots(i, carry):
        stage_pair(4 * i, 0)
        stage_pair(4 * i + 2, 2)
        return carry

    q_sq = jnp.square(q)
    reach = jnp.zeros((1, 1), _F32)
    for c in range(2):
        in_map = (lane >= c * DIFF_HEAD_DIM) & (lane < (c + 1) * DIFF_HEAD_DIM)
        q_norm = jnp.sqrt(jnp.max(jnp.sum(jnp.where(in_map, q_sq, 0.0), axis=1, keepdims=True),
                                  axis=0, keepdims=True))
        reach = jnp.maximum(reach, (2.0 * q_norm * knorm_ref[c][:, :1] + F32_EXP2_ZERO) / slope)
    tiles_in_reach = jnp.minimum(jnp.floor((reach - 1.0) / tk) + 1.0, float(1 << 20))
    n_walk = jnp.minimum(jnp.max(tiles_in_reach).astype(jnp.int32), n_past)
    pairs = (own + n_walk + 1) // 2
    trips = pairs // 2
    lax.fori_loop(0, trips, four_slots, 0)

    @pl.when(pairs % 2 == 1)
    def _():
        stage_pair(4 * trips, 0)
        for d in range(2):
            weigh(4 * trips + d, d)

    @pl.when(pairs % 2 == 0)
    def _():
        for d in range(2):
            weigh(4 * trips - 2 + d, 2 + d)

    lam = (jnp.exp(jnp.sum(lq1_ref[...] * lk1_ref[...], axis=-1, keepdims=True))
           - jnp.exp(jnp.sum(lq2_ref[...] * lk2_ref[...], axis=-1, keepdims=True)) + lam_init)
    o = jnp.concatenate([acc_ref[c] for c in range(2 * tq // MXU_WIDTH)], axis=1) / l_ref[...]
    o = o[:, :tq] - lam * o[:, tq:]
    o = o * lax.rsqrt(jnp.mean(o * o, axis=0, keepdims=True) + RMS_EPS)
    o_ref[0] = (o.T * g_ref[...] * (1.0 - lam_init)).astype(o_ref.dtype)


def _diff_prompt_call(q, k, vt, lq1, lk1, lq2, lk2, subln_g, *, tq, lam_init):
    b, nq, _ = q.shape
    tk = tq // 2
    assert 2 * DIFF_HEAD_DIM == LANES
    assert tk % CHUNK == 0 and CHUNK == 1 << CHUNK_SHIFT and tk <= 256
    q_spec, k_spec, vt_spec = _prompt_specs(nq, tq)
    lam_spec = pl.BlockSpec((1, DIFF_HEAD_DIM), lambda bi, g, qi: (0, 0))
    stat = pltpu.VMEM((1, 2 * tq), _F32)
    return pl.pallas_call(
        functools.partial(_diff_prompt_kernel, tq=tq, tk=tk, lam_init=lam_init),
        grid=(b, DIFF_HEADS, nq // tq),
        in_specs=[q_spec, k_spec, vt_spec, lam_spec, lam_spec, lam_spec, lam_spec,
                  pl.BlockSpec((1, LANES), lambda bi, g, qi: (0, 0))],
        out_specs=q_spec,
        out_shape=jax.ShapeDtypeStruct((b, nq, DIFF_WIDTH), _BF16),
        scratch_shapes=[stat, stat, pltpu.VMEM((2 * tq // MXU_WIDTH, LANES, MXU_WIDTH), _F32),
                        pltpu.VMEM((2, tk, 2 * tq), _F32), pltpu.VMEM((tk, LANES), _BF16),
                        pltpu.VMEM((2, 1, LANES), _F32),
                        pltpu.VMEM((DEPTH_SLOTS, 2 * tq // LANES, tk, LANES), _F32),
                        pltpu.VMEM((DEPTH_SLOTS, 2 * tq // MXU_WIDTH, tk, MXU_WIDTH), _BF16),
                        pltpu.VMEM((DEPTH_SLOTS, 1, 2 * tq), _F32)],
        compiler_params=pltpu.CompilerParams(
            dimension_semantics=("arbitrary", "arbitrary", "arbitrary"),
            vmem_limit_bytes=VMEM_LIMIT_BYTES),
        name="diff_prompt",
    )(q, k, vt, lq1, lk1, lq2, lk2, subln_g)


def _encoder_layer(x, past, w, lam_init):
    b, n, _ = x.shape
    m = b * n
    h = _ffn(x.reshape(m, D_MODEL), w["ff1_pre_g"], w["ff1_w_gate"], w["ff1_w_up"], w["ff1_w_down"],
             w["ff1_post_g"])
    lam_args = (w["lam_q1"], w["lam_k1"], w["lam_q2"], w["lam_k2"], w["subln_g"])

    def seq(a):
        return a.reshape(b, n, a.shape[-1])

    if past is None:
        sbq, sbk, sbv, dq, dk, dv, sbk16, sbvt16, dk16, dvt16 = _qkv(h, w["mix_pre_g"], w["w_in"], seq_len=n)
        sb_o = _sb_prompt_call(seq(sbq), seq(sbk16), sbvt16, t=SB_PROMPT_TILE, group=SB_PROMPT_GROUP)
        d_o = _diff_prompt_call(seq(dq), seq(dk16), dvt16, *lam_args, tq=DIFF_PROMPT_TILE, lam_init=lam_init)
        sbk, sbv = (a.reshape(b, SB_HEADS, SB_HEAD_DIM, n).transpose(0, 3, 1, 2) for a in (sbk, sbv))
    else:
        sbq, sbk, sbv, dq, dk, dv, sbk16, sbv16, dk16, dv16 = _qkv(h, w["mix_pre_g"], w["w_in"])
        sb_o = _sb_step_call(seq(sbq), seq(sbk16), seq(sbv16), past[0], past[1])
        d_o = _diff_step_call(seq(dq), seq(dk16), seq(dv16), past[2], past[3], *lam_args, lam_init=lam_init)
    y = _ffn(h, w["ff2_pre_g"], w["ff2_w_gate"], w["ff2_w_up"], w["ff2_w_down"], w["ff2_post_g"],
             mixer=(sb_o.reshape(m, SB_WIDTH), d_o.reshape(m, DIFF_WIDTH), w["w_out"], w["mix_post_g"]),
             final_g=w["final_g"])
    rows = (sbk.reshape(b, n, SB_HEADS, SB_HEAD_DIM), sbv.reshape(b, n, SB_HEADS, SB_HEAD_DIM),
            dk.reshape(b, n, DIFF_HEADS, 2 * DIFF_HEAD_DIM), dv.reshape(b, n, DIFF_HEADS, 2 * DIFF_HEAD_DIM))
    return y.reshape(b, n, D_MODEL), rows


_MATRICES = ("ff1_w_gate", "ff1_w_up", "ff1_w_down", "w_in", "w_out", "ff2_w_gate", "ff2_w_up", "ff2_w_down")


def kernel(x_prompt, x_sample, cache_sb_k, cache_sb_v, cache_diff_k, cache_diff_v, ff1_pre_g, ff1_w_gate, ff1_w_up, ff1_w_down, ff1_post_g, mix_pre_g, w_in, lam_q1, lam_k1, lam_q2, lam_k2, subln_g, w_out, mix_post_g, ff2_pre_g, ff2_w_gate, ff2_w_up, ff2_w_down, ff2_post_g, final_g):
    params = dict(ff1_pre_g=ff1_pre_g, ff1_w_gate=ff1_w_gate, ff1_w_up=ff1_w_up, ff1_w_down=ff1_w_down,
                  ff1_post_g=ff1_post_g, mix_pre_g=mix_pre_g, w_in=w_in, lam_q1=lam_q1, lam_k1=lam_k1,
                  lam_q2=lam_q2, lam_k2=lam_k2, subln_g=subln_g, w_out=w_out, mix_post_g=mix_post_g,
                  ff2_pre_g=ff2_pre_g, ff2_w_gate=ff2_w_gate, ff2_w_up=ff2_w_up, ff2_w_down=ff2_w_down,
                  ff2_post_g=ff2_post_g, final_g=final_g)
    yp, ys = x_prompt, x_sample
    rows_p, rows_s = [], []
    for l in range(DEPTH):
        lam_init = 0.8 - 0.6 * math.exp(-0.3 * l)
        w = {name: (p[l].astype(_BF16) if name in _MATRICES else p[l][None, :].astype(_F32))
             for name, p in params.items()}
        yp, rp = _encoder_layer(yp, None, w, lam_init)
        past = tuple([c[l].transpose(0, 2, 3, 1).reshape(c.shape[1], SB_WIDTH, c.shape[2])
                      for c in (cache_sb_k, cache_sb_v)]
                     + [c[l].reshape(c.shape[1], -1, c.shape[-1]) for c in (cache_diff_k, cache_diff_v)])
        ys, rs = _encoder_layer(ys, past, w, lam_init)
        rows_p.append(rp)
        rows_s.append(rs)
    stacked_p = [jnp.stack(r, axis=0) for r in zip(*rows_p)]
    stacked_s = [jnp.stack(r, axis=0) for r in zip(*rows_s)]
    return (yp, ys, *stacked_p, *stacked_s)
```

```python
import functools
import math

import jax
import jax.numpy as jnp
from jax import lax
from jax.experimental import pallas as pl
from jax.experimental.pallas import tpu as pltpu

D_MODEL = 1024
DEPTH = 1
CHUNK = 64
CHUNK_SHIFT = 6
SB_HEADS = 8
SB_HEAD_DIM = 64
DIFF_HEADS = 4
DIFF_HEAD_DIM = 64
SB_WIDTH = SB_HEADS * SB_HEAD_DIM
DIFF_WIDTH = DIFF_HEADS * 2 * DIFF_HEAD_DIM
MIX_WIDTH = SB_WIDTH + DIFF_WIDTH
IN_WIDTH = 3 * SB_WIDTH + 3 * DIFF_WIDTH
D_FF = 2816
RMS_EPS = 1e-6

LANES = 128
MXU_WIDTH = 256
VMEM_LIMIT_BYTES = 56 * 1024 * 1024
F32_EXP2_ZERO = 151.0
NEG_BIG = -1e30
LOG2E = math.log2(math.e)
DEPTH_SLOTS = 4
PAIRS_PER_TRIP = 2
SAFE_SPREAD = 90.0
SB_PROMPT_TILE = 256
SB_PROMPT_GROUP = 4
DIFF_PROMPT_TILE = 512

_F32 = jnp.float32
_BF16 = jnp.bfloat16


def _rms(x, g):
    return x * lax.rsqrt(jnp.mean(x * x, axis=-1, keepdims=True) + RMS_EPS) * g


def _stick_logs(z):
    log_beta = jnp.minimum(z, 0.0) - jnp.log2(1.0 + jnp.exp2(-jnp.abs(z)))
    return log_beta, log_beta - z


def _split_hi_lo(x):
    hi = lax.bitcast_convert_type(lax.bitcast_convert_type(x, jnp.uint32) & jnp.uint32(0xFFFF0000), _F32)
    return hi.astype(_BF16), (x - hi).astype(_BF16)


def _dot(a, b):
    return jnp.dot(a, b, preferred_element_type=_F32)


def _dot_nt(a, b):
    return lax.dot_general(a, b, (((1,), (1,)), ((), ())), preferred_element_type=_F32)


def _ffn_kernel(*refs, mixer, final_norm):
    refs = list(refs)
    o_ref = refs.pop()
    x = refs.pop(0)[...]
    if mixer:
        sbo_ref, do_ref, wo_ref, mix_g_ref = refs[:4]
        refs = refs[4:]
        mix = _dot(sbo_ref[...], wo_ref[:SB_WIDTH, :]) + _dot(do_ref[...], wo_ref[SB_WIDTH:, :])
        x = x + _rms(mix, mix_g_ref[...])
    pre_ref, wg_ref, wu_ref, wd_ref, post_ref = refs[:5]
    xn = _rms(x, pre_ref[...]).astype(_BF16)
    g = _dot(xn, wg_ref[...])
    u = _dot(xn, wu_ref[...])
    a = (g * jax.nn.sigmoid(g) * u).astype(_BF16)
    h = x + 0.5 * _rms(_dot(a, wd_ref[...]), post_ref[...])
    if final_norm:
        h = _rms(h, refs[5][...])
    o_ref[...] = h


def _ffn(x, pre_g, wg, wu, wd, post_g, *, mixer=None, final_g=None):
    m = x.shape[0]
    tm = min(512, m)
    assert m % tm == 0

    def resident(shape):
        return pl.BlockSpec(shape, lambda i: (0, 0), pipeline_mode=pl.Buffered(1))

    row = pl.BlockSpec((tm, D_MODEL), lambda i: (i, 0))
    vec = resident((1, D_MODEL))
    in_specs, args = [row], [x]
    if mixer is not None:
        half = pl.BlockSpec((tm, SB_WIDTH), lambda i: (i, 0))
        in_specs += [half, half, resident((MIX_WIDTH, D_MODEL)), vec]
        args += list(mixer)
    in_specs += [vec, resident((D_MODEL, D_FF)), resident((D_MODEL, D_FF)), resident((D_FF, D_MODEL)), vec]
    args += [pre_g, wg, wu, wd, post_g]
    if final_g is not None:
        in_specs.append(vec)
        args.append(final_g)
    return pl.pallas_call(
        functools.partial(_ffn_kernel, mixer=mixer is not None, final_norm=final_g is not None),
        grid=(m // tm,),
        in_specs=in_specs,
        out_specs=row,
        out_shape=jax.ShapeDtypeStruct((m, D_MODEL), _F32),
        compiler_params=pltpu.CompilerParams(
            dimension_semantics=("parallel",), vmem_limit_bytes=VMEM_LIMIT_BYTES),
        name="ffn_mix_final" if mixer is not None else "ffn",
    )(*args)


def _qkv_kernel(h_ref, g_ref, w_ref, sbq_ref, sbk_ref, sbv_ref, dq_ref, dk_ref, dv_ref,
                sbk16_ref, sbv16_ref, dk16_ref, dv16_ref, *, transposed):
    hn = _rms(h_ref[...], g_ref[...]).astype(_BF16)

    def proj(idx):
        return _dot(hn, w_ref[:, idx * SB_WIDTH:(idx + 1) * SB_WIDTH])

    sbq_ref[...] = (proj(0) * (SB_HEAD_DIM ** -0.5 * LOG2E)).astype(_BF16)
    dq_ref[...] = (proj(3) * (DIFF_HEAD_DIM ** -0.5 * LOG2E)).astype(_BF16)

    def store_rows(ref, p):
        head_dim = ref.shape[1]
        heads = p.shape[1] // head_dim
        for head in range(heads):
            ref[pl.ds(head, p.shape[0], stride=heads), :] = p[:, head * head_dim:(head + 1) * head_dim]

    for idx, full_ref, half_ref in ((4, dk_ref, dk16_ref), (5, dv_ref, dv16_ref)):
        p = proj(idx)
        store_rows(full_ref, p)
        if transposed and half_ref is dv16_ref:
            half_ref[0] = p.T.astype(_BF16)
        else:
            half_ref[...] = p.astype(_BF16)
    for idx, full_ref, half_ref in ((1, sbk_ref, sbk16_ref), (2, sbv_ref, sbv16_ref)):
        p = proj(idx)
        if transposed:
            p_t = p.T
            full_ref[0] = p_t
            if half_ref is sbv16_ref:
                half_ref[0] = p_t.astype(_BF16)
            else:
                half_ref[...] = p.astype(_BF16)
        else:
            store_rows(full_ref, p)
            half_ref[...] = p.astype(_BF16)


def _qkv(h, g, w, *, seq_len=None):
    assert SB_WIDTH == DIFF_WIDTH
    assert math.log2(SB_HEAD_DIM) % 2 == 0 and math.log2(DIFF_HEAD_DIM) % 2 == 0
    m = h.shape[0]
    tm = min(512, m)
    assert m % tm == 0
    row = pl.BlockSpec((tm, D_MODEL), lambda i: (i, 0))
    out = pl.BlockSpec((tm, SB_WIDTH), lambda i: (i, 0))
    b16 = jax.ShapeDtypeStruct((m, SB_WIDTH), _BF16)

    def cache_rows(heads, head_dim):
        return (jax.ShapeDtypeStruct((m * heads, head_dim), _F32),
                pl.BlockSpec((tm * heads, head_dim), lambda i: (i, 0)))

    sb_rows, sb_out = cache_rows(SB_HEADS, SB_HEAD_DIM)
    d_rows, d_out = cache_rows(DIFF_HEADS, 2 * DIFF_HEAD_DIM)
    v_out, v16 = out, b16
    if seq_len is not None:
        assert seq_len % tm == 0
        tiles = seq_len // tm
        v_out = pl.BlockSpec((1, SB_WIDTH, tm), lambda i: (i // tiles, 0, i % tiles))
        v16 = jax.ShapeDtypeStruct((m // seq_len, SB_WIDTH, seq_len), _BF16)
        sb_rows, sb_out = jax.ShapeDtypeStruct(v16.shape, _F32), v_out
    return pl.pallas_call(
        functools.partial(_qkv_kernel, transposed=seq_len is not None),
        grid=(m // tm,),
        in_specs=[row, pl.BlockSpec((1, D_MODEL), lambda i: (0, 0)),
                  pl.BlockSpec((D_MODEL, IN_WIDTH), lambda i: (0, 0))],
        out_specs=[out, sb_out, sb_out, out, d_out, d_out, out, v_out, out, v_out],
        out_shape=[b16, sb_rows, sb_rows, b16, d_rows, d_rows, b16, v16, b16, v16],
        compiler_params=pltpu.CompilerParams(
            dimension_semantics=("parallel",), vmem_limit_bytes=VMEM_LIMIT_BYTES),
        name="qkv",
    )(h, g, w)


def _sb_step_kernel(q_ref, kn_ref, vn_ref, kc_ref, vc_ref, o_ref, acc_ref, c_ref, *, tk):
    n = q_ref.shape[1]
    past = kc_ref.shape[2]

    def later(t):
        return (lax.broadcasted_iota(jnp.int32, (t, t), 0)
                > lax.broadcasted_iota(jnp.int32, (t, t), 1)).astype(_BF16)

    later_new, later_tile = later(n), later(tk)
    newer = (lax.broadcasted_iota(jnp.int32, (n, n), 1) < lax.broadcasted_iota(jnp.int32, (n, n), 0))

    def walk(z, c, later_t, mask):
        log_beta, log_stay = _stick_logs(z)
        if mask is not None:
            log_stay = jnp.where(mask, log_stay, 0.0)
        sums = _dot(jnp.concatenate(_split_hi_lo(log_stay), axis=0), later_t)
        w = jnp.exp2(log_beta + (sums[:n] + sums[n:] + c))
        if mask is not None:
            w = jnp.where(mask, w, 0.0)
        return c + jnp.sum(log_stay, axis=-1, keepdims=True), w.astype(_BF16)

    def head_cols(head):
        return slice(head * SB_HEAD_DIM, (head + 1) * SB_HEAD_DIM)

    def walk_cache(start, c_of, first):
        for head in range(SB_HEADS):
            rows = head_cols(head)
            k_t = kc_ref[0, rows, pl.ds(start, tk)].astype(_BF16)
            v_t = vc_ref[0, rows, pl.ds(start, tk)].astype(_BF16)
            c, w = walk(_dot(q_ref[0, :, rows], k_t), c_of(head), later_tile, None)
            o = _dot_nt(w, v_t)
            c_ref[head] = c
            acc_ref[head] = o + first[head] if first else acc_ref[head] + o

    new = []
    for head in range(SB_HEADS):
        cols = head_cols(head)
        c, w = walk(_dot_nt(q_ref[0, :, cols], kn_ref[0, :, cols]), jnp.zeros((n, 1), _F32), later_new, newer)
        new.append((c, _dot(w, vn_ref[0, :, cols])))
    walk_cache(past - tk, lambda head: new[head][0], [o for _, o in new])

    def cond(state):
        j, c_max = state
        return (j >= 0) & (c_max > -F32_EXP2_ZERO)

    def body(state):
        j, _ = state
        walk_cache(pl.multiple_of(j * tk, tk), lambda head: c_ref[head], None)
        return j - 1, jnp.max(c_ref[...])

    lax.while_loop(cond, body, (past // tk - 2, jnp.max(c_ref[...])))
    o_ref[0] = jnp.concatenate([acc_ref[head] for head in range(SB_HEADS)], axis=-1).astype(o_ref.dtype)


def _sb_step_call(q, k_new, v_new, k_cache, v_cache):
    b, n, _ = q.shape
    past = k_cache.shape[2]
    tk = min(256, past)
    assert past % tk == 0
    new = pl.BlockSpec((1, n, SB_WIDTH), lambda i: (i, 0, 0))
    cache = pl.BlockSpec((1, SB_WIDTH, past), lambda i: (i, 0, 0))
    return pl.pallas_call(
        functools.partial(_sb_step_kernel, tk=tk),
        grid=(b,),
        in_specs=[new, new, new, cache, cache],
        out_specs=new,
        out_shape=jax.ShapeDtypeStruct((b, n, SB_WIDTH), _BF16),
        scratch_shapes=[pltpu.VMEM((SB_HEADS, n, SB_HEAD_DIM), _F32), pltpu.VMEM((SB_HEADS, n, 1), _F32)],
        compiler_params=pltpu.CompilerParams(
            dimension_semantics=("arbitrary",), vmem_limit_bytes=VMEM_LIMIT_BYTES),
        name="sb_step",
    )(q, k_new, v_new, k_cache, v_cache)


def _alibi_slope_log2(head):
    assert 8 % DIFF_HEADS == 0
    exponent = (8 // DIFF_HEADS) * (head + 1)
    slope = lax.bitcast_convert_type(jnp.full((1, 1), (127 - exponent) << 23, jnp.int32), _F32)
    return slope * LOG2E


def _diff_step_kernel(q_ref, kn_ref, vn_ref, kc_ref, vc_ref, lq1_ref, lk1_ref, lq2_ref, lk2_ref, g_ref,
                      o_ref, *, lam_init):
    n = q_ref.shape[1]
    past = kc_ref.shape[1] // DIFF_HEADS

    def cache_head(ref, head):
        return ref[0, pl.ds(head, past, stride=DIFF_HEADS), :].astype(_BF16)
    lane = lax.broadcasted_iota(jnp.int32, (1, LANES), 1)
    q_idx_c = lax.broadcasted_iota(jnp.int32, (2 * n, past), 0) & (n - 1)
    k_idx_c = lax.broadcasted_iota(jnp.int32, (2 * n, past), 1)
    q_idx_n = lax.broadcasted_iota(jnp.int32, (2 * n, n), 0) & (n - 1)
    k_idx_n = lax.broadcasted_iota(jnp.int32, (2 * n, n), 1)
    distance_c = (past + q_idx_c - k_idx_c).astype(_F32)
    distance_n = jnp.abs(q_idx_n - k_idx_n).astype(_F32)
    visible_n =(lax.shift_right_logical(past + k_idx_n, CHUNK_SHIFT)
                 <= lax.shift_right_logical(past + q_idx_n, CHUNK_SHIFT))
    lam = (jnp.exp(jnp.sum(lq1_ref[...] * lk1_ref[...], axis=-1, keepdims=True))
           - jnp.exp(jnp.sum(lq2_ref[...] * lk2_ref[...], axis=-1, keepdims=True)) + lam_init)
    for head in range(DIFF_HEADS):
        cols = slice(head * LANES, (head + 1) * LANES)
        slope = _alibi_slope_log2(head)
        q = q_ref[0, :, cols].astype(_F32)
        q_maps = jnp.concatenate([jnp.where(lane < DIFF_HEAD_DIM, q, 0.0),
                                  jnp.where(lane >= DIFF_HEAD_DIM, q, 0.0)], axis=0).astype(_BF16)
        s_c = _dot_nt(q_maps, cache_head(kc_ref, head)) - slope * distance_c
        s_n = jnp.where(visible_n, _dot_nt(q_maps, kn_ref[0, :, cols]) - slope * distance_n, -jnp.inf)
        m = jnp.maximum(jnp.max(s_c, axis=-1, keepdims=True), jnp.max(s_n, axis=-1, keepdims=True))
        p_c = jnp.exp2(s_c - m)
        p_n = jnp.exp2(s_n - m)
        l = jnp.sum(p_c, axis=-1, keepdims=True) + jnp.sum(p_n, axis=-1, keepdims=True)
        o = (_dot(p_c.astype(_BF16), cache_head(vc_ref, head))
             + _dot(p_n.astype(_BF16), vn_ref[0, :, cols])) / l
        o = o[:n] - lam * o[n:]
        o_ref[0, :, cols] = (_rms(o, g_ref[...]) * (1.0 - lam_init)).astype(o_ref.dtype)


def _diff_step_call(q, k_new, v_new, k_cache, v_cache, lq1, lk1, lq2, lk2, subln_g, *, lam_init):
    b, n, _ = q.shape
    rows = k_cache.shape[1]
    assert 2 * DIFF_HEAD_DIM == LANES and CHUNK == 1 << CHUNK_SHIFT and n & (n - 1) == 0
    new = pl.BlockSpec((1, n, DIFF_WIDTH), lambda i: (i, 0, 0))
    cache = pl.BlockSpec((1, rows, LANES), lambda i: (i, 0, 0))
    lam_spec = pl.BlockSpec((1, DIFF_HEAD_DIM), lambda i: (0, 0))
    return pl.pallas_call(
        functools.partial(_diff_step_kernel, lam_init=lam_init),
        grid=(b,),
        in_specs=[new, new, new, cache, cache, lam_spec, lam_spec, lam_spec, lam_spec,
                  pl.BlockSpec((1, LANES), lambda i: (0, 0))],
        out_specs=new,
        out_shape=jax.ShapeDtypeStruct((b, n, DIFF_WIDTH), _BF16),
        compiler_params=pltpu.CompilerParams(
            dimension_semantics=("arbitrary",), vmem_limit_bytes=VMEM_LIMIT_BYTES),
        name="diff_step",
    )(q, k_new, v_new, k_cache, v_cache, lq1, lk1, lq2, lk2, subln_g)


def _prompt_specs(nq, t):
    assert nq % t == 0 and t & (t - 1) == 0
    q_spec = pl.BlockSpec((1, t, LANES), lambda bi, g, qi: (bi, qi, g))
    k_spec = pl.BlockSpec((1, nq, LANES), lambda bi, g, qi: (bi, 0, g))
    vt_spec = pl.BlockSpec((1, LANES, nq), lambda bi, g, qi: (bi, g, 0))
    return q_spec, k_spec, vt_spec


def _sb_prompt_kernel(q_ref, k_ref, vt_ref, o_ref, acc_ref, c_ref, z_ref, lb_ref, hl_ref, w_ref, *,
                      t, group):
    assert LANES == 2 * SB_HEAD_DIM
    strips = 2 * t // LANES
    qi = pl.program_id(2)
    lane = lax.broadcasted_iota(jnp.int32, (1, LANES), 1)
    key = lax.broadcasted_iota(jnp.int32, (t, LANES), 0)
    qry = lax.broadcasted_iota(jnp.int32, (t, LANES), 1)
    later = (lax.broadcasted_iota(jnp.int32, (t, t), 1)
             > lax.broadcasted_iota(jnp.int32, (t, t), 0)).astype(_BF16)
    def tile_queries(g):
        q = q_ref[0, g * t:(g + 1) * t, :].astype(_F32)
        return jnp.concatenate([jnp.where(lane < SB_HEAD_DIM, q, 0.0),
                                jnp.where(lane >= SB_HEAD_DIM, q, 0.0)], axis=0).astype(_BF16)

    q_heads = [tile_queries(g) for g in range(group)]

    def stage_scores(g, j, buf):
        z = _dot_nt(k_ref[0, pl.ds(pl.multiple_of(j * t, t), t), :], q_heads[g])
        for s in range(strips):
            z_ref[buf, s] = z[:, s * LANES:(s + 1) * LANES]

    def own_mask(s):
        return key < ((s * LANES + qry) & (t - 1))

    def stage_logs(c, own, buf):
        c_new = []
        for s in range(strips):
            lb_ref[buf, s], log_stay = _stick_logs(z_ref[buf, s])
            if own:
                log_stay = jnp.where(own_mask(s), log_stay, 0.0)
            hl_ref[buf, s] = jnp.concatenate(_split_hi_lo(log_stay), axis=1)
            c_new.append(c[:, s * LANES:(s + 1) * LANES] + jnp.sum(log_stay, axis=0, keepdims=True))
        return jnp.concatenate(c_new, axis=1)

    def stage_weights(c, own, buf):
        for s in range(strips):
            sums = _dot(later, hl_ref[buf, s])
            w = jnp.exp2(lb_ref[buf, s] + (sums[:, :LANES] + sums[:, LANES:] + c[:, s * LANES:(s + 1) * LANES]))
            if own:
                w = jnp.where(own_mask(s), w, 0.0)
            head, part = divmod(s, t // LANES)
            w_ref[buf, head, :, part * LANES:(part + 1) * LANES] = w.astype(_BF16)

    def stage_values(j, buf):
        vt = vt_ref[0, :, pl.ds(pl.multiple_of(j * t, t), t)]
        return jnp.concatenate([_dot(vt[:SB_HEAD_DIM], w_ref[buf, 0]), _dot(vt[SB_HEAD_DIM:], w_ref[buf, 1])],
                               axis=0)

    def walk_tile(g, j, c, own, buf):
        stage_scores(g, j, buf)
        c_new = stage_logs(c, own, buf)
        stage_weights(c, own, buf)
        return c_new, stage_values(j, buf)

    def first_tiles(has_prev):
        walks = [(g, qi * group + g - d, d == 0, 2 * g + d)
                 for g in range(group) for d in range(2) if d == 0 or has_prev(g)]
        for g, j, own, buf in walks:
            stage_scores(g, j, buf)
        c_in = {}
        for g, j, own, buf in walks:
            c_in[buf] = jnp.zeros((1, 2 * t), _F32) if own else c_ref[g]
            c_ref[g] = stage_logs(c_in[buf], own, buf)
        for g, j, own, buf in walks:
            stage_weights(c_in[buf], own, buf)
        for g, j, own, buf in walks:
            o = stage_values(j, buf)
            acc_ref[g] = o if own else acc_ref[g] + o

    @pl.when(qi == 0)
    def _():
        first_tiles(lambda g: g > 0)

    @pl.when(qi > 0)
    def _():
        first_tiles(lambda g: True)

    def older_tile(g, step):
        return qi * group + g - 2 - step

    def any_active(step):
        flags = [(older_tile(g, step) >= 0) & (jnp.max(c_ref[g]) > -F32_EXP2_ZERO) for g in range(group)]
        return functools.reduce(jnp.logical_or, flags)

    def walk_older(state):
        step, _ = state
        for g in range(group):
            @pl.when((older_tile(g, step) >= 0) & (jnp.max(c_ref[g]) > -F32_EXP2_ZERO))
            def _(g=g):
                c, o = walk_tile(g, older_tile(g, step), c_ref[g], False, 2 * g)
                c_ref[g] = c
                acc_ref[g] += o
        return step + 1, any_active(step + 1)

    lax.while_loop(lambda state: state[1], walk_older, (0, any_active(0)))
    for g in range(group):
        o_ref[0, g * t:(g + 1) * t, :] = acc_ref[g].T.astype(o_ref.dtype)


def _sb_prompt_call(q, k, vt, *, t, group):
    b, nq, _ = q.shape
    q_spec, k_spec, vt_spec = _prompt_specs(nq, group * t)
    sets = 2 * group
    strips = 2 * t // LANES
    return pl.pallas_call(
        functools.partial(_sb_prompt_kernel, t=t, group=group),
        grid=(b, SB_WIDTH // LANES, nq // (group * t)),
        in_specs=[q_spec, k_spec, vt_spec],
        out_specs=q_spec,
        out_shape=jax.ShapeDtypeStruct((b, nq, SB_WIDTH), _BF16),
        scratch_shapes=[pltpu.VMEM((group, LANES, t), _F32), pltpu.VMEM((group, 1, 2 * t), _F32),
                        pltpu.VMEM((sets, strips, t, LANES), _F32),
                        pltpu.VMEM((sets, strips, t, LANES), _F32),
                        pltpu.VMEM((sets, strips, t, 2 * LANES), _BF16),
                        pltpu.VMEM((sets, 2, t, t), _BF16)],
        compiler_params=pltpu.CompilerParams(
            dimension_semantics=("parallel", "parallel", "arbitrary"),
            vmem_limit_bytes=VMEM_LIMIT_BYTES),
        name="sb_prompt",
    )(q, k, vt)


def _diff_prompt_kernel(q_ref, k_ref, vt_ref, lq1_ref, lk1_ref, lq2_ref, lk2_ref, g_ref, o_ref,
                        m_ref, l_ref, acc_ref, pen_ref, pen2_ref, kpos_ref, kaug_ref, knorm_ref, *buffers,
                        tq, tk, lam_init):
    own = 2
    assert tq == own * tk
    head = pl.program_id(1)
    qi = pl.program_id(2)
    n_past = qi * own
    lane = lax.broadcasted_iota(jnp.int32, (1, LANES), 1)
    slope = _alibi_slope_log2(head)

    @pl.when(qi == 0)
    def _():
        key = lax.broadcasted_iota(jnp.int32, (tk, 2 * tq), 0)
        qry = lax.broadcasted_iota(jnp.int32, (tk, 2 * tq), 1) & (tq - 1)
        for d in range(own):
            k_pos = d * tk + key
            visible = lax.shift_right_logical(k_pos, CHUNK_SHIFT) <= lax.shift_right_logical(qry, CHUNK_SHIFT)
            bias = slope * (qry - key - jnp.abs(qry - k_pos)).astype(_F32)
            pen_ref[d] = jnp.where(visible, bias, -jnp.inf)
            ahead = -2.0 * slope * jnp.maximum(k_pos - qry, 0).astype(_F32)
            pen2_ref[d] = jnp.where(visible, ahead, -jnp.inf)
        row = lax.broadcasted_iota(jnp.int32, (tk, LANES), 0)
        col = lax.broadcasted_iota(jnp.int32, (tk, LANES), 1)
        kpos_ref[...] = jnp.where(col < 3, row, 0).astype(_F32).astype(_BF16)
        kaug_ref[...] = jnp.where(col < 3, row, jnp.where((col >= 6) & (col < 9), 1, 0)).astype(_F32)
        k_sq = jnp.square(k_ref[0].astype(_F32))
        for c in range(2):
            in_map = (lane >= c * DIFF_HEAD_DIM) & (lane < (c + 1) * DIFF_HEAD_DIM)
            norm_sq = jnp.sum(jnp.where(in_map, k_sq, 0.0), axis=1, keepdims=True)
            knorm_ref[c] = jnp.broadcast_to(jnp.sqrt(jnp.max(norm_sq, axis=0, keepdims=True)), (1, LANES))

    slope_hi = slope.astype(_BF16).astype(_F32)
    slope_mid = (slope - slope_hi).astype(_BF16).astype(_F32)
    slope_lo = slope - slope_hi - slope_mid
    slope_cols = jnp.where(lane == 0, slope_hi, jnp.where(lane == 1, slope_mid,
                                                          jnp.where(lane == 2, slope_lo, 0.0)))
    slope_cols = jnp.broadcast_to(slope_cols, (tq, LANES))
    q = q_ref[0].astype(_F32)
    q_maps = jnp.concatenate(
        [jnp.concatenate([jnp.where(lane < DIFF_HEAD_DIM, q, 0.0), slope_cols], axis=1),
         jnp.concatenate([jnp.where(lane >= DIFF_HEAD_DIM, q, 0.0), slope_cols], axis=1)],
        axis=0).astype(_BF16)
    m_ref[...] = jnp.full_like(m_ref, NEG_BIG)
    l_ref[...] = jnp.zeros_like(l_ref)
    acc_ref[...] = jnp.zeros_like(acc_ref)

    def slot_tile(n):
        return jnp.where(n < own, n_past + n, n_past - 1 - (n - own))

    def slot_shift(n):
        j = slot_tile(n)
        distance = (qi * tq - j * tk).astype(_F32)
        return jnp.where(n < own, 0.0, jnp.where(j >= 0, -slope * distance, NEG_BIG))

    def tile_start(n):
        return pl.multiple_of(jnp.clip(slot_tile(n), 0, n_past + own - 1) * tk, tk)

    def scores(n):
        keys = jnp.concatenate([k_ref[0, pl.ds(tile_start(n), tk), :], kpos_ref[...]], axis=1)
        return _dot_nt(keys, q_maps)

    s_refs, p_refs, a_refs = (buffers[i * DEPTH_SLOTS:(i + 1) * DEPTH_SLOTS] for i in range(3))

    def put_scores(buf, s):
        for c in range(2 * tq // LANES):
            s_refs[buf][c] = s[:, c * LANES:(c + 1) * LANES]

    def softmax(buf, shift):
        for c in range(2 * tq // LANES):
            strip = slice(c * LANES, (c + 1) * LANES)
            s = s_refs[buf][c]
            m_old = m_ref[:, strip]
            m_new = jnp.maximum(m_old, jnp.max(s, axis=0, keepdims=True) + shift)
            alpha = jnp.exp2(m_old - m_new)
            p = jnp.exp2(s - (m_new - shift))
            l_ref[:, strip] = alpha * l_ref[:, strip] + jnp.sum(p, axis=0, keepdims=True)
            m_ref[:, strip] = m_new
            a_refs[buf][:, strip] = alpha
            half = c % (MXU_WIDTH // LANES)
            p_refs[buf][c // (MXU_WIDTH // LANES), :, half * LANES:(half + 1) * LANES] = p.astype(_BF16)

    def weigh(n, buf):
        values = vt_ref[0, :, pl.ds(tile_start(n), tk)]
        for c in range(2 * tq // MXU_WIDTH):
            strip = slice(c * MXU_WIDTH, (c + 1) * MXU_WIDTH)
            acc_ref[c] = a_refs[buf][:, strip] * acc_ref[c] + _dot(values, p_refs[buf][c])

    def stage_pair(n, buf):
        for d in range(2):
            put_scores((buf + 2 + d) % DEPTH_SLOTS, scores(n + 2 + d))
        for d in range(2):
            softmax(buf + d, slot_shift(n + d))
        for d in range(2):
            weigh(n - 2 + d, (buf + 2 + d) % DEPTH_SLOTS)

    def pair_buffer(pair):
        return (2 * pair) % DEPTH_SLOTS

    def loop_trip(i, carry):
        for pair in range(PAIRS_PER_TRIP):
            stage_pair(2 * (PAIRS_PER_TRIP * i + pair), pair_buffer(pair))
        return carry

    q_sq = jnp.square(q)
    reach = jnp.zeros((1, 1), _F32)
    bounds = []
    for c in range(2):
        in_map = (lane >= c * DIFF_HEAD_DIM) & (lane < (c + 1) * DIFF_HEAD_DIM)
        q_norm = jnp.sqrt(jnp.max(jnp.sum(jnp.where(in_map, q_sq, 0.0), axis=1, keepdims=True),
                                  axis=0, keepdims=True))
        bounds.append(q_norm * knorm_ref[c][:, :1])
        reach = jnp.maximum(reach, (2.0 * bounds[c] + F32_EXP2_ZERO) / slope)
    tiles_in_reach = jnp.minimum(jnp.floor((reach - 1.0) / tk) + 1.0, float(1 << 20))
    n_walk = jnp.minimum(jnp.max(tiles_in_reach).astype(jnp.int32), n_past)
    pairs = (own + n_walk + 1) // 2

    def online_walk():
        for n in range(own):
            put_scores(n, scores(n) + pen_ref[n])
        for n in range(own, DEPTH_SLOTS):
            p_refs[n][...] = jnp.zeros_like(p_refs[n])
            a_refs[n][...] = jnp.ones_like(a_refs[n])
        trips = pairs // PAIRS_PER_TRIP
        lax.fori_loop(0, trips, loop_trip, 0)

        for pair in range(PAIRS_PER_TRIP - 1):
            @pl.when(pairs % PAIRS_PER_TRIP > pair)
            def _(pair=pair):
                stage_pair(2 * (PAIRS_PER_TRIP * trips + pair), pair_buffer(pair))

        for parity in range(2):
            @pl.when(pairs % 2 == parity)
            def _(parity=parity):
                for d in range(2):
                    weigh(2 * pairs - 2 + d, pair_buffer(parity + 1) + d)

    def fixed_walk():
        q_pos = (qi * tq + lax.broadcasted_iota(jnp.int32, (tq, 1), 0)).astype(_F32)

        def split3(x):
            hi = x.astype(_BF16).astype(_F32)
            mid = (x - hi).astype(_BF16).astype(_F32)
            return [hi, mid, x - hi - mid]

        blocks = []
        for c in range(2):
            cols = split3(slope) + split3(slope * tk) + split3(-slope * q_pos - bounds[c])
            aug = jnp.zeros((tq, LANES), _F32)
            for i, v in enumerate(cols):
                aug = jnp.where(lane == i, v, aug)
            in_map = (lane >= c * DIFF_HEAD_DIM) & (lane < (c + 1) * DIFF_HEAD_DIM)
            blocks.append(jnp.concatenate([jnp.where(in_map, q, 0.0), aug], axis=1))
        q_aug = jnp.concatenate(blocks, axis=0).astype(_BF16)

        def tile_keys(n):
            j = jnp.where(n < own + n_walk, slot_tile(n), -(1 << 20)).astype(_F32)
            aug = jnp.where((lane >= 3) & (lane < 6), j, kaug_ref[...]).astype(_BF16)
            return jnp.concatenate([k_ref[0, pl.ds(tile_start(n), tk), :], aug], axis=1)

        def weights(n, buf, own_slot):
            keys = tile_keys(n)
            for c in range(2 * tq // MXU_WIDTH):
                strip = slice(c * MXU_WIDTH, (c + 1) * MXU_WIDTH)
                e = _dot_nt(keys, q_aug[strip])
                if own_slot:
                    e = e + pen2_ref[n, :, strip]
                p = jnp.exp2(e)
                l_ref[:, strip] += jnp.sum(p, axis=0, keepdims=True)
                p_refs[buf][c] = p.astype(_BF16)

        def values(n, buf):
            vt = vt_ref[0, :, pl.ds(tile_start(n), tk)]
            for c in range(2 * tq // MXU_WIDTH):
                acc_ref[c] += _dot(vt, p_refs[buf][c])

        def pair_step(t, buf):
            for d in range(2):
                weights(2 * t + 2 + d, (buf + 2 + d) % DEPTH_SLOTS, False)
            for d in range(2):
                values(2 * t + d, buf + d)

        def trip(i, carry):
            for pair in range(PAIRS_PER_TRIP):
                pair_step(PAIRS_PER_TRIP * i + pair, pair_buffer(pair))
            return carry

        for n in range(own):
            weights(n, n, True)
        steps = pairs - 1
        trips = steps // PAIRS_PER_TRIP
        lax.fori_loop(0, trips, trip, 0)
        for pair in range(PAIRS_PER_TRIP - 1):
            @pl.when(steps % PAIRS_PER_TRIP > pair)
            def _(pair=pair):
                pair_step(PAIRS_PER_TRIP * trips + pair, pair_buffer(pair))

        for parity in range(2):
            @pl.when(pairs % 2 == parity)
            def _(parity=parity):
                for d in range(2):
                    values(2 * pairs - 2 + d, pair_buffer(parity + 1) + d)

    spread = 2.0 * jnp.max(jnp.maximum(bounds[0], bounds[1]))
    pl.when(spread < SAFE_SPREAD)(fixed_walk)
    pl.when(spread >= SAFE_SPREAD)(online_walk)

    lam = (jnp.exp(jnp.sum(lq1_ref[...] * lk1_ref[...], axis=-1, keepdims=True))
           - jnp.exp(jnp.sum(lq2_ref[...] * lk2_ref[...], axis=-1, keepdims=True)) + lam_init)
    o = jnp.concatenate([acc_ref[c] for c in range(2 * tq // MXU_WIDTH)], axis=1) / l_ref[...]
    o = o[:, :tq] - lam * o[:, tq:]
    o = o * lax.rsqrt(jnp.mean(o * o, axis=0, keepdims=True) + RMS_EPS)
    o_ref[0] = (o.T * g_ref[...] * (1.0 - lam_init)).astype(o_ref.dtype)


def _diff_prompt_call(q, k, vt, lq1, lk1, lq2, lk2, subln_g, *, tq, lam_init):
    b, nq, _ = q.shape
    tk = tq // 2
    assert 2 * DIFF_HEAD_DIM == LANES
    assert tk % CHUNK == 0 and CHUNK == 1 << CHUNK_SHIFT and tk <= 256
    q_spec, k_spec, vt_spec = _prompt_specs(nq, tq)
    lam_spec = pl.BlockSpec((1, DIFF_HEAD_DIM), lambda bi, g, qi: (0, 0))
    stat = pltpu.VMEM((1, 2 * tq), _F32)
    return pl.pallas_call(
        functools.partial(_diff_prompt_kernel, tq=tq, tk=tk, lam_init=lam_init),
        grid=(b, DIFF_HEADS, nq // tq),
        in_specs=[q_spec, k_spec, vt_spec, lam_spec, lam_spec, lam_spec, lam_spec,
                  pl.BlockSpec((1, LANES), lambda bi, g, qi: (0, 0))],
        out_specs=q_spec,
        out_shape=jax.ShapeDtypeStruct((b, nq, DIFF_WIDTH), _BF16),
        scratch_shapes=[stat, stat, pltpu.VMEM((2 * tq // MXU_WIDTH, LANES, MXU_WIDTH), _F32),
                        pltpu.VMEM((2, tk, 2 * tq), _F32), pltpu.VMEM((2, tk, 2 * tq), _F32),
                        pltpu.VMEM((tk, LANES), _BF16), pltpu.VMEM((tk, LANES), _F32),
                        pltpu.VMEM((2, 1, LANES), _F32)]
        + [pltpu.VMEM((2 * tq // LANES, tk, LANES), _F32)] * DEPTH_SLOTS
        + [pltpu.VMEM((2 * tq // MXU_WIDTH, tk, MXU_WIDTH), _BF16)] * DEPTH_SLOTS
        + [stat] * DEPTH_SLOTS,
        compiler_params=pltpu.CompilerParams(
            dimension_semantics=("arbitrary", "arbitrary", "arbitrary"),
            vmem_limit_bytes=VMEM_LIMIT_BYTES),
        name="diff_prompt",
    )(q, k, vt, lq1, lk1, lq2, lk2, subln_g)


def _encoder_layer(x, past, w, lam_init):
    b, n, _ = x.shape
    m = b * n
    h = _ffn(x.reshape(m, D_MODEL), w["ff1_pre_g"], w["ff1_w_gate"], w["ff1_w_up"], w["ff1_w_down"],
             w["ff1_post_g"])
    lam_args = (w["lam_q1"], w["lam_k1"], w["lam_q2"], w["lam_k2"], w["subln_g"])

    def seq(a):
        return a.reshape(b, n, a.shape[-1])

    if past is None:
        sbq, sbk, sbv, dq, dk, dv, sbk16, sbvt16, dk16, dvt16 = _qkv(h, w["mix_pre_g"], w["w_in"], seq_len=n)
        sb_o = _sb_prompt_call(seq(sbq), seq(sbk16), sbvt16, t=SB_PROMPT_TILE, group=SB_PROMPT_GROUP)
        d_o = _diff_prompt_call(seq(dq), seq(dk16), dvt16, *lam_args, tq=DIFF_PROMPT_TILE, lam_init=lam_init)
        sbk, sbv = (a.reshape(b, SB_HEADS, SB_HEAD_DIM, n).transpose(0, 3, 1, 2) for a in (sbk, sbv))
    else:
        sbq, sbk, sbv, dq, dk, dv, sbk16, sbv16, dk16, dv16 = _qkv(h, w["mix_pre_g"], w["w_in"])
        sb_o = _sb_step_call(seq(sbq), seq(sbk16), seq(sbv16), past[0], past[1])
        d_o = _diff_step_call(seq(dq), seq(dk16), seq(dv16), past[2], past[3], *lam_args, lam_init=lam_init)
    y = _ffn(h, w["ff2_pre_g"], w["ff2_w_gate"], w["ff2_w_up"], w["ff2_w_down"], w["ff2_post_g"],
             mixer=(sb_o.reshape(m, SB_WIDTH), d_o.reshape(m, DIFF_WIDTH), w["w_out"], w["mix_post_g"]),
             final_g=w["final_g"])
    rows = (sbk.reshape(b, n, SB_HEADS, SB_HEAD_DIM), sbv.reshape(b, n, SB_HEADS, SB_HEAD_DIM),
            dk.reshape(b, n, DIFF_HEADS, 2 * DIFF_HEAD_DIM), dv.reshape(b, n, DIFF_HEADS, 2 * DIFF_HEAD_DIM))
    return y.reshape(b, n, D_MODEL), rows


_MATRICES = ("ff1_w_gate", "ff1_w_up", "ff1_w_down", "w_in", "w_out", "ff2_w_gate", "ff2_w_up", "ff2_w_down")


def kernel(x_prompt, x_sample, cache_sb_k, cache_sb_v, cache_diff_k, cache_diff_v, ff1_pre_g, ff1_w_gate, ff1_w_up, ff1_w_down, ff1_post_g, mix_pre_g, w_in, lam_q1, lam_k1, lam_q2, lam_k2, subln_g, w_out, mix_post_g, ff2_pre_g, ff2_w_gate, ff2_w_up, ff2_w_down, ff2_post_g, final_g):
    params = dict(ff1_pre_g=ff1_pre_g, ff1_w_gate=ff1_w_gate, ff1_w_up=ff1_w_up, ff1_w_down=ff1_w_down,
                  ff1_post_g=ff1_post_g, mix_pre_g=mix_pre_g, w_in=w_in, lam_q1=lam_q1, lam_k1=lam_k1,
                  lam_q2=lam_q2, lam_k2=lam_k2, subln_g=subln_g, w_out=w_out, mix_post_g=mix_post_g,
                  ff2_pre_g=ff2_pre_g, ff2_w_gate=ff2_w_gate, ff2_w_up=ff2_w_up, ff2_w_down=ff2_w_down,
                  ff2_post_g=ff2_post_g, final_g=final_g)
    yp, ys = x_prompt, x_sample
    rows_p, rows_s = [], []
    for l in range(DEPTH):
        lam_init = 0.8 - 0.6 * math.exp(-0.3 * l)
        w = {name: (p[l].astype(_BF16) if name in _MATRICES else p[l][None, :].astype(_F32))
             for name, p in params.items()}
        yp, rp = _encoder_layer(yp, None, w, lam_init)
        past = tuple([c[l].transpose(0, 2, 3, 1).reshape(c.shape[1], SB_WIDTH, c.shape[2])
                      for c in (cache_sb_k, cache_sb_v)]
                     + [c[l].reshape(c.shape[1], -1, c.shape[-1]) for c in (cache_diff_k, cache_diff_v)])
        ys, rs = _encoder_layer(ys, past, w, lam_init)
        rows_p.append(rp)
        rows_s.append(rs)
    stacked_p = [jnp.stack(r, axis=0) for r in zip(*rows_p)]
    stacked_s = [jnp.stack(r, axis=0) for r in zip(*rows_s)]
    return (yp, ys, *stacked_p, *stacked_s)
```

```python
import functools
import math

import jax
import jax.numpy as jnp
from jax import lax
from jax.experimental import pallas as pl
from jax.experimental.pallas import tpu as pltpu

D_MODEL = 1024
DEPTH = 1
CHUNK = 64
CHUNK_SHIFT = 6
SB_HEADS = 8
SB_HEAD_DIM = 64
DIFF_HEADS = 4
DIFF_HEAD_DIM = 64
SB_WIDTH = SB_HEADS * SB_HEAD_DIM
DIFF_WIDTH = DIFF_HEADS * 2 * DIFF_HEAD_DIM
MIX_WIDTH = SB_WIDTH + DIFF_WIDTH
IN_WIDTH = 3 * SB_WIDTH + 3 * DIFF_WIDTH
D_FF = 2816
RMS_EPS = 1e-6

LANES = 128
MXU_WIDTH = 256
VMEM_LIMIT_BYTES = 56 * 1024 * 1024
F32_EXP2_ZERO = 151.0
NEG_BIG = -1e30
LOG2E = math.log2(math.e)
DEPTH_SLOTS = 4
PAIRS_PER_TRIP = 2
SAFE_SPREAD = 90.0
SB_PROMPT_TILE = 128
SB_PROMPT_GROUP = 8
SB_PROMPT_DEPTH = 3
DIFF_PROMPT_TILE = 512

_F32 = jnp.float32
_BF16 = jnp.bfloat16


def _rms(x, g):
    return x * lax.rsqrt(jnp.mean(x * x, axis=-1, keepdims=True) + RMS_EPS) * g


def _stick_logs(z):
    log_beta = jnp.minimum(z, 0.0) - jnp.log2(1.0 + jnp.exp2(-jnp.abs(z)))
    return log_beta, log_beta - z


def _split_hi_lo(x):
    hi = lax.bitcast_convert_type(lax.bitcast_convert_type(x, jnp.uint32) & jnp.uint32(0xFFFF0000), _F32)
    return hi.astype(_BF16), (x - hi).astype(_BF16)


def _dot(a, b):
    return jnp.dot(a, b, preferred_element_type=_F32)


def _dot_nt(a, b):
    return lax.dot_general(a, b, (((1,), (1,)), ((), ())), preferred_element_type=_F32)


def _ffn_kernel(*refs, mixer, final_norm):
    refs = list(refs)
    o_ref = refs.pop()
    x = refs.pop(0)[...]
    if mixer:
        sbo_ref, do_ref, wo_ref, mix_g_ref = refs[:4]
        refs = refs[4:]
        mix = _dot(sbo_ref[...], wo_ref[:SB_WIDTH, :]) + _dot(do_ref[...], wo_ref[SB_WIDTH:, :])
        x = x + _rms(mix, mix_g_ref[...])
    pre_ref, wg_ref, wu_ref, wd_ref, post_ref = refs[:5]
    xn = _rms(x, pre_ref[...]).astype(_BF16)
    g = _dot(xn, wg_ref[...])
    u = _dot(xn, wu_ref[...])
    a = (g * jax.nn.sigmoid(g) * u).astype(_BF16)
    h = x + 0.5 * _rms(_dot(a, wd_ref[...]), post_ref[...])
    if final_norm:
        h = _rms(h, refs[5][...])
    o_ref[...] = h


def _ffn(x, pre_g, wg, wu, wd, post_g, *, mixer=None, final_g=None):
    m = x.shape[0]
    tm = min(512, m)
    assert m % tm == 0

    def resident(shape):
        return pl.BlockSpec(shape, lambda i: (0, 0), pipeline_mode=pl.Buffered(1))

    row = pl.BlockSpec((tm, D_MODEL), lambda i: (i, 0))
    vec = resident((1, D_MODEL))
    in_specs, args = [row], [x]
    if mixer is not None:
        half = pl.BlockSpec((tm, SB_WIDTH), lambda i: (i, 0))
        in_specs += [half, half, resident((MIX_WIDTH, D_MODEL)), vec]
        args += list(mixer)
    in_specs += [vec, resident((D_MODEL, D_FF)), resident((D_MODEL, D_FF)), resident((D_FF, D_MODEL)), vec]
    args += [pre_g, wg, wu, wd, post_g]
    if final_g is not None:
        in_specs.append(vec)
        args.append(final_g)
    return pl.pallas_call(
        functools.partial(_ffn_kernel, mixer=mixer is not None, final_norm=final_g is not None),
        grid=(m // tm,),
        in_specs=in_specs,
        out_specs=row,
        out_shape=jax.ShapeDtypeStruct((m, D_MODEL), _F32),
        compiler_params=pltpu.CompilerParams(
            dimension_semantics=("parallel",), vmem_limit_bytes=VMEM_LIMIT_BYTES),
        name="ffn_mix_final" if mixer is not None else "ffn",
    )(*args)


def _qkv_kernel(h_ref, g_ref, w_ref, sbq_ref, sbk_ref, sbv_ref, dq_ref, dk_ref, dv_ref,
                sbk16_ref, sbv16_ref, dk16_ref, dv16_ref, *, transposed):
    hn = _rms(h_ref[...], g_ref[...]).astype(_BF16)

    def proj(idx):
        return _dot(hn, w_ref[:, idx * SB_WIDTH:(idx + 1) * SB_WIDTH])

    sbq_ref[...] = (proj(0) * (SB_HEAD_DIM ** -0.5 * LOG2E)).astype(_BF16)
    dq_ref[...] = (proj(3) * (DIFF_HEAD_DIM ** -0.5 * LOG2E)).astype(_BF16)

    def store_rows(ref, p):
        head_dim = ref.shape[1]
        heads = p.shape[1] // head_dim
        for head in range(heads):
            ref[pl.ds(head, p.shape[0], stride=heads), :] = p[:, head * head_dim:(head + 1) * head_dim]

    for idx, full_ref, half_ref in ((4, dk_ref, dk16_ref), (5, dv_ref, dv16_ref)):
        p = proj(idx)
        store_rows(full_ref, p)
        if transposed and half_ref is dv16_ref:
            half_ref[0] = p.T.astype(_BF16)
        else:
            half_ref[...] = p.astype(_BF16)
    for idx, full_ref, half_ref in ((1, sbk_ref, sbk16_ref), (2, sbv_ref, sbv16_ref)):
        p = proj(idx)
        if transposed:
            p_t = p.T
            full_ref[0] = p_t
            if half_ref is sbv16_ref:
                half_ref[0] = p_t.astype(_BF16)
            else:
                half_ref[...] = p.astype(_BF16)
        else:
            store_rows(full_ref, p)
            half_ref[...] = p.astype(_BF16)


def _qkv(h, g, w, *, seq_len=None):
    assert SB_WIDTH == DIFF_WIDTH
    assert math.log2(SB_HEAD_DIM) % 2 == 0 and math.log2(DIFF_HEAD_DIM) % 2 == 0
    m = h.shape[0]
    tm = min(512, m)
    assert m % tm == 0
    row = pl.BlockSpec((tm, D_MODEL), lambda i: (i, 0))
    out = pl.BlockSpec((tm, SB_WIDTH), lambda i: (i, 0))
    b16 = jax.ShapeDtypeStruct((m, SB_WIDTH), _BF16)

    def cache_rows(heads, head_dim):
        return (jax.ShapeDtypeStruct((m * heads, head_dim), _F32),
                pl.BlockSpec((tm * heads, head_dim), lambda i: (i, 0)))

    sb_rows, sb_out = cache_rows(SB_HEADS, SB_HEAD_DIM)
    d_rows, d_out = cache_rows(DIFF_HEADS, 2 * DIFF_HEAD_DIM)
    v_out, v16 = out, b16
    if seq_len is not None:
        assert seq_len % tm == 0
        tiles = seq_len // tm
        v_out = pl.BlockSpec((1, SB_WIDTH, tm), lambda i: (i // tiles, 0, i % tiles))
        v16 = jax.ShapeDtypeStruct((m // seq_len, SB_WIDTH, seq_len), _BF16)
        sb_rows, sb_out = jax.ShapeDtypeStruct(v16.shape, _F32), v_out
    return pl.pallas_call(
        functools.partial(_qkv_kernel, transposed=seq_len is not None),
        grid=(m // tm,),
        in_specs=[row, pl.BlockSpec((1, D_MODEL), lambda i: (0, 0)),
                  pl.BlockSpec((D_MODEL, IN_WIDTH), lambda i: (0, 0))],
        out_specs=[out, sb_out, sb_out, out, d_out, d_out, out, v_out, out, v_out],
        out_shape=[b16, sb_rows, sb_rows, b16, d_rows, d_rows, b16, v16, b16, v16],
        compiler_params=pltpu.CompilerParams(
            dimension_semantics=("parallel",), vmem_limit_bytes=VMEM_LIMIT_BYTES),
        name="qkv",
    )(h, g, w)


def _sb_step_kernel(q_ref, kn_ref, vn_ref, kc_ref, vc_ref, o_ref, acc_ref, c_ref, *, tk):
    n = q_ref.shape[1]
    past = kc_ref.shape[2]

    def later(t):
        return (lax.broadcasted_iota(jnp.int32, (t, t), 0)
                > lax.broadcasted_iota(jnp.int32, (t, t), 1)).astype(_BF16)

    later_new, later_tile = later(n), later(tk)
    newer = (lax.broadcasted_iota(jnp.int32, (n, n), 1) < lax.broadcasted_iota(jnp.int32, (n, n), 0))

    def walk(z, c, later_t, mask):
        log_beta, log_stay = _stick_logs(z)
        if mask is not None:
            log_stay = jnp.where(mask, log_stay, 0.0)
        sums = _dot(jnp.concatenate(_split_hi_lo(log_stay), axis=0), later_t)
        w = jnp.exp2(log_beta + (sums[:n] + sums[n:] + c))
        if mask is not None:
            w = jnp.where(mask, w, 0.0)
        return c + jnp.sum(log_stay, axis=-1, keepdims=True), w.astype(_BF16)

    def head_cols(head):
        return slice(head * SB_HEAD_DIM, (head + 1) * SB_HEAD_DIM)

    def walk_cache(start, c_of, first):
        for head in range(SB_HEADS):
            rows = head_cols(head)
            k_t = kc_ref[0, rows, pl.ds(start, tk)].astype(_BF16)
            v_t = vc_ref[0, rows, pl.ds(start, tk)].astype(_BF16)
            c, w = walk(_dot(q_ref[0, :, rows], k_t), c_of(head), later_tile, None)
            o = _dot_nt(w, v_t)
            c_ref[head] = c
            acc_ref[head] = o + first[head] if first else acc_ref[head] + o

    new = []
    for head in range(SB_HEADS):
        cols = head_cols(head)
        c, w = walk(_dot_nt(q_ref[0, :, cols], kn_ref[0, :, cols]), jnp.zeros((n, 1), _F32), later_new, newer)
        new.append((c, _dot(w, vn_ref[0, :, cols])))
    walk_cache(past - tk, lambda head: new[head][0], [o for _, o in new])

    def cond(state):
        j, c_max = state
        return (j >= 0) & (c_max > -F32_EXP2_ZERO)

    def body(state):
        j, _ = state
        walk_cache(pl.multiple_of(j * tk, tk), lambda head: c_ref[head], None)
        return j - 1, jnp.max(c_ref[...])

    lax.while_loop(cond, body, (past // tk - 2, jnp.max(c_ref[...])))
    o_ref[0] = jnp.concatenate([acc_ref[head] for head in range(SB_HEADS)], axis=-1).astype(o_ref.dtype)


def _sb_step_call(q, k_new, v_new, k_cache, v_cache):
    b, n, _ = q.shape
    past = k_cache.shape[2]
    tk = min(256, past)
    assert past % tk == 0
    new = pl.BlockSpec((1, n, SB_WIDTH), lambda i: (i, 0, 0))
    cache = pl.BlockSpec((1, SB_WIDTH, past), lambda i: (i, 0, 0))
    return pl.pallas_call(
        functools.partial(_sb_step_kernel, tk=tk),
        grid=(b,),
        in_specs=[new, new, new, cache, cache],
        out_specs=new,
        out_shape=jax.ShapeDtypeStruct((b, n, SB_WIDTH), _BF16),
        scratch_shapes=[pltpu.VMEM((SB_HEADS, n, SB_HEAD_DIM), _F32), pltpu.VMEM((SB_HEADS, n, 1), _F32)],
        compiler_params=pltpu.CompilerParams(
            dimension_semantics=("arbitrary",), vmem_limit_bytes=VMEM_LIMIT_BYTES),
        name="sb_step",
    )(q, k_new, v_new, k_cache, v_cache)


def _alibi_slope_log2(head):
    assert 8 % DIFF_HEADS == 0
    exponent = (8 // DIFF_HEADS) * (head + 1)
    slope = lax.bitcast_convert_type(jnp.full((1, 1), (127 - exponent) << 23, jnp.int32), _F32)
    return slope * LOG2E


def _diff_step_kernel(q_ref, kn_ref, vn_ref, kc_ref, vc_ref, lq1_ref, lk1_ref, lq2_ref, lk2_ref, g_ref,
                      o_ref, *, lam_init):
    n = q_ref.shape[1]
    past = kc_ref.shape[1] // DIFF_HEADS

    def cache_head(ref, head):
        return ref[0, pl.ds(head, past, stride=DIFF_HEADS), :].astype(_BF16)
    lane = lax.broadcasted_iota(jnp.int32, (1, LANES), 1)
    q_idx_c = lax.broadcasted_iota(jnp.int32, (2 * n, past), 0) & (n - 1)
    k_idx_c = lax.broadcasted_iota(jnp.int32, (2 * n, past), 1)
    q_idx_n = lax.broadcasted_iota(jnp.int32, (2 * n, n), 0) & (n - 1)
    k_idx_n = lax.broadcasted_iota(jnp.int32, (2 * n, n), 1)
    distance_c = (past + q_idx_c - k_idx_c).astype(_F32)
    distance_n = jnp.abs(q_idx_n - k_idx_n).astype(_F32)
    visible_n =(lax.shift_right_logical(past + k_idx_n, CHUNK_SHIFT)
                 <= lax.shift_right_logical(past + q_idx_n, CHUNK_SHIFT))
    lam = (jnp.exp(jnp.sum(lq1_ref[...] * lk1_ref[...], axis=-1, keepdims=True))
           - jnp.exp(jnp.sum(lq2_ref[...] * lk2_ref[...], axis=-1, keepdims=True)) + lam_init)
    for head in range(DIFF_HEADS):
        cols = slice(head * LANES, (head + 1) * LANES)
        slope = _alibi_slope_log2(head)
        q = q_ref[0, :, cols].astype(_F32)
        q_maps = jnp.concatenate([jnp.where(lane < DIFF_HEAD_DIM, q, 0.0),
                                  jnp.where(lane >= DIFF_HEAD_DIM, q, 0.0)], axis=0).astype(_BF16)
        s_c = _dot_nt(q_maps, cache_head(kc_ref, head)) - slope * distance_c
        s_n = jnp.where(visible_n, _dot_nt(q_maps, kn_ref[0, :, cols]) - slope * distance_n, -jnp.inf)
        m = jnp.maximum(jnp.max(s_c, axis=-1, keepdims=True), jnp.max(s_n, axis=-1, keepdims=True))
        p_c = jnp.exp2(s_c - m)
        p_n = jnp.exp2(s_n - m)
        l = jnp.sum(p_c, axis=-1, keepdims=True) + jnp.sum(p_n, axis=-1, keepdims=True)
        o = (_dot(p_c.astype(_BF16), cache_head(vc_ref, head))
             + _dot(p_n.astype(_BF16), vn_ref[0, :, cols])) / l
        o = o[:n] - lam * o[n:]
        o_ref[0, :, cols] = (_rms(o, g_ref[...]) * (1.0 - lam_init)).astype(o_ref.dtype)


def _diff_step_call(q, k_new, v_new, k_cache, v_cache, lq1, lk1, lq2, lk2, subln_g, *, lam_init):
    b, n, _ = q.shape
    rows = k_cache.shape[1]
    assert 2 * DIFF_HEAD_DIM == LANES and CHUNK == 1 << CHUNK_SHIFT and n & (n - 1) == 0
    new = pl.BlockSpec((1, n, DIFF_WIDTH), lambda i: (i, 0, 0))
    cache = pl.BlockSpec((1, rows, LANES), lambda i: (i, 0, 0))
    lam_spec = pl.BlockSpec((1, DIFF_HEAD_DIM), lambda i: (0, 0))
    return pl.pallas_call(
        functools.partial(_diff_step_kernel, lam_init=lam_init),
        grid=(b,),
        in_specs=[new, new, new, cache, cache, lam_spec, lam_spec, lam_spec, lam_spec,
                  pl.BlockSpec((1, LANES), lambda i: (0, 0))],
        out_specs=new,
        out_shape=jax.ShapeDtypeStruct((b, n, DIFF_WIDTH), _BF16),
        compiler_params=pltpu.CompilerParams(
            dimension_semantics=("arbitrary",), vmem_limit_bytes=VMEM_LIMIT_BYTES),
        name="diff_step",
    )(q, k_new, v_new, k_cache, v_cache, lq1, lk1, lq2, lk2, subln_g)


def _prompt_specs(nq, t):
    assert nq % t == 0 and t & (t - 1) == 0
    q_spec = pl.BlockSpec((1, t, LANES), lambda bi, g, qi: (bi, qi, g))
    k_spec = pl.BlockSpec((1, nq, LANES), lambda bi, g, qi: (bi, 0, g))
    vt_spec = pl.BlockSpec((1, LANES, nq), lambda bi, g, qi: (bi, g, 0))
    return q_spec, k_spec, vt_spec


def _sb_prompt_kernel(q_ref, k_ref, vt_ref, o_ref, acc_ref, c_ref, z_ref, lb_ref, hl_ref, w_ref, *,
                      t, group, depth):
    assert LANES == 2 * SB_HEAD_DIM
    strips = 2 * t // LANES
    qi = pl.program_id(2)
    lane = lax.broadcasted_iota(jnp.int32, (1, LANES), 1)
    key = lax.broadcasted_iota(jnp.int32, (t, LANES), 0)
    qry = lax.broadcasted_iota(jnp.int32, (t, LANES), 1)
    later = (lax.broadcasted_iota(jnp.int32, (t, t), 1)
             > lax.broadcasted_iota(jnp.int32, (t, t), 0)).astype(_BF16)
    def tile_queries(g):
        q = q_ref[0, g * t:(g + 1) * t, :].astype(_F32)
        return jnp.concatenate([jnp.where(lane < SB_HEAD_DIM, q, 0.0),
                                jnp.where(lane >= SB_HEAD_DIM, q, 0.0)], axis=0).astype(_BF16)

    q_heads = [tile_queries(g) for g in range(group)]

    def stage_scores(g, j, buf):
        z = _dot_nt(k_ref[0, pl.ds(pl.multiple_of(j * t, t), t), :], q_heads[g])
        for s in range(strips):
            z_ref[buf, s] = z[:, s * LANES:(s + 1) * LANES]

    def own_mask(s):
        return key < ((s * LANES + qry) & (t - 1))

    def stage_logs(c, own, buf):
        c_new = []
        for s in range(strips):
            lb_ref[buf, s], log_stay = _stick_logs(z_ref[buf, s])
            if own:
                log_stay = jnp.where(own_mask(s), log_stay, 0.0)
            hl_ref[buf, s] = jnp.concatenate(_split_hi_lo(log_stay), axis=1)
            c_new.append(c[:, s * LANES:(s + 1) * LANES] + jnp.sum(log_stay, axis=0, keepdims=True))
        return jnp.concatenate(c_new, axis=1)

    def stage_weights(c, own, buf):
        for s in range(strips):
            sums = _dot(later, hl_ref[buf, s])
            w = jnp.exp2(lb_ref[buf, s] + (sums[:, :LANES] + sums[:, LANES:] + c[:, s * LANES:(s + 1) * LANES]))
            if own:
                w = jnp.where(own_mask(s), w, 0.0)
            head, part = divmod(s, t // LANES)
            w_ref[buf, head, :, part * LANES:(part + 1) * LANES] = w.astype(_BF16)

    def stage_values(j, buf):
        vt = vt_ref[0, :, pl.ds(pl.multiple_of(j * t, t), t)]
        return jnp.concatenate([_dot(vt[:SB_HEAD_DIM], w_ref[buf, 0]), _dot(vt[SB_HEAD_DIM:], w_ref[buf, 1])],
                               axis=0)

    def walk_tile(g, j, c, own, buf):
        stage_scores(g, j, buf)
        c_new = stage_logs(c, own, buf)
        stage_weights(c, own, buf)
        return c_new, stage_values(j, buf)

    def first_tiles(exists):
        walks = [(g, qi * group + g - d, d == 0, depth * g + d)
                 for g in range(group) for d in range(depth) if exists(g, d)]
        for g, j, own, buf in walks:
            stage_scores(g, j, buf)
        c_in = {}
        for g, j, own, buf in walks:
            c_in[buf] = jnp.zeros((1, 2 * t), _F32) if own else c_ref[g]
            c_ref[g] = stage_logs(c_in[buf], own, buf)
        for g, j, own, buf in walks:
            stage_weights(c_in[buf], own, buf)
        for g, j, own, buf in walks:
            o = stage_values(j, buf)
            acc_ref[g] = o if own else acc_ref[g] + o

    assert group >= depth - 1

    @pl.when(qi == 0)
    def _():
        first_tiles(lambda g, d: g >= d)

    @pl.when(qi > 0)
    def _():
        first_tiles(lambda g, d: True)

    def older_tile(g, step):
        return qi * group + g - depth - step

    def any_active(step):
        flags = [(older_tile(g, step) >= 0) & (jnp.max(c_ref[g]) > -F32_EXP2_ZERO) for g in range(group)]
        return functools.reduce(jnp.logical_or, flags)

    def walk_older(state):
        step, _ = state
        for g in range(group):
            @pl.when((older_tile(g, step) >= 0) & (jnp.max(c_ref[g]) > -F32_EXP2_ZERO))
            def _(g=g):
                c, o = walk_tile(g, older_tile(g, step), c_ref[g], False, depth * g)
                c_ref[g] = c
                acc_ref[g] += o
        return step + 1, any_active(step + 1)

    lax.while_loop(lambda state: state[1], walk_older, (0, any_active(0)))
    for g in range(group):
        o_ref[0, g * t:(g + 1) * t, :] = acc_ref[g].T.astype(o_ref.dtype)


def _sb_prompt_call(q, k, vt, *, t, group, depth):
    b, nq, _ = q.shape
    q_spec, k_spec, vt_spec = _prompt_specs(nq, group * t)
    sets = depth * group
    strips = 2 * t // LANES
    return pl.pallas_call(
        functools.partial(_sb_prompt_kernel, t=t, group=group, depth=depth),
        grid=(b, SB_WIDTH // LANES, nq // (group * t)),
        in_specs=[q_spec, k_spec, vt_spec],
        out_specs=q_spec,
        out_shape=jax.ShapeDtypeStruct((b, nq, SB_WIDTH), _BF16),
        scratch_shapes=[pltpu.VMEM((group, LANES, t), _F32), pltpu.VMEM((group, 1, 2 * t), _F32),
                        pltpu.VMEM((sets, strips, t, LANES), _F32),
                        pltpu.VMEM((sets, strips, t, LANES), _F32),
                        pltpu.VMEM((sets, strips, t, 2 * LANES), _BF16),
                        pltpu.VMEM((sets, 2, t, t), _BF16)],
        compiler_params=pltpu.CompilerParams(
            dimension_semantics=("parallel", "parallel", "arbitrary"),
            vmem_limit_bytes=VMEM_LIMIT_BYTES),
        name="sb_prompt",
    )(q, k, vt)


def _diff_prompt_kernel(q_ref, k_ref, vt_ref, lq1_ref, lk1_ref, lq2_ref, lk2_ref, g_ref, o_ref,
                        m_ref, l_ref, acc_ref, pen_ref, pen2_ref, kpos_ref, kaug_ref, knorm_ref, *buffers,
                        tq, tk, lam_init):
    own = 2
    assert tq == own * tk
    head = pl.program_id(1)
    qi = pl.program_id(2)
    n_past = qi * own
    lane = lax.broadcasted_iota(jnp.int32, (1, LANES), 1)
    slope = _alibi_slope_log2(head)

    @pl.when(qi == 0)
    def _():
        key = lax.broadcasted_iota(jnp.int32, (tk, 2 * tq), 0)
        qry = lax.broadcasted_iota(jnp.int32, (tk, 2 * tq), 1) & (tq - 1)
        for d in range(own):
            k_pos = d * tk + key
            visible = lax.shift_right_logical(k_pos, CHUNK_SHIFT) <= lax.shift_right_logical(qry, CHUNK_SHIFT)
            bias = slope * (qry - key - jnp.abs(qry - k_pos)).astype(_F32)
            pen_ref[d] = jnp.where(visible, bias, -jnp.inf)
            ahead = -2.0 * slope * jnp.maximum(k_pos - qry, 0).astype(_F32)
            pen2_ref[d] = jnp.where(visible, ahead, -jnp.inf)
        row = lax.broadcasted_iota(jnp.int32, (tk, LANES), 0)
        col = lax.broadcasted_iota(jnp.int32, (tk, LANES), 1)
        kpos_ref[...] = jnp.where(col < 3, row, 0).astype(_F32).astype(_BF16)
        kaug_ref[...] = jnp.where(col < 3, row, jnp.where((col >= 6) & (col < 9), 1, 0)).astype(_F32)
        k_sq = jnp.square(k_ref[0].astype(_F32))
        for c in range(2):
            in_map = (lane >= c * DIFF_HEAD_DIM) & (lane < (c + 1) * DIFF_HEAD_DIM)
            norm_sq = jnp.sum(jnp.where(in_map, k_sq, 0.0), axis=1, keepdims=True)
            knorm_ref[c] = jnp.broadcast_to(jnp.sqrt(jnp.max(norm_sq, axis=0, keepdims=True)), (1, LANES))

    slope_hi = slope.astype(_BF16).astype(_F32)
    slope_mid = (slope - slope_hi).astype(_BF16).astype(_F32)
    slope_lo = slope - slope_hi - slope_mid
    slope_cols = jnp.where(lane == 0, slope_hi, jnp.where(lane == 1, slope_mid,
                                                          jnp.where(lane == 2, slope_lo, 0.0)))
    slope_cols = jnp.broadcast_to(slope_cols, (tq, LANES))
    q = q_ref[0].astype(_F32)
    q_maps = jnp.concatenate(
        [jnp.concatenate([jnp.where(lane < DIFF_HEAD_DIM, q, 0.0), slope_cols], axis=1),
         jnp.concatenate([jnp.where(lane >= DIFF_HEAD_DIM, q, 0.0), slope_cols], axis=1)],
        axis=0).astype(_BF16)
    m_ref[...] = jnp.full_like(m_ref, NEG_BIG)
    l_ref[...] = jnp.zeros_like(l_ref)
    acc_ref[...] = jnp.zeros_like(acc_ref)

    def slot_tile(n):
        return jnp.where(n < own, n_past + n, n_past - 1 - (n - own))

    def slot_shift(n):
        j = slot_tile(n)
        distance = (qi * tq - j * tk).astype(_F32)
        return jnp.where(n < own, 0.0, jnp.where(j >= 0, -slope * distance, NEG_BIG))

    def tile_start(n):
        return pl.multiple_of(jnp.clip(slot_tile(n), 0, n_past + own - 1) * tk, tk)

    def scores(n):
        keys = jnp.concatenate([k_ref[0, pl.ds(tile_start(n), tk), :], kpos_ref[...]], axis=1)
        return _dot_nt(keys, q_maps)

    s_refs, p_refs, a_refs = (buffers[i * DEPTH_SLOTS:(i + 1) * DEPTH_SLOTS] for i in range(3))

    def put_scores(buf, s):
        for c in range(2 * tq // LANES):
            s_refs[buf][c] = s[:, c * LANES:(c + 1) * LANES]

    def softmax(buf, shift):
        for c in range(2 * tq // LANES):
            strip = slice(c * LANES, (c + 1) * LANES)
            s = s_refs[buf][c]
            m_old = m_ref[:, strip]
            m_new = jnp.maximum(m_old, jnp.max(s, axis=0, keepdims=True) + shift)
            alpha = jnp.exp2(m_old - m_new)
            p = jnp.exp2(s - (m_new - shift))
            l_ref[:, strip] = alpha * l_ref[:, strip] + jnp.sum(p, axis=0, keepdims=True)
            m_ref[:, strip] = m_new
            a_refs[buf][:, strip] = alpha
            half = c % (MXU_WIDTH // LANES)
            p_refs[buf][c // (MXU_WIDTH // LANES), :, half * LANES:(half + 1) * LANES] = p.astype(_BF16)

    def weigh(n, buf):
        values = vt_ref[0, :, pl.ds(tile_start(n), tk)]
        for c in range(2 * tq // MXU_WIDTH):
            strip = slice(c * MXU_WIDTH, (c + 1) * MXU_WIDTH)
            acc_ref[c] = a_refs[buf][:, strip] * acc_ref[c] + _dot(values, p_refs[buf][c])

    def stage_pair(n, buf):
        for d in range(2):
            put_scores((buf + 2 + d) % DEPTH_SLOTS, scores(n + 2 + d))
        for d in range(2):
            softmax(buf + d, slot_shift(n + d))
        for d in range(2):
            weigh(n - 2 + d, (buf + 2 + d) % DEPTH_SLOTS)

    def pair_buffer(pair):
        return (2 * pair) % DEPTH_SLOTS

    def loop_trip(i, carry):
        for pair in range(PAIRS_PER_TRIP):
            stage_pair(2 * (PAIRS_PER_TRIP * i + pair), pair_buffer(pair))
        return carry

    q_sq = jnp.square(q)
    reach = jnp.zeros((1, 1), _F32)
    bounds = []
    for c in range(2):
        in_map = (lane >= c * DIFF_HEAD_DIM) & (lane < (c + 1) * DIFF_HEAD_DIM)
        q_norm = jnp.sqrt(jnp.max(jnp.sum(jnp.where(in_map, q_sq, 0.0), axis=1, keepdims=True),
                                  axis=0, keepdims=True))
        bounds.append(q_norm * knorm_ref[c][:, :1])
        reach = jnp.maximum(reach, (2.0 * bounds[c] + F32_EXP2_ZERO) / slope)
    tiles_in_reach = jnp.minimum(jnp.floor((reach - 1.0) / tk) + 1.0, float(1 << 20))
    n_walk = jnp.minimum(jnp.max(tiles_in_reach).astype(jnp.int32), n_past)
    pairs = (own + n_walk + 1) // 2

    def online_walk():
        for n in range(own):
            put_scores(n, scores(n) + pen_ref[n])
        for n in range(own, DEPTH_SLOTS):
            p_refs[n][...] = jnp.zeros_like(p_refs[n])
            a_refs[n][...] = jnp.ones_like(a_refs[n])
        trips = pairs // PAIRS_PER_TRIP
        lax.fori_loop(0, trips, loop_trip, 0)

        for pair in range(PAIRS_PER_TRIP - 1):
            @pl.when(pairs % PAIRS_PER_TRIP > pair)
            def _(pair=pair):
                stage_pair(2 * (PAIRS_PER_TRIP * trips + pair), pair_buffer(pair))

        for parity in range(2):
            @pl.when(pairs % 2 == parity)
            def _(parity=parity):
                for d in range(2):
                    weigh(2 * pairs - 2 + d, pair_buffer(parity + 1) + d)

    def fixed_walk():
        q_pos = (qi * tq + lax.broadcasted_iota(jnp.int32, (tq, 1), 0)).astype(_F32)

        def split3(x):
            hi = x.astype(_BF16).astype(_F32)
            mid = (x - hi).astype(_BF16).astype(_F32)
            return [hi, mid, x - hi - mid]

        blocks = []
        for c in range(2):
            cols = split3(slope) + split3(slope * tk) + split3(-slope * q_pos - bounds[c])
            aug = jnp.zeros((tq, LANES), _F32)
            for i, v in enumerate(cols):
                aug = jnp.where(lane == i, v, aug)
            in_map = (lane >= c * DIFF_HEAD_DIM) & (lane < (c + 1) * DIFF_HEAD_DIM)
            blocks.append(jnp.concatenate([jnp.where(in_map, q, 0.0), aug], axis=1))
        q_aug = jnp.concatenate(blocks, axis=0).astype(_BF16)

        def tile_keys(n):
            j = jnp.where(n < own + n_walk, slot_tile(n), -(1 << 20)).astype(_F32)
            aug = jnp.where((lane >= 3) & (lane < 6), j, kaug_ref[...]).astype(_BF16)
            return jnp.concatenate([k_ref[0, pl.ds(tile_start(n), tk), :], aug], axis=1)

        def weights(n, buf, own_slot):
            keys = tile_keys(n)
            for c in range(2 * tq // MXU_WIDTH):
                strip = slice(c * MXU_WIDTH, (c + 1) * MXU_WIDTH)
                e = _dot_nt(keys, q_aug[strip])
                if own_slot:
                    e = e + pen2_ref[n, :, strip]
                p = jnp.exp2(e)
                l_ref[:, strip] += jnp.sum(p, axis=0, keepdims=True)
                p_refs[buf][c] = p.astype(_BF16)

        def values(n, buf):
            vt = vt_ref[0, :, pl.ds(tile_start(n), tk)]
            for c in range(2 * tq // MXU_WIDTH):
                acc_ref[c] += _dot(vt, p_refs[buf][c])

        def pair_step(t, buf):
            for d in range(2):
                weights(2 * t + 2 + d, (buf + 2 + d) % DEPTH_SLOTS, False)
            for d in range(2):
                values(2 * t + d, buf + d)

        def trip(i, carry):
            for pair in range(PAIRS_PER_TRIP):
                pair_step(PAIRS_PER_TRIP * i + pair, pair_buffer(pair))
            return carry

        for n in range(own):
            weights(n, n, True)
        steps = pairs - 1
        trips = steps // PAIRS_PER_TRIP
        lax.fori_loop(0, trips, trip, 0)
        for pair in range(PAIRS_PER_TRIP - 1):
            @pl.when(steps % PAIRS_PER_TRIP > pair)
            def _(pair=pair):
                pair_step(PAIRS_PER_TRIP * trips + pair, pair_buffer(pair))

        for parity in range(2):
            @pl.when(pairs % 2 == parity)
            def _(parity=parity):
                for d in range(2):
                    values(2 * pairs - 2 + d, pair_buffer(parity + 1) + d)

    spread = 2.0 * jnp.max(jnp.maximum(bounds[0], bounds[1]))
    pl.when(spread < SAFE_SPREAD)(fixed_walk)
    pl.when(spread >= SAFE_SPREAD)(online_walk)

    lam = (jnp.exp(jnp.sum(lq1_ref[...] * lk1_ref[...], axis=-1, keepdims=True))
           - jnp.exp(jnp.sum(lq2_ref[...] * lk2_ref[...], axis=-1, keepdims=True)) + lam_init)
    o = jnp.concatenate([acc_ref[c] for c in range(2 * tq // MXU_WIDTH)], axis=1) / l_ref[...]
    o = o[:, :tq] - lam * o[:, tq:]
    o = o * lax.rsqrt(jnp.mean(o * o, axis=0, keepdims=True) + RMS_EPS)
    o_ref[0] = (o.T * g_ref[...] * (1.0 - lam_init)).astype(o_ref.dtype)


def _diff_prompt_call(q, k, vt, lq1, lk1, lq2, lk2, subln_g, *, tq, lam_init):
    b, nq, _ = q.shape
    tk = tq // 2
    assert 2 * DIFF_HEAD_DIM == LANES
    assert tk % CHUNK == 0 and CHUNK == 1 << CHUNK_SHIFT and tk <= 256
    q_spec, k_spec, vt_spec = _prompt_specs(nq, tq)
    lam_spec = pl.BlockSpec((1, DIFF_HEAD_DIM), lambda bi, g, qi: (0, 0))
    stat = pltpu.VMEM((1, 2 * tq), _F32)
    return pl.pallas_call(
        functools.partial(_diff_prompt_kernel, tq=tq, tk=tk, lam_init=lam_init),
        grid=(b, DIFF_HEADS, nq // tq),
        in_specs=[q_spec, k_spec, vt_spec, lam_spec, lam_spec, lam_spec, lam_spec,
                  pl.BlockSpec((1, LANES), lambda bi, g, qi: (0, 0))],
        out_specs=q_spec,
        out_shape=jax.ShapeDtypeStruct((b, nq, DIFF_WIDTH), _BF16),
        scratch_shapes=[stat, stat, pltpu.VMEM((2 * tq // MXU_WIDTH, LANES, MXU_WIDTH), _F32),
                        pltpu.VMEM((2, tk, 2 * tq), _F32), pltpu.VMEM((2, tk, 2 * tq), _F32),
                        pltpu.VMEM((tk, LANES), _BF16), pltpu.VMEM((tk, LANES), _F32),
                        pltpu.VMEM((2, 1, LANES), _F32)]
        + [pltpu.VMEM((2 * tq // LANES, tk, LANES), _F32)] * DEPTH_SLOTS
        + [pltpu.VMEM((2 * tq // MXU_WIDTH, tk, MXU_WIDTH), _BF16)] * DEPTH_SLOTS
        + [stat] * DEPTH_SLOTS,
        compiler_params=pltpu.CompilerParams(
            dimension_semantics=("arbitrary", "arbitrary", "arbitrary"),
            vmem_limit_bytes=VMEM_LIMIT_BYTES),
        name="diff_prompt",
    )(q, k, vt, lq1, lk1, lq2, lk2, subln_g)


def _encoder_layer(x, past, w, lam_init):
    b, n, _ = x.shape
    m = b * n
    h = _ffn(x.reshape(m, D_MODEL), w["ff1_pre_g"], w["ff1_w_gate"], w["ff1_w_up"], w["ff1_w_down"],
             w["ff1_post_g"])
    lam_args = (w["lam_q1"], w["lam_k1"], w["lam_q2"], w["lam_k2"], w["subln_g"])

    def seq(a):
        return a.reshape(b, n, a.shape[-1])

    if past is None:
        sbq, sbk, sbv, dq, dk, dv, sbk16, sbvt16, dk16, dvt16 = _qkv(h, w["mix_pre_g"], w["w_in"], seq_len=n)
        sb_o = _sb_prompt_call(seq(sbq), seq(sbk16), sbvt16, t=SB_PROMPT_TILE, group=SB_PROMPT_GROUP,
                               depth=SB_PROMPT_DEPTH)
        d_o = _diff_prompt_call(seq(dq), seq(dk16), dvt16, *lam_args, tq=DIFF_PROMPT_TILE, lam_init=lam_init)
        sbk, sbv = (a.reshape(b, SB_HEADS, SB_HEAD_DIM, n).transpose(0, 3, 1, 2) for a in (sbk, sbv))
    else:
        sbq, sbk, sbv, dq, dk, dv, sbk16, sbv16, dk16, dv16 = _qkv(h, w["mix_pre_g"], w["w_in"])
        sb_o = _sb_step_call(seq(sbq), seq(sbk16), seq(sbv16), past[0], past[1])
        d_o = _diff_step_call(seq(dq), seq(dk16), seq(dv16), past[2], past[3], *lam_args, lam_init=lam_init)
    y = _ffn(h, w["ff2_pre_g"], w["ff2_w_gate"], w["ff2_w_up"], w["ff2_w_down"], w["ff2_post_g"],
             mixer=(sb_o.reshape(m, SB_WIDTH), d_o.reshape(m, DIFF_WIDTH), w["w_out"], w["mix_post_g"]),
             final_g=w["final_g"])
    rows = (sbk.reshape(b, n, SB_HEADS, SB_HEAD_DIM), sbv.reshape(b, n, SB_HEADS, SB_HEAD_DIM),
            dk.reshape(b, n, DIFF_HEADS, 2 * DIFF_HEAD_DIM), dv.reshape(b, n, DIFF_HEADS, 2 * DIFF_HEAD_DIM))
    return y.reshape(b, n, D_MODEL), rows


_MATRICES = ("ff1_w_gate", "ff1_w_up", "ff1_w_down", "w_in", "w_out", "ff2_w_gate", "ff2_w_up", "ff2_w_down")


def kernel(x_prompt, x_sample, cache_sb_k, cache_sb_v, cache_diff_k, cache_diff_v, ff1_pre_g, ff1_w_gate, ff1_w_up, ff1_w_down, ff1_post_g, mix_pre_g, w_in, lam_q1, lam_k1, lam_q2, lam_k2, subln_g, w_out, mix_post_g, ff2_pre_g, ff2_w_gate, ff2_w_up, ff2_w_down, ff2_post_g, final_g):
    params = dict(ff1_pre_g=ff1_pre_g, ff1_w_gate=ff1_w_gate, ff1_w_up=ff1_w_up, ff1_w_down=ff1_w_down,
                  ff1_post_g=ff1_post_g, mix_pre_g=mix_pre_g, w_in=w_in, lam_q1=lam_q1, lam_k1=lam_k1,
                  lam_q2=lam_q2, lam_k2=lam_k2, subln_g=subln_g, w_out=w_out, mix_post_g=mix_post_g,
                  ff2_pre_g=ff2_pre_g, ff2_w_gate=ff2_w_gate, ff2_w_up=ff2_w_up, ff2_w_down=ff2_w_down,
                  ff2_post_g=ff2_post_g, final_g=final_g)
    yp, ys = x_prompt, x_sample
    rows_p, rows_s = [], []
    for l in range(DEPTH):
        lam_init = 0.8 - 0.6 * math.exp(-0.3 * l)
        w = {name: (p[l].astype(_BF16) if name in _MATRICES else p[l][None, :].astype(_F32))
             for name, p in params.items()}
        yp, rp = _encoder_layer(yp, None, w, lam_init)
        past = tuple([c[l].transpose(0, 2, 3, 1).reshape(c.shape[1], SB_WIDTH, c.shape[2])
                      for c in (cache_sb_k, cache_sb_v)]
                     + [c[l].reshape(c.shape[1], -1, c.shape[-1]) for c in (cache_diff_k, cache_diff_v)])
        ys, rs = _encoder_layer(ys, past, w, lam_init)
        rows_p.append(rp)
        rows_s.append(rs)
    stacked_p = [jnp.stack(r, axis=0) for r in zip(*rows_p)]
    stacked_s = [jnp.stack(r, axis=0) for r in zip(*rows_s)]
    return (yp, ys, *stacked_p, *stacked_s)
```

```python
import functools
import math

import jax
import jax.numpy as jnp
from jax import lax
from jax.experimental import pallas as pl
from jax.experimental.pallas import tpu as pltpu

D_MODEL = 1024
DEPTH = 1
CHUNK = 64
CHUNK_SHIFT = 6
SB_HEADS = 8
SB_HEAD_DIM = 64
DIFF_HEADS = 4
DIFF_HEAD_DIM = 64
SB_WIDTH = SB_HEADS * SB_HEAD_DIM
DIFF_WIDTH = DIFF_HEADS * 2 * DIFF_HEAD_DIM
MIX_WIDTH = SB_WIDTH + DIFF_WIDTH
IN_WIDTH = 3 * SB_WIDTH + 3 * DIFF_WIDTH
D_FF = 2816
RMS_EPS = 1e-6

LANES = 128
MXU_WIDTH = 256
VMEM_LIMIT_BYTES = 56 * 1024 * 1024
F32_EXP2_ZERO = 151.0
NEG_BIG = -1e30
LOG2E = math.log2(math.e)
DEPTH_SLOTS = 4
PAIRS_PER_TRIP = 2
SAFE_SPREAD = 90.0
TOKEN_TILE = 512
SB_STEP_TILE = 256
SB_PROMPT_TILE = 128
SB_PROMPT_GROUP = 8
SB_PROMPT_DEPTH = 3
DIFF_PROMPT_TILE = 512

_F32 = jnp.float32
_BF16 = jnp.bfloat16


def _rms(x, g):
    return x * lax.rsqrt(jnp.mean(x * x, axis=-1, keepdims=True) + RMS_EPS) * g


def _stick_logs(z):
    log_beta = jnp.minimum(z, 0.0) - jnp.log2(1.0 + jnp.exp2(-jnp.abs(z)))
    return log_beta, log_beta - z


def _split_hi_lo(x):
    hi = lax.bitcast_convert_type(lax.bitcast_convert_type(x, jnp.uint32) & jnp.uint32(0xFFFF0000), _F32)
    return hi.astype(_BF16), (x - hi).astype(_BF16)


def _dot(a, b):
    return jnp.dot(a, b, preferred_element_type=_F32)


def _dot_nt(a, b):
    return lax.dot_general(a, b, (((1,), (1,)), ((), ())), preferred_element_type=_F32)


def _ffn_kernel(*refs, mixer, final_norm):
    refs = list(refs)
    o_ref = refs.pop()
    x = refs.pop(0)[...]
    if mixer:
        sbo_ref, do_ref, wo_ref, mix_g_ref = refs[:4]
        refs = refs[4:]
        mix = _dot(sbo_ref[...], wo_ref[:SB_WIDTH, :]) + _dot(do_ref[...], wo_ref[SB_WIDTH:, :])
        x = x + _rms(mix, mix_g_ref[...])
    pre_ref, wg_ref, wu_ref, wd_ref, post_ref = refs[:5]
    xn = _rms(x, pre_ref[...]).astype(_BF16)
    g = _dot(xn, wg_ref[...])
    u = _dot(xn, wu_ref[...])
    a = (g * jax.nn.sigmoid(g) * u).astype(_BF16)
    h = x + 0.5 * _rms(_dot(a, wd_ref[...]), post_ref[...])
    if final_norm:
        h = _rms(h, refs[5][...])
    o_ref[...] = h


def _ffn(x, pre_g, wg, wu, wd, post_g, *, mixer=None, final_g=None):
    m = x.shape[0]
    tm = min(TOKEN_TILE, m)
    assert m % tm == 0

    def resident(shape):
        return pl.BlockSpec(shape, lambda i: (0, 0), pipeline_mode=pl.Buffered(1))

    row = pl.BlockSpec((tm, D_MODEL), lambda i: (i, 0))
    vec = resident((1, D_MODEL))
    in_specs, args = [row], [x]
    if mixer is not None:
        half = pl.BlockSpec((tm, SB_WIDTH), lambda i: (i, 0))
        in_specs += [half, half, resident((MIX_WIDTH, D_MODEL)), vec]
        args += list(mixer)
    in_specs += [vec, resident((D_MODEL, D_FF)), resident((D_MODEL, D_FF)), resident((D_FF, D_MODEL)), vec]
    args += [pre_g, wg, wu, wd, post_g]
    if final_g is not None:
        in_specs.append(vec)
        args.append(final_g)
    return pl.pallas_call(
        functools.partial(_ffn_kernel, mixer=mixer is not None, final_norm=final_g is not None),
        grid=(m // tm,),
        in_specs=in_specs,
        out_specs=row,
        out_shape=jax.ShapeDtypeStruct((m, D_MODEL), _F32),
        compiler_params=pltpu.CompilerParams(
            dimension_semantics=("parallel",), vmem_limit_bytes=VMEM_LIMIT_BYTES),
        name="ffn_mix_final" if mixer is not None else "ffn",
    )(*args)


def _qkv_kernel(h_ref, g_ref, w_ref, sbq_ref, sbk_ref, sbv_ref, dq_ref, dk_ref, dv_ref,
                sbk16_ref, sbv16_ref, dk16_ref, dv16_ref, *, transposed):
    hn = _rms(h_ref[...], g_ref[...]).astype(_BF16)

    def proj(idx):
        return _dot(hn, w_ref[:, idx * SB_WIDTH:(idx + 1) * SB_WIDTH])

    sbq_ref[...] = (proj(0) * (SB_HEAD_DIM ** -0.5 * LOG2E)).astype(_BF16)
    dq_ref[...] = (proj(3) * (DIFF_HEAD_DIM ** -0.5 * LOG2E)).astype(_BF16)

    def store_rows(ref, p):
        head_dim = ref.shape[1]
        heads = p.shape[1] // head_dim
        for head in range(heads):
            ref[pl.ds(head, p.shape[0], stride=heads), :] = p[:, head * head_dim:(head + 1) * head_dim]

    for idx, full_ref, half_ref in ((4, dk_ref, dk16_ref), (5, dv_ref, dv16_ref)):
        p = proj(idx)
        store_rows(full_ref, p)
        if transposed and half_ref is dv16_ref:
            half_ref[0] = p.T.astype(_BF16)
        else:
            half_ref[...] = p.astype(_BF16)
    for idx, full_ref, half_ref in ((1, sbk_ref, sbk16_ref), (2, sbv_ref, sbv16_ref)):
        p = proj(idx)
        if transposed:
            p_t = p.T
            full_ref[0] = p_t
            if half_ref is sbv16_ref:
                half_ref[0] = p_t.astype(_BF16)
            else:
                half_ref[...] = p.astype(_BF16)
        else:
            store_rows(full_ref, p)
            half_ref[...] = p.astype(_BF16)


def _qkv(h, g, w, *, seq_len=None):
    assert SB_WIDTH == DIFF_WIDTH
    m = h.shape[0]
    tm = min(TOKEN_TILE, m)
    assert m % tm == 0
    row = pl.BlockSpec((tm, D_MODEL), lambda i: (i, 0))
    out = pl.BlockSpec((tm, SB_WIDTH), lambda i: (i, 0))
    b16 = jax.ShapeDtypeStruct((m, SB_WIDTH), _BF16)

    def cache_rows(heads, head_dim):
        return (jax.ShapeDtypeStruct((m * heads, head_dim), _F32),
                pl.BlockSpec((tm * heads, head_dim), lambda i: (i, 0)))

    sb_rows, sb_out = cache_rows(SB_HEADS, SB_HEAD_DIM)
    d_rows, d_out = cache_rows(DIFF_HEADS, 2 * DIFF_HEAD_DIM)
    v_out, v16 = out, b16
    if seq_len is not None:
        assert seq_len % tm == 0
        tiles = seq_len // tm
        v_out = pl.BlockSpec((1, SB_WIDTH, tm), lambda i: (i // tiles, 0, i % tiles))
        v16 = jax.ShapeDtypeStruct((m // seq_len, SB_WIDTH, seq_len), _BF16)
        sb_rows, sb_out = jax.ShapeDtypeStruct(v16.shape, _F32), v_out
    return pl.pallas_call(
        functools.partial(_qkv_kernel, transposed=seq_len is not None),
        grid=(m // tm,),
        in_specs=[row, pl.BlockSpec((1, D_MODEL), lambda i: (0, 0)),
                  pl.BlockSpec((D_MODEL, IN_WIDTH), lambda i: (0, 0))],
        out_specs=[out, sb_out, sb_out, out, d_out, d_out, out, v_out, out, v_out],
        out_shape=[b16, sb_rows, sb_rows, b16, d_rows, d_rows, b16, v16, b16, v16],
        compiler_params=pltpu.CompilerParams(
            dimension_semantics=("parallel",), vmem_limit_bytes=VMEM_LIMIT_BYTES),
        name="qkv",
    )(h, g, w)


def _sb_step_kernel(q_ref, kn_ref, vn_ref, kc_ref, vc_ref, o_ref, acc_ref, c_ref, *, tk):
    n = q_ref.shape[1]
    past = kc_ref.shape[2]

    def later(t):
        return (lax.broadcasted_iota(jnp.int32, (t, t), 0)
                > lax.broadcasted_iota(jnp.int32, (t, t), 1)).astype(_BF16)

    later_new, later_tile = later(n), later(tk)
    newer = (lax.broadcasted_iota(jnp.int32, (n, n), 1) < lax.broadcasted_iota(jnp.int32, (n, n), 0))

    def walk(z, c, later_t, mask):
        log_beta, log_stay = _stick_logs(z)
        if mask is not None:
            log_stay = jnp.where(mask, log_stay, 0.0)
        sums = _dot(jnp.concatenate(_split_hi_lo(log_stay), axis=0), later_t)
        w = jnp.exp2(log_beta + (sums[:n] + sums[n:] + c))
        if mask is not None:
            w = jnp.where(mask, w, 0.0)
        return c + jnp.sum(log_stay, axis=-1, keepdims=True), w.astype(_BF16)

    def head_cols(head):
        return slice(head * SB_HEAD_DIM, (head + 1) * SB_HEAD_DIM)

    def walk_cache(start):
        for head in range(SB_HEADS):
            rows = head_cols(head)
            k_t = kc_ref[0, rows, pl.ds(start, tk)].astype(_BF16)
            v_t = vc_ref[0, rows, pl.ds(start, tk)].astype(_BF16)
            c_ref[head], w = walk(_dot(q_ref[0, :, rows], k_t), c_ref[head], later_tile, None)
            acc_ref[head] += _dot_nt(w, v_t)

    for head in range(SB_HEADS):
        cols = head_cols(head)
        c_ref[head], w = walk(_dot_nt(q_ref[0, :, cols], kn_ref[0, :, cols]), jnp.zeros((n, 1), _F32),
                              later_new, newer)
        acc_ref[head] = _dot(w, vn_ref[0, :, cols])
    walk_cache(past - tk)

    def cond(state):
        j, c_max = state
        return (j >= 0) & (c_max > -F32_EXP2_ZERO)

    def body(state):
        j, _ = state
        walk_cache(pl.multiple_of(j * tk, tk))
        return j - 1, jnp.max(c_ref[...])

    lax.while_loop(cond, body, (past // tk - 2, jnp.max(c_ref[...])))
    o_ref[0] = jnp.concatenate([acc_ref[head] for head in range(SB_HEADS)], axis=-1).astype(o_ref.dtype)


def _sb_step_call(q, k_new, v_new, k_cache, v_cache):
    b, n, _ = q.shape
    past = k_cache.shape[2]
    tk = min(SB_STEP_TILE, past)
    assert past % tk == 0
    new = pl.BlockSpec((1, n, SB_WIDTH), lambda i: (i, 0, 0))
    cache = pl.BlockSpec((1, SB_WIDTH, past), lambda i: (i, 0, 0))
    return pl.pallas_call(
        functools.partial(_sb_step_kernel, tk=tk),
        grid=(b,),
        in_specs=[new, new, new, cache, cache],
        out_specs=new,
        out_shape=jax.ShapeDtypeStruct((b, n, SB_WIDTH), _BF16),
        scratch_shapes=[pltpu.VMEM((SB_HEADS, n, SB_HEAD_DIM), _F32), pltpu.VMEM((SB_HEADS, n, 1), _F32)],
        compiler_params=pltpu.CompilerParams(
            dimension_semantics=("arbitrary",), vmem_limit_bytes=VMEM_LIMIT_BYTES),
        name="sb_step",
    )(q, k_new, v_new, k_cache, v_cache)


def _alibi_slope_log2(head):
    assert 8 % DIFF_HEADS == 0
    exponent = (8 // DIFF_HEADS) * (head + 1)
    slope = lax.bitcast_convert_type(jnp.full((1, 1), (127 - exponent) << 23, jnp.int32), _F32)
    return slope * LOG2E


def _diff_step_kernel(q_ref, kn_ref, vn_ref, kc_ref, vc_ref, lq1_ref, lk1_ref, lq2_ref, lk2_ref, g_ref,
                      o_ref, *, lam_init):
    n = q_ref.shape[1]
    past = kc_ref.shape[1] // DIFF_HEADS

    def cache_head(ref, head):
        return ref[0, pl.ds(head, past, stride=DIFF_HEADS), :].astype(_BF16)

    lane = lax.broadcasted_iota(jnp.int32, (1, LANES), 1)
    q_idx_c = lax.broadcasted_iota(jnp.int32, (2 * n, past), 0) & (n - 1)
    k_idx_c = lax.broadcasted_iota(jnp.int32, (2 * n, past), 1)
    q_idx_n = lax.broadcasted_iota(jnp.int32, (2 * n, n), 0) & (n - 1)
    k_idx_n = lax.broadcasted_iota(jnp.int32, (2 * n, n), 1)
    distance_c = (past + q_idx_c - k_idx_c).astype(_F32)
    distance_n = jnp.abs(q_idx_n - k_idx_n).astype(_F32)
    visible_n = (lax.shift_right_logical(past + k_idx_n, CHUNK_SHIFT)
                 <= lax.shift_right_logical(past + q_idx_n, CHUNK_SHIFT))
    lam = (jnp.exp(jnp.sum(lq1_ref[...] * lk1_ref[...], axis=-1, keepdims=True))
           - jnp.exp(jnp.sum(lq2_ref[...] * lk2_ref[...], axis=-1, keepdims=True)) + lam_init)
    for head in range(DIFF_HEADS):
        cols = slice(head * LANES, (head + 1) * LANES)
        slope = _alibi_slope_log2(head)
        q = q_ref[0, :, cols].astype(_F32)
        q_maps = jnp.concatenate([jnp.where(lane < DIFF_HEAD_DIM, q, 0.0),
                                  jnp.where(lane >= DIFF_HEAD_DIM, q, 0.0)], axis=0).astype(_BF16)
        s_c = _dot_nt(q_maps, cache_head(kc_ref, head)) - slope * distance_c
        s_n = jnp.where(visible_n, _dot_nt(q_maps, kn_ref[0, :, cols]) - slope * distance_n, -jnp.inf)
        m = jnp.maximum(jnp.max(s_c, axis=-1, keepdims=True), jnp.max(s_n, axis=-1, keepdims=True))
        p_c = jnp.exp2(s_c - m)
        p_n = jnp.exp2(s_n - m)
        l = jnp.sum(p_c, axis=-1, keepdims=True) + jnp.sum(p_n, axis=-1, keepdims=True)
        o = (_dot(p_c.astype(_BF16), cache_head(vc_ref, head))
             + _dot(p_n.astype(_BF16), vn_ref[0, :, cols])) / l
        o = o[:n] - lam * o[n:]
        o_ref[0, :, cols] = (_rms(o, g_ref[...]) * (1.0 - lam_init)).astype(o_ref.dtype)


def _diff_step_call(q, k_new, v_new, k_cache, v_cache, lq1, lk1, lq2, lk2, subln_g, *, lam_init):
    b, n, _ = q.shape
    rows = k_cache.shape[1]
    assert 2 * DIFF_HEAD_DIM == LANES and CHUNK == 1 << CHUNK_SHIFT and n & (n - 1) == 0
    new = pl.BlockSpec((1, n, DIFF_WIDTH), lambda i: (i, 0, 0))
    cache = pl.BlockSpec((1, rows, LANES), lambda i: (i, 0, 0))
    lam_spec = pl.BlockSpec((1, DIFF_HEAD_DIM), lambda i: (0, 0))
    return pl.pallas_call(
        functools.partial(_diff_step_kernel, lam_init=lam_init),
        grid=(b,),
        in_specs=[new, new, new, cache, cache, lam_spec, lam_spec, lam_spec, lam_spec,
                  pl.BlockSpec((1, LANES), lambda i: (0, 0))],
        out_specs=new,
        out_shape=jax.ShapeDtypeStruct((b, n, DIFF_WIDTH), _BF16),
        compiler_params=pltpu.CompilerParams(
            dimension_semantics=("arbitrary",), vmem_limit_bytes=VMEM_LIMIT_BYTES),
        name="diff_step",
    )(q, k_new, v_new, k_cache, v_cache, lq1, lk1, lq2, lk2, subln_g)


def _prompt_specs(nq, t):
    assert nq % t == 0 and t & (t - 1) == 0
    q_spec = pl.BlockSpec((1, t, LANES), lambda bi, g, qi: (bi, qi, g))
    k_spec = pl.BlockSpec((1, nq, LANES), lambda bi, g, qi: (bi, 0, g))
    vt_spec = pl.BlockSpec((1, LANES, nq), lambda bi, g, qi: (bi, g, 0))
    return q_spec, k_spec, vt_spec


def _sb_prompt_kernel(q_ref, k_ref, vt_ref, o_ref, acc_ref, c_ref, z_ref, lb_ref, hl_ref, w_ref, *,
                      t, group, depth):
    assert LANES == 2 * SB_HEAD_DIM
    strips = 2 * t // LANES
    qi = pl.program_id(2)
    lane = lax.broadcasted_iota(jnp.int32, (1, LANES), 1)
    key = lax.broadcasted_iota(jnp.int32, (t, LANES), 0)
    qry = lax.broadcasted_iota(jnp.int32, (t, LANES), 1)
    later = (lax.broadcasted_iota(jnp.int32, (t, t), 1)
             > lax.broadcasted_iota(jnp.int32, (t, t), 0)).astype(_BF16)

    def tile_queries(g):
        q = q_ref[0, g * t:(g + 1) * t, :].astype(_F32)
        return jnp.concatenate([jnp.where(lane < SB_HEAD_DIM, q, 0.0),
                                jnp.where(lane >= SB_HEAD_DIM, q, 0.0)], axis=0).astype(_BF16)

    q_heads = [tile_queries(g) for g in range(group)]

    def stage_scores(g, j, buf):
        z = _dot_nt(k_ref[0, pl.ds(pl.multiple_of(j * t, t), t), :], q_heads[g])
        for s in range(strips):
            z_ref[buf, s] = z[:, s * LANES:(s + 1) * LANES]

    def own_mask(s):
        return key < ((s * LANES + qry) & (t - 1))

    def stage_logs(c, own, buf):
        c_new = []
        for s in range(strips):
            lb_ref[buf, s], log_stay = _stick_logs(z_ref[buf, s])
            if own:
                log_stay = jnp.where(own_mask(s), log_stay, 0.0)
            hl_ref[buf, s] = jnp.concatenate(_split_hi_lo(log_stay), axis=1)
            c_new.append(c[:, s * LANES:(s + 1) * LANES] + jnp.sum(log_stay, axis=0, keepdims=True))
        return jnp.concatenate(c_new, axis=1)

    def stage_weights(c, own, buf):
        for s in range(strips):
            sums = _dot(later, hl_ref[buf, s])
            w = jnp.exp2(lb_ref[buf, s] + (sums[:, :LANES] + sums[:, LANES:] + c[:, s * LANES:(s + 1) * LANES]))
            if own:
                w = jnp.where(own_mask(s), w, 0.0)
            head, part = divmod(s, t // LANES)
            w_ref[buf, head, :, part * LANES:(part + 1) * LANES] = w.astype(_BF16)

    def stage_values(j, buf):
        vt = vt_ref[0, :, pl.ds(pl.multiple_of(j * t, t), t)]
        return jnp.concatenate([_dot(vt[:SB_HEAD_DIM], w_ref[buf, 0]), _dot(vt[SB_HEAD_DIM:], w_ref[buf, 1])],
                               axis=0)

    def walk_tile(g, j, c, own, buf):
        stage_scores(g, j, buf)
        c_new = stage_logs(c, own, buf)
        stage_weights(c, own, buf)
        return c_new, stage_values(j, buf)

    def first_tiles(exists):
        walks = [(g, qi * group + g - d, d == 0, depth * g + d)
                 for g in range(group) for d in range(depth) if exists(g, d)]
        for g, j, own, buf in walks:
            stage_scores(g, j, buf)
        c_in = {}
        for g, j, own, buf in walks:
            c_in[buf] = jnp.zeros((1, 2 * t), _F32) if own else c_ref[g]
            c_ref[g] = stage_logs(c_in[buf], own, buf)
        for g, j, own, buf in walks:
            stage_weights(c_in[buf], own, buf)
        for g, j, own, buf in walks:
            o = stage_values(j, buf)
            acc_ref[g] = o if own else acc_ref[g] + o

    assert group >= depth - 1

    @pl.when(qi == 0)
    def _():
        first_tiles(lambda g, d: g >= d)

    @pl.when(qi > 0)
    def _():
        first_tiles(lambda g, d: True)

    def older_tile(g, step):
        return qi * group + g - depth - step

    def any_active(step):
        flags = [(older_tile(g, step) >= 0) & (jnp.max(c_ref[g]) > -F32_EXP2_ZERO) for g in range(group)]
        return functools.reduce(jnp.logical_or, flags)

    def walk_older(state):
        step, _ = state
        for g in range(group):
            @pl.when((older_tile(g, step) >= 0) & (jnp.max(c_ref[g]) > -F32_EXP2_ZERO))
            def _(g=g):
                c, o = walk_tile(g, older_tile(g, step), c_ref[g], False, depth * g)
                c_ref[g] = c
                acc_ref[g] += o
        return step + 1, any_active(step + 1)

    lax.while_loop(lambda state: state[1], walk_older, (0, any_active(0)))
    for g in range(group):
        o_ref[0, g * t:(g + 1) * t, :] = acc_ref[g].T.astype(o_ref.dtype)


def _sb_prompt_call(q, k, vt, *, t, group, depth):
    b, nq, _ = q.shape
    q_spec, k_spec, vt_spec = _prompt_specs(nq, group * t)
    sets = depth * group
    strips = 2 * t // LANES
    return pl.pallas_call(
        functools.partial(_sb_prompt_kernel, t=t, group=group, depth=depth),
        grid=(b, SB_WIDTH // LANES, nq // (group * t)),
        in_specs=[q_spec, k_spec, vt_spec],
        out_specs=q_spec,
        out_shape=jax.ShapeDtypeStruct((b, nq, SB_WIDTH), _BF16),
        scratch_shapes=[pltpu.VMEM((group, LANES, t), _F32), pltpu.VMEM((group, 1, 2 * t), _F32),
                        pltpu.VMEM((sets, strips, t, LANES), _F32),
                        pltpu.VMEM((sets, strips, t, LANES), _F32),
                        pltpu.VMEM((sets, strips, t, 2 * LANES), _BF16),
                        pltpu.VMEM((sets, 2, t, t), _BF16)],
        compiler_params=pltpu.CompilerParams(
            dimension_semantics=("parallel", "parallel", "arbitrary"),
            vmem_limit_bytes=VMEM_LIMIT_BYTES),
        name="sb_prompt",
    )(q, k, vt)


def _diff_prompt_kernel(q_ref, k_ref, vt_ref, lq1_ref, lk1_ref, lq2_ref, lk2_ref, g_ref, o_ref,
                        m_ref, l_ref, acc_ref, pen_ref, pen2_ref, kpos_ref, kaug_ref, knorm_ref, *buffers,
                        tq, tk, lam_init):
    own = 2
    assert tq == own * tk
    head = pl.program_id(1)
    qi = pl.program_id(2)
    n_past = qi * own
    lane = lax.broadcasted_iota(jnp.int32, (1, LANES), 1)
    slope = _alibi_slope_log2(head)

    @pl.when(qi == 0)
    def _():
        key = lax.broadcasted_iota(jnp.int32, (tk, 2 * tq), 0)
        qry = lax.broadcasted_iota(jnp.int32, (tk, 2 * tq), 1) & (tq - 1)
        for d in range(own):
            k_pos = d * tk + key
            visible = lax.shift_right_logical(k_pos, CHUNK_SHIFT) <= lax.shift_right_logical(qry, CHUNK_SHIFT)
            bias = slope * (qry - key - jnp.abs(qry - k_pos)).astype(_F32)
            pen_ref[d] = jnp.where(visible, bias, -jnp.inf)
            ahead = -2.0 * slope * jnp.maximum(k_pos - qry, 0).astype(_F32)
            pen2_ref[d] = jnp.where(visible, ahead, -jnp.inf)
        row = lax.broadcasted_iota(jnp.int32, (tk, LANES), 0)
        col = lax.broadcasted_iota(jnp.int32, (tk, LANES), 1)
        kpos_ref[...] = jnp.where(col < 3, row, 0).astype(_F32).astype(_BF16)
        kaug_ref[...] = jnp.where(col < 3, row, jnp.where((col >= 6) & (col < 9), 1, 0)).astype(_F32)
        k_sq = jnp.square(k_ref[0].astype(_F32))
        for c in range(2):
            in_map = (lane >= c * DIFF_HEAD_DIM) & (lane < (c + 1) * DIFF_HEAD_DIM)
            norm_sq = jnp.sum(jnp.where(in_map, k_sq, 0.0), axis=1, keepdims=True)
            knorm_ref[c] = jnp.broadcast_to(jnp.sqrt(jnp.max(norm_sq, axis=0, keepdims=True)), (1, LANES))

    def split3(x):
        hi = x.astype(_BF16).astype(_F32)
        mid = (x - hi).astype(_BF16).astype(_F32)
        return [hi, mid, x - hi - mid]

    def augmented_queries(columns):
        blocks = []
        for c in range(2):
            aug = jnp.zeros((tq, LANES), _F32)
            for i, value in enumerate(columns[c]):
                aug = jnp.where(lane == i, value, aug)
            in_map = (lane >= c * DIFF_HEAD_DIM) & (lane < (c + 1) * DIFF_HEAD_DIM)
            blocks.append(jnp.concatenate([jnp.where(in_map, q, 0.0), aug], axis=1))
        return jnp.concatenate(blocks, axis=0).astype(_BF16)

    q = q_ref[0].astype(_F32)
    online_queries = []
    l_ref[...] = jnp.zeros_like(l_ref)
    acc_ref[...] = jnp.zeros_like(acc_ref)

    def slot_tile(n):
        return jnp.where(n < own, n_past + n, n_past - 1 - (n - own))

    def slot_shift(n):
        j = slot_tile(n)
        distance = (qi * tq - j * tk).astype(_F32)
        return jnp.where(n < own, 0.0, jnp.where(j >= 0, -slope * distance, NEG_BIG))

    def tile_start(n):
        return pl.multiple_of(jnp.clip(slot_tile(n), 0, n_past + own - 1) * tk, tk)

    def scores(n):
        keys = jnp.concatenate([k_ref[0, pl.ds(tile_start(n), tk), :], kpos_ref[...]], axis=1)
        return _dot_nt(keys, online_queries[0])

    s_refs, p_refs, a_refs = (buffers[i * DEPTH_SLOTS:(i + 1) * DEPTH_SLOTS] for i in range(3))

    def put_scores(buf, s):
        for c in range(2 * tq // LANES):
            s_refs[buf][c] = s[:, c * LANES:(c + 1) * LANES]

    def softmax(buf, shift):
        for c in range(2 * tq // LANES):
            strip = slice(c * LANES, (c + 1) * LANES)
            s = s_refs[buf][c]
            m_old = m_ref[:, strip]
            m_new = jnp.maximum(m_old, jnp.max(s, axis=0, keepdims=True) + shift)
            alpha = jnp.exp2(m_old - m_new)
            p = jnp.exp2(s - (m_new - shift))
            l_ref[:, strip] = alpha * l_ref[:, strip] + jnp.sum(p, axis=0, keepdims=True)
            m_ref[:, strip] = m_new
            a_refs[buf][:, strip] = alpha
            half = c % (MXU_WIDTH // LANES)
            p_refs[buf][c // (MXU_WIDTH // LANES), :, half * LANES:(half + 1) * LANES] = p.astype(_BF16)

    def weigh(n, buf):
        values = vt_ref[0, :, pl.ds(tile_start(n), tk)]
        for c in range(2 * tq // MXU_WIDTH):
            strip = slice(c * MXU_WIDTH, (c + 1) * MXU_WIDTH)
            acc_ref[c] = a_refs[buf][:, strip] * acc_ref[c] + _dot(values, p_refs[buf][c])

    def stage_pair(n, buf):
        for d in range(2):
            put_scores((buf + 2 + d) % DEPTH_SLOTS, scores(n + 2 + d))
        for d in range(2):
            softmax(buf + d, slot_shift(n + d))
        for d in range(2):
            weigh(n - 2 + d, (buf + 2 + d) % DEPTH_SLOTS)

    def pair_buffer(pair):
        return (2 * pair) % DEPTH_SLOTS

    def loop_trip(i, carry):
        for pair in range(PAIRS_PER_TRIP):
            stage_pair(2 * (PAIRS_PER_TRIP * i + pair), pair_buffer(pair))
        return carry

    q_sq = jnp.square(q)
    reach = jnp.zeros((1, 1), _F32)
    bounds = []
    for c in range(2):
        in_map = (lane >= c * DIFF_HEAD_DIM) & (lane < (c + 1) * DIFF_HEAD_DIM)
        q_norm = jnp.sqrt(jnp.max(jnp.sum(jnp.where(in_map, q_sq, 0.0), axis=1, keepdims=True),
                                  axis=0, keepdims=True))
        bounds.append(q_norm * knorm_ref[c][:, :1])
        reach = jnp.maximum(reach, (2.0 * bounds[c] + F32_EXP2_ZERO) / slope)
    tiles_in_reach = jnp.minimum(jnp.floor((reach - 1.0) / tk) + 1.0, float(1 << 20))
    n_walk = jnp.minimum(jnp.max(tiles_in_reach).astype(jnp.int32), n_past)
    pairs = (own + n_walk + 1) // 2

    def online_walk():
        online_queries.append(augmented_queries([split3(slope)] * 2))
        m_ref[...] = jnp.full_like(m_ref, NEG_BIG)
        for n in range(own):
            put_scores(n, scores(n) + pen_ref[n])
        for n in range(own, DEPTH_SLOTS):
            p_refs[n][...] = jnp.zeros_like(p_refs[n])
            a_refs[n][...] = jnp.ones_like(a_refs[n])
        trips = pairs // PAIRS_PER_TRIP
        lax.fori_loop(0, trips, loop_trip, 0)

        for pair in range(PAIRS_PER_TRIP - 1):
            @pl.when(pairs % PAIRS_PER_TRIP > pair)
            def _(pair=pair):
                stage_pair(2 * (PAIRS_PER_TRIP * trips + pair), pair_buffer(pair))

        for parity in range(2):
            @pl.when(pairs % 2 == parity)
            def _(parity=parity):
                for d in range(2):
                    weigh(2 * pairs - 2 + d, pair_buffer(parity + 1) + d)

    def fixed_walk():
        q_pos = (qi * tq + lax.broadcasted_iota(jnp.int32, (tq, 1), 0)).astype(_F32)
        q_aug = augmented_queries([split3(slope) + split3(slope * tk) + split3(-slope * q_pos - bounds[c])
                                   for c in range(2)])

        def tile_keys(n):
            j = jnp.where(n < own + n_walk, slot_tile(n), -(1 << 20)).astype(_F32)
            aug = jnp.where((lane >= 3) & (lane < 6), j, kaug_ref[...]).astype(_BF16)
            return jnp.concatenate([k_ref[0, pl.ds(tile_start(n), tk), :], aug], axis=1)

        def weights(n, buf, own_slot):
            keys = tile_keys(n)
            for c in range(2 * tq // MXU_WIDTH):
                strip = slice(c * MXU_WIDTH, (c + 1) * MXU_WIDTH)
                e = _dot_nt(keys, q_aug[strip])
                if own_slot:
                    e = e + pen2_ref[n, :, strip]
                p = jnp.exp2(e)
                l_ref[:, strip] += jnp.sum(p, axis=0, keepdims=True)
                p_refs[buf][c] = p.astype(_BF16)

        def values(n, buf):
            vt = vt_ref[0, :, pl.ds(tile_start(n), tk)]
            for c in range(2 * tq // MXU_WIDTH):
                acc_ref[c] += _dot(vt, p_refs[buf][c])

        def pair_step(t, buf):
            for d in range(2):
                weights(2 * t + 2 + d, (buf + 2 + d) % DEPTH_SLOTS, False)
            for d in range(2):
                values(2 * t + d, buf + d)

        def trip(i, carry):
            for pair in range(PAIRS_PER_TRIP):
                pair_step(PAIRS_PER_TRIP * i + pair, pair_buffer(pair))
            return carry

        for n in range(own):
            weights(n, n, True)
        steps = pairs - 1
        trips = steps // PAIRS_PER_TRIP
        lax.fori_loop(0, trips, trip, 0)
        for pair in range(PAIRS_PER_TRIP - 1):
            @pl.when(steps % PAIRS_PER_TRIP > pair)
            def _(pair=pair):
                pair_step(PAIRS_PER_TRIP * trips + pair, pair_buffer(pair))

        for parity in range(2):
            @pl.when(pairs % 2 == parity)
            def _(parity=parity):
                for d in range(2):
                    values(2 * pairs - 2 + d, pair_buffer(parity + 1) + d)

    spread = 2.0 * jnp.max(jnp.maximum(bounds[0], bounds[1]))
    pl.when(spread < SAFE_SPREAD)(fixed_walk)
    pl.when(spread >= SAFE_SPREAD)(online_walk)

    lam = (jnp.exp(jnp.sum(lq1_ref[...] * lk1_ref[...], axis=-1, keepdims=True))
           - jnp.exp(jnp.sum(lq2_ref[...] * lk2_ref[...], axis=-1, keepdims=True)) + lam_init)
    o = jnp.concatenate([acc_ref[c] for c in range(2 * tq // MXU_WIDTH)], axis=1) / l_ref[...]
    o = o[:, :tq] - lam * o[:, tq:]
    o = o * lax.rsqrt(jnp.mean(o * o, axis=0, keepdims=True) + RMS_EPS)
    o_ref[0] = (o.T * g_ref[...] * (1.0 - lam_init)).astype(o_ref.dtype)


def _diff_prompt_call(q, k, vt, lq1, lk1, lq2, lk2, subln_g, *, tq, lam_init):
    b, nq, _ = q.shape
    tk = tq // 2
    assert 2 * DIFF_HEAD_DIM == LANES
    assert tk % CHUNK == 0 and CHUNK == 1 << CHUNK_SHIFT and tk <= 256
    q_spec, k_spec, vt_spec = _prompt_specs(nq, tq)
    lam_spec = pl.BlockSpec((1, DIFF_HEAD_DIM), lambda bi, g, qi: (0, 0))
    stat = pltpu.VMEM((1, 2 * tq), _F32)
    return pl.pallas_call(
        functools.partial(_diff_prompt_kernel, tq=tq, tk=tk, lam_init=lam_init),
        grid=(b, DIFF_HEADS, nq // tq),
        in_specs=[q_spec, k_spec, vt_spec, lam_spec, lam_spec, lam_spec, lam_spec,
                  pl.BlockSpec((1, LANES), lambda bi, g, qi: (0, 0))],
        out_specs=q_spec,
        out_shape=jax.ShapeDtypeStruct((b, nq, DIFF_WIDTH), _BF16),
        scratch_shapes=[stat, stat, pltpu.VMEM((2 * tq // MXU_WIDTH, LANES, MXU_WIDTH), _F32),
                        pltpu.VMEM((2, tk, 2 * tq), _F32), pltpu.VMEM((2, tk, 2 * tq), _F32),
                        pltpu.VMEM((tk, LANES), _BF16), pltpu.VMEM((tk, LANES), _F32),
                        pltpu.VMEM((2, 1, LANES), _F32)]
        + [pltpu.VMEM((2 * tq // LANES, tk, LANES), _F32)] * DEPTH_SLOTS
        + [pltpu.VMEM((2 * tq // MXU_WIDTH, tk, MXU_WIDTH), _BF16)] * DEPTH_SLOTS
        + [stat] * DEPTH_SLOTS,
        compiler_params=pltpu.CompilerParams(
            dimension_semantics=("arbitrary", "arbitrary", "arbitrary"),
            vmem_limit_bytes=VMEM_LIMIT_BYTES),
        name="diff_prompt",
    )(q, k, vt, lq1, lk1, lq2, lk2, subln_g)


def _encoder_layer(x, past, w, lam_init):
    b, n, _ = x.shape
    m = b * n
    h = _ffn(x.reshape(m, D_MODEL), w["ff1_pre_g"], w["ff1_w_gate"], w["ff1_w_up"], w["ff1_w_down"],
             w["ff1_post_g"])
    lam_args = (w["lam_q1"], w["lam_k1"], w["lam_q2"], w["lam_k2"], w["subln_g"])

    def seq(a):
        return a.reshape(b, n, a.shape[-1])

    if past is None:
        sbq, sbk, sbv, dq, dk, dv, sbk16, sbvt16, dk16, dvt16 = _qkv(h, w["mix_pre_g"], w["w_in"], seq_len=n)
        sb_o = _sb_prompt_call(seq(sbq), seq(sbk16), sbvt16, t=SB_PROMPT_TILE, group=SB_PROMPT_GROUP,
                               depth=SB_PROMPT_DEPTH)
        d_o = _diff_prompt_call(seq(dq), seq(dk16), dvt16, *lam_args, tq=DIFF_PROMPT_TILE, lam_init=lam_init)
        sbk, sbv = (a.reshape(b, SB_HEADS, SB_HEAD_DIM, n).transpose(0, 3, 1, 2) for a in (sbk, sbv))
    else:
        sbq, sbk, sbv, dq, dk, dv, sbk16, sbv16, dk16, dv16 = _qkv(h, w["mix_pre_g"], w["w_in"])
        sb_o = _sb_step_call(seq(sbq), seq(sbk16), seq(sbv16), past[0], past[1])
        d_o = _diff_step_call(seq(dq), seq(dk16), seq(dv16), past[2], past[3], *lam_args, lam_init=lam_init)
    y = _ffn(h, w["ff2_pre_g"], w["ff2_w_gate"], w["ff2_w_up"], w["ff2_w_down"], w["ff2_post_g"],
             mixer=(sb_o.reshape(m, SB_WIDTH), d_o.reshape(m, DIFF_WIDTH), w["w_out"], w["mix_post_g"]),
             final_g=w["final_g"])
    rows = (sbk.reshape(b, n, SB_HEADS, SB_HEAD_DIM), sbv.reshape(b, n, SB_HEADS, SB_HEAD_DIM),
            dk.reshape(b, n, DIFF_HEADS, 2 * DIFF_HEAD_DIM), dv.reshape(b, n, DIFF_HEADS, 2 * DIFF_HEAD_DIM))
    return y.reshape(b, n, D_MODEL), rows


_MATRICES = ("ff1_w_gate", "ff1_w_up", "ff1_w_down", "w_in", "w_out", "ff2_w_gate", "ff2_w_up", "ff2_w_down")


def kernel(x_prompt, x_sample, cache_sb_k, cache_sb_v, cache_diff_k, cache_diff_v, ff1_pre_g, ff1_w_gate, ff1_w_up, ff1_w_down, ff1_post_g, mix_pre_g, w_in, lam_q1, lam_k1, lam_q2, lam_k2, subln_g, w_out, mix_post_g, ff2_pre_g, ff2_w_gate, ff2_w_up, ff2_w_down, ff2_post_g, final_g):
    params = dict(ff1_pre_g=ff1_pre_g, ff1_w_gate=ff1_w_gate, ff1_w_up=ff1_w_up, ff1_w_down=ff1_w_down,
                  ff1_post_g=ff1_post_g, mix_pre_g=mix_pre_g, w_in=w_in, lam_q1=lam_q1, lam_k1=lam_k1,
                  lam_q2=lam_q2, lam_k2=lam_k2, subln_g=subln_g, w_out=w_out, mix_post_g=mix_post_g,
                  ff2_pre_g=ff2_pre_g, ff2_w_gate=ff2_w_gate, ff2_w_up=ff2_w_up, ff2_w_down=ff2_w_down,
                  ff2_post_g=ff2_post_g, final_g=final_g)
    yp, ys = x_prompt, x_sample
    rows_p, rows_s = [], []
    for l in range(DEPTH):
        lam_init = 0.8 - 0.6 * math.exp(-0.3 * l)
        w = {name: (p[l].astype(_BF16) if name in _MATRICES else p[l][None, :].astype(_F32))
             for name, p in params.items()}
        yp, rp = _encoder_layer(yp, None, w, lam_init)
        past = tuple([c[l].transpose(0, 2, 3, 1).reshape(c.shape[1], SB_WIDTH, c.shape[2])
                      for c in (cache_sb_k, cache_sb_v)]
                     + [c[l].reshape(c.shape[1], -1, c.shape[-1]) for c in (cache_diff_k, cache_diff_v)])
        ys, rs = _encoder_layer(ys, past, w, lam_init)
        rows_p.append(rp)
        rows_s.append(rs)
    stacked_p = [jnp.stack(r, axis=0) for r in zip(*rows_p)]
    stacked_s = [jnp.stack(r, axis=0) for r in zip(*rows_s)]
    return (yp, ys, *stacked_p, *stacked_s)
```

```python
import functools
import math

import jax
import jax.numpy as jnp
from jax import lax
from jax.experimental import pallas as pl
from jax.experimental.pallas import tpu as pltpu

D_MODEL = 1024
DEPTH = 1
CHUNK = 64
CHUNK_SHIFT = 6
SB_HEADS = 8
SB_HEAD_DIM = 64
DIFF_HEADS = 4
DIFF_HEAD_DIM = 64
SB_WIDTH = SB_HEADS * SB_HEAD_DIM
DIFF_WIDTH = DIFF_HEADS * 2 * DIFF_HEAD_DIM
MIX_WIDTH = SB_WIDTH + DIFF_WIDTH
IN_WIDTH = 3 * SB_WIDTH + 3 * DIFF_WIDTH
D_FF = 2816
RMS_EPS = 1e-6

LANES = 128
MXU_WIDTH = 256
VMEM_LIMIT_BYTES = 56 * 1024 * 1024
F32_EXP2_ZERO = 151.0
NEG_BIG = -1e30
LOG2E = math.log2(math.e)
DEPTH_SLOTS = 4
PAIRS_PER_TRIP = 2
SAFE_SPREAD = 90.0
TOKEN_TILE = 512
SB_STEP_TILE = 256
SB_PROMPT_TILE = 128
SB_PROMPT_GROUP = 8
SB_PROMPT_DEPTH = 3
DIFF_PROMPT_TILE = 1024
DIFF_KEY_TILE = 256

_F32 = jnp.float32
_BF16 = jnp.bfloat16


def _rms(x, g):
    return x * lax.rsqrt(jnp.mean(x * x, axis=-1, keepdims=True) + RMS_EPS) * g


def _stick_logs(z):
    log_beta = jnp.minimum(z, 0.0) - jnp.log2(1.0 + jnp.exp2(-jnp.abs(z)))
    return log_beta, log_beta - z


def _split_hi_lo(x):
    hi = lax.bitcast_convert_type(lax.bitcast_convert_type(x, jnp.uint32) & jnp.uint32(0xFFFF0000), _F32)
    return hi.astype(_BF16), (x - hi).astype(_BF16)


def _dot(a, b):
    return jnp.dot(a, b, preferred_element_type=_F32)


def _dot_nt(a, b):
    return lax.dot_general(a, b, (((1,), (1,)), ((), ())), preferred_element_type=_F32)


def _ffn_kernel(*refs, mixer, final_norm):
    refs = list(refs)
    o_ref = refs.pop()
    x = refs.pop(0)[...]
    if mixer:
        sbo_ref, do_ref, wo_ref, mix_g_ref = refs[:4]
        refs = refs[4:]
        mix = _dot(sbo_ref[...], wo_ref[:SB_WIDTH, :]) + _dot(do_ref[...], wo_ref[SB_WIDTH:, :])
        x = x + _rms(mix, mix_g_ref[...])
    pre_ref, wg_ref, wu_ref, wd_ref, post_ref = refs[:5]
    xn = _rms(x, pre_ref[...]).astype(_BF16)
    g = _dot(xn, wg_ref[...])
    u = _dot(xn, wu_ref[...])
    a = (g * jax.nn.sigmoid(g) * u).astype(_BF16)
    h = x + 0.5 * _rms(_dot(a, wd_ref[...]), post_ref[...])
    if final_norm:
        h = _rms(h, refs[5][...])
    o_ref[...] = h


def _ffn(x, pre_g, wg, wu, wd, post_g, *, mixer=None, final_g=None):
    m = x.shape[0]
    tm = min(TOKEN_TILE, m)
    assert m % tm == 0

    def resident(shape):
        return pl.BlockSpec(shape, lambda i: (0, 0), pipeline_mode=pl.Buffered(1))

    row = pl.BlockSpec((tm, D_MODEL), lambda i: (i, 0))
    vec = resident((1, D_MODEL))
    in_specs, args = [row], [x]
    if mixer is not None:
        half = pl.BlockSpec((tm, SB_WIDTH), lambda i: (i, 0))
        in_specs += [half, half, resident((MIX_WIDTH, D_MODEL)), vec]
        args += list(mixer)
    in_specs += [vec, resident((D_MODEL, D_FF)), resident((D_MODEL, D_FF)), resident((D_FF, D_MODEL)), vec]
    args += [pre_g, wg, wu, wd, post_g]
    if final_g is not None:
        in_specs.append(vec)
        args.append(final_g)
    return pl.pallas_call(
        functools.partial(_ffn_kernel, mixer=mixer is not None, final_norm=final_g is not None),
        grid=(m // tm,),
        in_specs=in_specs,
        out_specs=row,
        out_shape=jax.ShapeDtypeStruct((m, D_MODEL), _F32),
        compiler_params=pltpu.CompilerParams(
            dimension_semantics=("parallel",), vmem_limit_bytes=VMEM_LIMIT_BYTES),
        name="ffn_mix_final" if mixer is not None else "ffn",
    )(*args)


def _qkv_kernel(h_ref, g_ref, w_ref, sbq_ref, sbk_ref, sbv_ref, dq_ref, dk_ref, dv_ref,
                sbk16_ref, sbv16_ref, dk16_ref, dv16_ref, *, transposed):
    hn = _rms(h_ref[...], g_ref[...]).astype(_BF16)

    def proj(idx):
        return _dot(hn, w_ref[:, idx * SB_WIDTH:(idx + 1) * SB_WIDTH])

    sbq_ref[...] = (proj(0) * (SB_HEAD_DIM ** -0.5 * LOG2E)).astype(_BF16)
    dq_ref[...] = (proj(3) * (DIFF_HEAD_DIM ** -0.5 * LOG2E)).astype(_BF16)

    def store_rows(ref, p):
        head_dim = ref.shape[1]
        heads = p.shape[1] // head_dim
        for head in range(heads):
            ref[pl.ds(head, p.shape[0], stride=heads), :] = p[:, head * head_dim:(head + 1) * head_dim]

    for idx, full_ref, half_ref in ((4, dk_ref, dk16_ref), (5, dv_ref, dv16_ref)):
        p = proj(idx)
        store_rows(full_ref, p)
        if transposed and half_ref is dv16_ref:
            half_ref[0] = p.T.astype(_BF16)
        else:
            half_ref[...] = p.astype(_BF16)
    for idx, full_ref, half_ref in ((1, sbk_ref, sbk16_ref), (2, sbv_ref, sbv16_ref)):
        p = proj(idx)
        if transposed:
            p_t = p.T
            full_ref[0] = p_t
            if half_ref is sbv16_ref:
                half_ref[0] = p_t.astype(_BF16)
            else:
                half_ref[...] = p.astype(_BF16)
        else:
            store_rows(full_ref, p)
            half_ref[...] = p.astype(_BF16)


def _qkv(h, g, w, *, seq_len=None):
    assert SB_WIDTH == DIFF_WIDTH
    m = h.shape[0]
    tm = min(TOKEN_TILE, m)
    assert m % tm == 0
    row = pl.BlockSpec((tm, D_MODEL), lambda i: (i, 0))
    out = pl.BlockSpec((tm, SB_WIDTH), lambda i: (i, 0))
    b16 = jax.ShapeDtypeStruct((m, SB_WIDTH), _BF16)

    def cache_rows(heads, head_dim):
        return (jax.ShapeDtypeStruct((m * heads, head_dim), _F32),
                pl.BlockSpec((tm * heads, head_dim), lambda i: (i, 0)))

    sb_rows, sb_out = cache_rows(SB_HEADS, SB_HEAD_DIM)
    d_rows, d_out = cache_rows(DIFF_HEADS, 2 * DIFF_HEAD_DIM)
    v_out, v16 = out, b16
    if seq_len is not None:
        assert seq_len % tm == 0
        tiles = seq_len // tm
        v_out = pl.BlockSpec((1, SB_WIDTH, tm), lambda i: (i // tiles, 0, i % tiles))
        v16 = jax.ShapeDtypeStruct((m // seq_len, SB_WIDTH, seq_len), _BF16)
        sb_rows, sb_out = jax.ShapeDtypeStruct(v16.shape, _F32), v_out
    return pl.pallas_call(
        functools.partial(_qkv_kernel, transposed=seq_len is not None),
        grid=(m // tm,),
        in_specs=[row, pl.BlockSpec((1, D_MODEL), lambda i: (0, 0)),
                  pl.BlockSpec((D_MODEL, IN_WIDTH), lambda i: (0, 0))],
        out_specs=[out, sb_out, sb_out, out, d_out, d_out, out, v_out, out, v_out],
        out_shape=[b16, sb_rows, sb_rows, b16, d_rows, d_rows, b16, v16, b16, v16],
        compiler_params=pltpu.CompilerParams(
            dimension_semantics=("parallel",), vmem_limit_bytes=VMEM_LIMIT_BYTES),
        name="qkv",
    )(h, g, w)


def _sb_step_kernel(q_ref, kn_ref, vn_ref, kc_ref, vc_ref, o_ref, acc_ref, c_ref, *, tk):
    n = q_ref.shape[1]
    past = kc_ref.shape[2]

    def later(t):
        return (lax.broadcasted_iota(jnp.int32, (t, t), 0)
                > lax.broadcasted_iota(jnp.int32, (t, t), 1)).astype(_BF16)

    later_new, later_tile = later(n), later(tk)
    newer = (lax.broadcasted_iota(jnp.int32, (n, n), 1) < lax.broadcasted_iota(jnp.int32, (n, n), 0))

    def walk(z, c, later_t, mask):
        log_beta, log_stay = _stick_logs(z)
        if mask is not None:
            log_stay = jnp.where(mask, log_stay, 0.0)
        sums = _dot(jnp.concatenate(_split_hi_lo(log_stay), axis=0), later_t)
        w = jnp.exp2(log_beta + (sums[:n] + sums[n:] + c))
        if mask is not None:
            w = jnp.where(mask, w, 0.0)
        return c + jnp.sum(log_stay, axis=-1, keepdims=True), w.astype(_BF16)

    def head_cols(head):
        return slice(head * SB_HEAD_DIM, (head + 1) * SB_HEAD_DIM)

    def walk_cache(start):
        for head in range(SB_HEADS):
            rows = head_cols(head)
            k_t = kc_ref[0, rows, pl.ds(start, tk)].astype(_BF16)
            v_t = vc_ref[0, rows, pl.ds(start, tk)].astype(_BF16)
            c_ref[head], w = walk(_dot(q_ref[0, :, rows], k_t), c_ref[head], later_tile, None)
            acc_ref[head] += _dot_nt(w, v_t)

    for head in range(SB_HEADS):
        cols = head_cols(head)
        c_ref[head], w = walk(_dot_nt(q_ref[0, :, cols], kn_ref[0, :, cols]), jnp.zeros((n, 1), _F32),
                              later_new, newer)
        acc_ref[head] = _dot(w, vn_ref[0, :, cols])
    walk_cache(past - tk)

    def cond(state):
        j, c_max = state
        return (j >= 0) & (c_max > -F32_EXP2_ZERO)

    def body(state):
        j, _ = state
        walk_cache(pl.multiple_of(j * tk, tk))
        return j - 1, jnp.max(c_ref[...])

    lax.while_loop(cond, body, (past // tk - 2, jnp.max(c_ref[...])))
    o_ref[0] = jnp.concatenate([acc_ref[head] for head in range(SB_HEADS)], axis=-1).astype(o_ref.dtype)


def _sb_step_call(q, k_new, v_new, k_cache, v_cache):
    b, n, _ = q.shape
    past = k_cache.shape[2]
    tk = min(SB_STEP_TILE, past)
    assert past % tk == 0
    new = pl.BlockSpec((1, n, SB_WIDTH), lambda i: (i, 0, 0))
    cache = pl.BlockSpec((1, SB_WIDTH, past), lambda i: (i, 0, 0))
    return pl.pallas_call(
        functools.partial(_sb_step_kernel, tk=tk),
        grid=(b,),
        in_specs=[new, new, new, cache, cache],
        out_specs=new,
        out_shape=jax.ShapeDtypeStruct((b, n, SB_WIDTH), _BF16),
        scratch_shapes=[pltpu.VMEM((SB_HEADS, n, SB_HEAD_DIM), _F32), pltpu.VMEM((SB_HEADS, n, 1), _F32)],
        compiler_params=pltpu.CompilerParams(
            dimension_semantics=("arbitrary",), vmem_limit_bytes=VMEM_LIMIT_BYTES),
        name="sb_step",
    )(q, k_new, v_new, k_cache, v_cache)


def _alibi_slope_log2(head):
    assert 8 % DIFF_HEADS == 0
    exponent = (8 // DIFF_HEADS) * (head + 1)
    slope = lax.bitcast_convert_type(jnp.full((1, 1), (127 - exponent) << 23, jnp.int32), _F32)
    return slope * LOG2E


def _diff_step_kernel(q_ref, kn_ref, vn_ref, kc_ref, vc_ref, lq1_ref, lk1_ref, lq2_ref, lk2_ref, g_ref,
                      o_ref, *, lam_init):
    n = q_ref.shape[1]
    past = kc_ref.shape[1] // DIFF_HEADS

    def cache_head(ref, head):
        return ref[0, pl.ds(head, past, stride=DIFF_HEADS), :].astype(_BF16)

    lane = lax.broadcasted_iota(jnp.int32, (1, LANES), 1)
    q_idx_c = lax.broadcasted_iota(jnp.int32, (2 * n, past), 0) & (n - 1)
    k_idx_c = lax.broadcasted_iota(jnp.int32, (2 * n, past), 1)
    q_idx_n = lax.broadcasted_iota(jnp.int32, (2 * n, n), 0) & (n - 1)
    k_idx_n = lax.broadcasted_iota(jnp.int32, (2 * n, n), 1)
    distance_c = (past + q_idx_c - k_idx_c).astype(_F32)
    distance_n = jnp.abs(q_idx_n - k_idx_n).astype(_F32)
    visible_n = (lax.shift_right_logical(past + k_idx_n, CHUNK_SHIFT)
                 <= lax.shift_right_logical(past + q_idx_n, CHUNK_SHIFT))
    lam = (jnp.exp(jnp.sum(lq1_ref[...] * lk1_ref[...], axis=-1, keepdims=True))
           - jnp.exp(jnp.sum(lq2_ref[...] * lk2_ref[...], axis=-1, keepdims=True)) + lam_init)
    for head in range(DIFF_HEADS):
        cols = slice(head * LANES, (head + 1) * LANES)
        slope = _alibi_slope_log2(head)
        q = q_ref[0, :, cols].astype(_F32)
        q_maps = jnp.concatenate([jnp.where(lane < DIFF_HEAD_DIM, q, 0.0),
                                  jnp.where(lane >= DIFF_HEAD_DIM, q, 0.0)], axis=0).astype(_BF16)
        s_c = _dot_nt(q_maps, cache_head(kc_ref, head)) - slope * distance_c
        s_n = jnp.where(visible_n, _dot_nt(q_maps, kn_ref[0, :, cols]) - slope * distance_n, -jnp.inf)
        m = jnp.maximum(jnp.max(s_c, axis=-1, keepdims=True), jnp.max(s_n, axis=-1, keepdims=True))
        p_c = jnp.exp2(s_c - m)
        p_n = jnp.exp2(s_n - m)
        l = jnp.sum(p_c, axis=-1, keepdims=True) + jnp.sum(p_n, axis=-1, keepdims=True)
        o = (_dot(p_c.astype(_BF16), cache_head(vc_ref, head))
             + _dot(p_n.astype(_BF16), vn_ref[0, :, cols])) / l
        o = o[:n] - lam * o[n:]
        o_ref[0, :, cols] = (_rms(o, g_ref[...]) * (1.0 - lam_init)).astype(o_ref.dtype)


def _diff_step_call(q, k_new, v_new, k_cache, v_cache, lq1, lk1, lq2, lk2, subln_g, *, lam_init):
    b, n, _ = q.shape
    rows = k_cache.shape[1]
    assert 2 * DIFF_HEAD_DIM == LANES and CHUNK == 1 << CHUNK_SHIFT and n & (n - 1) == 0
    new = pl.BlockSpec((1, n, DIFF_WIDTH), lambda i: (i, 0, 0))
    cache = pl.BlockSpec((1, rows, LANES), lambda i: (i, 0, 0))
    lam_spec = pl.BlockSpec((1, DIFF_HEAD_DIM), lambda i: (0, 0))
    return pl.pallas_call(
        functools.partial(_diff_step_kernel, lam_init=lam_init),
        grid=(b,),
        in_specs=[new, new, new, cache, cache, lam_spec, lam_spec, lam_spec, lam_spec,
                  pl.BlockSpec((1, LANES), lambda i: (0, 0))],
        out_specs=new,
        out_shape=jax.ShapeDtypeStruct((b, n, DIFF_WIDTH), _BF16),
        compiler_params=pltpu.CompilerParams(
            dimension_semantics=("arbitrary",), vmem_limit_bytes=VMEM_LIMIT_BYTES),
        name="diff_step",
    )(q, k_new, v_new, k_cache, v_cache, lq1, lk1, lq2, lk2, subln_g)


def _prompt_specs(nq, t):
    assert nq % t == 0 and t & (t - 1) == 0
    q_spec = pl.BlockSpec((1, t, LANES), lambda bi, g, qi: (bi, qi, g))
    k_spec = pl.BlockSpec((1, nq, LANES), lambda bi, g, qi: (bi, 0, g))
    vt_spec = pl.BlockSpec((1, LANES, nq), lambda bi, g, qi: (bi, g, 0))
    return q_spec, k_spec, vt_spec


def _sb_prompt_kernel(q_ref, k_ref, vt_ref, o_ref, acc_ref, c_ref, z_ref, lb_ref, hl_ref, w_ref, *,
                      t, group, depth):
    assert LANES == 2 * SB_HEAD_DIM
    strips = 2 * t // LANES
    qi = pl.program_id(2)
    lane = lax.broadcasted_iota(jnp.int32, (1, LANES), 1)
    key = lax.broadcasted_iota(jnp.int32, (t, LANES), 0)
    qry = lax.broadcasted_iota(jnp.int32, (t, LANES), 1)
    later = (lax.broadcasted_iota(jnp.int32, (t, t), 1)
             > lax.broadcasted_iota(jnp.int32, (t, t), 0)).astype(_BF16)

    def tile_queries(g):
        q = q_ref[0, g * t:(g + 1) * t, :].astype(_F32)
        return jnp.concatenate([jnp.where(lane < SB_HEAD_DIM, q, 0.0),
                                jnp.where(lane >= SB_HEAD_DIM, q, 0.0)], axis=0).astype(_BF16)

    q_heads = [tile_queries(g) for g in range(group)]

    def stage_scores(g, j, buf):
        z = _dot_nt(k_ref[0, pl.ds(pl.multiple_of(j * t, t), t), :], q_heads[g])
        for s in range(strips):
            z_ref[buf, s] = z[:, s * LANES:(s + 1) * LANES]

    def own_mask(s):
        return key < ((s * LANES + qry) & (t - 1))

    def stage_logs(c, own, buf):
        c_new = []
        for s in range(strips):
            lb_ref[buf, s], log_stay = _stick_logs(z_ref[buf, s])
            if own:
                log_stay = jnp.where(own_mask(s), log_stay, 0.0)
            hl_ref[buf, s] = jnp.concatenate(_split_hi_lo(log_stay), axis=1)
            c_new.append(c[:, s * LANES:(s + 1) * LANES] + jnp.sum(log_stay, axis=0, keepdims=True))
        return jnp.concatenate(c_new, axis=1)

    def stage_weights(c, own, buf):
        for s in range(strips):
            sums = _dot(later, hl_ref[buf, s])
            w = jnp.exp2(lb_ref[buf, s] + (sums[:, :LANES] + sums[:, LANES:] + c[:, s * LANES:(s + 1) * LANES]))
            if own:
                w = jnp.where(own_mask(s), w, 0.0)
            head, part = divmod(s, t // LANES)
            w_ref[buf, head, :, part * LANES:(part + 1) * LANES] = w.astype(_BF16)

    def stage_values(j, buf):
        vt = vt_ref[0, :, pl.ds(pl.multiple_of(j * t, t), t)]
        return jnp.concatenate([_dot(vt[:SB_HEAD_DIM], w_ref[buf, 0]), _dot(vt[SB_HEAD_DIM:], w_ref[buf, 1])],
                               axis=0)

    def walk_tile(g, j, c, own, buf):
        stage_scores(g, j, buf)
        c_new = stage_logs(c, own, buf)
        stage_weights(c, own, buf)
        return c_new, stage_values(j, buf)

    def first_tiles(exists):
        walks = [(g, qi * group + g - d, d == 0, depth * g + d)
                 for g in range(group) for d in range(depth) if exists(g, d)]
        for g, j, own, buf in walks:
            stage_scores(g, j, buf)
        c_in = {}
        for g, j, own, buf in walks:
            c_in[buf] = jnp.zeros((1, 2 * t), _F32) if own else c_ref[g]
            c_ref[g] = stage_logs(c_in[buf], own, buf)
        for g, j, own, buf in walks:
            stage_weights(c_in[buf], own, buf)
        for g, j, own, buf in walks:
            o = stage_values(j, buf)
            acc_ref[g] = o if own else acc_ref[g] + o

    assert group >= depth - 1

    @pl.when(qi == 0)
    def _():
        first_tiles(lambda g, d: g >= d)

    @pl.when(qi > 0)
    def _():
        first_tiles(lambda g, d: True)

    def older_tile(g, step):
        return qi * group + g - depth - step

    def any_active(step):
        flags = [(older_tile(g, step) >= 0) & (jnp.max(c_ref[g]) > -F32_EXP2_ZERO) for g in range(group)]
        return functools.reduce(jnp.logical_or, flags)

    def walk_older(state):
        step, _ = state
        for g in range(group):
            @pl.when((older_tile(g, step) >= 0) & (jnp.max(c_ref[g]) > -F32_EXP2_ZERO))
            def _(g=g):
                c, o = walk_tile(g, older_tile(g, step), c_ref[g], False, depth * g)
                c_ref[g] = c
                acc_ref[g] += o
        return step + 1, any_active(step + 1)

    lax.while_loop(lambda state: state[1], walk_older, (0, any_active(0)))
    for g in range(group):
        o_ref[0, g * t:(g + 1) * t, :] = acc_ref[g].T.astype(o_ref.dtype)


def _sb_prompt_call(q, k, vt, *, t, group, depth):
    b, nq, _ = q.shape
    q_spec, k_spec, vt_spec = _prompt_specs(nq, group * t)
    sets = depth * group
    strips = 2 * t // LANES
    return pl.pallas_call(
        functools.partial(_sb_prompt_kernel, t=t, group=group, depth=depth),
        grid=(b, SB_WIDTH // LANES, nq // (group * t)),
        in_specs=[q_spec, k_spec, vt_spec],
        out_specs=q_spec,
        out_shape=jax.ShapeDtypeStruct((b, nq, SB_WIDTH), _BF16),
        scratch_shapes=[pltpu.VMEM((group, LANES, t), _F32), pltpu.VMEM((group, 1, 2 * t), _F32),
                        pltpu.VMEM((sets, strips, t, LANES), _F32),
                        pltpu.VMEM((sets, strips, t, LANES), _F32),
                        pltpu.VMEM((sets, strips, t, 2 * LANES), _BF16),
                        pltpu.VMEM((sets, 2, t, t), _BF16)],
        compiler_params=pltpu.CompilerParams(
            dimension_semantics=("parallel", "parallel", "arbitrary"),
            vmem_limit_bytes=VMEM_LIMIT_BYTES),
        name="sb_prompt",
    )(q, k, vt)


def _diff_prompt_kernel(q_ref, k_ref, vt_ref, lq1_ref, lk1_ref, lq2_ref, lk2_ref, g_ref, o_ref,
                        m_ref, l_ref, acc_ref, pen_ref, kaug_ref, knorm_ref, *p_refs, tq, tk, lam_init):
    own = tq // tk
    ahead_pairs = own // 2
    assert tq == own * tk and own % 2 == 0 and own <= DEPTH_SLOTS
    strips = 2 * tq // MXU_WIDTH
    head = pl.program_id(1)
    qi = pl.program_id(2)
    n_past = qi * own
    lane = lax.broadcasted_iota(jnp.int32, (1, LANES), 1)
    slope = _alibi_slope_log2(head)

    @pl.when(qi == 0)
    def _():
        key = lax.broadcasted_iota(jnp.int32, (tk, 2 * tq), 0)
        qry = lax.broadcasted_iota(jnp.int32, (tk, 2 * tq), 1) & (tq - 1)
        for d in range(own):
            k_pos = d * tk + key
            visible = lax.shift_right_logical(k_pos, CHUNK_SHIFT) <= lax.shift_right_logical(qry, CHUNK_SHIFT)
            ahead = -2.0 * slope * jnp.maximum(k_pos - qry, 0).astype(_F32)
            pen_ref[d] = jnp.where(visible, ahead, -jnp.inf)
        row = lax.broadcasted_iota(jnp.int32, (tk, LANES), 0)
        col = lax.broadcasted_iota(jnp.int32, (tk, LANES), 1)
        kaug_ref[...] = jnp.where(col < 3, row, jnp.where((col >= 6) & (col < 9), 1, 0)).astype(_F32)
        k_sq = jnp.square(k_ref[0].astype(_F32))
        for c in range(2):
            in_map = (lane >= c * DIFF_HEAD_DIM) & (lane < (c + 1) * DIFF_HEAD_DIM)
            norm_sq = jnp.sum(jnp.where(in_map, k_sq, 0.0), axis=1, keepdims=True)
            knorm_ref[c] = jnp.broadcast_to(jnp.sqrt(jnp.max(norm_sq, axis=0, keepdims=True)), (1, LANES))

    def split3(x):
        hi = x.astype(_BF16).astype(_F32)
        mid = (x - hi).astype(_BF16).astype(_F32)
        return [hi, mid, x - hi - mid]

    def augmented_queries(columns):
        blocks = []
        for c in range(2):
            aug = jnp.zeros((tq, LANES), _F32)
            for i, value in enumerate(columns[c]):
                aug = jnp.where(lane == i, value, aug)
            in_map = (lane >= c * DIFF_HEAD_DIM) & (lane < (c + 1) * DIFF_HEAD_DIM)
            blocks.append(jnp.concatenate([jnp.where(in_map, q, 0.0), aug], axis=1))
        return jnp.concatenate(blocks, axis=0).astype(_BF16)

    q = q_ref[0].astype(_F32)
    l_ref[...] = jnp.zeros_like(l_ref)
    acc_ref[...] = jnp.zeros_like(acc_ref)

    def slot_tile(n):
        return jnp.where(n < own, n_past + n, n_past - 1 - (n - own))

    def tile_start(n):
        return pl.multiple_of(jnp.clip(slot_tile(n), 0, n_past + own - 1) * tk, tk)

    def pair_buffer(pair):
        return (2 * pair) % DEPTH_SLOTS

    q_sq = jnp.square(q)
    reach = jnp.zeros((1, 1), _F32)
    bounds = []
    for c in range(2):
        in_map = (lane >= c * DIFF_HEAD_DIM) & (lane < (c + 1) * DIFF_HEAD_DIM)
        q_norm = jnp.sqrt(jnp.max(jnp.sum(jnp.where(in_map, q_sq, 0.0), axis=1, keepdims=True),
                                  axis=0, keepdims=True))
        bounds.append(q_norm * knorm_ref[c][:, :1])
        reach = jnp.maximum(reach, (2.0 * bounds[c] + F32_EXP2_ZERO) / slope)
    tiles_in_reach = jnp.minimum(jnp.floor((reach - 1.0) / tk) + 1.0, float(1 << 20))
    n_walk = jnp.minimum(jnp.max(tiles_in_reach).astype(jnp.int32), n_past)
    pairs = (own + n_walk + 1) // 2

    def online_walk():
        queries = augmented_queries([split3(slope)] * 2)
        key = lax.broadcasted_iota(jnp.int32, (tk, 2 * tq), 0)
        qry = lax.broadcasted_iota(jnp.int32, (tk, 2 * tq), 1) & (tq - 1)
        m_ref[...] = jnp.full_like(m_ref, NEG_BIG)

        def slot(n, carry):
            start = tile_start(n)
            k_pos = (slot_tile(n) - n_past) * tk + key
            keys = jnp.concatenate([k_ref[0, pl.ds(start, tk), :], kaug_ref[...].astype(_BF16)], axis=1)
            s = _dot_nt(keys, queries) + slope * (qry - key - jnp.abs(qry - k_pos)).astype(_F32)
            visible = (lax.shift_right_arithmetic(k_pos, CHUNK_SHIFT)
                       <= lax.shift_right_logical(qry, CHUNK_SHIFT))
            s = jnp.where(visible, s, -jnp.inf)
            m_old = m_ref[...]
            m_new = jnp.maximum(m_old, jnp.max(s, axis=0, keepdims=True))
            alpha = jnp.exp2(m_old - m_new)
            p = jnp.exp2(s - m_new)
            l_ref[...] = alpha * l_ref[...] + jnp.sum(p, axis=0, keepdims=True)
            m_ref[...] = m_new
            pv = _dot(vt_ref[0, :, pl.ds(start, tk)], p.astype(_BF16))
            for c in range(strips):
                cols = slice(c * MXU_WIDTH, (c + 1) * MXU_WIDTH)
                acc_ref[c] = alpha[:, cols] * acc_ref[c] + pv[:, cols]
            return carry

        lax.fori_loop(0, own + n_walk, slot, 0)

    def fixed_walk():
        q_pos = (qi * tq + lax.broadcasted_iota(jnp.int32, (tq, 1), 0)).astype(_F32)
        q_aug = augmented_queries([split3(slope) + split3(slope * tk) + split3(-slope * q_pos - bounds[c])
                                   for c in range(2)])

        def tile_keys(n):
            j = jnp.where(n < own + n_walk, slot_tile(n), -(1 << 20)).astype(_F32)
            aug = jnp.where((lane >= 3) & (lane < 6), j, kaug_ref[...]).astype(_BF16)
            return jnp.concatenate([k_ref[0, pl.ds(tile_start(n), tk), :], aug], axis=1)

        def weights(n, buf, own_slot):
            keys = tile_keys(n)
            for c in range(strips):
                strip = slice(c * MXU_WIDTH, (c + 1) * MXU_WIDTH)
                e = _dot_nt(keys, q_aug[strip])
                if own_slot:
                    e = e + pen_ref[n, :, strip]
                p = jnp.exp2(e)
                l_ref[:, strip] += jnp.sum(p, axis=0, keepdims=True)
                p_refs[buf][c] = p.astype(_BF16)

        def values(n, buf):
            vt = vt_ref[0, :, pl.ds(tile_start(n), tk)]
            for c in range(strips):
                acc_ref[c] += _dot(vt, p_refs[buf][c])

        def pair_step(t, pair):
            for d in range(2):
                values(2 * t + d, pair_buffer(pair) + d)
            for d in range(2):
                weights(2 * (t + ahead_pairs) + d, pair_buffer(pair + ahead_pairs) + d, False)

        def trip(i, carry):
            for pair in range(PAIRS_PER_TRIP):
                pair_step(PAIRS_PER_TRIP * i + pair, pair)
            return carry

        for n in range(own):
            weights(n, n, True)
        steps = pairs - ahead_pairs
        trips = steps // PAIRS_PER_TRIP
        lax.fori_loop(0, trips, trip, 0)
        for pair in range(PAIRS_PER_TRIP - 1):
            @pl.when(steps % PAIRS_PER_TRIP > pair)
            def _(pair=pair):
                pair_step(PAIRS_PER_TRIP * trips + pair, pair)

        for parity in range(2):
            @pl.when(steps % 2 == parity)
            def _(parity=parity):
                for r in range(ahead_pairs):
                    for d in range(2):
                        values(2 * (steps + r) + d, pair_buffer(parity + r) + d)

    spread = 2.0 * jnp.max(jnp.maximum(bounds[0], bounds[1]))
    pl.when(spread < SAFE_SPREAD)(fixed_walk)
    pl.when(spread >= SAFE_SPREAD)(online_walk)

    lam = (jnp.exp(jnp.sum(lq1_ref[...] * lk1_ref[...], axis=-1, keepdims=True))
           - jnp.exp(jnp.sum(lq2_ref[...] * lk2_ref[...], axis=-1, keepdims=True)) + lam_init)
    o = jnp.concatenate([acc_ref[c] for c in range(2 * tq // MXU_WIDTH)], axis=1) / l_ref[...]
    o = o[:, :tq] - lam * o[:, tq:]
    o = o * lax.rsqrt(jnp.mean(o * o, axis=0, keepdims=True) + RMS_EPS)
    o_ref[0] = (o.T * g_ref[...] * (1.0 - lam_init)).astype(o_ref.dtype)


def _diff_prompt_call(q, k, vt, lq1, lk1, lq2, lk2, subln_g, *, tq, lam_init):
    b, nq, _ = q.shape
    tk = DIFF_KEY_TILE
    assert 2 * DIFF_HEAD_DIM == LANES
    assert tk % CHUNK == 0 and CHUNK == 1 << CHUNK_SHIFT and tk <= 256 and nq // tk <= 256
    q_spec, k_spec, vt_spec = _prompt_specs(nq, tq)
    lam_spec = pl.BlockSpec((1, DIFF_HEAD_DIM), lambda bi, g, qi: (0, 0))
    stat = pltpu.VMEM((1, 2 * tq), _F32)
    return pl.pallas_call(
        functools.partial(_diff_prompt_kernel, tq=tq, tk=tk, lam_init=lam_init),
        grid=(b, DIFF_HEADS, nq // tq),
        in_specs=[q_spec, k_spec, vt_spec, lam_spec, lam_spec, lam_spec, lam_spec,
                  pl.BlockSpec((1, LANES), lambda bi, g, qi: (0, 0))],
        out_specs=q_spec,
        out_shape=jax.ShapeDtypeStruct((b, nq, DIFF_WIDTH), _BF16),
        scratch_shapes=[stat, stat, pltpu.VMEM((2 * tq // MXU_WIDTH, LANES, MXU_WIDTH), _F32),
                        pltpu.VMEM((tq // tk, tk, 2 * tq), _F32), pltpu.VMEM((tk, LANES), _F32),
                        pltpu.VMEM((2, 1, LANES), _F32)]
        + [pltpu.VMEM((2 * tq // MXU_WIDTH, tk, MXU_WIDTH), _BF16)] * DEPTH_SLOTS,
        compiler_params=pltpu.CompilerParams(
            dimension_semantics=("arbitrary", "arbitrary", "arbitrary"),
            vmem_limit_bytes=VMEM_LIMIT_BYTES),
        name="diff_prompt",
    )(q, k, vt, lq1, lk1, lq2, lk2, subln_g)


def _encoder_layer(x, past, w, lam_init):
    b, n, _ = x.shape
    m = b * n
    h = _ffn(x.reshape(m, D_MODEL), w["ff1_pre_g"], w["ff1_w_gate"], w["ff1_w_up"], w["ff1_w_down"],
             w["ff1_post_g"])
    lam_args = (w["lam_q1"], w["lam_k1"], w["lam_q2"], w["lam_k2"], w["subln_g"])

    def seq(a):
        return a.reshape(b, n, a.shape[-1])

    if past is None:
        sbq, sbk, sbv, dq, dk, dv, sbk16, sbvt16, dk16, dvt16 = _qkv(h, w["mix_pre_g"], w["w_in"], seq_len=n)
        sb_o = _sb_prompt_call(seq(sbq), seq(sbk16), sbvt16, t=SB_PROMPT_TILE, group=SB_PROMPT_GROUP,
                               depth=SB_PROMPT_DEPTH)
        d_o = _diff_prompt_call(seq(dq), seq(dk16), dvt16, *lam_args, tq=DIFF_PROMPT_TILE, lam_init=lam_init)
        sbk, sbv = (a.reshape(b, SB_HEADS, SB_HEAD_DIM, n).transpose(0, 3, 1, 2) for a in (sbk, sbv))
    else:
        sbq, sbk, sbv, dq, dk, dv, sbk16, sbv16, dk16, dv16 = _qkv(h, w["mix_pre_g"], w["w_in"])
        sb_o = _sb_step_call(seq(sbq), seq(sbk16), seq(sbv16), past[0], past[1])
        d_o = _diff_step_call(seq(dq), seq(dk16), seq(dv16), past[2], past[3], *lam_args, lam_init=lam_init)
    y = _ffn(h, w["ff2_pre_g"], w["ff2_w_gate"], w["ff2_w_up"], w["ff2_w_down"], w["ff2_post_g"],
             mixer=(sb_o.reshape(m, SB_WIDTH), d_o.reshape(m, DIFF_WIDTH), w["w_out"], w["mix_post_g"]),
             final_g=w["final_g"])
    rows = (sbk.reshape(b, n, SB_HEADS, SB_HEAD_DIM), sbv.reshape(b, n, SB_HEADS, SB_HEAD_DIM),
            dk.reshape(b, n, DIFF_HEADS, 2 * DIFF_HEAD_DIM), dv.reshape(b, n, DIFF_HEADS, 2 * DIFF_HEAD_DIM))
    return y.reshape(b, n, D_MODEL), rows


_MATRICES = ("ff1_w_gate", "ff1_w_up", "ff1_w_down", "w_in", "w_out", "ff2_w_gate", "ff2_w_up", "ff2_w_down")


def kernel(x_prompt, x_sample, cache_sb_k, cache_sb_v, cache_diff_k, cache_diff_v, ff1_pre_g, ff1_w_gate, ff1_w_up, ff1_w_down, ff1_post_g, mix_pre_g, w_in, lam_q1, lam_k1, lam_q2, lam_k2, subln_g, w_out, mix_post_g, ff2_pre_g, ff2_w_gate, ff2_w_up, ff2_w_down, ff2_post_g, final_g):
    params = dict(ff1_pre_g=ff1_pre_g, ff1_w_gate=ff1_w_gate, ff1_w_up=ff1_w_up, ff1_w_down=ff1_w_down,
                  ff1_post_g=ff1_post_g, mix_pre_g=mix_pre_g, w_in=w_in, lam_q1=lam_q1, lam_k1=lam_k1,
                  lam_q2=lam_q2, lam_k2=lam_k2, subln_g=subln_g, w_out=w_out, mix_post_g=mix_post_g,
                  ff2_pre_g=ff2_pre_g, ff2_w_gate=ff2_w_gate, ff2_w_up=ff2_w_up, ff2_w_down=ff2_w_down,
                  ff2_post_g=ff2_post_g, final_g=final_g)
    yp, ys = x_prompt, x_sample
    rows_p, rows_s = [], []
    for l in range(DEPTH):
        lam_init = 0.8 - 0.6 * math.exp(-0.3 * l)
        w = {name: (p[l].astype(_BF16) if name in _MATRICES else p[l][None, :].astype(_F32))
             for name, p in params.items()}
        yp, rp = _encoder_layer(yp, None, w, lam_init)
        past = tuple([c[l].transpose(0, 2, 3, 1).reshape(c.shape[1], SB_WIDTH, c.shape[2])
                      for c in (cache_sb_k, cache_sb_v)]
                     + [c[l].reshape(c.shape[1], -1, c.shape[-1]) for c in (cache_diff_k, cache_diff_v)])
        ys, rs = _encoder_layer(ys, past, w, lam_init)
        rows_p.append(rp)
        rows_s.append(rs)
    stacked_p = [jnp.stack(r, axis=0) for r in zip(*rows_p)]
    stacked_s = [jnp.stack(r, axis=0) for r in zip(*rows_s)]
    return (yp, ys, *stacked_p, *stacked_s)
```

```python
import functools
import math

import jax
import jax.numpy as jnp
from jax import lax
from jax.experimental import pallas as pl
from jax.experimental.pallas import tpu as pltpu

D_MODEL = 1024
DEPTH = 1
CHUNK = 64
CHUNK_SHIFT = 6
SB_HEADS = 8
SB_HEAD_DIM = 64
DIFF_HEADS = 4
DIFF_HEAD_DIM = 64
SB_WIDTH = SB_HEADS * SB_HEAD_DIM
DIFF_WIDTH = DIFF_HEADS * 2 * DIFF_HEAD_DIM
MIX_WIDTH = SB_WIDTH + DIFF_WIDTH
IN_WIDTH = 3 * SB_WIDTH + 3 * DIFF_WIDTH
D_FF = 2816
RMS_EPS = 1e-6

LANES = 128
MXU_WIDTH = 256
VMEM_LIMIT_BYTES = 56 * 1024 * 1024
F32_EXP2_ZERO = 151.0
NEG_BIG = -1e30
LOG2E = math.log2(math.e)
DEPTH_SLOTS = 4
PAIRS_PER_TRIP = 2
SAFE_SPREAD = 90.0
TOKEN_TILE = 512
SB_STEP_TILE = 256
SB_PROMPT_TILE = 128
SB_PROMPT_GROUP = 8
SB_PROMPT_DEPTH = 3
DIFF_PROMPT_TILE = 1024
DIFF_KEY_TILE = 256

_F32 = jnp.float32
_BF16 = jnp.bfloat16


def _rms(x, g):
    return x * lax.rsqrt(jnp.mean(x * x, axis=-1, keepdims=True) + RMS_EPS) * g


def _stick_logs(z):
    log_beta = jnp.minimum(z, 0.0) - jnp.log2(1.0 + jnp.exp2(-jnp.abs(z)))
    return log_beta, log_beta - z


def _split_hi_lo(x):
    hi = lax.bitcast_convert_type(lax.bitcast_convert_type(x, jnp.uint32) & jnp.uint32(0xFFFF0000), _F32)
    return hi.astype(_BF16), (x - hi).astype(_BF16)


def _dot(a, b):
    return jnp.dot(a, b, preferred_element_type=_F32)


def _dot_nt(a, b):
    return lax.dot_general(a, b, (((1,), (1,)), ((), ())), preferred_element_type=_F32)


def _ffn_kernel(*refs, mixer, final_norm):
    refs = list(refs)
    o_ref = refs.pop()
    x = refs.pop(0)[...]
    if mixer:
        sbo_ref, do_ref, wo_ref, mix_g_ref = refs[:4]
        refs = refs[4:]
        mix = _dot(sbo_ref[...], wo_ref[:SB_WIDTH, :]) + _dot(do_ref[...], wo_ref[SB_WIDTH:, :])
        x = x + _rms(mix, mix_g_ref[...])
    pre_ref, wg_ref, wu_ref, wd_ref, post_ref = refs[:5]
    xn = _rms(x, pre_ref[...]).astype(_BF16)
    g = _dot(xn, wg_ref[...])
    u = _dot(xn, wu_ref[...])
    a = (g * jax.nn.sigmoid(g) * u).astype(_BF16)
    h = x + 0.5 * _rms(_dot(a, wd_ref[...]), post_ref[...])
    if final_norm:
        h = _rms(h, refs[5][...])
    o_ref[...] = h


def _ffn(x, pre_g, wg, wu, wd, post_g, *, mixer=None, final_g=None):
    m = x.shape[0]
    tm = min(TOKEN_TILE, m)
    assert m % tm == 0

    def resident(shape):
        return pl.BlockSpec(shape, lambda i: (0, 0), pipeline_mode=pl.Buffered(1))

    row = pl.BlockSpec((tm, D_MODEL), lambda i: (i, 0))
    vec = resident((1, D_MODEL))
    in_specs, args = [row], [x]
    if mixer is not None:
        half = pl.BlockSpec((tm, SB_WIDTH), lambda i: (i, 0))
        in_specs += [half, half, resident((MIX_WIDTH, D_MODEL)), vec]
        args += list(mixer)
    in_specs += [vec, resident((D_MODEL, D_FF)), resident((D_MODEL, D_FF)), resident((D_FF, D_MODEL)), vec]
    args += [pre_g, wg, wu, wd, post_g]
    if final_g is not None:
        in_specs.append(vec)
        args.append(final_g)
    return pl.pallas_call(
        functools.partial(_ffn_kernel, mixer=mixer is not None, final_norm=final_g is not None),
        grid=(m // tm,),
        in_specs=in_specs,
        out_specs=row,
        out_shape=jax.ShapeDtypeStruct((m, D_MODEL), _F32),
        compiler_params=pltpu.CompilerParams(
            dimension_semantics=("parallel",), vmem_limit_bytes=VMEM_LIMIT_BYTES),
        name="ffn_mix_final" if mixer is not None else "ffn",
    )(*args)


def _qkv_kernel(h_ref, g_ref, w_ref, sbq_ref, sbk_ref, sbv_ref, dq_ref, dk_ref, dv_ref,
                sbk16_ref, sbv16_ref, dk16_ref, dv16_ref, *, transposed):
    hn = _rms(h_ref[...], g_ref[...]).astype(_BF16)

    def proj(idx):
        return _dot(hn, w_ref[:, idx * SB_WIDTH:(idx + 1) * SB_WIDTH])

    sbq_ref[...] = (proj(0) * (SB_HEAD_DIM ** -0.5 * LOG2E)).astype(_BF16)
    dq_ref[...] = (proj(3) * (DIFF_HEAD_DIM ** -0.5 * LOG2E)).astype(_BF16)

    def store_rows(ref, p):
        head_dim = ref.shape[1]
        heads = p.shape[1] // head_dim
        for head in range(heads):
            ref[pl.ds(head, p.shape[0], stride=heads), :] = p[:, head * head_dim:(head + 1) * head_dim]

    for idx, full_ref, half_ref in ((4, dk_ref, dk16_ref), (5, dv_ref, dv16_ref)):
        p = proj(idx)
        store_rows(full_ref, p)
        if transposed and half_ref is dv16_ref:
            half_ref[0] = p.T.astype(_BF16)
        else:
            half_ref[...] = p.astype(_BF16)
    for idx, full_ref, half_ref in ((1, sbk_ref, sbk16_ref), (2, sbv_ref, sbv16_ref)):
        p = proj(idx)
        if transposed:
            p_t = p.T
            full_ref[0] = p_t
            if half_ref is sbv16_ref:
                half_ref[0] = p_t.astype(_BF16)
            else:
                half_ref[...] = p.astype(_BF16)
        else:
            store_rows(full_ref, p)
            half_ref[...] = p.astype(_BF16)


def _qkv(h, g, w, *, seq_len=None):
    assert SB_WIDTH == DIFF_WIDTH
    m = h.shape[0]
    tm = min(TOKEN_TILE, m)
    assert m % tm == 0
    row = pl.BlockSpec((tm, D_MODEL), lambda i: (i, 0))
    out = pl.BlockSpec((tm, SB_WIDTH), lambda i: (i, 0))
    b16 = jax.ShapeDtypeStruct((m, SB_WIDTH), _BF16)

    def cache_rows(heads, head_dim):
        return (jax.ShapeDtypeStruct((m * heads, head_dim), _F32),
                pl.BlockSpec((tm * heads, head_dim), lambda i: (i, 0)))

    sb_rows, sb_out = cache_rows(SB_HEADS, SB_HEAD_DIM)
    d_rows, d_out = cache_rows(DIFF_HEADS, 2 * DIFF_HEAD_DIM)
    v_out, v16 = out, b16
    if seq_len is not None:
        assert seq_len % tm == 0
        tiles = seq_len // tm
        v_out = pl.BlockSpec((1, SB_WIDTH, tm), lambda i: (i // tiles, 0, i % tiles))
        v16 = jax.ShapeDtypeStruct((m // seq_len, SB_WIDTH, seq_len), _BF16)
        sb_rows, sb_out = jax.ShapeDtypeStruct(v16.shape, _F32), v_out
    return pl.pallas_call(
        functools.partial(_qkv_kernel, transposed=seq_len is not None),
        grid=(m // tm,),
        in_specs=[row, pl.BlockSpec((1, D_MODEL), lambda i: (0, 0)),
                  pl.BlockSpec((D_MODEL, IN_WIDTH), lambda i: (0, 0))],
        out_specs=[out, sb_out, sb_out, out, d_out, d_out, out, v_out, out, v_out],
        out_shape=[b16, sb_rows, sb_rows, b16, d_rows, d_rows, b16, v16, b16, v16],
        compiler_params=pltpu.CompilerParams(
            dimension_semantics=("parallel",), vmem_limit_bytes=VMEM_LIMIT_BYTES),
        name="qkv",
    )(h, g, w)


def _sb_step_kernel(q_ref, kn_ref, vn_ref, kc_ref, vc_ref, o_ref, acc_ref, c_ref, *, tk):
    n = q_ref.shape[1]
    past = kc_ref.shape[2]

    def later(t):
        return (lax.broadcasted_iota(jnp.int32, (t, t), 0)
                > lax.broadcasted_iota(jnp.int32, (t, t), 1)).astype(_BF16)

    later_new, later_tile = later(n), later(tk)
    newer = (lax.broadcasted_iota(jnp.int32, (n, n), 1) < lax.broadcasted_iota(jnp.int32, (n, n), 0))

    def walk(z, c, later_t, mask):
        log_beta, log_stay = _stick_logs(z)
        if mask is not None:
            log_stay = jnp.where(mask, log_stay, 0.0)
        sums = _dot(jnp.concatenate(_split_hi_lo(log_stay), axis=0), later_t)
        w = jnp.exp2(log_beta + (sums[:n] + sums[n:] + c))
        if mask is not None:
            w = jnp.where(mask, w, 0.0)
        return c + jnp.sum(log_stay, axis=-1, keepdims=True), w.astype(_BF16)

    def head_cols(head):
        return slice(head * SB_HEAD_DIM, (head + 1) * SB_HEAD_DIM)

    def walk_cache(start):
        for head in range(SB_HEADS):
            rows = head_cols(head)
            k_t = kc_ref[0, rows, pl.ds(start, tk)].astype(_BF16)
            v_t = vc_ref[0, rows, pl.ds(start, tk)].astype(_BF16)
            c_ref[head], w = walk(_dot(q_ref[0, :, rows], k_t), c_ref[head], later_tile, None)
            acc_ref[head] += _dot_nt(w, v_t)

    for head in range(SB_HEADS):
        cols = head_cols(head)
        c_ref[head], w = walk(_dot_nt(q_ref[0, :, cols], kn_ref[0, :, cols]), jnp.zeros((n, 1), _F32),
                              later_new, newer)
        acc_ref[head] = _dot(w, vn_ref[0, :, cols])
    walk_cache(past - tk)

    def cond(state):
        j, c_max = state
        return (j >= 0) & (c_max > -F32_EXP2_ZERO)

    def body(state):
        j, _ = state
        walk_cache(pl.multiple_of(j * tk, tk))
        return j - 1, jnp.max(c_ref[...])

    lax.while_loop(cond, body, (past // tk - 2, jnp.max(c_ref[...])))
    o_ref[0] = jnp.concatenate([acc_ref[head] for head in range(SB_HEADS)], axis=-1).astype(o_ref.dtype)


def _sb_step_call(q, k_new, v_new, k_cache, v_cache):
    b, n, _ = q.shape
    past = k_cache.shape[2]
    tk = min(SB_STEP_TILE, past)
    assert past % tk == 0
    new = pl.BlockSpec((1, n, SB_WIDTH), lambda i: (i, 0, 0))
    cache = pl.BlockSpec((1, SB_WIDTH, past), lambda i: (i, 0, 0))
    return pl.pallas_call(
        functools.partial(_sb_step_kernel, tk=tk),
        grid=(b,),
        in_specs=[new, new, new, cache, cache],
        out_specs=new,
        out_shape=jax.ShapeDtypeStruct((b, n, SB_WIDTH), _BF16),
        scratch_shapes=[pltpu.VMEM((SB_HEADS, n, SB_HEAD_DIM), _F32), pltpu.VMEM((SB_HEADS, n, 1), _F32)],
        compiler_params=pltpu.CompilerParams(
            dimension_semantics=("arbitrary",), vmem_limit_bytes=VMEM_LIMIT_BYTES),
        name="sb_step",
    )(q, k_new, v_new, k_cache, v_cache)


def _alibi_slope_log2(head):
    assert 8 % DIFF_HEADS == 0
    exponent = (8 // DIFF_HEADS) * (head + 1)
    slope = lax.bitcast_convert_type(jnp.full((1, 1), (127 - exponent) << 23, jnp.int32), _F32)
    return slope * LOG2E


def _diff_step_kernel(q_ref, kn_ref, vn_ref, kc_ref, vc_ref, lq1_ref, lk1_ref, lq2_ref, lk2_ref, g_ref,
                      o_ref, *, lam_init):
    n = q_ref.shape[1]
    past = kc_ref.shape[1] // DIFF_HEADS

    def cache_head(ref, head):
        return ref[0, pl.ds(head, past, stride=DIFF_HEADS), :].astype(_BF16)

    lane = lax.broadcasted_iota(jnp.int32, (1, LANES), 1)
    q_idx_c = lax.broadcasted_iota(jnp.int32, (2 * n, past), 0) & (n - 1)
    k_idx_c = lax.broadcasted_iota(jnp.int32, (2 * n, past), 1)
    q_idx_n = lax.broadcasted_iota(jnp.int32, (2 * n, n), 0) & (n - 1)
    k_idx_n = lax.broadcasted_iota(jnp.int32, (2 * n, n), 1)
    distance_c = (past + q_idx_c - k_idx_c).astype(_F32)
    distance_n = jnp.abs(q_idx_n - k_idx_n).astype(_F32)
    visible_n = (lax.shift_right_logical(past + k_idx_n, CHUNK_SHIFT)
                 <= lax.shift_right_logical(past + q_idx_n, CHUNK_SHIFT))
    lam = (jnp.exp(jnp.sum(lq1_ref[...] * lk1_ref[...], axis=-1, keepdims=True))
           - jnp.exp(jnp.sum(lq2_ref[...] * lk2_ref[...], axis=-1, keepdims=True)) + lam_init)
    for head in range(DIFF_HEADS):
        cols = slice(head * LANES, (head + 1) * LANES)
        slope = _alibi_slope_log2(head)
        q = q_ref[0, :, cols].astype(_F32)
        q_maps = jnp.concatenate([jnp.where(lane < DIFF_HEAD_DIM, q, 0.0),
                                  jnp.where(lane >= DIFF_HEAD_DIM, q, 0.0)], axis=0).astype(_BF16)
        s_c = _dot_nt(q_maps, cache_head(kc_ref, head)) - slope * distance_c
        s_n = jnp.where(visible_n, _dot_nt(q_maps, kn_ref[0, :, cols]) - slope * distance_n, -jnp.inf)
        m = jnp.maximum(jnp.max(s_c, axis=-1, keepdims=True), jnp.max(s_n, axis=-1, keepdims=True))
        p_c = jnp.exp2(s_c - m)
        p_n = jnp.exp2(s_n - m)
        l = jnp.sum(p_c, axis=-1, keepdims=True) + jnp.sum(p_n, axis=-1, keepdims=True)
        o = (_dot(p_c.astype(_BF16), cache_head(vc_ref, head))
             + _dot(p_n.astype(_BF16), vn_ref[0, :, cols])) / l
        o = o[:n] - lam * o[n:]
        o_ref[0, :, cols] = (_rms(o, g_ref[...]) * (1.0 - lam_init)).astype(o_ref.dtype)


def _diff_step_call(q, k_new, v_new, k_cache, v_cache, lq1, lk1, lq2, lk2, subln_g, *, lam_init):
    b, n, _ = q.shape
    rows = k_cache.shape[1]
    assert 2 * DIFF_HEAD_DIM == LANES and CHUNK == 1 << CHUNK_SHIFT and n & (n - 1) == 0
    new = pl.BlockSpec((1, n, DIFF_WIDTH), lambda i: (i, 0, 0))
    cache = pl.BlockSpec((1, rows, LANES), lambda i: (i, 0, 0))
    lam_spec = pl.BlockSpec((1, DIFF_HEAD_DIM), lambda i: (0, 0))
    return pl.pallas_call(
        functools.partial(_diff_step_kernel, lam_init=lam_init),
        grid=(b,),
        in_specs=[new, new, new, cache, cache, lam_spec, lam_spec, lam_spec, lam_spec,
                  pl.BlockSpec((1, LANES), lambda i: (0, 0))],
        out_specs=new,
        out_shape=jax.ShapeDtypeStruct((b, n, DIFF_WIDTH), _BF16),
        compiler_params=pltpu.CompilerParams(
            dimension_semantics=("arbitrary",), vmem_limit_bytes=VMEM_LIMIT_BYTES),
        name="diff_step",
    )(q, k_new, v_new, k_cache, v_cache, lq1, lk1, lq2, lk2, subln_g)


def _prompt_specs(nq, t):
    assert nq % t == 0 and t & (t - 1) == 0
    q_spec = pl.BlockSpec((1, t, LANES), lambda bi, g, qi: (bi, qi, g))
    k_spec = pl.BlockSpec((1, nq, LANES), lambda bi, g, qi: (bi, 0, g))
    vt_spec = pl.BlockSpec((1, LANES, nq), lambda bi, g, qi: (bi, g, 0))
    return q_spec, k_spec, vt_spec


def _sb_prompt_kernel(q_ref, k_ref, vt_ref, o_ref, acc_ref, c_ref, z_ref, lb_ref, hl_ref, w_ref, *,
                      t, group, depth):
    assert LANES == 2 * SB_HEAD_DIM
    strips = 2 * t // LANES
    qi = pl.program_id(2)
    lane = lax.broadcasted_iota(jnp.int32, (1, LANES), 1)
    key = lax.broadcasted_iota(jnp.int32, (t, LANES), 0)
    qry = lax.broadcasted_iota(jnp.int32, (t, LANES), 1)
    later = (lax.broadcasted_iota(jnp.int32, (t, t), 1)
             > lax.broadcasted_iota(jnp.int32, (t, t), 0)).astype(_BF16)

    def tile_queries(g):
        q = q_ref[0, g * t:(g + 1) * t, :].astype(_F32)
        return jnp.concatenate([jnp.where(lane < SB_HEAD_DIM, q, 0.0),
                                jnp.where(lane >= SB_HEAD_DIM, q, 0.0)], axis=0).astype(_BF16)

    q_heads = [tile_queries(g) for g in range(group)]

    def stage_scores(g, j, buf):
        z = _dot_nt(k_ref[0, pl.ds(pl.multiple_of(j * t, t), t), :], q_heads[g])
        for s in range(strips):
            z_ref[buf, s] = z[:, s * LANES:(s + 1) * LANES]

    def own_mask(s):
        return key < ((s * LANES + qry) & (t - 1))

    def stage_logs(c, own, buf):
        c_new = []
        for s in range(strips):
            lb_ref[buf, s], log_stay = _stick_logs(z_ref[buf, s])
            if own:
                log_stay = jnp.where(own_mask(s), log_stay, 0.0)
            hl_ref[buf, s] = jnp.concatenate(_split_hi_lo(log_stay), axis=1)
            c_new.append(c[:, s * LANES:(s + 1) * LANES] + jnp.sum(log_stay, axis=0, keepdims=True))
        return jnp.concatenate(c_new, axis=1)

    def stage_weights(c, own, buf):
        for s in range(strips):
            sums = _dot(later, hl_ref[buf, s])
            w = jnp.exp2(lb_ref[buf, s] + (sums[:, :LANES] + sums[:, LANES:] + c[:, s * LANES:(s + 1) * LANES]))
            if own:
                w = jnp.where(own_mask(s), w, 0.0)
            head, part = divmod(s, t // LANES)
            w_ref[buf, head, :, part * LANES:(part + 1) * LANES] = w.astype(_BF16)

    def stage_values(j, buf):
        vt = vt_ref[0, :, pl.ds(pl.multiple_of(j * t, t), t)]
        return jnp.concatenate([_dot(vt[:SB_HEAD_DIM], w_ref[buf, 0]), _dot(vt[SB_HEAD_DIM:], w_ref[buf, 1])],
                               axis=0)

    def walk_tile(g, j, c, own, buf):
        stage_scores(g, j, buf)
        c_new = stage_logs(c, own, buf)
        stage_weights(c, own, buf)
        return c_new, stage_values(j, buf)

    def first_tiles(exists):
        walks = [(g, qi * group + g - d, d == 0, depth * g + d)
                 for g in range(group) for d in range(depth) if exists(g, d)]
        for g, j, own, buf in walks:
            stage_scores(g, j, buf)
        c_in = {}
        for g, j, own, buf in walks:
            c_in[buf] = jnp.zeros((1, 2 * t), _F32) if own else c_ref[g]
            c_ref[g] = stage_logs(c_in[buf], own, buf)
        for g, j, own, buf in walks:
            stage_weights(c_in[buf], own, buf)
        for g, j, own, buf in walks:
            o = stage_values(j, buf)
            acc_ref[g] = o if own else acc_ref[g] + o

    assert group >= depth - 1

    @pl.when(qi == 0)
    def _():
        first_tiles(lambda g, d: g >= d)

    @pl.when(qi > 0)
    def _():
        first_tiles(lambda g, d: True)

    def older_tile(g, step):
        return qi * group + g - depth - step

    def any_active(step):
        flags = [(older_tile(g, step) >= 0) & (jnp.max(c_ref[g]) > -F32_EXP2_ZERO) for g in range(group)]
        return functools.reduce(jnp.logical_or, flags)

    def walk_older(state):
        step, _ = state
        for g in range(group):
            @pl.when((older_tile(g, step) >= 0) & (jnp.max(c_ref[g]) > -F32_EXP2_ZERO))
            def _(g=g):
                c, o = walk_tile(g, older_tile(g, step), c_ref[g], False, depth * g)
                c_ref[g] = c
                acc_ref[g] += o
        return step + 1, any_active(step + 1)

    lax.while_loop(lambda state: state[1], walk_older, (0, any_active(0)))
    for g in range(group):
        o_ref[0, g * t:(g + 1) * t, :] = acc_ref[g].T.astype(o_ref.dtype)


def _sb_prompt_call(q, k, vt, *, t, group, depth):
    b, nq, _ = q.shape
    q_spec, k_spec, vt_spec = _prompt_specs(nq, group * t)
    sets = depth * group
    strips = 2 * t // LANES
    return pl.pallas_call(
        functools.partial(_sb_prompt_kernel, t=t, group=group, depth=depth),
        grid=(b, SB_WIDTH // LANES, nq // (group * t)),
        in_specs=[q_spec, k_spec, vt_spec],
        out_specs=q_spec,
        out_shape=jax.ShapeDtypeStruct((b, nq, SB_WIDTH), _BF16),
        scratch_shapes=[pltpu.VMEM((group, LANES, t), _F32), pltpu.VMEM((group, 1, 2 * t), _F32),
                        pltpu.VMEM((sets, strips, t, LANES), _F32),
                        pltpu.VMEM((sets, strips, t, LANES), _F32),
                        pltpu.VMEM((sets, strips, t, 2 * LANES), _BF16),
                        pltpu.VMEM((sets, 2, t, t), _BF16)],
        compiler_params=pltpu.CompilerParams(
            dimension_semantics=("parallel", "parallel", "arbitrary"),
            vmem_limit_bytes=VMEM_LIMIT_BYTES),
        name="sb_prompt",
    )(q, k, vt)


def _diff_prompt_kernel(q_ref, k_ref, vt_ref, lq1_ref, lk1_ref, lq2_ref, lk2_ref, g_ref, o_ref,
                        m_ref, l_ref, acc_ref, pen_ref, kaug_ref, knorm_ref, *p_refs, tq, tk, lam_init):
    own = tq // tk
    ahead_pairs = own // 2
    assert tq == own * tk and own % 2 == 0 and own <= DEPTH_SLOTS
    strips = 2 * tq // MXU_WIDTH
    head = pl.program_id(1)
    qi = pl.program_id(2)
    n_past = qi * own
    lane = lax.broadcasted_iota(jnp.int32, (1, LANES), 1)
    slope = _alibi_slope_log2(head)

    @pl.when(qi == 0)
    def _():
        key = lax.broadcasted_iota(jnp.int32, (tk, 2 * tq), 0)
        qry = lax.broadcasted_iota(jnp.int32, (tk, 2 * tq), 1) & (tq - 1)
        for d in range(own):
            k_pos = d * tk + key
            visible = lax.shift_right_logical(k_pos, CHUNK_SHIFT) <= lax.shift_right_logical(qry, CHUNK_SHIFT)
            ahead = -2.0 * slope * jnp.maximum(k_pos - qry, 0).astype(_F32)
            pen_ref[d] = jnp.where(visible, ahead, -jnp.inf)
        row = lax.broadcasted_iota(jnp.int32, (tk, LANES), 0)
        col = lax.broadcasted_iota(jnp.int32, (tk, LANES), 1)
        kaug_ref[...] = jnp.where(col < 3, row, jnp.where((col >= 6) & (col < 9), 1, 0)).astype(_F32)
        k_sq = jnp.square(k_ref[0].astype(_F32))
        for c in range(2):
            in_map = (lane >= c * DIFF_HEAD_DIM) & (lane < (c + 1) * DIFF_HEAD_DIM)
            norm_sq = jnp.sum(jnp.where(in_map, k_sq, 0.0), axis=1, keepdims=True)
            knorm_ref[c] = jnp.broadcast_to(jnp.sqrt(jnp.max(norm_sq, axis=0, keepdims=True)), (1, LANES))

    def split3(x):
        hi = x.astype(_BF16).astype(_F32)
        mid = (x - hi).astype(_BF16).astype(_F32)
        return [hi, mid, x - hi - mid]

    def augmented_queries(columns):
        blocks = []
        for c in range(2):
            aug = jnp.zeros((tq, LANES), _F32)
            for i, value in enumerate(columns[c]):
                aug = jnp.where(lane == i, value, aug)
            in_map = (lane >= c * DIFF_HEAD_DIM) & (lane < (c + 1) * DIFF_HEAD_DIM)
            blocks.append(jnp.concatenate([jnp.where(in_map, q, 0.0), aug], axis=1))
        return jnp.concatenate(blocks, axis=0).astype(_BF16)

    q = q_ref[0].astype(_F32)
    l_ref[...] = jnp.zeros_like(l_ref)
    acc_ref[...] = jnp.zeros_like(acc_ref)

    def slot_tile(n):
        return jnp.where(n < own, n_past + n, n_past - 1 - (n - own))

    def tile_start(n):
        return pl.multiple_of(jnp.clip(slot_tile(n), 0, n_past + own - 1) * tk, tk)

    def pair_buffer(pair):
        return (2 * pair) % DEPTH_SLOTS

    q_sq = jnp.square(q)
    reach = jnp.zeros((1, 1), _F32)
    bounds = []
    for c in range(2):
        in_map = (lane >= c * DIFF_HEAD_DIM) & (lane < (c + 1) * DIFF_HEAD_DIM)
        q_norm = jnp.sqrt(jnp.max(jnp.sum(jnp.where(in_map, q_sq, 0.0), axis=1, keepdims=True),
                                  axis=0, keepdims=True))
        bounds.append(q_norm * knorm_ref[c][:, :1])
        reach = jnp.maximum(reach, (2.0 * bounds[c] + F32_EXP2_ZERO) / slope)
    tiles_in_reach = jnp.minimum(jnp.floor((reach - 1.0) / tk) + 1.0, float(1 << 20))
    n_walk = jnp.minimum(jnp.max(tiles_in_reach).astype(jnp.int32), n_past)
    pairs = (own + n_walk + 1) // 2

    def online_walk():
        queries = augmented_queries([split3(slope)] * 2)
        key = lax.broadcasted_iota(jnp.int32, (tk, 2 * tq), 0)
        qry = lax.broadcasted_iota(jnp.int32, (tk, 2 * tq), 1) & (tq - 1)
        m_ref[...] = jnp.full_like(m_ref, NEG_BIG)

        def slot(n, carry):
            start = tile_start(n)
            k_pos = (slot_tile(n) - n_past) * tk + key
            keys = jnp.concatenate([k_ref[0, pl.ds(start, tk), :], kaug_ref[...].astype(_BF16)], axis=1)
            s = _dot_nt(keys, queries) + slope * (qry - key - jnp.abs(qry - k_pos)).astype(_F32)
            visible = (lax.shift_right_arithmetic(k_pos, CHUNK_SHIFT)
                       <= lax.shift_right_logical(qry, CHUNK_SHIFT))
            s = jnp.where(visible, s, -jnp.inf)
            m_old = m_ref[...]
            m_new = jnp.maximum(m_old, jnp.max(s, axis=0, keepdims=True))
            alpha = jnp.exp2(m_old - m_new)
            p = jnp.exp2(s - m_new)
            l_ref[...] = alpha * l_ref[...] + jnp.sum(p, axis=0, keepdims=True)
            m_ref[...] = m_new
            pv = _dot(vt_ref[0, :, pl.ds(start, tk)], p.astype(_BF16))
            for c in range(strips):
                cols = slice(c * MXU_WIDTH, (c + 1) * MXU_WIDTH)
                acc_ref[c] = alpha[:, cols] * acc_ref[c] + pv[:, cols]
            return carry

        lax.fori_loop(0, own + n_walk, slot, 0)

    def fixed_walk():
        q_pos = (qi * tq + lax.broadcasted_iota(jnp.int32, (tq, 1), 0)).astype(_F32)
        q_aug = augmented_queries([split3(slope) + split3(slope * tk) + split3(-slope * q_pos - bounds[c])
                                   for c in range(2)])

        def tile_keys(n):
            j = jnp.where(n < own + n_walk, slot_tile(n), -(1 << 20)).astype(_F32)
            aug = jnp.where((lane >= 3) & (lane < 6), j, kaug_ref[...]).astype(_BF16)
            return jnp.concatenate([k_ref[0, pl.ds(tile_start(n), tk), :], aug], axis=1)

        def weights(n, buf, own_slot):
            keys = tile_keys(n)
            for c in range(strips):
                strip = slice(c * MXU_WIDTH, (c + 1) * MXU_WIDTH)
                if own_slot and tk == MXU_WIDTH and c % (strips // 2) < n:
                    p_refs[buf][c] = jnp.zeros((tk, MXU_WIDTH), _BF16)
                    continue
                e = _dot_nt(keys, q_aug[strip])
                if own_slot:
                    e = e + pen_ref[n, :, strip]
                p = jnp.exp2(e)
                l_ref[:, strip] += jnp.sum(p, axis=0, keepdims=True)
                p_refs[buf][c] = p.astype(_BF16)

        def values(n, buf):
            vt = vt_ref[0, :, pl.ds(tile_start(n), tk)]
            for c in range(strips):
                acc_ref[c] += _dot(vt, p_refs[buf][c])

        def pair_step(t, pair):
            for d in range(2):
                values(2 * t + d, pair_buffer(pair) + d)
            for d in range(2):
                weights(2 * (t + ahead_pairs) + d, pair_buffer(pair + ahead_pairs) + d, False)

        def trip(i, carry):
            for pair in range(PAIRS_PER_TRIP):
                pair_step(PAIRS_PER_TRIP * i + pair, pair)
            return carry

        for n in range(own):
            weights(n, n, True)
        steps = pairs - ahead_pairs
        trips = steps // PAIRS_PER_TRIP
        lax.fori_loop(0, trips, trip, 0)
        for pair in range(PAIRS_PER_TRIP - 1):
            @pl.when(steps % PAIRS_PER_TRIP > pair)
            def _(pair=pair):
                pair_step(PAIRS_PER_TRIP * trips + pair, pair)

        for parity in range(2):
            @pl.when(steps % 2 == parity)
            def _(parity=parity):
                for r in range(ahead_pairs):
                    for d in range(2):
                        values(2 * (steps + r) + d, pair_buffer(parity + r) + d)

    spread = 2.0 * jnp.max(jnp.maximum(bounds[0], bounds[1]))
    pl.when(spread < SAFE_SPREAD)(fixed_walk)
    pl.when(spread >= SAFE_SPREAD)(online_walk)

    lam = (jnp.exp(jnp.sum(lq1_ref[...] * lk1_ref[...], axis=-1, keepdims=True))
           - jnp.exp(jnp.sum(lq2_ref[...] * lk2_ref[...], axis=-1, keepdims=True)) + lam_init)
    o = jnp.concatenate([acc_ref[c] for c in range(2 * tq // MXU_WIDTH)], axis=1) / l_ref[...]
    o = o[:, :tq] - lam * o[:, tq:]
    o = o * lax.rsqrt(jnp.mean(o * o, axis=0, keepdims=True) + RMS_EPS)
    o_ref[0] = (o.T * g_ref[...] * (1.0 - lam_init)).astype(o_ref.dtype)


def _diff_prompt_call(q, k, vt, lq1, lk1, lq2, lk2, subln_g, *, tq, lam_init):
    b, nq, _ = q.shape
    tk = DIFF_KEY_TILE
    assert 2 * DIFF_HEAD_DIM == LANES
    assert tk % CHUNK == 0 and CHUNK == 1 << CHUNK_SHIFT and tk <= 256 and nq // tk <= 256
    q_spec, k_spec, vt_spec = _prompt_specs(nq, tq)
    lam_spec = pl.BlockSpec((1, DIFF_HEAD_DIM), lambda bi, g, qi: (0, 0))
    stat = pltpu.VMEM((1, 2 * tq), _F32)
    return pl.pallas_call(
        functools.partial(_diff_prompt_kernel, tq=tq, tk=tk, lam_init=lam_init),
        grid=(b, DIFF_HEADS, nq // tq),
        in_specs=[q_spec, k_spec, vt_spec, lam_spec, lam_spec, lam_spec, lam_spec,
                  pl.BlockSpec((1, LANES), lambda bi, g, qi: (0, 0))],
        out_specs=q_spec,
        out_shape=jax.ShapeDtypeStruct((b, nq, DIFF_WIDTH), _BF16),
        scratch_shapes=[stat, stat, pltpu.VMEM((2 * tq // MXU_WIDTH, LANES, MXU_WIDTH), _F32),
                        pltpu.VMEM((tq // tk, tk, 2 * tq), _F32), pltpu.VMEM((tk, LANES), _F32),
                        pltpu.VMEM((2, 1, LANES), _F32)]
        + [pltpu.VMEM((2 * tq // MXU_WIDTH, tk, MXU_WIDTH), _BF16)] * DEPTH_SLOTS,
        compiler_params=pltpu.CompilerParams(
            dimension_semantics=("arbitrary", "arbitrary", "arbitrary"),
            vmem_limit_bytes=VMEM_LIMIT_BYTES),
        name="diff_prompt",
    )(q, k, vt, lq1, lk1, lq2, lk2, subln_g)


def _encoder_layer(x, past, w, lam_init):
    b, n, _ = x.shape
    m = b * n
    h = _ffn(x.reshape(m, D_MODEL), w["ff1_pre_g"], w["ff1_w_gate"], w["ff1_w_up"], w["ff1_w_down"],
             w["ff1_post_g"])
    lam_args = (w["lam_q1"], w["lam_k1"], w["lam_q2"], w["lam_k2"], w["subln_g"])

    def seq(a):
        return a.reshape(b, n, a.shape[-1])

    if past is None:
        sbq, sbk, sbv, dq, dk, dv, sbk16, sbvt16, dk16, dvt16 = _qkv(h, w["mix_pre_g"], w["w_in"], seq_len=n)
        sb_o = _sb_prompt_call(seq(sbq), seq(sbk16), sbvt16, t=SB_PROMPT_TILE, group=SB_PROMPT_GROUP,
                               depth=SB_PROMPT_DEPTH)
        d_o = _diff_prompt_call(seq(dq), seq(dk16), dvt16, *lam_args, tq=DIFF_PROMPT_TILE, lam_init=lam_init)
        sbk, sbv = (a.reshape(b, SB_HEADS, SB_HEAD_DIM, n).transpose(0, 3, 1, 2) for a in (sbk, sbv))
    else:
        sbq, sbk, sbv, dq, dk, dv, sbk16, sbv16, dk16, dv16 = _qkv(h, w["mix_pre_g"], w["w_in"])
        sb_o = _sb_step_call(seq(sbq), seq(sbk16), seq(sbv16), past[0], past[1])
        d_o = _diff_step_call(seq(dq), seq(dk16), seq(dv16), past[2], past[3], *lam_args, lam_init=lam_init)
    y = _ffn(h, w["ff2_pre_g"], w["ff2_w_gate"], w["ff2_w_up"], w["ff2_w_down"], w["ff2_post_g"],
             mixer=(sb_o.reshape(m, SB_WIDTH), d_o.reshape(m, DIFF_WIDTH), w["w_out"], w["mix_post_g"]),
             final_g=w["final_g"])
    rows = (sbk.reshape(b, n, SB_HEADS, SB_HEAD_DIM), sbv.reshape(b, n, SB_HEADS, SB_HEAD_DIM),
            dk.reshape(b, n, DIFF_HEADS, 2 * DIFF_HEAD_DIM), dv.reshape(b, n, DIFF_HEADS, 2 * DIFF_HEAD_DIM))
    return y.reshape(b, n, D_MODEL), rows


_MATRICES = ("ff1_w_gate", "ff1_w_up", "ff1_w_down", "w_in", "w_out", "ff2_w_gate", "ff2_w_up", "ff2_w_down")


def kernel(x_prompt, x_sample, cache_sb_k, cache_sb_v, cache_diff_k, cache_diff_v, ff1_pre_g, ff1_w_gate, ff1_w_up, ff1_w_down, ff1_post_g, mix_pre_g, w_in, lam_q1, lam_k1, lam_q2, lam_k2, subln_g, w_out, mix_post_g, ff2_pre_g, ff2_w_gate, ff2_w_up, ff2_w_down, ff2_post_g, final_g):
    params = dict(ff1_pre_g=ff1_pre_g, ff1_w_gate=ff1_w_gate, ff1_w_up=ff1_w_up, ff1_w_down=ff1_w_down,
                  ff1_post_g=ff1_post_g, mix_pre_g=mix_pre_g, w_in=w_in, lam_q1=lam_q1, lam_k1=lam_k1,
                  lam_q2=lam_q2, lam_k2=lam_k2, subln_g=subln_g, w_out=w_out, mix_post_g=mix_post_g,
                  ff2_pre_g=ff2_pre_g, ff2_w_gate=ff2_w_gate, ff2_w_up=ff2_w_up, ff2_w_down=ff2_w_down,
                  ff2_post_g=ff2_post_g, final_g=final_g)
    yp, ys = x_prompt, x_sample
    rows_p, rows_s = [], []
    for l in range(DEPTH):
        lam_init = 0.8 - 0.6 * math.exp(-0.3 * l)
        w = {name: (p[l].astype(_BF16) if name in _MATRICES else p[l][None, :].astype(_F32))
             for name, p in params.items()}
        yp, rp = _encoder_layer(yp, None, w, lam_init)
        past = tuple([c[l].transpose(0, 2, 3, 1).reshape(c.shape[1], SB_WIDTH, c.shape[2])
                      for c in (cache_sb_k, cache_sb_v)]
                     + [c[l].reshape(c.shape[1], -1, c.shape[-1]) for c in (cache_diff_k, cache_diff_v)])
        ys, rs = _encoder_layer(ys, past, w, lam_init)
        rows_p.append(rp)
        rows_s.append(rs)
    stacked_p = [jnp.stack(r, axis=0) for r in zip(*rows_p)]
    stacked_s = [jnp.stack(r, axis=0) for r in zip(*rows_s)]
    return (yp, ys, *stacked_p, *stacked_s)
```

```python
import functools
import math

import jax
import jax.numpy as jnp
from jax import lax
from jax.experimental import pallas as pl
from jax.experimental.pallas import tpu as pltpu

D_MODEL = 1024
DEPTH = 1
CHUNK = 64
CHUNK_SHIFT = 6
SB_HEADS = 8
SB_HEAD_DIM = 64
DIFF_HEADS = 4
DIFF_HEAD_DIM = 64
SB_WIDTH = SB_HEADS * SB_HEAD_DIM
DIFF_WIDTH = DIFF_HEADS * 2 * DIFF_HEAD_DIM
MIX_WIDTH = SB_WIDTH + DIFF_WIDTH
IN_WIDTH = 3 * SB_WIDTH + 3 * DIFF_WIDTH
D_FF = 2816
RMS_EPS = 1e-6

LANES = 128
MXU_WIDTH = 256
VMEM_LIMIT_BYTES = 56 * 1024 * 1024
F32_EXP2_ZERO = 151.0
NEG_BIG = -1e30
LOG2E = math.log2(math.e)
DEPTH_SLOTS = 4
PAIRS_PER_TRIP = 2
SAFE_SPREAD = 90.0
TOKEN_TILE = 512
SB_STEP_TILE = 256
SB_PROMPT_TILE = 128
SB_PROMPT_GROUP = 8
SB_PROMPT_DEPTH = 3
DIFF_PROMPT_TILE = 1024
DIFF_KEY_TILE = 256

_F32 = jnp.float32
_BF16 = jnp.bfloat16


def _rms(x, g):
    return x * lax.rsqrt(jnp.mean(x * x, axis=-1, keepdims=True) + RMS_EPS) * g


def _stick_logs(z):
    log_beta = jnp.minimum(z, 0.0) - jnp.log2(1.0 + jnp.exp2(-jnp.abs(z)))
    return log_beta, log_beta - z


def _split_hi_lo(x):
    hi = lax.bitcast_convert_type(lax.bitcast_convert_type(x, jnp.uint32) & jnp.uint32(0xFFFF0000), _F32)
    return hi.astype(_BF16), (x - hi).astype(_BF16)


def _dot(a, b):
    return jnp.dot(a, b, preferred_element_type=_F32)


def _dot_nt(a, b):
    return lax.dot_general(a, b, (((1,), (1,)), ((), ())), preferred_element_type=_F32)


def _ffn_kernel(*refs, mixer, final_norm):
    refs = list(refs)
    o_ref = refs.pop()
    x = refs.pop(0)[...]
    if mixer:
        sbo_ref, do_ref, wo_ref, mix_g_ref = refs[:4]
        refs = refs[4:]
        mix = _dot(sbo_ref[...], wo_ref[:SB_WIDTH, :]) + _dot(do_ref[...], wo_ref[SB_WIDTH:, :])
        x = x + _rms(mix, mix_g_ref[...])
    pre_ref, wg_ref, wu_ref, wd_ref, post_ref = refs[:5]
    xn = _rms(x, pre_ref[...]).astype(_BF16)
    g = _dot(xn, wg_ref[...])
    u = _dot(xn, wu_ref[...])
    a = (g * jax.nn.sigmoid(g) * u).astype(_BF16)
    h = x + 0.5 * _rms(_dot(a, wd_ref[...]), post_ref[...])
    if final_norm:
        h = _rms(h, refs[5][...])
    o_ref[...] = h


def _ffn(x, pre_g, wg, wu, wd, post_g, *, mixer=None, final_g=None):
    m = x.shape[0]
    tm = min(TOKEN_TILE, m)
    assert m % tm == 0

    def resident(shape):
        return pl.BlockSpec(shape, lambda i: (0, 0), pipeline_mode=pl.Buffered(1))

    row = pl.BlockSpec((tm, D_MODEL), lambda i: (i, 0))
    vec = resident((1, D_MODEL))
    in_specs, args = [row], [x]
    if mixer is not None:
        half = pl.BlockSpec((tm, SB_WIDTH), lambda i: (i, 0))
        in_specs += [half, half, resident((MIX_WIDTH, D_MODEL)), vec]
        args += list(mixer)
    in_specs += [vec, resident((D_MODEL, D_FF)), resident((D_MODEL, D_FF)), resident((D_FF, D_MODEL)), vec]
    args += [pre_g, wg, wu, wd, post_g]
    if final_g is not None:
        in_specs.append(vec)
        args.append(final_g)
    return pl.pallas_call(
        functools.partial(_ffn_kernel, mixer=mixer is not None, final_norm=final_g is not None),
        grid=(m // tm,),
        in_specs=in_specs,
        out_specs=row,
        out_shape=jax.ShapeDtypeStruct((m, D_MODEL), _F32),
        compiler_params=pltpu.CompilerParams(
            dimension_semantics=("parallel",), vmem_limit_bytes=VMEM_LIMIT_BYTES),
        name="ffn_mix_final" if mixer is not None else "ffn",
    )(*args)


def _qkv_kernel(h_ref, g_ref, w_ref, sbq_ref, sbk_ref, sbv_ref, dq_ref, dk_ref, dv_ref,
                sbk16_ref, sbv16_ref, dk16_ref, dv16_ref, *, transposed):
    hn = _rms(h_ref[...], g_ref[...]).astype(_BF16)

    def proj(idx):
        return _dot(hn, w_ref[:, idx * SB_WIDTH:(idx + 1) * SB_WIDTH])

    sbq_ref[...] = (proj(0) * (SB_HEAD_DIM ** -0.5 * LOG2E)).astype(_BF16)
    dq_ref[...] = (proj(3) * (DIFF_HEAD_DIM ** -0.5 * LOG2E)).astype(_BF16)

    def store_rows(ref, p):
        head_dim = ref.shape[1]
        heads = p.shape[1] // head_dim
        for head in range(heads):
            ref[pl.ds(head, p.shape[0], stride=heads), :] = p[:, head * head_dim:(head + 1) * head_dim]

    for idx, full_ref, half_ref in ((4, dk_ref, dk16_ref), (5, dv_ref, dv16_ref)):
        p = proj(idx)
        store_rows(full_ref, p)
        if transposed and half_ref is dv16_ref:
            half_ref[0] = p.T.astype(_BF16)
        else:
            half_ref[...] = p.astype(_BF16)
    for idx, full_ref, half_ref in ((1, sbk_ref, sbk16_ref), (2, sbv_ref, sbv16_ref)):
        p = proj(idx)
        if transposed:
            p_t = p.T
            full_ref[0] = p_t
            if half_ref is sbv16_ref:
                half_ref[0] = p_t.astype(_BF16)
            else:
                half_ref[...] = p.astype(_BF16)
        else:
            store_rows(full_ref, p)
            half_ref[...] = p.astype(_BF16)


def _qkv(h, g, w, *, seq_len=None):
    assert SB_WIDTH == DIFF_WIDTH
    m = h.shape[0]
    tm = min(TOKEN_TILE, m)
    assert m % tm == 0
    row = pl.BlockSpec((tm, D_MODEL), lambda i: (i, 0))
    out = pl.BlockSpec((tm, SB_WIDTH), lambda i: (i, 0))
    b16 = jax.ShapeDtypeStruct((m, SB_WIDTH), _BF16)

    def cache_rows(heads, head_dim):
        return (jax.ShapeDtypeStruct((m * heads, head_dim), _F32),
                pl.BlockSpec((tm * heads, head_dim), lambda i: (i, 0)))

    sb_rows, sb_out = cache_rows(SB_HEADS, SB_HEAD_DIM)
    d_rows, d_out = cache_rows(DIFF_HEADS, 2 * DIFF_HEAD_DIM)
    v_out, v16 = out, b16
    if seq_len is not None:
        assert seq_len % tm == 0
        tiles = seq_len // tm
        v_out = pl.BlockSpec((1, SB_WIDTH, tm), lambda i: (i // tiles, 0, i % tiles))
        v16 = jax.ShapeDtypeStruct((m // seq_len, SB_WIDTH, seq_len), _BF16)
        sb_rows, sb_out = jax.ShapeDtypeStruct(v16.shape, _F32), v_out
    return pl.pallas_call(
        functools.partial(_qkv_kernel, transposed=seq_len is not None),
        grid=(m // tm,),
        in_specs=[row, pl.BlockSpec((1, D_MODEL), lambda i: (0, 0)),
                  pl.BlockSpec((D_MODEL, IN_WIDTH), lambda i: (0, 0))],
        out_specs=[out, sb_out, sb_out, out, d_out, d_out, out, v_out, out, v_out],
        out_shape=[b16, sb_rows, sb_rows, b16, d_rows, d_rows, b16, v16, b16, v16],
        compiler_params=pltpu.CompilerParams(
            dimension_semantics=("parallel",), vmem_limit_bytes=VMEM_LIMIT_BYTES),
        name="qkv",
    )(h, g, w)


def _sb_step_kernel(q_ref, kn_ref, vn_ref, kc_ref, vc_ref, o_ref, acc_ref, c_ref, *, tk):
    n = q_ref.shape[1]
    past = kc_ref.shape[2]

    def later(t):
        return (lax.broadcasted_iota(jnp.int32, (t, t), 0)
                > lax.broadcasted_iota(jnp.int32, (t, t), 1)).astype(_BF16)

    later_new, later_tile = later(n), later(tk)
    newer = (lax.broadcasted_iota(jnp.int32, (n, n), 1) < lax.broadcasted_iota(jnp.int32, (n, n), 0))

    def walk(z, c, later_t, mask):
        log_beta, log_stay = _stick_logs(z)
        if mask is not None:
            log_stay = jnp.where(mask, log_stay, 0.0)
        sums = _dot(jnp.concatenate(_split_hi_lo(log_stay), axis=0), later_t)
        w = jnp.exp2(log_beta + (sums[:n] + sums[n:] + c))
        if mask is not None:
            w = jnp.where(mask, w, 0.0)
        return c + jnp.sum(log_stay, axis=-1, keepdims=True), w.astype(_BF16)

    def head_cols(head):
        return slice(head * SB_HEAD_DIM, (head + 1) * SB_HEAD_DIM)

    def walk_cache(start):
        for head in range(SB_HEADS):
            rows = head_cols(head)
            k_t = kc_ref[0, rows, pl.ds(start, tk)].astype(_BF16)
            v_t = vc_ref[0, rows, pl.ds(start, tk)].astype(_BF16)
            c_ref[head], w = walk(_dot(q_ref[0, :, rows], k_t), c_ref[head], later_tile, None)
            acc_ref[head] += _dot_nt(w, v_t)

    for head in range(SB_HEADS):
        cols = head_cols(head)
        c_ref[head], w = walk(_dot_nt(q_ref[0, :, cols], kn_ref[0, :, cols]), jnp.zeros((n, 1), _F32),
                              later_new, newer)
        acc_ref[head] = _dot(w, vn_ref[0, :, cols])
    walk_cache(past - tk)

    def cond(state):
        j, c_max = state
        return (j >= 0) & (c_max > -F32_EXP2_ZERO)

    def body(state):
        j, _ = state
        walk_cache(pl.multiple_of(j * tk, tk))
        return j - 1, jnp.max(c_ref[...])

    lax.while_loop(cond, body, (past // tk - 2, jnp.max(c_ref[...])))
    o_ref[0] = jnp.concatenate([acc_ref[head] for head in range(SB_HEADS)], axis=-1).astype(o_ref.dtype)


def _sb_step_call(q, k_new, v_new, k_cache, v_cache):
    b, n, _ = q.shape
    past = k_cache.shape[2]
    tk = min(SB_STEP_TILE, past)
    assert past % tk == 0
    new = pl.BlockSpec((1, n, SB_WIDTH), lambda i: (i, 0, 0))
    cache = pl.BlockSpec((1, SB_WIDTH, past), lambda i: (i, 0, 0))
    return pl.pallas_call(
        functools.partial(_sb_step_kernel, tk=tk),
        grid=(b,),
        in_specs=[new, new, new, cache, cache],
        out_specs=new,
        out_shape=jax.ShapeDtypeStruct((b, n, SB_WIDTH), _BF16),
        scratch_shapes=[pltpu.VMEM((SB_HEADS, n, SB_HEAD_DIM), _F32), pltpu.VMEM((SB_HEADS, n, 1), _F32)],
        compiler_params=pltpu.CompilerParams(
            dimension_semantics=("arbitrary",), vmem_limit_bytes=VMEM_LIMIT_BYTES),
        name="sb_step",
    )(q, k_new, v_new, k_cache, v_cache)


def _alibi_slope_log2(head):
    assert 8 % DIFF_HEADS == 0
    exponent = (8 // DIFF_HEADS) * (head + 1)
    slope = lax.bitcast_convert_type(jnp.full((1, 1), (127 - exponent) << 23, jnp.int32), _F32)
    return slope * LOG2E


def _diff_step_kernel(q_ref, kn_ref, vn_ref, kc_ref, vc_ref, lq1_ref, lk1_ref, lq2_ref, lk2_ref, g_ref,
                      o_ref, *, lam_init):
    n = q_ref.shape[1]
    past = kc_ref.shape[1] // DIFF_HEADS

    def cache_head(ref, head):
        return ref[0, pl.ds(head, past, stride=DIFF_HEADS), :].astype(_BF16)

    lane = lax.broadcasted_iota(jnp.int32, (1, LANES), 1)
    q_idx_c = lax.broadcasted_iota(jnp.int32, (2 * n, past), 0) & (n - 1)
    k_idx_c = lax.broadcasted_iota(jnp.int32, (2 * n, past), 1)
    q_idx_n = lax.broadcasted_iota(jnp.int32, (2 * n, n), 0) & (n - 1)
    k_idx_n = lax.broadcasted_iota(jnp.int32, (2 * n, n), 1)
    distance_c = (past + q_idx_c - k_idx_c).astype(_F32)
    distance_n = jnp.abs(q_idx_n - k_idx_n).astype(_F32)
    visible_n = (lax.shift_right_logical(past + k_idx_n, CHUNK_SHIFT)
                 <= lax.shift_right_logical(past + q_idx_n, CHUNK_SHIFT))
    lam = (jnp.exp(jnp.sum(lq1_ref[...] * lk1_ref[...], axis=-1, keepdims=True))
           - jnp.exp(jnp.sum(lq2_ref[...] * lk2_ref[...], axis=-1, keepdims=True)) + lam_init)
    for head in range(DIFF_HEADS):
        cols = slice(head * LANES, (head + 1) * LANES)
        slope = _alibi_slope_log2(head)
        q = q_ref[0, :, cols].astype(_F32)
        q_maps = jnp.concatenate([jnp.where(lane < DIFF_HEAD_DIM, q, 0.0),
                                  jnp.where(lane >= DIFF_HEAD_DIM, q, 0.0)], axis=0).astype(_BF16)
        s_c = _dot_nt(q_maps, cache_head(kc_ref, head)) - slope * distance_c
        s_n = jnp.where(visible_n, _dot_nt(q_maps, kn_ref[0, :, cols]) - slope * distance_n, -jnp.inf)
        m = jnp.maximum(jnp.max(s_c, axis=-1, keepdims=True), jnp.max(s_n, axis=-1, keepdims=True))
        p_c = jnp.exp2(s_c - m)
        p_n = jnp.exp2(s_n - m)
        l = jnp.sum(p_c, axis=-1, keepdims=True) + jnp.sum(p_n, axis=-1, keepdims=True)
        o = (_dot(p_c.astype(_BF16), cache_head(vc_ref, head))
             + _dot(p_n.astype(_BF16), vn_ref[0, :, cols])) / l
        o = o[:n] - lam * o[n:]
        o_ref[0, :, cols] = (_rms(o, g_ref[...]) * (1.0 - lam_init)).astype(o_ref.dtype)


def _diff_step_call(q, k_new, v_new, k_cache, v_cache, lq1, lk1, lq2, lk2, subln_g, *, lam_init):
    b, n, _ = q.shape
    rows = k_cache.shape[1]
    assert 2 * DIFF_HEAD_DIM == LANES and CHUNK == 1 << CHUNK_SHIFT and n & (n - 1) == 0
    new = pl.BlockSpec((1, n, DIFF_WIDTH), lambda i: (i, 0, 0))
    cache = pl.BlockSpec((1, rows, LANES), lambda i: (i, 0, 0))
    lam_spec = pl.BlockSpec((1, DIFF_HEAD_DIM), lambda i: (0, 0))
    return pl.pallas_call(
        functools.partial(_diff_step_kernel, lam_init=lam_init),
        grid=(b,),
        in_specs=[new, new, new, cache, cache, lam_spec, lam_spec, lam_spec, lam_spec,
                  pl.BlockSpec((1, LANES), lambda i: (0, 0))],
        out_specs=new,
        out_shape=jax.ShapeDtypeStruct((b, n, DIFF_WIDTH), _BF16),
        compiler_params=pltpu.CompilerParams(
            dimension_semantics=("arbitrary",), vmem_limit_bytes=VMEM_LIMIT_BYTES),
        name="diff_step",
    )(q, k_new, v_new, k_cache, v_cache, lq1, lk1, lq2, lk2, subln_g)


def _prompt_specs(nq, t):
    assert nq % t == 0 and t & (t - 1) == 0
    q_spec = pl.BlockSpec((1, t, LANES), lambda bi, g, qi: (bi, qi, g))
    k_spec = pl.BlockSpec((1, nq, LANES), lambda bi, g, qi: (bi, 0, g))
    vt_spec = pl.BlockSpec((1, LANES, nq), lambda bi, g, qi: (bi, g, 0))
    return q_spec, k_spec, vt_spec


def _sb_prompt_parts(q_ref, k_ref, vt_ref, o_ref, acc_ref, c_ref, z_ref, lb_ref, hl_ref, w_ref, *,
                     t, group, depth):
    assert LANES == 2 * SB_HEAD_DIM
    strips = 2 * t // LANES
    qi = pl.program_id(2)
    lane = lax.broadcasted_iota(jnp.int32, (1, LANES), 1)
    key = lax.broadcasted_iota(jnp.int32, (t, LANES), 0)
    qry = lax.broadcasted_iota(jnp.int32, (t, LANES), 1)
    later = (lax.broadcasted_iota(jnp.int32, (t, t), 1)
             > lax.broadcasted_iota(jnp.int32, (t, t), 0)).astype(_BF16)

    def tile_queries(g):
        q = q_ref[0, g * t:(g + 1) * t, :].astype(_F32)
        return jnp.concatenate([jnp.where(lane < SB_HEAD_DIM, q, 0.0),
                                jnp.where(lane >= SB_HEAD_DIM, q, 0.0)], axis=0).astype(_BF16)

    q_heads = [tile_queries(g) for g in range(group)]

    def stage_scores(g, j, buf):
        z = _dot_nt(k_ref[0, pl.ds(pl.multiple_of(j * t, t), t), :], q_heads[g])
        for s in range(strips):
            z_ref[buf, s] = z[:, s * LANES:(s + 1) * LANES]

    def tile_mask(own, s):
        return key < ((s * LANES + qry) & (t - 1)) if own is True else own

    def stage_logs(c, own, buf):
        c_new = []
        for s in range(strips):
            lb_ref[buf, s], log_stay = _stick_logs(z_ref[buf, s])
            if own is not False:
                log_stay = jnp.where(tile_mask(own, s), log_stay, 0.0)
            hl_ref[buf, s] = jnp.concatenate(_split_hi_lo(log_stay), axis=1)
            c_new.append(c[:, s * LANES:(s + 1) * LANES] + jnp.sum(log_stay, axis=0, keepdims=True))
        return jnp.concatenate(c_new, axis=1)

    def stage_weights(c, own, buf):
        for s in range(strips):
            sums = _dot(later, hl_ref[buf, s])
            w = jnp.exp2(lb_ref[buf, s] + (sums[:, :LANES] + sums[:, LANES:] + c[:, s * LANES:(s + 1) * LANES]))
            if own is not False:
                w = jnp.where(tile_mask(own, s), w, 0.0)
            head, part = divmod(s, t // LANES)
            w_ref[buf, head, :, part * LANES:(part + 1) * LANES] = w.astype(_BF16)

    def stage_values(j, buf):
        vt = vt_ref[0, :, pl.ds(pl.multiple_of(j * t, t), t)]
        return jnp.concatenate([_dot(vt[:SB_HEAD_DIM], w_ref[buf, 0]), _dot(vt[SB_HEAD_DIM:], w_ref[buf, 1])],
                               axis=0)

    def walk_tile(g, j, c, own, buf):
        stage_scores(g, j, buf)
        c_new = stage_logs(c, own, buf)
        stage_weights(c, own, buf)
        return c_new, stage_values(j, buf)

    def first_tiles(between=()):
        between = list(between) + [lambda: None] * 4
        walks = []
        for g in range(group):
            for d in range(depth):
                j = qi * group + g - d
                own = True if d == 0 else (j >= 0 if d > g else False)
                walks.append((g, jnp.maximum(j, 0), own, depth * g + d))
        for g, j, own, buf in walks:
            stage_scores(g, j, buf)
        between[0]()
        c_in = {}
        for g, j, own, buf in walks:
            c_in[buf] = jnp.zeros((1, 2 * t), _F32) if own is True else c_ref[g]
            c_ref[g] = stage_logs(c_in[buf], own, buf)
        between[1]()
        for g, j, own, buf in walks:
            stage_weights(c_in[buf], own, buf)
        between[2]()
        for g, j, own, buf in walks:
            o = stage_values(j, buf)
            acc_ref[g] = o if own is True else acc_ref[g] + o
        between[3]()

    assert group >= depth - 1

    def older_tile(g, step):
        return qi * group + g - depth - step

    def any_active(step):
        flags = [(older_tile(g, step) >= 0) & (jnp.max(c_ref[g]) > -F32_EXP2_ZERO) for g in range(group)]
        return functools.reduce(jnp.logical_or, flags)

    def walk_older(state):
        step, _ = state
        for g in range(group):
            @pl.when((older_tile(g, step) >= 0) & (jnp.max(c_ref[g]) > -F32_EXP2_ZERO))
            def _(g=g):
                c, o = walk_tile(g, older_tile(g, step), c_ref[g], False, depth * g)
                c_ref[g] = c
                acc_ref[g] += o
        return step + 1, any_active(step + 1)

    def finish():
        lax.while_loop(lambda state: state[1], walk_older, (0, any_active(0)))
        for g in range(group):
            o_ref[0, g * t:(g + 1) * t, :] = acc_ref[g].T.astype(o_ref.dtype)

    return first_tiles, finish


def _sb_prompt_scratch(t, group, depth):
    sets = depth * group
    strips = 2 * t // LANES
    return [pltpu.VMEM((group, LANES, t), _F32), pltpu.VMEM((group, 1, 2 * t), _F32),
            pltpu.VMEM((sets, strips, t, LANES), _F32), pltpu.VMEM((sets, strips, t, LANES), _F32),
            pltpu.VMEM((sets, strips, t, 2 * LANES), _BF16), pltpu.VMEM((sets, 2, t, t), _BF16)]


def _diff_prompt_kernel(q_ref, k_ref, vt_ref, lq1_ref, lk1_ref, lq2_ref, lk2_ref, g_ref, o_ref,
                        m_ref, l_ref, acc_ref, pen_ref, kaug_ref, knorm_ref, *p_refs, tq, tk, lam_init,
                        side_work=None):
    own = tq // tk
    ahead_pairs = own // 2
    assert tq == own * tk and own % 2 == 0 and own <= DEPTH_SLOTS
    strips = 2 * tq // MXU_WIDTH
    head = pl.program_id(1)
    qi = pl.program_id(2)
    n_past = qi * own
    lane = lax.broadcasted_iota(jnp.int32, (1, LANES), 1)
    slope = _alibi_slope_log2(head)

    @pl.when(qi == 0)
    def _():
        key = lax.broadcasted_iota(jnp.int32, (tk, 2 * tq), 0)
        qry = lax.broadcasted_iota(jnp.int32, (tk, 2 * tq), 1) & (tq - 1)
        for d in range(own):
            k_pos = d * tk + key
            visible = lax.shift_right_logical(k_pos, CHUNK_SHIFT) <= lax.shift_right_logical(qry, CHUNK_SHIFT)
            ahead = -2.0 * slope * jnp.maximum(k_pos - qry, 0).astype(_F32)
            pen_ref[d] = jnp.where(visible, ahead, -jnp.inf)
        row = lax.broadcasted_iota(jnp.int32, (tk, LANES), 0)
        col = lax.broadcasted_iota(jnp.int32, (tk, LANES), 1)
        kaug_ref[...] = jnp.where(col < 3, row, jnp.where((col >= 6) & (col < 9), 1, 0)).astype(_F32)
        k_sq = jnp.square(k_ref[0].astype(_F32))
        for c in range(2):
            in_map = (lane >= c * DIFF_HEAD_DIM) & (lane < (c + 1) * DIFF_HEAD_DIM)
            norm_sq = jnp.sum(jnp.where(in_map, k_sq, 0.0), axis=1, keepdims=True)
            knorm_ref[c] = jnp.broadcast_to(jnp.sqrt(jnp.max(norm_sq, axis=0, keepdims=True)), (1, LANES))

    def split3(x):
        hi = x.astype(_BF16).astype(_F32)
        mid = (x - hi).astype(_BF16).astype(_F32)
        return [hi, mid, x - hi - mid]

    def augmented_queries(columns):
        blocks = []
        for c in range(2):
            aug = jnp.zeros((tq, LANES), _F32)
            for i, value in enumerate(columns[c]):
                aug = jnp.where(lane == i, value, aug)
            in_map = (lane >= c * DIFF_HEAD_DIM) & (lane < (c + 1) * DIFF_HEAD_DIM)
            blocks.append(jnp.concatenate([jnp.where(in_map, q, 0.0), aug], axis=1))
        return jnp.concatenate(blocks, axis=0).astype(_BF16)

    q = q_ref[0].astype(_F32)
    l_ref[...] = jnp.zeros_like(l_ref)
    acc_ref[...] = jnp.zeros_like(acc_ref)

    def slot_tile(n):
        return jnp.where(n < own, n_past + n, n_past - 1 - (n - own))

    def tile_start(n):
        return pl.multiple_of(jnp.clip(slot_tile(n), 0, n_past + own - 1) * tk, tk)

    def pair_buffer(pair):
        return (2 * pair) % DEPTH_SLOTS

    q_sq = jnp.square(q)
    reach = jnp.zeros((1, 1), _F32)
    bounds = []
    for c in range(2):
        in_map = (lane >= c * DIFF_HEAD_DIM) & (lane < (c + 1) * DIFF_HEAD_DIM)
        q_norm = jnp.sqrt(jnp.max(jnp.sum(jnp.where(in_map, q_sq, 0.0), axis=1, keepdims=True),
                                  axis=0, keepdims=True))
        bounds.append(q_norm * knorm_ref[c][:, :1])
        reach = jnp.maximum(reach, (2.0 * bounds[c] + F32_EXP2_ZERO) / slope)
    tiles_in_reach = jnp.minimum(jnp.floor((reach - 1.0) / tk) + 1.0, float(1 << 20))
    n_walk = jnp.minimum(jnp.max(tiles_in_reach).astype(jnp.int32), n_past)
    pairs = (own + n_walk + 1) // 2

    def online_walk():
        if side_work is not None:
            side_work()
        queries = augmented_queries([split3(slope)] * 2)
        key = lax.broadcasted_iota(jnp.int32, (tk, 2 * tq), 0)
        qry = lax.broadcasted_iota(jnp.int32, (tk, 2 * tq), 1) & (tq - 1)
        m_ref[...] = jnp.full_like(m_ref, NEG_BIG)

        def slot(n, carry):
            start = tile_start(n)
            k_pos = (slot_tile(n) - n_past) * tk + key
            keys = jnp.concatenate([k_ref[0, pl.ds(start, tk), :], kaug_ref[...].astype(_BF16)], axis=1)
            s = _dot_nt(keys, queries) + slope * (qry - key - jnp.abs(qry - k_pos)).astype(_F32)
            visible = (lax.shift_right_arithmetic(k_pos, CHUNK_SHIFT)
                       <= lax.shift_right_logical(qry, CHUNK_SHIFT))
            s = jnp.where(visible, s, -jnp.inf)
            m_old = m_ref[...]
            m_new = jnp.maximum(m_old, jnp.max(s, axis=0, keepdims=True))
            alpha = jnp.exp2(m_old - m_new)
            p = jnp.exp2(s - m_new)
            l_ref[...] = alpha * l_ref[...] + jnp.sum(p, axis=0, keepdims=True)
            m_ref[...] = m_new
            pv = _dot(vt_ref[0, :, pl.ds(start, tk)], p.astype(_BF16))
            for c in range(strips):
                cols = slice(c * MXU_WIDTH, (c + 1) * MXU_WIDTH)
                acc_ref[c] = alpha[:, cols] * acc_ref[c] + pv[:, cols]
            return carry

        lax.fori_loop(0, own + n_walk, slot, 0)

    def fixed_walk():
        q_pos = (qi * tq + lax.broadcasted_iota(jnp.int32, (tq, 1), 0)).astype(_F32)
        q_aug = augmented_queries([split3(slope) + split3(slope * tk) + split3(-slope * q_pos - bounds[c])
                                   for c in range(2)])

        def tile_keys(n):
            j = jnp.where(n < own + n_walk, slot_tile(n), -(1 << 20)).astype(_F32)
            aug = jnp.where((lane >= 3) & (lane < 6), j, kaug_ref[...]).astype(_BF16)
            return jnp.concatenate([k_ref[0, pl.ds(tile_start(n), tk), :], aug], axis=1)

        def weights(n, buf, own_slot):
            keys = tile_keys(n)
            for c in range(strips):
                strip = slice(c * MXU_WIDTH, (c + 1) * MXU_WIDTH)
                if own_slot and tk == MXU_WIDTH and c % (strips // 2) < n:
                    p_refs[buf][c] = jnp.zeros((tk, MXU_WIDTH), _BF16)
                    continue
                e = _dot_nt(keys, q_aug[strip])
                if own_slot:
                    e = e + pen_ref[n, :, strip]
                p = jnp.exp2(e)
                l_ref[:, strip] += jnp.sum(p, axis=0, keepdims=True)
                p_refs[buf][c] = p.astype(_BF16)

        def values(n, buf):
            vt = vt_ref[0, :, pl.ds(tile_start(n), tk)]
            for c in range(strips):
                acc_ref[c] += _dot(vt, p_refs[buf][c])

        def pair_step(t, pair):
            for d in range(2):
                values(2 * t + d, pair_buffer(pair) + d)
            for d in range(2):
                weights(2 * (t + ahead_pairs) + d, pair_buffer(pair + ahead_pairs) + d, False)

        def trip(i, carry):
            for pair in range(PAIRS_PER_TRIP):
                pair_step(PAIRS_PER_TRIP * i + pair, pair)
            return carry

        own_weights = [functools.partial(weights, n, n, True) for n in range(own)]
        if side_work is None:
            for emit in own_weights:
                emit()
        else:
            side_work(own_weights)
        steps = pairs - ahead_pairs
        trips = steps // PAIRS_PER_TRIP
        lax.fori_loop(0, trips, trip, 0)
        for pair in range(PAIRS_PER_TRIP - 1):
            @pl.when(steps % PAIRS_PER_TRIP > pair)
            def _(pair=pair):
                pair_step(PAIRS_PER_TRIP * trips + pair, pair)

        for parity in range(2):
            @pl.when(steps % 2 == parity)
            def _(parity=parity):
                for r in range(ahead_pairs):
                    for d in range(2):
                        values(2 * (steps + r) + d, pair_buffer(parity + r) + d)

    spread = 2.0 * jnp.max(jnp.maximum(bounds[0], bounds[1]))
    pl.when(spread < SAFE_SPREAD)(fixed_walk)
    pl.when(spread >= SAFE_SPREAD)(online_walk)

    lam = (jnp.exp(jnp.sum(lq1_ref[...] * lk1_ref[...], axis=-1, keepdims=True))
           - jnp.exp(jnp.sum(lq2_ref[...] * lk2_ref[...], axis=-1, keepdims=True)) + lam_init)
    o = jnp.concatenate([acc_ref[c] for c in range(2 * tq // MXU_WIDTH)], axis=1) / l_ref[...]
    o = o[:, :tq] - lam * o[:, tq:]
    o = o * lax.rsqrt(jnp.mean(o * o, axis=0, keepdims=True) + RMS_EPS)
    o_ref[0] = (o.T * g_ref[...] * (1.0 - lam_init)).astype(o_ref.dtype)


N_SB_SCRATCH = 6


def _prompt_mixer_kernel(sbq_ref, sbk_ref, sbvt_ref, dq_ref, dk_ref, dvt_ref, lq1_ref, lk1_ref, lq2_ref,
                         lk2_ref, g_ref, sbo_ref, do_ref, *scratch, sb_params, diff_params):
    first_tiles, finish = _sb_prompt_parts(sbq_ref, sbk_ref, sbvt_ref, sbo_ref, *scratch[:N_SB_SCRATCH],
                                           **sb_params)
    _diff_prompt_kernel(dq_ref, dk_ref, dvt_ref, lq1_ref, lk1_ref, lq2_ref, lk2_ref, g_ref, do_ref,
                        *scratch[N_SB_SCRATCH:], side_work=first_tiles, **diff_params)
    finish()


def _prompt_mixer_call(sbq, sbk, sbvt, dq, dk, dvt, lq1, lk1, lq2, lk2, subln_g, *, lam_init):
    b, nq, _ = dq.shape
    tq, tk = DIFF_PROMPT_TILE, DIFF_KEY_TILE
    sb_params = dict(t=SB_PROMPT_TILE, group=SB_PROMPT_GROUP, depth=SB_PROMPT_DEPTH)
    assert 2 * DIFF_HEAD_DIM == LANES and SB_WIDTH // LANES == DIFF_HEADS
    assert SB_PROMPT_TILE * SB_PROMPT_GROUP == tq
    assert tk % CHUNK == 0 and CHUNK == 1 << CHUNK_SHIFT and tk <= 256 and nq // tk <= 256
    q_spec, k_spec, vt_spec = _prompt_specs(nq, tq)
    lam_spec = pl.BlockSpec((1, DIFF_HEAD_DIM), lambda bi, g, qi: (0, 0))
    stat = pltpu.VMEM((1, 2 * tq), _F32)
    sb_scratch = _sb_prompt_scratch(**sb_params)
    assert len(sb_scratch) == N_SB_SCRATCH
    out = jax.ShapeDtypeStruct((b, nq, DIFF_WIDTH), _BF16)
    return pl.pallas_call(
        functools.partial(_prompt_mixer_kernel, sb_params=sb_params,
                          diff_params=dict(tq=tq, tk=tk, lam_init=lam_init)),
        grid=(b, DIFF_HEADS, nq // tq),
        in_specs=[q_spec, k_spec, vt_spec, q_spec, k_spec, vt_spec, lam_spec, lam_spec, lam_spec, lam_spec,
                  pl.BlockSpec((1, LANES), lambda bi, g, qi: (0, 0))],
        out_specs=[q_spec, q_spec],
        out_shape=[out, out],
        scratch_shapes=sb_scratch
        + [stat, stat, pltpu.VMEM((2 * tq // MXU_WIDTH, LANES, MXU_WIDTH), _F32),
           pltpu.VMEM((tq // tk, tk, 2 * tq), _F32), pltpu.VMEM((tk, LANES), _F32),
           pltpu.VMEM((2, 1, LANES), _F32)]
        + [pltpu.VMEM((2 * tq // MXU_WIDTH, tk, MXU_WIDTH), _BF16)] * DEPTH_SLOTS,
        compiler_params=pltpu.CompilerParams(
            dimension_semantics=("arbitrary", "arbitrary", "arbitrary"),
            vmem_limit_bytes=VMEM_LIMIT_BYTES),
        name="prompt_mixer",
    )(sbq, sbk, sbvt, dq, dk, dvt, lq1, lk1, lq2, lk2, subln_g)


def _encoder_layer(x, past, w, lam_init):
    b, n, _ = x.shape
    m = b * n
    h = _ffn(x.reshape(m, D_MODEL), w["ff1_pre_g"], w["ff1_w_gate"], w["ff1_w_up"], w["ff1_w_down"],
             w["ff1_post_g"])
    lam_args = (w["lam_q1"], w["lam_k1"], w["lam_q2"], w["lam_k2"], w["subln_g"])

    def seq(a):
        return a.reshape(b, n, a.shape[-1])

    if past is None:
        sbq, sbk, sbv, dq, dk, dv, sbk16, sbvt16, dk16, dvt16 = _qkv(h, w["mix_pre_g"], w["w_in"], seq_len=n)
        sb_o, d_o = _prompt_mixer_call(seq(sbq), seq(sbk16), sbvt16, seq(dq), seq(dk16), dvt16, *lam_args,
                                       lam_init=lam_init)
        sbk, sbv = (a.reshape(b, SB_HEADS, SB_HEAD_DIM, n).transpose(0, 3, 1, 2) for a in (sbk, sbv))
    else:
        sbq, sbk, sbv, dq, dk, dv, sbk16, sbv16, dk16, dv16 = _qkv(h, w["mix_pre_g"], w["w_in"])
        sb_o = _sb_step_call(seq(sbq), seq(sbk16), seq(sbv16), past[0], past[1])
        d_o = _diff_step_call(seq(dq), seq(dk16), seq(dv16), past[2], past[3], *lam_args, lam_init=lam_init)
    y = _ffn(h, w["ff2_pre_g"], w["ff2_w_gate"], w["ff2_w_up"], w["ff2_w_down"], w["ff2_post_g"],
             mixer=(sb_o.reshape(m, SB_WIDTH), d_o.reshape(m, DIFF_WIDTH), w["w_out"], w["mix_post_g"]),
             final_g=w["final_g"])
    rows = (sbk.reshape(b, n, SB_HEADS, SB_HEAD_DIM), sbv.reshape(b, n, SB_HEADS, SB_HEAD_DIM),
            dk.reshape(b, n, DIFF_HEADS, 2 * DIFF_HEAD_DIM), dv.reshape(b, n, DIFF_HEADS, 2 * DIFF_HEAD_DIM))
    return y.reshape(b, n, D_MODEL), rows


_MATRICES = ("ff1_w_gate", "ff1_w_up", "ff1_w_down", "w_in", "w_out", "ff2_w_gate", "ff2_w_up", "ff2_w_down")


def kernel(x_prompt, x_sample, cache_sb_k, cache_sb_v, cache_diff_k, cache_diff_v, ff1_pre_g, ff1_w_gate, ff1_w_up, ff1_w_down, ff1_post_g, mix_pre_g, w_in, lam_q1, lam_k1, lam_q2, lam_k2, subln_g, w_out, mix_post_g, ff2_pre_g, ff2_w_gate, ff2_w_up, ff2_w_down, ff2_post_g, final_g):
    params = dict(ff1_pre_g=ff1_pre_g, ff1_w_gate=ff1_w_gate, ff1_w_up=ff1_w_up, ff1_w_down=ff1_w_down,
                  ff1_post_g=ff1_post_g, mix_pre_g=mix_pre_g, w_in=w_in, lam_q1=lam_q1, lam_k1=lam_k1,
                  lam_q2=lam_q2, lam_k2=lam_k2, subln_g=subln_g, w_out=w_out, mix_post_g=mix_post_g,
                  ff2_pre_g=ff2_pre_g, ff2_w_gate=ff2_w_gate, ff2_w_up=ff2_w_up, ff2_w_down=ff2_w_down,
                  ff2_post_g=ff2_post_g, final_g=final_g)
    yp, ys = x_prompt, x_sample
    rows_p, rows_s = [], []
    for l in range(DEPTH):
        lam_init = 0.8 - 0.6 * math.exp(-0.3 * l)
        w = {name: (p[l].astype(_BF16) if name in _MATRICES else p[l][None, :].astype(_F32))
             for name, p in params.items()}
        yp, rp = _encoder_layer(yp, None, w, lam_init)
        past = tuple([c[l].transpose(0, 2, 3, 1).reshape(c.shape[1], SB_WIDTH, c.shape[2])
                      for c in (cache_sb_k, cache_sb_v)]
                     + [c[l].reshape(c.shape[1], -1, c.shape[-1]) for c in (cache_diff_k, cache_diff_v)])
        ys, rs = _encoder_layer(ys, past, w, lam_init)
        rows_p.append(rp)
        rows_s.append(rs)
    stacked_p = [jnp.stack(r, axis=0) for r in zip(*rows_p)]
    stacked_s = [jnp.stack(r, axis=0) for r in zip(*rows_s)]
    return (yp, ys, *stacked_p, *stacked_s)
```

```python
import functools
import math

import jax
import jax.numpy as jnp
from jax import lax
from jax.experimental import pallas as pl
from jax.experimental.pallas import tpu as pltpu

D_MODEL = 1024
DEPTH = 1
CHUNK = 64
CHUNK_SHIFT = 6
SB_HEADS = 8
SB_HEAD_DIM = 64
DIFF_HEADS = 4
DIFF_HEAD_DIM = 64
SB_WIDTH = SB_HEADS * SB_HEAD_DIM
DIFF_WIDTH = DIFF_HEADS * 2 * DIFF_HEAD_DIM
MIX_WIDTH = SB_WIDTH + DIFF_WIDTH
IN_WIDTH = 3 * SB_WIDTH + 3 * DIFF_WIDTH
D_FF = 2816
RMS_EPS = 1e-6

LANES = 128
MXU_WIDTH = 256
VMEM_LIMIT_BYTES = 56 * 1024 * 1024
F32_EXP2_ZERO = 151.0
NEG_BIG = -1e30
LOG2E = math.log2(math.e)
DEPTH_SLOTS = 4
PAIRS_PER_TRIP = 2
SAFE_SPREAD = 90.0
TOKEN_TILE = 512
SB_STEP_TILE = 256
SB_PROMPT_TILE = 128
SB_PROMPT_GROUP = 8
SB_PROMPT_DEPTH = 3
DIFF_PROMPT_TILE = 1024
DIFF_KEY_TILE = 256

_F32 = jnp.float32
_BF16 = jnp.bfloat16


def _rms(x, g):
    return x * lax.rsqrt(jnp.mean(x * x, axis=-1, keepdims=True) + RMS_EPS) * g


def _stick_logs(z):
    log_beta = jnp.minimum(z, 0.0) - jnp.log2(1.0 + jnp.exp2(-jnp.abs(z)))
    return log_beta, log_beta - z


def _split_hi_lo(x):
    hi = lax.bitcast_convert_type(lax.bitcast_convert_type(x, jnp.uint32) & jnp.uint32(0xFFFF0000), _F32)
    return hi.astype(_BF16), (x - hi).astype(_BF16)


def _dot(a, b):
    return jnp.dot(a, b, preferred_element_type=_F32)


def _dot_nt(a, b):
    return lax.dot_general(a, b, (((1,), (1,)), ((), ())), preferred_element_type=_F32)


def _ffn_kernel(*refs, mixer, final_norm):
    refs = list(refs)
    o_ref = refs.pop()
    x = refs.pop(0)[...]
    if mixer:
        sbo_ref, do_ref, wo_ref, mix_g_ref = refs[:4]
        refs = refs[4:]
        mix = _dot(sbo_ref[...], wo_ref[:SB_WIDTH, :]) + _dot(do_ref[...], wo_ref[SB_WIDTH:, :])
        x = x + _rms(mix, mix_g_ref[...])
    pre_ref, wg_ref, wu_ref, wd_ref, post_ref = refs[:5]
    xn = _rms(x, pre_ref[...]).astype(_BF16)
    g = _dot(xn, wg_ref[...])
    u = _dot(xn, wu_ref[...])
    a = (g * jax.nn.sigmoid(g) * u).astype(_BF16)
    h = x + 0.5 * _rms(_dot(a, wd_ref[...]), post_ref[...])
    if final_norm:
        h = _rms(h, refs[5][...])
    o_ref[...] = h


def _ffn(x, pre_g, wg, wu, wd, post_g, *, mixer=None, final_g=None):
    m = x.shape[0]
    tm = min(TOKEN_TILE, m)
    assert m % tm == 0

    def resident(shape):
        return pl.BlockSpec(shape, lambda i: (0, 0), pipeline_mode=pl.Buffered(1))

    row = pl.BlockSpec((tm, D_MODEL), lambda i: (i, 0))
    vec = resident((1, D_MODEL))
    in_specs, args = [row], [x]
    if mixer is not None:
        half = pl.BlockSpec((tm, SB_WIDTH), lambda i: (i, 0))
        in_specs += [half, half, resident((MIX_WIDTH, D_MODEL)), vec]
        args += list(mixer)
    in_specs += [vec, resident((D_MODEL, D_FF)), resident((D_MODEL, D_FF)), resident((D_FF, D_MODEL)), vec]
    args += [pre_g, wg, wu, wd, post_g]
    if final_g is not None:
        in_specs.append(vec)
        args.append(final_g)
    return pl.pallas_call(
        functools.partial(_ffn_kernel, mixer=mixer is not None, final_norm=final_g is not None),
        grid=(m // tm,),
        in_specs=in_specs,
        out_specs=row,
        out_shape=jax.ShapeDtypeStruct((m, D_MODEL), _F32),
        compiler_params=pltpu.CompilerParams(
            dimension_semantics=("parallel",), vmem_limit_bytes=VMEM_LIMIT_BYTES),
        name="ffn_mix_final" if mixer is not None else "ffn",
    )(*args)


def _qkv_kernel(h_ref, g_ref, w_ref, sbq_ref, sbk_ref, sbv_ref, dq_ref, dk_ref, dv_ref,
                sbk16_ref, sbv16_ref, dk16_ref, dv16_ref, *, transposed):
    hn = _rms(h_ref[...], g_ref[...]).astype(_BF16)

    def proj(idx):
        return _dot(hn, w_ref[:, idx * SB_WIDTH:(idx + 1) * SB_WIDTH])

    sbq_ref[...] = (proj(0) * (SB_HEAD_DIM ** -0.5 * LOG2E)).astype(_BF16)
    dq_ref[...] = (proj(3) * (DIFF_HEAD_DIM ** -0.5 * LOG2E)).astype(_BF16)

    def store_rows(ref, p):
        head_dim = ref.shape[1]
        heads = p.shape[1] // head_dim
        for head in range(heads):
            ref[pl.ds(head, p.shape[0], stride=heads), :] = p[:, head * head_dim:(head + 1) * head_dim]

    for idx, full_ref, half_ref in ((4, dk_ref, dk16_ref), (5, dv_ref, dv16_ref)):
        p = proj(idx)
        store_rows(full_ref, p)
        if transposed and half_ref is dv16_ref:
            half_ref[0] = p.T.astype(_BF16)
        else:
            half_ref[...] = p.astype(_BF16)
    for idx, full_ref, half_ref in ((1, sbk_ref, sbk16_ref), (2, sbv_ref, sbv16_ref)):
        p = proj(idx)
        if transposed:
            p_t = p.T
            full_ref[0] = p_t
            if half_ref is sbv16_ref:
                half_ref[0] = p_t.astype(_BF16)
            else:
                half_ref[...] = p.astype(_BF16)
        else:
            store_rows(full_ref, p)
            half_ref[...] = p.astype(_BF16)


def _qkv(h, g, w, *, seq_len=None):
    assert SB_WIDTH == DIFF_WIDTH
    m = h.shape[0]
    tm = min(TOKEN_TILE, m)
    assert m % tm == 0
    row = pl.BlockSpec((tm, D_MODEL), lambda i: (i, 0))
    out = pl.BlockSpec((tm, SB_WIDTH), lambda i: (i, 0))
    b16 = jax.ShapeDtypeStruct((m, SB_WIDTH), _BF16)

    def cache_rows(heads, head_dim):
        return (jax.ShapeDtypeStruct((m * heads, head_dim), _F32),
                pl.BlockSpec((tm * heads, head_dim), lambda i: (i, 0)))

    sb_rows, sb_out = cache_rows(SB_HEADS, SB_HEAD_DIM)
    d_rows, d_out = cache_rows(DIFF_HEADS, 2 * DIFF_HEAD_DIM)
    v_out, v16 = out, b16
    if seq_len is not None:
        assert seq_len % tm == 0
        tiles = seq_len // tm
        v_out = pl.BlockSpec((1, SB_WIDTH, tm), lambda i: (i // tiles, 0, i % tiles))
        v16 = jax.ShapeDtypeStruct((m // seq_len, SB_WIDTH, seq_len), _BF16)
        sb_rows, sb_out = jax.ShapeDtypeStruct(v16.shape, _F32), v_out
    return pl.pallas_call(
        functools.partial(_qkv_kernel, transposed=seq_len is not None),
        grid=(m // tm,),
        in_specs=[row, pl.BlockSpec((1, D_MODEL), lambda i: (0, 0)),
                  pl.BlockSpec((D_MODEL, IN_WIDTH), lambda i: (0, 0))],
        out_specs=[out, sb_out, sb_out, out, d_out, d_out, out, v_out, out, v_out],
        out_shape=[b16, sb_rows, sb_rows, b16, d_rows, d_rows, b16, v16, b16, v16],
        compiler_params=pltpu.CompilerParams(
            dimension_semantics=("parallel",), vmem_limit_bytes=VMEM_LIMIT_BYTES),
        name="qkv",
    )(h, g, w)


def _sb_step_kernel(q_ref, kn_ref, vn_ref, kc_ref, vc_ref, o_ref, acc_ref, c_ref, *, tk):
    n = q_ref.shape[1]
    past = kc_ref.shape[2]

    def later(t):
        return (lax.broadcasted_iota(jnp.int32, (t, t), 0)
                > lax.broadcasted_iota(jnp.int32, (t, t), 1)).astype(_BF16)

    later_new, later_tile = later(n), later(tk)
    newer = (lax.broadcasted_iota(jnp.int32, (n, n), 1) < lax.broadcasted_iota(jnp.int32, (n, n), 0))

    def walk(z, c, later_t, mask):
        log_beta, log_stay = _stick_logs(z)
        if mask is not None:
            log_stay = jnp.where(mask, log_stay, 0.0)
        sums = _dot(jnp.concatenate(_split_hi_lo(log_stay), axis=0), later_t)
        w = jnp.exp2(log_beta + (sums[:n] + sums[n:] + c))
        if mask is not None:
            w = jnp.where(mask, w, 0.0)
        return c + jnp.sum(log_stay, axis=-1, keepdims=True), w.astype(_BF16)

    def head_cols(head):
        return slice(head * SB_HEAD_DIM, (head + 1) * SB_HEAD_DIM)

    def walk_cache(start):
        for head in range(SB_HEADS):
            rows = head_cols(head)
            k_t = kc_ref[0, rows, pl.ds(start, tk)].astype(_BF16)
            v_t = vc_ref[0, rows, pl.ds(start, tk)].astype(_BF16)
            c_ref[head], w = walk(_dot(q_ref[0, :, rows], k_t), c_ref[head], later_tile, None)
            acc_ref[head] += _dot_nt(w, v_t)

    for head in range(SB_HEADS):
        cols = head_cols(head)
        c_ref[head], w = walk(_dot_nt(q_ref[0, :, cols], kn_ref[0, :, cols]), jnp.zeros((n, 1), _F32),
                              later_new, newer)
        acc_ref[head] = _dot(w, vn_ref[0, :, cols])
    walk_cache(past - tk)

    def cond(state):
        j, c_max = state
        return (j >= 0) & (c_max > -F32_EXP2_ZERO)

    def body(state):
        j, _ = state
        walk_cache(pl.multiple_of(j * tk, tk))
        return j - 1, jnp.max(c_ref[...])

    lax.while_loop(cond, body, (past // tk - 2, jnp.max(c_ref[...])))
    o_ref[0] = jnp.concatenate([acc_ref[head] for head in range(SB_HEADS)], axis=-1).astype(o_ref.dtype)


def _sb_step_call(q, k_new, v_new, k_cache, v_cache):
    b, n, _ = q.shape
    past = k_cache.shape[2]
    tk = min(SB_STEP_TILE, past)
    assert past % tk == 0
    new = pl.BlockSpec((1, n, SB_WIDTH), lambda i: (i, 0, 0))
    cache = pl.BlockSpec((1, SB_WIDTH, past), lambda i: (i, 0, 0))
    return pl.pallas_call(
        functools.partial(_sb_step_kernel, tk=tk),
        grid=(b,),
        in_specs=[new, new, new, cache, cache],
        out_specs=new,
        out_shape=jax.ShapeDtypeStruct((b, n, SB_WIDTH), _BF16),
        scratch_shapes=[pltpu.VMEM((SB_HEADS, n, SB_HEAD_DIM), _F32), pltpu.VMEM((SB_HEADS, n, 1), _F32)],
        compiler_params=pltpu.CompilerParams(
            dimension_semantics=("arbitrary",), vmem_limit_bytes=VMEM_LIMIT_BYTES),
        name="sb_step",
    )(q, k_new, v_new, k_cache, v_cache)


def _alibi_slope_log2(head):
    assert 8 % DIFF_HEADS == 0
    exponent = (8 // DIFF_HEADS) * (head + 1)
    slope = lax.bitcast_convert_type(jnp.full((1, 1), (127 - exponent) << 23, jnp.int32), _F32)
    return slope * LOG2E


def _diff_step_kernel(q_ref, kn_ref, vn_ref, kc_ref, vc_ref, lq1_ref, lk1_ref, lq2_ref, lk2_ref, g_ref,
                      o_ref, *, lam_init):
    n = q_ref.shape[1]
    past = kc_ref.shape[1] // DIFF_HEADS

    def cache_head(ref, head):
        return ref[0, pl.ds(head, past, stride=DIFF_HEADS), :].astype(_BF16)

    lane = lax.broadcasted_iota(jnp.int32, (1, LANES), 1)
    q_idx_c = lax.broadcasted_iota(jnp.int32, (2 * n, past), 0) & (n - 1)
    k_idx_c = lax.broadcasted_iota(jnp.int32, (2 * n, past), 1)
    q_idx_n = lax.broadcasted_iota(jnp.int32, (2 * n, n), 0) & (n - 1)
    k_idx_n = lax.broadcasted_iota(jnp.int32, (2 * n, n), 1)
    distance_c = (past + q_idx_c - k_idx_c).astype(_F32)
    distance_n = jnp.abs(q_idx_n - k_idx_n).astype(_F32)
    visible_n = (lax.shift_right_logical(past + k_idx_n, CHUNK_SHIFT)
                 <= lax.shift_right_logical(past + q_idx_n, CHUNK_SHIFT))
    lam = (jnp.exp(jnp.sum(lq1_ref[...] * lk1_ref[...], axis=-1, keepdims=True))
           - jnp.exp(jnp.sum(lq2_ref[...] * lk2_ref[...], axis=-1, keepdims=True)) + lam_init)
    for head in range(DIFF_HEADS):
        cols = slice(head * LANES, (head + 1) * LANES)
        slope = _alibi_slope_log2(head)
        q = q_ref[0, :, cols].astype(_F32)
        q_maps = jnp.concatenate([jnp.where(lane < DIFF_HEAD_DIM, q, 0.0),
                                  jnp.where(lane >= DIFF_HEAD_DIM, q, 0.0)], axis=0).astype(_BF16)
        s_c = _dot_nt(q_maps, cache_head(kc_ref, head)) - slope * distance_c
        s_n = jnp.where(visible_n, _dot_nt(q_maps, kn_ref[0, :, cols]) - slope * distance_n, -jnp.inf)
        m = jnp.maximum(jnp.max(s_c, axis=-1, keepdims=True), jnp.max(s_n, axis=-1, keepdims=True))
        p_c = jnp.exp2(s_c - m)
        p_n = jnp.exp2(s_n - m)
        l = jnp.sum(p_c, axis=-1, keepdims=True) + jnp.sum(p_n, axis=-1, keepdims=True)
        o = (_dot(p_c.astype(_BF16), cache_head(vc_ref, head))
             + _dot(p_n.astype(_BF16), vn_ref[0, :, cols])) / l
        o = o[:n] - lam * o[n:]
        o_ref[0, :, cols] = (_rms(o, g_ref[...]) * (1.0 - lam_init)).astype(o_ref.dtype)


def _diff_step_call(q, k_new, v_new, k_cache, v_cache, lq1, lk1, lq2, lk2, subln_g, *, lam_init):
    b, n, _ = q.shape
    rows = k_cache.shape[1]
    assert 2 * DIFF_HEAD_DIM == LANES and CHUNK == 1 << CHUNK_SHIFT and n & (n - 1) == 0
    new = pl.BlockSpec((1, n, DIFF_WIDTH), lambda i: (i, 0, 0))
    cache = pl.BlockSpec((1, rows, LANES), lambda i: (i, 0, 0))
    lam_spec = pl.BlockSpec((1, DIFF_HEAD_DIM), lambda i: (0, 0))
    return pl.pallas_call(
        functools.partial(_diff_step_kernel, lam_init=lam_init),
        grid=(b,),
        in_specs=[new, new, new, cache, cache, lam_spec, lam_spec, lam_spec, lam_spec,
                  pl.BlockSpec((1, LANES), lambda i: (0, 0))],
        out_specs=new,
        out_shape=jax.ShapeDtypeStruct((b, n, DIFF_WIDTH), _BF16),
        compiler_params=pltpu.CompilerParams(
            dimension_semantics=("arbitrary",), vmem_limit_bytes=VMEM_LIMIT_BYTES),
        name="diff_step",
    )(q, k_new, v_new, k_cache, v_cache, lq1, lk1, lq2, lk2, subln_g)


def _prompt_specs(nq, t):
    assert nq % t == 0 and t & (t - 1) == 0
    q_spec = pl.BlockSpec((1, t, LANES), lambda bi, g, qi: (bi, qi, g))
    k_spec = pl.BlockSpec((1, nq, LANES), lambda bi, g, qi: (bi, 0, g))
    vt_spec = pl.BlockSpec((1, LANES, nq), lambda bi, g, qi: (bi, g, 0))
    return q_spec, k_spec, vt_spec


def _sb_prompt_parts(q_ref, k_ref, vt_ref, o_ref, acc_ref, c_ref, z_ref, lb_ref, hl_ref, w_ref, *,
                     t, group, depth):
    assert LANES == 2 * SB_HEAD_DIM
    strips = 2 * t // LANES
    qi = pl.program_id(2)
    lane = lax.broadcasted_iota(jnp.int32, (1, LANES), 1)
    key = lax.broadcasted_iota(jnp.int32, (t, LANES), 0)
    qry = lax.broadcasted_iota(jnp.int32, (t, LANES), 1)
    later = (lax.broadcasted_iota(jnp.int32, (t, t), 1)
             > lax.broadcasted_iota(jnp.int32, (t, t), 0)).astype(_BF16)

    def tile_queries(g):
        q = q_ref[0, g * t:(g + 1) * t, :].astype(_F32)
        return jnp.concatenate([jnp.where(lane < SB_HEAD_DIM, q, 0.0),
                                jnp.where(lane >= SB_HEAD_DIM, q, 0.0)], axis=0).astype(_BF16)

    q_heads = [tile_queries(g) for g in range(group)]

    def stage_scores(g, j, buf):
        z = _dot_nt(k_ref[0, pl.ds(pl.multiple_of(j * t, t), t), :], q_heads[g])
        for s in range(strips):
            z_ref[buf, s] = z[:, s * LANES:(s + 1) * LANES]

    def tile_mask(own, s):
        return key < ((s * LANES + qry) & (t - 1)) if own is True else own

    def stage_logs(c, own, buf):
        c_new = []
        for s in range(strips):
            lb_ref[buf, s], log_stay = _stick_logs(z_ref[buf, s])
            if own is not False:
                log_stay = jnp.where(tile_mask(own, s), log_stay, 0.0)
            hl_ref[buf, s] = jnp.concatenate(_split_hi_lo(log_stay), axis=1)
            c_new.append(c[:, s * LANES:(s + 1) * LANES] + jnp.sum(log_stay, axis=0, keepdims=True))
        return jnp.concatenate(c_new, axis=1)

    def stage_weights(c, own, buf):
        for s in range(strips):
            sums = _dot(later, hl_ref[buf, s])
            w = jnp.exp2(lb_ref[buf, s] + (sums[:, :LANES] + sums[:, LANES:] + c[:, s * LANES:(s + 1) * LANES]))
            if own is not False:
                w = jnp.where(tile_mask(own, s), w, 0.0)
            head, part = divmod(s, t // LANES)
            w_ref[buf, head, :, part * LANES:(part + 1) * LANES] = w.astype(_BF16)

    def stage_values(j, buf):
        vt = vt_ref[0, :, pl.ds(pl.multiple_of(j * t, t), t)]
        return jnp.concatenate([_dot(vt[:SB_HEAD_DIM], w_ref[buf, 0]), _dot(vt[SB_HEAD_DIM:], w_ref[buf, 1])],
                               axis=0)

    def walk_tile(g, j, c, own, buf):
        stage_scores(g, j, buf)
        c_new = stage_logs(c, own, buf)
        stage_weights(c, own, buf)
        return c_new, stage_values(j, buf)

    def first_tiles(between=()):
        between = list(between) + [lambda: None] * 4
        walks = []
        for g in range(group):
            for d in range(depth):
                j = qi * group + g - d
                own = True if d == 0 else (j >= 0 if d > g else False)
                walks.append((g, jnp.maximum(j, 0), own, depth * g + d))
        for g, j, own, buf in walks:
            stage_scores(g, j, buf)
        between[0]()
        c_in = {}
        for g, j, own, buf in walks:
            c_in[buf] = jnp.zeros((1, 2 * t), _F32) if own is True else c_ref[g]
            c_ref[g] = stage_logs(c_in[buf], own, buf)
        between[1]()
        for g, j, own, buf in walks:
            stage_weights(c_in[buf], own, buf)
        between[2]()
        for g, j, own, buf in walks:
            o = stage_values(j, buf)
            acc_ref[g] = o if own is True else acc_ref[g] + o
        between[3]()

    assert group >= depth - 1

    def older_tile(g, step):
        return qi * group + g - depth - step

    def any_active(step):
        flags = [(older_tile(g, step) >= 0) & (jnp.max(c_ref[g]) > -F32_EXP2_ZERO) for g in range(group)]
        return functools.reduce(jnp.logical_or, flags)

    def walk_older(state):
        step, _ = state
        for g in range(group):
            @pl.when((older_tile(g, step) >= 0) & (jnp.max(c_ref[g]) > -F32_EXP2_ZERO))
            def _(g=g):
                c, o = walk_tile(g, older_tile(g, step), c_ref[g], False, depth * g)
                c_ref[g] = c
                acc_ref[g] += o
        return step + 1, any_active(step + 1)

    def finish():
        lax.while_loop(lambda state: state[1], walk_older, (0, any_active(0)))
        for g in range(group):
            o_ref[0, g * t:(g + 1) * t, :] = acc_ref[g].T.astype(o_ref.dtype)

    return first_tiles, finish


def _sb_prompt_scratch(t, group, depth):
    sets = depth * group
    strips = 2 * t // LANES
    return [pltpu.VMEM((group, LANES, t), _F32), pltpu.VMEM((group, 1, 2 * t), _F32),
            pltpu.VMEM((sets, strips, t, LANES), _F32), pltpu.VMEM((sets, strips, t, LANES), _F32),
            pltpu.VMEM((sets, strips, t, 2 * LANES), _BF16), pltpu.VMEM((sets, 2, t, t), _BF16)]


def _diff_prompt_kernel(q_ref, k_ref, vt_ref, lq1_ref, lk1_ref, lq2_ref, lk2_ref, g_ref, o_ref,
                        m_ref, l_ref, acc_ref, pen_ref, kaug_ref, knorm_ref, *p_refs, tq, tk, lam_init,
                        side_work=None):
    own = tq // tk
    ahead_pairs = own // 2
    assert tq == own * tk and own % 2 == 0 and own <= DEPTH_SLOTS
    strips = 2 * tq // MXU_WIDTH
    head = pl.program_id(1)
    qi = pl.program_id(2)
    n_past = qi * own
    lane = lax.broadcasted_iota(jnp.int32, (1, LANES), 1)
    slope = _alibi_slope_log2(head)

    @pl.when(qi == 0)
    def _():
        key = lax.broadcasted_iota(jnp.int32, (tk, 2 * tq), 0)
        qry = lax.broadcasted_iota(jnp.int32, (tk, 2 * tq), 1) & (tq - 1)
        for d in range(own):
            k_pos = d * tk + key
            visible = lax.shift_right_logical(k_pos, CHUNK_SHIFT) <= lax.shift_right_logical(qry, CHUNK_SHIFT)
            ahead = -2.0 * slope * jnp.maximum(k_pos - qry, 0).astype(_F32)
            pen_ref[d] = jnp.where(visible, ahead, -jnp.inf)
        row = lax.broadcasted_iota(jnp.int32, (tk, LANES), 0)
        col = lax.broadcasted_iota(jnp.int32, (tk, LANES), 1)
        kaug_ref[...] = jnp.where(col < 3, row, jnp.where((col >= 6) & (col < 9), 1, 0)).astype(_F32)
        k_sq = jnp.square(k_ref[0].astype(_F32))
        for c in range(2):
            in_map = (lane >= c * DIFF_HEAD_DIM) & (lane < (c + 1) * DIFF_HEAD_DIM)
            norm_sq = jnp.sum(jnp.where(in_map, k_sq, 0.0), axis=1, keepdims=True)
            knorm_ref[c] = jnp.broadcast_to(jnp.sqrt(jnp.max(norm_sq, axis=0, keepdims=True)), (1, LANES))

    def split3(x):
        hi = x.astype(_BF16).astype(_F32)
        mid = (x - hi).astype(_BF16).astype(_F32)
        return [hi, mid, x - hi - mid]

    def augmented_queries(columns):
        blocks = []
        for c in range(2):
            aug = jnp.zeros((tq, LANES), _F32)
            for i, value in enumerate(columns[c]):
                aug = jnp.where(lane == i, value, aug)
            in_map = (lane >= c * DIFF_HEAD_DIM) & (lane < (c + 1) * DIFF_HEAD_DIM)
            blocks.append(jnp.concatenate([jnp.where(in_map, q, 0.0), aug], axis=1))
        return jnp.concatenate(blocks, axis=0).astype(_BF16)

    q = q_ref[0].astype(_F32)
    l_ref[...] = jnp.zeros_like(l_ref)
    acc_ref[...] = jnp.zeros_like(acc_ref)

    def slot_tile(n):
        return jnp.where(n < own, n_past + n, n_past - 1 - (n - own))

    def tile_start(n):
        return pl.multiple_of(jnp.clip(slot_tile(n), 0, n_past + own - 1) * tk, tk)

    def pair_buffer(pair):
        return (2 * pair) % DEPTH_SLOTS

    q_sq = jnp.square(q)
    reach = jnp.zeros((1, 1), _F32)
    bounds = []
    for c in range(2):
        in_map = (lane >= c * DIFF_HEAD_DIM) & (lane < (c + 1) * DIFF_HEAD_DIM)
        q_norm = jnp.sqrt(jnp.max(jnp.sum(jnp.where(in_map, q_sq, 0.0), axis=1, keepdims=True),
                                  axis=0, keepdims=True))
        bounds.append(q_norm * knorm_ref[c][:, :1])
        reach = jnp.maximum(reach, (2.0 * bounds[c] + F32_EXP2_ZERO) / slope)
    tiles_in_reach = jnp.minimum(jnp.floor((reach - 1.0) / tk) + 1.0, float(1 << 20))
    n_walk = jnp.minimum(jnp.max(tiles_in_reach).astype(jnp.int32), n_past)
    pairs = (own + n_walk + 1) // 2

    def online_walk():
        if side_work is not None:
            side_work()
        queries = augmented_queries([split3(slope)] * 2)
        key = lax.broadcasted_iota(jnp.int32, (tk, 2 * tq), 0)
        qry = lax.broadcasted_iota(jnp.int32, (tk, 2 * tq), 1) & (tq - 1)
        m_ref[...] = jnp.full_like(m_ref, NEG_BIG)

        def slot(n, carry):
            start = tile_start(n)
            k_pos = (slot_tile(n) - n_past) * tk + key
            keys = jnp.concatenate([k_ref[0, pl.ds(start, tk), :], kaug_ref[...].astype(_BF16)], axis=1)
            s = _dot_nt(keys, queries) + slope * (qry - key - jnp.abs(qry - k_pos)).astype(_F32)
            visible = (lax.shift_right_arithmetic(k_pos, CHUNK_SHIFT)
                       <= lax.shift_right_logical(qry, CHUNK_SHIFT))
            s = jnp.where(visible, s, -jnp.inf)
            m_old = m_ref[...]
            m_new = jnp.maximum(m_old, jnp.max(s, axis=0, keepdims=True))
            alpha = jnp.exp2(m_old - m_new)
            p = jnp.exp2(s - m_new)
            l_ref[...] = alpha * l_ref[...] + jnp.sum(p, axis=0, keepdims=True)
            m_ref[...] = m_new
            pv = _dot(vt_ref[0, :, pl.ds(start, tk)], p.astype(_BF16))
            for c in range(strips):
                cols = slice(c * MXU_WIDTH, (c + 1) * MXU_WIDTH)
                acc_ref[c] = alpha[:, cols] * acc_ref[c] + pv[:, cols]
            return carry

        lax.fori_loop(0, own + n_walk, slot, 0)

    def fixed_walk():
        q_pos = (qi * tq + lax.broadcasted_iota(jnp.int32, (tq, 1), 0)).astype(_F32)
        q_aug = augmented_queries([split3(slope) + split3(slope * tk) + split3(-slope * q_pos - bounds[c])
                                   for c in range(2)])

        def tile_keys(n):
            j = jnp.where(n < own + n_walk, slot_tile(n), -(1 << 20)).astype(_F32)
            aug = jnp.where((lane >= 3) & (lane < 6), j, kaug_ref[...]).astype(_BF16)
            return jnp.concatenate([k_ref[0, pl.ds(tile_start(n), tk), :], aug], axis=1)

        def weights(n, buf, own_slot):
            keys = tile_keys(n)
            for c in range(strips):
                strip = slice(c * MXU_WIDTH, (c + 1) * MXU_WIDTH)
                if own_slot and tk == MXU_WIDTH and c % (strips // 2) < n:
                    p_refs[buf][c] = jnp.zeros((tk, MXU_WIDTH), _BF16)
                    continue
                e = _dot_nt(keys, q_aug[strip])
                if own_slot:
                    e = e + pen_ref[n, :, strip]
                p = jnp.exp2(e)
                l_ref[:, strip] += jnp.sum(p, axis=0, keepdims=True)
                p_refs[buf][c] = p.astype(_BF16)

        def values(n, buf):
            vt = vt_ref[0, :, pl.ds(tile_start(n), tk)]
            for c in range(strips):
                acc_ref[c] += _dot(vt, p_refs[buf][c])

        def pair_step(t, pair):
            for d in range(2):
                values(2 * t + d, pair_buffer(pair) + d)
            for d in range(2):
                weights(2 * (t + ahead_pairs) + d, pair_buffer(pair + ahead_pairs) + d, False)

        def trip(i, carry):
            for pair in range(PAIRS_PER_TRIP):
                pair_step(PAIRS_PER_TRIP * i + pair, pair)
            return carry

        own_weights = [functools.partial(weights, n, n, True) for n in range(own)]
        if side_work is None:
            for emit in own_weights:
                emit()
        else:
            side_work(own_weights)
        steps = pairs - ahead_pairs
        trips = steps // PAIRS_PER_TRIP
        lax.fori_loop(0, trips, trip, 0)
        for pair in range(PAIRS_PER_TRIP - 1):
            @pl.when(steps % PAIRS_PER_TRIP > pair)
            def _(pair=pair):
                pair_step(PAIRS_PER_TRIP * trips + pair, pair)

        for parity in range(2):
            @pl.when(steps % 2 == parity)
            def _(parity=parity):
                for r in range(ahead_pairs):
                    for d in range(2):
                        values(2 * (steps + r) + d, pair_buffer(parity + r) + d)

    spread = 2.0 * jnp.max(jnp.maximum(bounds[0], bounds[1]))
    pl.when(spread < SAFE_SPREAD)(fixed_walk)
    pl.when(spread >= SAFE_SPREAD)(online_walk)

    lam = (jnp.exp(jnp.sum(lq1_ref[...] * lk1_ref[...], axis=-1, keepdims=True))
           - jnp.exp(jnp.sum(lq2_ref[...] * lk2_ref[...], axis=-1, keepdims=True)) + lam_init)
    o = jnp.concatenate([acc_ref[c] for c in range(2 * tq // MXU_WIDTH)], axis=1) / l_ref[...]
    o = o[:, :tq] - lam * o[:, tq:]
    o = o * lax.rsqrt(jnp.mean(o * o, axis=0, keepdims=True) + RMS_EPS)
    o_ref[0] = (o.T * g_ref[...] * (1.0 - lam_init)).astype(o_ref.dtype)


N_SB_SCRATCH = 6


def _prompt_mixer_kernel(sbq_ref, sbk_ref, sbvt_ref, dq_ref, dk_ref, dvt_ref, lq1_ref, lk1_ref, lq2_ref,
                         lk2_ref, g_ref, sbo_ref, do_ref, *scratch, sb_params, diff_params):
    first_tiles, finish = _sb_prompt_parts(sbq_ref, sbk_ref, sbvt_ref, sbo_ref, *scratch[:N_SB_SCRATCH],
                                           **sb_params)
    _diff_prompt_kernel(dq_ref, dk_ref, dvt_ref, lq1_ref, lk1_ref, lq2_ref, lk2_ref, g_ref, do_ref,
                        *scratch[N_SB_SCRATCH:], side_work=first_tiles, **diff_params)
    finish()


def _prompt_mixer_call(sbq, sbk, sbvt, dq, dk, dvt, lq1, lk1, lq2, lk2, subln_g, *, lam_init):
    b, nq, _ = dq.shape
    tq, tk = DIFF_PROMPT_TILE, DIFF_KEY_TILE
    sb_params = dict(t=SB_PROMPT_TILE, group=SB_PROMPT_GROUP, depth=SB_PROMPT_DEPTH)
    assert 2 * DIFF_HEAD_DIM == LANES and SB_WIDTH // LANES == DIFF_HEADS
    assert SB_PROMPT_TILE * SB_PROMPT_GROUP == tq
    assert tk % CHUNK == 0 and CHUNK == 1 << CHUNK_SHIFT and tk <= 256 and nq // tk <= 256
    q_spec, k_spec, vt_spec = _prompt_specs(nq, tq)
    lam_spec = pl.BlockSpec((1, DIFF_HEAD_DIM), lambda bi, g, qi: (0, 0))
    stat = pltpu.VMEM((1, 2 * tq), _F32)
    sb_scratch = _sb_prompt_scratch(**sb_params)
    assert len(sb_scratch) == N_SB_SCRATCH
    out = jax.ShapeDtypeStruct((b, nq, DIFF_WIDTH), _BF16)
    return pl.pallas_call(
        functools.partial(_prompt_mixer_kernel, sb_params=sb_params,
                          diff_params=dict(tq=tq, tk=tk, lam_init=lam_init)),
        grid=(b, DIFF_HEADS, nq // tq),
        in_specs=[q_spec, k_spec, vt_spec, q_spec, k_spec, vt_spec, lam_spec, lam_spec, lam_spec, lam_spec,
                  pl.BlockSpec((1, LANES), lambda bi, g, qi: (0, 0))],
        out_specs=[q_spec, q_spec],
        out_shape=[out, out],
        scratch_shapes=sb_scratch
        + [stat, stat, pltpu.VMEM((2 * tq // MXU_WIDTH, LANES, MXU_WIDTH), _F32),
           pltpu.VMEM((tq // tk, tk, 2 * tq), _F32), pltpu.VMEM((tk, LANES), _F32),
           pltpu.VMEM((2, 1, LANES), _F32)]
        + [pltpu.VMEM((2 * tq // MXU_WIDTH, tk, MXU_WIDTH), _BF16)] * DEPTH_SLOTS,
        compiler_params=pltpu.CompilerParams(
            dimension_semantics=("arbitrary", "arbitrary", "arbitrary"),
            vmem_limit_bytes=VMEM_LIMIT_BYTES),
        name="prompt_mixer",
    )(sbq, sbk, sbvt, dq, dk, dvt, lq1, lk1, lq2, lk2, subln_g)


def _step_mixer_kernel(sbq, sbkn, sbvn, sbkc, sbvc, dq, dkn, dvn, dkc, dvc, lq1, lk1, lq2, lk2, g, sbo, do,
                       acc_ref, c_ref, *, tk, lam_init):
    _sb_step_kernel(sbq, sbkn, sbvn, sbkc, sbvc, sbo, acc_ref, c_ref, tk=tk)
    _diff_step_kernel(dq, dkn, dvn, dkc, dvc, lq1, lk1, lq2, lk2, g, do, lam_init=lam_init)


def _step_mixer_call(sbq, sbkn, sbvn, sbkc, sbvc, dq, dkn, dvn, dkc, dvc, lq1, lk1, lq2, lk2, subln_g, *,
                     lam_init):
    b, n, _ = sbq.shape
    past = sbkc.shape[2]
    tk = min(SB_STEP_TILE, past)
    assert past % tk == 0 and n & (n - 1) == 0 and SB_WIDTH == DIFF_WIDTH
    new = pl.BlockSpec((1, n, SB_WIDTH), lambda i: (i, 0, 0))
    sb_cache = pl.BlockSpec((1, SB_WIDTH, past), lambda i: (i, 0, 0))
    d_cache = pl.BlockSpec((1, dkc.shape[1], LANES), lambda i: (i, 0, 0))
    lam_spec = pl.BlockSpec((1, DIFF_HEAD_DIM), lambda i: (0, 0))
    out = jax.ShapeDtypeStruct((b, n, SB_WIDTH), _BF16)
    return pl.pallas_call(
        functools.partial(_step_mixer_kernel, tk=tk, lam_init=lam_init),
        grid=(b,),
        in_specs=[new, new, new, sb_cache, sb_cache, new, new, new, d_cache, d_cache,
                  lam_spec, lam_spec, lam_spec, lam_spec, pl.BlockSpec((1, LANES), lambda i: (0, 0))],
        out_specs=[new, new],
        out_shape=[out, out],
        scratch_shapes=[pltpu.VMEM((SB_HEADS, n, SB_HEAD_DIM), _F32), pltpu.VMEM((SB_HEADS, n, 1), _F32)],
        compiler_params=pltpu.CompilerParams(
            dimension_semantics=("arbitrary",), vmem_limit_bytes=VMEM_LIMIT_BYTES),
        name="step_mixer",
    )(sbq, sbkn, sbvn, sbkc, sbvc, dq, dkn, dvn, dkc, dvc, lq1, lk1, lq2, lk2, subln_g)


def _encoder_layer(x, past, w, lam_init):
    b, n, _ = x.shape
    m = b * n
    h = _ffn(x.reshape(m, D_MODEL), w["ff1_pre_g"], w["ff1_w_gate"], w["ff1_w_up"], w["ff1_w_down"],
             w["ff1_post_g"])
    lam_args = (w["lam_q1"], w["lam_k1"], w["lam_q2"], w["lam_k2"], w["subln_g"])

    def seq(a):
        return a.reshape(b, n, a.shape[-1])

    if past is None:
        sbq, sbk, sbv, dq, dk, dv, sbk16, sbvt16, dk16, dvt16 = _qkv(h, w["mix_pre_g"], w["w_in"], seq_len=n)
        sb_o, d_o = _prompt_mixer_call(seq(sbq), seq(sbk16), sbvt16, seq(dq), seq(dk16), dvt16, *lam_args,
                                       lam_init=lam_init)
        sbk, sbv = (a.reshape(b, SB_HEADS, SB_HEAD_DIM, n).transpose(0, 3, 1, 2) for a in (sbk, sbv))
    else:
        sbq, sbk, sbv, dq, dk, dv, sbk16, sbv16, dk16, dv16 = _qkv(h, w["mix_pre_g"], w["w_in"])
        sb_o, d_o = _step_mixer_call(seq(sbq), seq(sbk16), seq(sbv16), past[0], past[1],
                                     seq(dq), seq(dk16), seq(dv16), past[2], past[3], *lam_args, lam_init=lam_init)
    y = _ffn(h, w["ff2_pre_g"], w["ff2_w_gate"], w["ff2_w_up"], w["ff2_w_down"], w["ff2_post_g"],
             mixer=(sb_o.reshape(m, SB_WIDTH), d_o.reshape(m, DIFF_WIDTH), w["w_out"], w["mix_post_g"]),
             final_g=w["final_g"])
    rows = (sbk.reshape(b, n, SB_HEADS, SB_HEAD_DIM), sbv.reshape(b, n, SB_HEADS, SB_HEAD_DIM),
            dk.reshape(b, n, DIFF_HEADS, 2 * DIFF_HEAD_DIM), dv.reshape(b, n, DIFF_HEADS, 2 * DIFF_HEAD_DIM))
    return y.reshape(b, n, D_MODEL), rows


_MATRICES = ("ff1_w_gate", "ff1_w_up", "ff1_w_down", "w_in", "w_out", "ff2_w_gate", "ff2_w_up", "ff2_w_down")


def kernel(x_prompt, x_sample, cache_sb_k, cache_sb_v, cache_diff_k, cache_diff_v, ff1_pre_g, ff1_w_gate, ff1_w_up, ff1_w_down, ff1_post_g, mix_pre_g, w_in, lam_q1, lam_k1, lam_q2, lam_k2, subln_g, w_out, mix_post_g, ff2_pre_g, ff2_w_gate, ff2_w_up, ff2_w_down, ff2_post_g, final_g):
    params = dict(ff1_pre_g=ff1_pre_g, ff1_w_gate=ff1_w_gate, ff1_w_up=ff1_w_up, ff1_w_down=ff1_w_down,
                  ff1_post_g=ff1_post_g, mix_pre_g=mix_pre_g, w_in=w_in, lam_q1=lam_q1, lam_k1=lam_k1,
                  lam_q2=lam_q2, lam_k2=lam_k2, subln_g=subln_g, w_out=w_out, mix_post_g=mix_post_g,
                  ff2_pre_g=ff2_pre_g, ff2_w_gate=ff2_w_gate, ff2_w_up=ff2_w_up, ff2_w_down=ff2_w_down,
                  ff2_post_g=ff2_post_g, final_g=final_g)
    yp, ys = x_prompt, x_sample
    rows_p, rows_s = [], []
    for l in range(DEPTH):
        lam_init = 0.8 - 0.6 * math.exp(-0.3 * l)
        w = {name: (p[l].astype(_BF16) if name in _MATRICES else p[l][None, :].astype(_F32))
             for name, p in params.items()}
        yp, rp = _encoder_layer(yp, None, w, lam_init)
        past = tuple([c[l].transpose(0, 2, 3, 1).reshape(c.shape[1], SB_WIDTH, c.shape[2])
                      for c in (cache_sb_k, cache_sb_v)]
                     + [c[l].reshape(c.shape[1], -1, c.shape[-1]) for c in (cache_diff_k, cache_diff_v)])
        ys, rs = _encoder_layer(ys, past, w, lam_init)
        rows_p.append(rp)
        rows_s.append(rs)
    stacked_p = [jnp.stack(r, axis=0) for r in zip(*rows_p)]
    stacked_s = [jnp.stack(r, axis=0) for r in zip(*rows_s)]
    return (yp, ys, *stacked_p, *stacked_s)
```
